```python
import math
import jax
import jax.numpy as jnp
from jax import lax
import numpy as np

D_MODEL = 1024
BATCH = 8
SEQ = 2048
DEPTH = 2

GRID_W = 64
CTX_LEN = 256
NORM_EPS = 1e-6
ROPE_BASE = 10000.0
RET_HEADS = D_MODEL // 256
RET_DK = 64
RET_DV = 2 * RET_DK
RET_CHUNK = 128
DIFF_HEADS = D_MODEL // 256
DIFF_DH = 64
DIFF_DV = 2 * DIFF_DH
Q_BLOCK = 128
POOL_WINDOWS = (2, 4, 8, 16)
POOL_GROUP = D_MODEL // len(POOL_WINDOWS)
N_EXPERTS = 16
N_EXPERT_GROUPS = 4
EXPERTS_PER_GROUP = N_EXPERTS // N_EXPERT_GROUPS
TOP_K = 2
D_EXPERT = 512
RET_Q_W = RET_HEADS * RET_DK
DIFF_Q_W = DIFF_HEADS * 2 * DIFF_DH
RET_G_W = RET_HEADS * RET_DV
RET_K_W = RET_HEADS * RET_DK
RET_V_W = RET_HEADS * RET_DV
DIFF_K_W = DIFF_HEADS * 2 * DIFF_DH
DIFF_V_W = DIFF_HEADS * DIFF_DV
QG_W = RET_Q_W + DIFF_Q_W + RET_G_W
KV_W = RET_K_W + RET_V_W + DIFF_K_W + DIFF_V_W
IN_W = QG_W + KV_W
QG_SPLITS = (RET_Q_W, RET_Q_W + DIFF_Q_W)
KV_SPLITS = (RET_K_W, RET_K_W + RET_V_W, RET_K_W + RET_V_W + DIFF_K_W)
MIX_W = RET_HEADS * RET_DV + DIFF_HEADS * DIFF_DV
N_EVEN = (DEPTH + 1) // 2
N_ODD = DEPTH // 2

kernel_name = 'hybrid_retention_diffattn_pool_moe_dit'


def rms_norm(x, w):
    xf = x.astype(jnp.float32)
    y = xf * lax.rsqrt(jnp.mean(xf * xf, axis=-1, keepdims=True) + NORM_EPS)
    return (y * w.astype(jnp.float32)).astype(x.dtype)


def head_rms(o):
    of = o.astype(jnp.float32)
    return of * lax.rsqrt(jnp.mean(of * of, axis=-1, keepdims=True) + NORM_EPS)


def modulate(h, shift, scale):
    return h * (1 + scale) + shift


def rope_1d(x, pos):
    half = x.shape[-1] // 2
    inv = ROPE_BASE ** (-jnp.arange(half, dtype=jnp.float32) / half)
    ang = pos.astype(jnp.float32)[:, None] * inv[None, :]
    cos = jnp.cos(ang).astype(x.dtype)
    sin = jnp.sin(ang).astype(x.dtype)
    x1, x2 = x[..., :half], x[..., half:]
    return jnp.concatenate([x1 * cos - x2 * sin, x1 * sin + x2 * cos], axis=-1)


def rope_2d(x, row, col):
    h = x.shape[-1] // 2
    return jnp.concatenate([rope_1d(x[..., :h], row), rope_1d(x[..., h:], col)], axis=-1)


def to_heads(a, n_heads):
    b, t, _ = a.shape
    return a.reshape(b, t, n_heads, -1).transpose(0, 2, 1, 3)


def to_pair_heads(a):
    b, t, _ = a.shape
    return a.reshape(b, t, DIFF_HEADS, 2, DIFF_DH).transpose(0, 2, 3, 1, 4)


def from_heads(o):
    b, h, t, d = o.shape
    return o.transpose(0, 2, 1, 3).reshape(b, t, h * d)


def retention_chunked(q, k, v, log_g, s0):
    b, h, t, dk = q.shape
    dv = v.shape[-1]
    n = t // RET_CHUNK
    qc = q.reshape(b, h, n, RET_CHUNK, dk)
    kc = k.reshape(b, h, n, RET_CHUNK, dk)
    vc = v.reshape(b, h, n, RET_CHUNK, dv)
    pos = jnp.arange(RET_CHUNK, dtype=jnp.float32)
    lg = log_g[:, None]
    gap = pos[:, None] - pos[None, :]
    mask = jnp.where(gap >= 0, jnp.exp(lg[:, :, None] * jnp.maximum(gap, 0.0)[None]), 0.0)
    scores = jnp.einsum('bhnid,bhnjd->bhnij', qc, kc) * mask[None, :, None]
    o_intra = jnp.einsum('bhnij,bhnje->bhnie', scores, vc)
    k_dec = jnp.exp(lg * (RET_CHUNK - 1 - pos))
    q_dec = jnp.exp(lg * (pos + 1))
    c_dec = jnp.exp(log_g * RET_CHUNK)
    kv = jnp.einsum('bhnjd,bhnje->nbhde', kc * k_dec[None, :, None, :, None], vc)

    def step(s, kv_i):
        return c_dec[None, :, None, None] * s + kv_i, s

    s_final, s_in = lax.scan(step, s0, kv)
    o_cross = jnp.einsum('bhnid,nbhde->bhnie', qc * q_dec[None, :, None, :, None], s_in)
    return (o_intra + o_cross).reshape(b, h, t, dv), s_final


def retention_state(k, v, log_g):
    t = k.shape[2]
    dec = jnp.exp(log_g[:, None] * (t - 1 - jnp.arange(t, dtype=jnp.float32)))
    return jnp.einsum('bhtd,bhte->bhde', k * dec[None, :, :, None], v)


def bidir_retention(q, k, v, log_decay, s0_f, s0_b):
    o_f, s_f = retention_chunked(q, k, v, log_decay[0], s0_f)
    o_b, s_b = retention_chunked(jnp.flip(q, 2), jnp.flip(k, 2), jnp.flip(v, 2), log_decay[1], s0_b)
    return o_f + jnp.flip(o_b, 2), s_f, s_b


def diff_softmax_pair(q, k, v, lam):
    s = jnp.einsum('bhcqd,bhckd->bhcqk', q, k).astype(jnp.float32) * (DIFF_DH ** -0.5)
    p = jax.nn.softmax(s, axis=-1)
    a = p[:, :, 0] - lam * p[:, :, 1]
    return jnp.einsum('bhqk,bhkd->bhqd', a.astype(v.dtype), v)


def diff_attention_blocked(q, k_all, v_all, lam):
    b, h, _, t, dh = q.shape
    nb = t // Q_BLOCK
    qb = q.reshape(b, h, 2, nb, Q_BLOCK, dh).transpose(3, 0, 1, 2, 4, 5)
    ob = lax.map(lambda blk: diff_softmax_pair(blk, k_all, v_all, lam), qb)
    return ob.transpose(1, 2, 0, 3, 4).reshape(b, h, t, -1)


def even_mixer(hx, hc, w_in, w_out, log_decay, lam_vec, subln_w, layer_idx, row, col, ctx_out):
    b = hx.shape[0]
    f32 = jnp.float32
    lam_init = 0.8 - 0.6 * math.exp(-0.3 * layer_idx)
    lv = lam_vec.astype(f32)
    lam = jnp.exp(jnp.sum(lv[0] * lv[1])) - jnp.exp(jnp.sum(lv[2] * lv[3])) + lam_init
    ld = log_decay.astype(f32)
    k_scale = RET_DK ** -0.5

    xq, xdq, xg = jnp.split(hx @ w_in[:, :QG_W], QG_SPLITS, axis=-1)
    xk, xv, xdk, xdv = jnp.split(hx @ w_in[:, QG_W:], KV_SPLITS, axis=-1)
    ck, cv, cdk, cdv = jnp.split(hc @ w_in[:, QG_W:], KV_SPLITS, axis=-1)

    rq = rope_2d(to_heads(xq, RET_HEADS), row, col).astype(f32)
    rk = (rope_2d(to_heads(xk, RET_HEADS), row, col) * k_scale).astype(f32)
    rv = to_heads(xv, RET_HEADS).astype(f32)
    crk = (to_heads(ck, RET_HEADS) * k_scale).astype(f32)
    crv = to_heads(cv, RET_HEADS).astype(f32)
    if ctx_out:
        zeros = jnp.zeros((b, RET_HEADS, RET_DK, RET_DV), f32)
        cq, cdq, cg = jnp.split(hc @ w_in[:, :QG_W], QG_SPLITS, axis=-1)
        c_ret, s_f, s_b = bidir_retention(to_heads(cq, RET_HEADS).astype(f32), crk, crv, ld, zeros, zeros)
    else:
        s_f = retention_state(crk, crv, ld[0])
        s_b = retention_state(jnp.flip(crk, 2), jnp.flip(crv, 2), ld[1])
    x_ret, _, _ = bidir_retention(rq, rk, rv, ld, s_f, s_b)

    dq = rope_2d(to_pair_heads(xdq), row, col)
    dk = rope_2d(to_pair_heads(xdk), row, col)
    dv = to_heads(xdv, DIFF_HEADS)
    cdk_h = to_pair_heads(cdk)
    cdv_h = to_heads(cdv, DIFF_HEADS)
    k_all = jnp.concatenate([cdk_h, dk], axis=3)
    v_all = jnp.concatenate([cdv_h, dv], axis=2)
    x_diff = diff_attention_blocked(dq, k_all, v_all, lam)

    def merge(o_ret, gate, o_diff):
        ret = from_heads(head_rms(o_ret)).astype(gate.dtype) * jax.nn.silu(gate)
        dif = from_heads(head_rms(o_diff) * subln_w.astype(f32) * (1.0 - lam_init)).astype(gate.dtype)
        return jnp.concatenate([ret, dif], axis=-1) @ w_out

    out_x = merge(x_ret, xg, x_diff)
    out_c = None
    if ctx_out:
        c_diff = diff_softmax_pair(to_pair_heads(cdq), cdk_h, cdv_h, lam)
        out_c = merge(c_ret, cg, c_diff)
    return out_x, out_c


def pool_mixer(h, pool_w, pool_scale):
    b, t, d = h.shape
    hf = h.astype(jnp.float32)
    cs = jnp.concatenate([jnp.zeros((b, 1, d), jnp.float32), jnp.cumsum(hf, axis=1)], axis=1)
    pos = jnp.arange(t)
    groups = []
    for gi, w in enumerate(POOL_WINDOWS):
        left = w // 2
        right = w - 1 - left
        lo = jnp.clip(pos - left, 0, t)
        hi = jnp.clip(pos + right + 1, 0, t)
        sl = cs[:, :, gi * POOL_GROUP:(gi + 1) * POOL_GROUP]
        mean = (sl[:, hi] - sl[:, lo]) / (hi - lo).astype(jnp.float32)[None, :, None]
        groups.append(mean - hf[:, :, gi * POOL_GROUP:(gi + 1) * POOL_GROUP])
    pooled = jnp.stack(groups, axis=2).astype(h.dtype)
    mixed = jnp.einsum('btgc,gcd->btgd', pooled, pool_w).reshape(b, t, d)
    return mixed * pool_scale


def moe_route(flat, router_w, router_b):
    n = flat.shape[0]
    s = jax.nn.sigmoid((flat @ router_w).astype(jnp.float32))
    biased = s + router_b.astype(jnp.float32)
    grp_score = jnp.sum(lax.top_k(biased.reshape(n, N_EXPERT_GROUPS, EXPERTS_PER_GROUP), TOP_K)[0], axis=-1)
    g_sel = jnp.argmax(grp_score, axis=-1)
    in_group = (jnp.arange(N_EXPERTS) // EXPERTS_PER_GROUP)[None, :] == g_sel[:, None]
    _, idx = lax.top_k(jnp.where(in_group, biased, -jnp.inf), TOP_K)
    w = jnp.take_along_axis(s, idx, axis=-1)
    w = w / jnp.sum(w, axis=-1, keepdims=True)
    return jnp.sum(jax.nn.one_hot(idx, N_EXPERTS, dtype=jnp.float32) * w[..., None], axis=1)


def moe_ffn(h, router_w, router_b, w_gate, w_up, w_down):
    b, t, d = h.shape
    flat = h.reshape(-1, d)
    combine = moe_route(flat, router_w, router_b)
    out = jnp.zeros(flat.shape, jnp.float32)
    for e in range(N_EXPERTS):
        hid = jax.nn.silu(flat @ w_gate[e]) * (flat @ w_up[e])
        out = out + combine[:, e:e + 1] * (hid @ w_down[e]).astype(jnp.float32)
    return out.astype(h.dtype).reshape(b, t, d)


def setup_inputs(seed: int = 0) -> dict:
    key = jax.random.key(seed)
    ks = jax.random.split(key, 21)
    f32 = jnp.float32

    def nrm(k, shape, scale):
        return jax.random.normal(k, shape, f32) * scale

    base_decay = jnp.log1p(-(2.0 ** (-5.0 - jnp.arange(RET_HEADS, dtype=f32))))
    return {
        'x': nrm(ks[0], (BATCH, SEQ, D_MODEL), 1.0),
        'c': nrm(ks[1], (BATCH, D_MODEL), 1.0),
        'ctx': nrm(ks[2], (BATCH, CTX_LEN, D_MODEL), 1.0),
        'c_ctx': nrm(ks[3], (D_MODEL,), 1.0),
        'ada_w': nrm(ks[4], (DEPTH, D_MODEL, 6 * D_MODEL), 0.5 * D_MODEL ** -0.5),
        'ada_b': nrm(ks[5], (DEPTH, 6 * D_MODEL), 0.02),
        'norm_mix_w': 1.0 + nrm(ks[6], (DEPTH, D_MODEL), 0.01),
        'norm_ffn_w': 1.0 + nrm(ks[7], (DEPTH, D_MODEL), 0.01),
        'w_in': nrm(ks[8], (N_EVEN, D_MODEL, IN_W), D_MODEL ** -0.5),
        'w_out': nrm(ks[9], (N_EVEN, MIX_W, D_MODEL), MIX_W ** -0.5),
        'ret_log_decay': base_decay[None, None, :] * (1.0 + nrm(ks[10], (N_EVEN, 2, RET_HEADS), 0.05)),
        'diff_lambda': nrm(ks[11], (N_EVEN, 4, DIFF_DH), 0.1),
        'diff_subln_w': 1.0 + nrm(ks[12], (N_EVEN, DIFF_DV), 0.01),
        'pool_w': nrm(ks[13], (N_ODD, len(POOL_WINDOWS), POOL_GROUP, POOL_GROUP), POOL_GROUP ** -0.5),
        'pool_scale': 1.0 + nrm(ks[14], (N_ODD, D_MODEL), 0.1),
        'router_w': nrm(ks[15], (D_MODEL, N_EXPERTS), D_MODEL ** -0.5),
        'router_b': nrm(ks[16], (N_EXPERTS,), 0.01),
        'moe_w_gate': nrm(ks[17], (DEPTH, N_EXPERTS, D_MODEL, D_EXPERT), D_MODEL ** -0.5),
        'moe_w_up': nrm(ks[18], (DEPTH, N_EXPERTS, D_MODEL, D_EXPERT), D_MODEL ** -0.5),
        'moe_w_down': nrm(ks[19], (DEPTH, N_EXPERTS, D_EXPERT, D_MODEL), D_EXPERT ** -0.5),
        'final_norm_w': 1.0 + nrm(ks[20], (D_MODEL,), 0.01),
    }


def reference(x, c, ctx, c_ctx, ada_w, ada_b, norm_mix_w, norm_ffn_w, w_in, w_out, ret_log_decay,
              diff_lambda, diff_subln_w, pool_w, pool_scale, router_w, router_b, moe_w_gate, moe_w_up,
              moe_w_down, final_norm_w):
    t = x.shape[1]
    rows = t // GRID_W
    row = jnp.repeat(jnp.arange(rows), GRID_W)
    col = jnp.tile(jnp.arange(GRID_W), rows)
    silu_c = jax.nn.silu(c)
    silu_cc = jax.nn.silu(c_ctx)
    cx = ctx
    for l in range(DEPTH):
        even = l % 2 == 0
        li = l // 2
        ctx_after = any(j % 2 == 0 for j in range(l + 1, DEPTH))
        ctx_in = even or ctx_after
        mod = (silu_c @ ada_w[l] + ada_b[l])[:, None, :]
        sh1, sc1, g1, sh2, sc2, g2 = jnp.split(mod, 6, axis=-1)
        hx = modulate(rms_norm(x, norm_mix_w[l]), sh1, sc1)
        hc = None
        if ctx_in:
            cmod = (silu_cc @ ada_w[l] + ada_b[l])[None, None, :]
            csh1, csc1, cg1, csh2, csc2, cg2 = jnp.split(cmod, 6, axis=-1)
            hc = modulate(rms_norm(cx, norm_mix_w[l]), csh1, csc1)
        if even:
            mx, mc = even_mixer(hx, hc, w_in[li], w_out[li], ret_log_decay[li], diff_lambda[li],
                                diff_subln_w[li], l, row, col, ctx_after)
        else:
            mx = pool_mixer(hx, pool_w[li], pool_scale[li])
            mc = pool_mixer(hc, pool_w[li], pool_scale[li]) if ctx_after else None
        x = x + g1 * mx
        x = x + g2 * moe_ffn(modulate(rms_norm(x, norm_ffn_w[l]), sh2, sc2), router_w, router_b,
                             moe_w_gate[l], moe_w_up[l], moe_w_down[l])
        if ctx_after:
            cx = cx + cg1 * mc
            cx = cx + cg2 * moe_ffn(modulate(rms_norm(cx, norm_ffn_w[l]), csh2, csc2), router_w, router_b,
                                    moe_w_gate[l], moe_w_up[l], moe_w_down[l])
    return rms_norm(x, final_norm_w)
```

```python
import functools
import math

import jax
import jax.numpy as jnp
from jax import lax
from jax.experimental import pallas as pl
from jax.experimental.pallas import tpu as pltpu

F32 = jnp.float32
BF16 = jnp.bfloat16

D = 1024
B = 8
T = 2048
GRID_W = 64
LC = 256
EPS = 1e-6
ROPE_BASE = 10000.0
NH = 4
DK = 64
HV = 128
CH = 256
RB = LC + T
NCH = RB // CH
POOL_WINDOWS = (2, 4, 8, 16)
PG = D // len(POOL_WINDOWS)
NE = 16
NGRP = 4
EPG = NE // NGRP
DE = 512
IN_W = 3072
HALO = 8

TM_PROJ = 256
TM_POOL = 512
TM_MOE = 512
VMEM_LIMIT = 56 * 1024 * 1024


def _cparams(sem):
    return pltpu.CompilerParams(dimension_semantics=sem, vmem_limit_bytes=VMEM_LIMIT)


def _sigmoid(x):
    return 1.0 / (1.0 + jnp.exp(-x))


def _silu(x):
    return x * _sigmoid(x)


def _rms(x):
    return x * lax.rsqrt(jnp.mean(x * x, axis=-1, keepdims=True) + EPS)


def _ada_kernel(cc_ref, w_ref, b_ref, o_ref):
    s = _silu(cc_ref[...])
    o_ref[0] = jnp.dot(s, w_ref[0], preferred_element_type=F32, precision=lax.Precision.HIGHEST) + b_ref[0]


def _ada_mod(cc, ada_w, ada_b):
    depth = ada_w.shape[0]
    tn = 1536
    return pl.pallas_call(
        _ada_kernel,
        grid=(depth, 6 * D // tn),
        in_specs=[
            pl.BlockSpec((16, D), lambda l, n: (0, 0)),
            pl.BlockSpec((1, D, tn), lambda l, n: (l, 0, n)),
            pl.BlockSpec((1, 1, tn), lambda l, n: (l, 0, n)),
        ],
        out_specs=pl.BlockSpec((1, 16, tn), lambda l, n: (l, 0, n)),
        out_shape=jax.ShapeDtypeStruct((depth, 16, 6 * D), F32),
        compiler_params=_cparams(("arbitrary", "arbitrary")),
        name="ada_mod",
    )(cc, ada_w, ada_b.reshape(depth, 1, 6 * D))


def _rope(seg, cos, sin_signed, lo_mask):
    w = seg.shape[1]
    from_hi = pltpu.roll(seg, w - 16, axis=1)
    from_lo = pltpu.roll(seg, 16, axis=1)
    partner = jnp.where(lo_mask, from_hi, from_lo)
    reps = w // cos.shape[1]
    c = jnp.concatenate([cos] * reps, axis=1)
    s = jnp.concatenate([sin_signed] * reps, axis=1)
    return seg * c + partner * s


def _inproj_kernel(x_ref, c_ref, mod_ref, nw_ref, w_ref, cos_ref, sin_ref, o_ref):
    j = pl.program_id(1)
    xt = jnp.where(j == 0, c_ref[0], x_ref[0])
    sh = mod_ref[0, :, 0:D]
    sc = mod_ref[0, :, D:2 * D]
    h = (_rms(xt) * nw_ref[...]) * (1.0 + sc) + sh
    acc = jnp.dot(h.astype(BF16), w_ref[...], preferred_element_type=F32)
    lane = lax.broadcasted_iota(jnp.int32, (TM_PROJ, 512), 1)
    lo_mask = (lane % 32) < 16
    cos = cos_ref[...]
    sin = sin_ref[...]
    for g in range(6):
        seg = acc[:, g * 512:(g + 1) * 512]
        if g in (0, 3, 4):
            seg = _rope(seg, cos, sin, lo_mask)
        o_ref[0, :, g * 512:(g + 1) * 512] = seg.astype(BF16)


def _inproj(x, ctx, mod3, norm_w, w_perm, cos_t, sin_t):
    nj = RB // TM_PROJ
    return pl.pallas_call(
        _inproj_kernel,
        grid=(B, nj),
        in_specs=[
            pl.BlockSpec((1, TM_PROJ, D), lambda b, j: (b, jnp.maximum(j - 1, 0), 0)),
            pl.BlockSpec((1, LC, D), lambda b, j: (b, 0, 0)),
            pl.BlockSpec((1, 1, 2 * D), lambda b, j: (jnp.where(j == 0, B, b), 0, 0)),
            pl.BlockSpec((1, D), lambda b, j: (0, 0)),
            pl.BlockSpec((D, IN_W), lambda b, j: (0, 0)),
            pl.BlockSpec((TM_PROJ, 128), lambda b, j: (j, 0)),
            pl.BlockSpec((TM_PROJ, 128), lambda b, j: (j, 0)),
        ],
        out_specs=pl.BlockSpec((1, TM_PROJ, IN_W), lambda b, j: (b, j, 0)),
        out_shape=jax.ShapeDtypeStruct((B, RB, IN_W), BF16),
        compiler_params=_cparams(("arbitrary", "arbitrary")),
        name="inproj",
    )(x, ctx, mod3, norm_w, w_perm, cos_t, sin_t)


def _retention_kernel(ld_ref, qk_ref, v_ref, g_ref, o_ref, st_ref):
    h = pl.program_id(1)
    lgf = ld_ref[0, h]
    lgb = ld_ref[1, h]
    lane = lax.broadcasted_iota(jnp.int32, (CH, 128), 1)
    fwd_lane = lane < DK
    pos = lax.broadcasted_iota(jnp.int32, (CH, 128), 0).astype(F32)
    kdec = jnp.where(fwd_lane, jnp.exp(lgf * (CH - 1 - pos)), jnp.exp(lgb * pos))
    qdec = jnp.where(fwd_lane, jnp.exp(lgf * (pos + 1.0)), jnp.exp(lgb * (CH - pos)))
    ii = lax.broadcasted_iota(jnp.int32, (CH, CH), 0)
    jj = lax.broadcasted_iota(jnp.int32, (CH, CH), 1)
    gap = (ii - jj).astype(F32)
    mask = (jnp.where(gap >= 0, jnp.exp(lgf * jnp.maximum(gap, 0.0)), 0.0)
            + jnp.where(gap <= 0, jnp.exp(lgb * jnp.maximum(-gap, 0.0)), 0.0))
    ones = jnp.ones((DK, 128), F32)
    cf = jnp.exp(lgf * CH * ones)
    cb = jnp.exp(lgb * CH * ones)

    def chunk(n):
        a = qk_ref[0, n * CH:(n + 1) * CH, :].astype(F32)
        swapped = pltpu.roll(a, DK, axis=1)
        return a, swapped

    kv = []
    for n in range(NCH):
        a, swapped = chunk(n)
        kk = jnp.where(fwd_lane, swapped, a)
        kb = (kk * kdec).astype(BF16)
        vn = v_ref[0, n * CH:(n + 1) * CH, :]
        kv.append(lax.dot_general(kb, vn, (((0,), (0,)), ((), ())), preferred_element_type=F32))
    sf = kv[0][:DK]
    for n in range(1, NCH):
        st_ref[n, 0:DK, :] = sf
        sf = cf * sf + kv[n][:DK]
    sb = kv[0][DK:]
    for n in range(NCH - 1, 0, -1):
        st_ref[n, DK:2 * DK, :] = sb
        sb = cb * sb + kv[n][DK:]

    for n in range(1, NCH):
        a, swapped = chunk(n)
        q = a[:, :DK].astype(BF16)
        k = swapped[:, :DK].astype(BF16)
        scores = lax.dot_general(q, k, (((1,), (1,)), ((), ())), preferred_element_type=F32)
        p = (scores * mask).astype(BF16)
        vn = v_ref[0, n * CH:(n + 1) * CH, :]
        qq = jnp.where(fwd_lane, a, swapped)
        qd = (qq * qdec).astype(BF16)
        o = (jnp.dot(p, vn, preferred_element_type=F32)
             + jnp.dot(qd, st_ref[n].astype(BF16), preferred_element_type=F32))
        gate = g_ref[0, n * CH:(n + 1) * CH, :].astype(F32)
        o_ref[0, (n - 1) * CH:n * CH, :] = (_rms(o) * _silu(gate)).astype(BF16)


def _retention(proj, log_decay):
    return pl.pallas_call(
        _retention_kernel,
        grid=(B, NH),
        in_specs=[
            pl.BlockSpec(memory_space=pltpu.SMEM),
            pl.BlockSpec((1, RB, 128), lambda b, h: (b, 0, h)),
            pl.BlockSpec((1, RB, 128), lambda b, h: (b, 0, NH + h)),
            pl.BlockSpec((1, RB, 128), lambda b, h: (b, 0, 2 * NH + h)),
        ],
        out_specs=pl.BlockSpec((1, T, 128), lambda b, h: (b, 0, h)),
        out_shape=jax.ShapeDtypeStruct((B, T, NH * HV), BF16),
        scratch_shapes=[pltpu.VMEM((NCH, 128, 128), F32)],
        compiler_params=_cparams(("arbitrary", "arbitrary")),
        name="retention",
    )(log_decay, proj, proj, proj)


def _softmax(s):
    m = jnp.max(s, axis=-1, keepdims=True)
    e = jnp.exp(s - m)
    return e / jnp.sum(e, axis=-1, keepdims=True)


def _diffattn_kernel(lam_ref, q_ref, k_ref, v_ref, sw_ref, o_ref, *, out_scale):
    lam = lam_ref[0]
    q = q_ref[0]
    k = k_ref[0]
    lane = lax.broadcasted_iota(jnp.int32, q.shape, 1)
    zero = jnp.zeros_like(q)
    q1 = jnp.where(lane < DK, q, zero)
    q2 = jnp.where(lane >= DK, q, zero)
    nt = (((1,), (1,)), ((), ()))
    p1 = _softmax(lax.dot_general(q1, k, nt, preferred_element_type=F32))
    p2 = _softmax(lax.dot_general(q2, k, nt, preferred_element_type=F32))
    a = (p1 - lam * p2).astype(BF16)
    o = jnp.dot(a, v_ref[0], preferred_element_type=F32)
    o_ref[0] = (_rms(o) * sw_ref[...] * out_scale).astype(BF16)


def _diffattn(proj, lam, subln_w, out_scale):
    nq = T // TM_PROJ
    off = LC // TM_PROJ
    return pl.pallas_call(
        functools.partial(_diffattn_kernel, out_scale=out_scale),
        grid=(B, NH, nq),
        in_specs=[
            pl.BlockSpec(memory_space=pltpu.SMEM),
            pl.BlockSpec((1, TM_PROJ, 128), lambda b, h, j: (b, j + off, 3 * NH + h)),
            pl.BlockSpec((1, RB, 128), lambda b, h, j: (b, 0, 4 * NH + h)),
            pl.BlockSpec((1, RB, 128), lambda b, h, j: (b, 0, 5 * NH + h)),
            pl.BlockSpec((1, HV), lambda b, h, j: (0, 0)),
        ],
        out_specs=pl.BlockSpec((1, TM_PROJ, 128), lambda b, h, j: (b, j, h)),
        out_shape=jax.ShapeDtypeStruct((B, T, NH * HV), BF16),
        compiler_params=_cparams(("arbitrary", "arbitrary", "arbitrary")),
        name="diffattn",
    )(lam, proj, proj, proj, subln_w)


def _route(logits_t, bias):
    s = [_sigmoid(logits_t[e:e + 1, :]) for e in range(NE)]
    bz = [s[e] + bias[e:e + 1, :] for e in range(NE)]
    grp = []
    for g in range(NGRP):
        m = bz[g * EPG:(g + 1) * EPG]
        best = None
        for i in range(EPG):
            for k in range(i + 1, EPG):
                pair = m[i] + m[k]
                best = pair if best is None else jnp.maximum(best, pair)
        grp.append(best)
    gbest = grp[0]
    gsel = jnp.zeros_like(gbest, dtype=jnp.int32)
    for g in range(1, NGRP):
        better = grp[g] > gbest
        gsel = jnp.where(better, g, gsel)
        gbest = jnp.where(better, grp[g], gbest)
    cb = [bz[i] for i in range(EPG)]
    cs = [s[i] for i in range(EPG)]
    for g in range(1, NGRP):
        pick = gsel == g
        cb = [jnp.where(pick, bz[g * EPG + i], cb[i]) for i in range(EPG)]
        cs = [jnp.where(pick, s[g * EPG + i], cs[i]) for i in range(EPG)]
    i1 = jnp.zeros_like(gsel)
    b1 = cb[0]
    for i in range(1, EPG):
        better = cb[i] > b1
        i1 = jnp.where(better, i, i1)
        b1 = jnp.where(better, cb[i], b1)
    neg = jnp.full_like(b1, -jnp.inf)
    rest = [jnp.where(i1 == i, neg, cb[i]) for i in range(EPG)]
    i2 = jnp.zeros_like(gsel)
    b2 = rest[0]
    for i in range(1, EPG):
        better = rest[i] > b2
        i2 = jnp.where(better, i, i2)
        b2 = jnp.where(better, rest[i], b2)
    zero = jnp.zeros_like(b1)
    w_local = [jnp.where((i1 == i) | (i2 == i), cs[i], zero) for i in range(EPG)]
    denom = w_local[0] + w_local[1] + w_local[2] + w_local[3]
    rows = []
    for e in range(NE):
        g, i = divmod(e, EPG)
        rows.append(jnp.where(gsel == g, w_local[i] / denom, zero))
    return jnp.concatenate(rows, axis=0)


def _ffn_prologue(x1, mod, nfw_ref, rw_ref, rb_ref, h_ref, comb_ref):
    sh2 = mod[:, 3 * D:4 * D]
    sc2 = mod[:, 4 * D:5 * D]
    h2 = (_rms(x1) * nfw_ref[...]) * (1.0 + sc2) + sh2
    h_ref[0] = h2.astype(BF16)
    logits_t = lax.dot_general(rw_ref[...], h2, (((1,), (1,)), ((), ())),
                               preferred_element_type=F32, precision=lax.Precision.HIGHEST)
    comb_ref[...] = _route(logits_t, rb_ref[...])


def _outproj_kernel(ret_ref, dif_ref, w_ref, x_ref, mod_ref, nfw_ref, rw_ref, rb_ref,
                    x1_ref, h_ref, comb_ref):
    mod = mod_ref[0]
    mx = (jnp.dot(ret_ref[0], w_ref[0:NH * HV, :], preferred_element_type=F32)
          + jnp.dot(dif_ref[0], w_ref[NH * HV:, :], preferred_element_type=F32))
    x1 = x_ref[0] + mod[:, 2 * D:3 * D] * mx
    x1_ref[0] = x1
    _ffn_prologue(x1, mod, nfw_ref, rw_ref, rb_ref, h_ref, comb_ref)


def _outproj(ret, dif, w_out, x, mod3, nfw, rw_t, rb):
    nj = T // TM_PROJ
    return pl.pallas_call(
        _outproj_kernel,
        grid=(B, nj),
        in_specs=[
            pl.BlockSpec((1, TM_PROJ, NH * HV), lambda b, j: (b, j, 0)),
            pl.BlockSpec((1, TM_PROJ, NH * HV), lambda b, j: (b, j, 0)),
            pl.BlockSpec((2 * NH * HV, D), lambda b, j: (0, 0)),
            pl.BlockSpec((1, TM_PROJ, D), lambda b, j: (b, j, 0)),
            pl.BlockSpec((1, 1, 6 * D), lambda b, j: (b, 0, 0)),
            pl.BlockSpec((1, D), lambda b, j: (0, 0)),
            pl.BlockSpec((NE, D), lambda b, j: (0, 0)),
            pl.BlockSpec((NE, 1), lambda b, j: (0, 0)),
        ],
        out_specs=[
            pl.BlockSpec((1, TM_PROJ, D), lambda b, j: (b, j, 0)),
            pl.BlockSpec((1, TM_PROJ, D), lambda b, j: (b, j, 0)),
            pl.BlockSpec((NE, TM_PROJ), lambda b, j: (0, b * nj + j)),
        ],
        out_shape=[
            jax.ShapeDtypeStruct((B, T, D), F32),
            jax.ShapeDtypeStruct((B, T, D), BF16),
            jax.ShapeDtypeStruct((NE, B * T), F32),
        ],
        compiler_params=_cparams(("arbitrary", "arbitrary")),
        name="outproj",
    )(ret, dif, w_out, x, mod3, nfw, rw_t, rb)


def _pool_kernel(x_ref, prev_ref, next_ref, mod_ref, nmw_ref, pw_ref, ps_ref, nfw_ref, rw_ref, rb_ref,
                 x1_ref, h_ref, comb_ref, ext_ref):
    i = pl.program_id(1)
    last = pl.num_programs(1) - 1
    mod = mod_ref[0]
    sh1 = mod[:, 0:D]
    sc1 = mod[:, D:2 * D]

    def modnorm(v):
        return (_rms(v) * nmw_ref[...]) * (1.0 + sc1) + sh1

    x = x_ref[0]
    hc = modnorm(x)
    ext_ref[0:HALO, :] = jnp.where(i > 0, modnorm(prev_ref[0]), 0.0)
    ext_ref[HALO:HALO + TM_POOL, :] = hc
    ext_ref[HALO + TM_POOL:, :] = jnp.where(i < last, modnorm(next_ref[0]), 0.0)
    pos = i * TM_POOL + lax.broadcasted_iota(jnp.int32, (TM_POOL, 1), 0)
    mixed = []
    for gi, w in enumerate(POOL_WINDOWS):
        left = w // 2
        right = w - 1 - left
        cols = slice(gi * PG, (gi + 1) * PG)
        tot = None
        for d in range(-left, right + 1):
            part = ext_ref[HALO + d:HALO + d + TM_POOL, cols]
            tot = part if tot is None else tot + part
        cnt = (jnp.minimum(pos + right + 1, T) - jnp.maximum(pos - left, 0)).astype(F32)
        pooled = (tot / cnt - hc[:, cols]).astype(BF16)
        mixed.append(jnp.dot(pooled, pw_ref[gi], preferred_element_type=F32))
    mixed = jnp.concatenate(mixed, axis=1) * ps_ref[...]
    x1 = x + mod[:, 2 * D:3 * D] * mixed
    x1_ref[0] = x1
    _ffn_prologue(x1, mod, nfw_ref, rw_ref, rb_ref, h_ref, comb_ref)


def _pool_layer(x, mod3, nmw, pool_w, pool_scale, nfw, rw_t, rb):
    ni = T // TM_POOL
    hb = TM_POOL // HALO
    return pl.pallas_call(
        _pool_kernel,
        grid=(B, ni),
        in_specs=[
            pl.BlockSpec((1, TM_POOL, D), lambda b, i: (b, i, 0)),
            pl.BlockSpec((1, HALO, D), lambda b, i: (b, jnp.maximum(i * hb - 1, 0), 0)),
            pl.BlockSpec((1, HALO, D), lambda b, i: (b, jnp.minimum((i + 1) * hb, T // HALO - 1), 0)),
            pl.BlockSpec((1, 1, 6 * D), lambda b, i: (b, 0, 0)),
            pl.BlockSpec((1, D), lambda b, i: (0, 0)),
            pl.BlockSpec((len(POOL_WINDOWS), PG, PG), lambda b, i: (0, 0, 0)),
            pl.BlockSpec((1, D), lambda b, i: (0, 0)),
            pl.BlockSpec((1, D), lambda b, i: (0, 0)),
            pl.BlockSpec((NE, D), lambda b, i: (0, 0)),
            pl.BlockSpec((NE, 1), lambda b, i: (0, 0)),
        ],
        out_specs=[
            pl.BlockSpec((1, TM_POOL, D), lambda b, i: (b, i, 0)),
            pl.BlockSpec((1, TM_POOL, D), lambda b, i: (b, i, 0)),
            pl.BlockSpec((NE, TM_POOL), lambda b, i: (0, b * ni + i)),
        ],
        out_shape=[
            jax.ShapeDtypeStruct((B, T, D), F32),
            jax.ShapeDtypeStruct((B, T, D), BF16),
            jax.ShapeDtypeStruct((NE, B * T), F32),
        ],
        scratch_shapes=[pltpu.VMEM((TM_POOL + 2 * HALO, D), F32)],
        compiler_params=_cparams(("arbitrary", "arbitrary")),
        name="pool_layer",
    )(x, x, x, mod3, nmw, pool_w, pool_scale, nfw, rw_t, rb)


def _moe_kernel(h_ref, comb_ref, wg_ref, wu_ref, wd_ref, x_ref, mod_ref, fw_ref, o_ref, acc_ref, *, final):
    e = pl.program_id(1)

    @pl.when(e == 0)
    def _():
        acc_ref[...] = jnp.zeros_like(acc_ref)

    h = h_ref[...]
    hid = _silu(jnp.dot(h, wg_ref[0], preferred_element_type=F32)) * jnp.dot(h, wu_ref[0], preferred_element_type=F32)
    y = jnp.dot(hid.astype(BF16), wd_ref[0], preferred_element_type=F32)
    lane = lax.broadcasted_iota(jnp.int32, comb_ref.shape, 1)
    c = jnp.sum(jnp.where(lane == e, comb_ref[...], 0.0), axis=1, keepdims=True)
    acc_ref[...] += c * y

    @pl.when(e == NE - 1)
    def _():
        out = x_ref[...] + mod_ref[0][:, 5 * D:6 * D] * acc_ref[...]
        if final:
            out = _rms(out) * fw_ref[...]
        o_ref[...] = out


def _moe(h2, comb, wg, wu, wd, x1, mod3, final_w, final):
    n = B * T
    per_b = T // TM_MOE
    return pl.pallas_call(
        functools.partial(_moe_kernel, final=final),
        grid=(n // TM_MOE, NE),
        in_specs=[
            pl.BlockSpec((TM_MOE, D), lambda i, e: (i, 0)),
            pl.BlockSpec((TM_MOE, NE), lambda i, e: (i, 0)),
            pl.BlockSpec((1, D, DE), lambda i, e: (e, 0, 0)),
            pl.BlockSpec((1, D, DE), lambda i, e: (e, 0, 0)),
            pl.BlockSpec((1, DE, D), lambda i, e: (e, 0, 0)),
            pl.BlockSpec((TM_MOE, D), lambda i, e: (i, 0)),
            pl.BlockSpec((1, 1, 6 * D), lambda i, e: (i // per_b, 0, 0)),
            pl.BlockSpec((1, D), lambda i, e: (0, 0)),
        ],
        out_specs=pl.BlockSpec((TM_MOE, D), lambda i, e: (i, 0)),
        out_shape=jax.ShapeDtypeStruct((n, D), F32),
        scratch_shapes=[pltpu.VMEM((TM_MOE, D), F32)],
        compiler_params=_cparams(("arbitrary", "arbitrary")),
        name="moe",
    )(h2.reshape(n, D), comb, wg, wu, wd, x1.reshape(n, D), mod3, final_w)


def _rope_tables():
    half = 16
    inv = ROPE_BASE ** (-jnp.arange(half, dtype=F32) / half)
    t = jnp.arange(T)
    row = (t // GRID_W).astype(F32)
    col = (t % GRID_W).astype(F32)
    ang_r = row[:, None] * inv[None, :]
    ang_c = col[:, None] * inv[None, :]
    ang = jnp.concatenate([ang_r, ang_r, ang_c, ang_c], axis=1)
    sign = jnp.tile(jnp.concatenate([-jnp.ones(half, F32), jnp.ones(half, F32)]), 2)
    cos = jnp.cos(ang)
    sin = jnp.sin(ang) * sign[None, :]
    cos = jnp.concatenate([jnp.ones((LC, 64), F32), cos], axis=0)
    sin = jnp.concatenate([jnp.zeros((LC, 64), F32), sin], axis=0)
    return jnp.tile(cos, (1, 2)), jnp.tile(sin, (1, 2))


def _permute_w_in(w):
    rq = w[:, 0:256].reshape(D, NH, DK)
    dq = w[:, 256:768]
    rg = w[:, 768:1280]
    rk = w[:, 1280:1536].reshape(D, NH, DK)
    rv = w[:, 1536:2048]
    dk = w[:, 2048:2560]
    dv = w[:, 2560:3072]
    qk = jnp.concatenate([rq, rk * (DK ** -0.5)], axis=2).reshape(D, NH * 2 * DK)
    return jnp.concatenate([qk, rv, rg, dq * (DK ** -0.5), dk, dv], axis=1).astype(BF16)


def kernel(x, c, ctx, c_ctx, ada_w, ada_b, norm_mix_w, norm_ffn_w, w_in, w_out, ret_log_decay, diff_lambda,
           diff_subln_w, pool_w, pool_scale, router_w, router_b, moe_w_gate, moe_w_up, moe_w_down, final_norm_w):
    assert x.shape == (B, T, D) and ctx.shape == (B, LC, D) and ada_w.shape[0] == 2
    cc = jnp.concatenate([c, c_ctx[None, :], jnp.zeros((16 - B - 1, D), F32)], axis=0)
    mod = _ada_mod(cc, ada_w, ada_b)
    rw_t = router_w.T
    rb = router_b.reshape(NE, 1)
    wg = moe_w_gate.astype(BF16)
    wu = moe_w_up.astype(BF16)
    wd = moe_w_down.astype(BF16)

    mod0 = mod[0].reshape(16, 1, 6 * D)
    cos_t, sin_t = _rope_tables()
    proj = _inproj(x, ctx, mod0[:, :, :2 * D], norm_mix_w[0:1], _permute_w_in(w_in[0]), cos_t, sin_t)
    ret = _retention(proj, ret_log_decay[0])
    lam_init = 0.8 - 0.6 * math.exp(-0.3 * 0)
    lv = diff_lambda[0]
    lam = jnp.exp(jnp.sum(lv[0] * lv[1])) - jnp.exp(jnp.sum(lv[2] * lv[3])) + lam_init
    dif = _diffattn(proj, lam.reshape(1), diff_subln_w[0:1], 1.0 - lam_init)
    x1, h2, comb_t = _outproj(ret, dif, w_out[0].astype(BF16), x, mod0, norm_ffn_w[0:1], rw_t, rb)
    x2 = _moe(h2, comb_t.T, wg[0], wu[0], wd[0], x1, mod0, final_norm_w.reshape(1, D), False)

    mod1 = mod[1].reshape(16, 1, 6 * D)
    x3, h3, comb_t = _pool_layer(x2.reshape(B, T, D), mod1, norm_mix_w[1:2], pool_w[0].astype(BF16),
                                 pool_scale[0:1], norm_ffn_w[1:2], rw_t, rb)
    out = _moe(h3, comb_t.T, wg[1], wu[1], wd[1], x3, mod1, final_norm_w.reshape(1, D), True)
    return out.reshape(B, T, D)
```

```python
import functools
import math

import jax
import jax.numpy as jnp
from jax import lax
from jax.experimental import pallas as pl
from jax.experimental.pallas import tpu as pltpu

F32 = jnp.float32
BF16 = jnp.bfloat16
I32 = jnp.int32

D = 1024
B = 8
T = 2048
N = B * T
GRID_W = 64
LC = 256
EPS = 1e-6
ROPE_BASE = 10000.0
NH = 4
DK = 64
HV = 128
CH = 256
RB = LC + T
NCH = RB // CH
POOL_WINDOWS = (2, 4, 8, 16)
PG = D // len(POOL_WINDOWS)
NE = 16
NGRP = 4
EPG = NE // NGRP
DE = 512
IN_W = 3072
HALO = 8

PAIR_A = (0, 0, 0, 1, 1, 3)
PAIR_B = (1, 2, 3, 3, 2, 2)
NCLS = NGRP * len(PAIR_A)
CLS_PAD = 32
XW = D + 128

TM_PROJ = 256
TM_POOL = 512
TM_MOE = 256
N_UNITS = N // TM_MOE + NCLS
PERM_CHUNK = 512
PERM_UNROLL = 8
VMEM_LIMIT = 56 * 1024 * 1024


def _cparams(sem):
    return pltpu.CompilerParams(dimension_semantics=sem, vmem_limit_bytes=VMEM_LIMIT)


def _sigmoid(x):
    return 1.0 / (1.0 + jnp.exp(-x))


def _silu(x):
    return x * _sigmoid(x)


def _rms(x):
    return x * lax.rsqrt(jnp.mean(x * x, axis=-1, keepdims=True) + EPS)


def _ada_kernel(cc_ref, w_ref, b_ref, o_ref):
    s = _silu(cc_ref[...])
    o_ref[0] = jnp.dot(s, w_ref[0], preferred_element_type=F32, precision=lax.Precision.HIGHEST) + b_ref[0]


def _ada_mod(cc, ada_w, ada_b):
    depth = ada_w.shape[0]
    tn = 1536
    return pl.pallas_call(
        _ada_kernel,
        grid=(depth, 6 * D // tn),
        in_specs=[
            pl.BlockSpec((16, D), lambda l, n: (0, 0)),
            pl.BlockSpec((1, D, tn), lambda l, n: (l, 0, n)),
            pl.BlockSpec((1, 1, tn), lambda l, n: (l, 0, n)),
        ],
        out_specs=pl.BlockSpec((1, 16, tn), lambda l, n: (l, 0, n)),
        out_shape=jax.ShapeDtypeStruct((depth, 16, 6 * D), F32),
        compiler_params=_cparams(("arbitrary", "arbitrary")),
        name="ada_mod",
    )(cc, ada_w, ada_b.reshape(depth, 1, 6 * D))


def _rope(seg, cos, sin_signed, lo_mask):
    w = seg.shape[1]
    from_hi = pltpu.roll(seg, w - 16, axis=1)
    from_lo = pltpu.roll(seg, 16, axis=1)
    partner = jnp.where(lo_mask, from_hi, from_lo)
    reps = w // cos.shape[1]
    c = jnp.concatenate([cos] * reps, axis=1)
    s = jnp.concatenate([sin_signed] * reps, axis=1)
    return seg * c + partner * s


def _inproj_kernel(x_ref, c_ref, mod_ref, nw_ref, w_ref, cos_ref, sin_ref, o_ref):
    j = pl.program_id(1)
    xt = jnp.where(j == 0, c_ref[0], x_ref[0])
    sh = mod_ref[0, :, 0:D]
    sc = mod_ref[0, :, D:2 * D]
    h = (_rms(xt) * nw_ref[...]) * (1.0 + sc) + sh
    acc = jnp.dot(h.astype(BF16), w_ref[...], preferred_element_type=F32)
    lane = lax.broadcasted_iota(I32, (TM_PROJ, 512), 1)
    lo_mask = (lane % 32) < 16
    cos = cos_ref[...]
    sin = sin_ref[...]
    for g in range(6):
        seg = acc[:, g * 512:(g + 1) * 512]
        if g in (0, 3, 4):
            seg = _rope(seg, cos, sin, lo_mask)
        o_ref[0, :, g * 512:(g + 1) * 512] = seg.astype(BF16)


def _inproj(x, ctx, mod3, norm_w, w_perm, cos_t, sin_t):
    nj = RB // TM_PROJ
    return pl.pallas_call(
        _inproj_kernel,
        grid=(B, nj),
        in_specs=[
            pl.BlockSpec((1, TM_PROJ, D), lambda b, j: (b, jnp.maximum(j - 1, 0), 0)),
            pl.BlockSpec((1, LC, D), lambda b, j: (b, 0, 0)),
            pl.BlockSpec((1, 1, 2 * D), lambda b, j: (jnp.where(j == 0, B, b), 0, 0)),
            pl.BlockSpec((1, D), lambda b, j: (0, 0)),
            pl.BlockSpec((D, IN_W), lambda b, j: (0, 0)),
            pl.BlockSpec((TM_PROJ, 128), lambda b, j: (j, 0)),
            pl.BlockSpec((TM_PROJ, 128), lambda b, j: (j, 0)),
        ],
        out_specs=pl.BlockSpec((1, TM_PROJ, IN_W), lambda b, j: (b, j, 0)),
        out_shape=jax.ShapeDtypeStruct((B, RB, IN_W), BF16),
        compiler_params=_cparams(("arbitrary", "arbitrary")),
        name="inproj",
    )(x, ctx, mod3, norm_w, w_perm, cos_t, sin_t)


def _retention_kernel(ld_ref, qk_ref, v_ref, g_ref, o_ref, st_ref):
    h = pl.program_id(1)
    lgf = ld_ref[0, h]
    lgb = ld_ref[1, h]
    lane = lax.broadcasted_iota(I32, (CH, 128), 1)
    fwd_lane = lane < DK
    pos = lax.broadcasted_iota(I32, (CH, 128), 0).astype(F32)
    kdec = jnp.where(fwd_lane, jnp.exp(lgf * (CH - 1 - pos)), jnp.exp(lgb * pos))
    qdec = jnp.where(fwd_lane, jnp.exp(lgf * (pos + 1.0)), jnp.exp(lgb * (CH - pos)))
    ii = lax.broadcasted_iota(I32, (CH, CH), 0)
    jj = lax.broadcasted_iota(I32, (CH, CH), 1)
    gap = (ii - jj).astype(F32)
    mask = (jnp.where(gap >= 0, jnp.exp(lgf * jnp.maximum(gap, 0.0)), 0.0)
            + jnp.where(gap <= 0, jnp.exp(lgb * jnp.maximum(-gap, 0.0)), 0.0))
    ones = jnp.ones((DK, 128), F32)
    cf = jnp.exp(lgf * CH * ones)
    cb = jnp.exp(lgb * CH * ones)

    def chunk(n):
        a = qk_ref[0, n * CH:(n + 1) * CH, :].astype(F32)
        swapped = pltpu.roll(a, DK, axis=1)
        return a, swapped

    kv = []
    for n in range(NCH):
        a, swapped = chunk(n)
        kk = jnp.where(fwd_lane, swapped, a)
        kb = (kk * kdec).astype(BF16)
        vn = v_ref[0, n * CH:(n + 1) * CH, :]
        kv.append(lax.dot_general(kb, vn, (((0,), (0,)), ((), ())), preferred_element_type=F32))
    sf = kv[0][:DK]
    for n in range(1, NCH):
        st_ref[n, 0:DK, :] = sf
        sf = cf * sf + kv[n][:DK]
    sb = kv[0][DK:]
    for n in range(NCH - 1, 0, -1):
        st_ref[n, DK:2 * DK, :] = sb
        sb = cb * sb + kv[n][DK:]

    for n in range(1, NCH):
        a, swapped = chunk(n)
        q = a[:, :DK].astype(BF16)
        k = swapped[:, :DK].astype(BF16)
        scores = lax.dot_general(q, k, (((1,), (1,)), ((), ())), preferred_element_type=F32)
        p = (scores * mask).astype(BF16)
        vn = v_ref[0, n * CH:(n + 1) * CH, :]
        qq = jnp.where(fwd_lane, a, swapped)
        qd = (qq * qdec).astype(BF16)
        o = (jnp.dot(p, vn, preferred_element_type=F32)
             + jnp.dot(qd, st_ref[n].astype(BF16), preferred_element_type=F32))
        gate = g_ref[0, n * CH:(n + 1) * CH, :].astype(F32)
        o_ref[0, (n - 1) * CH:n * CH, :] = (_rms(o) * _silu(gate)).astype(BF16)


def _retention(proj, log_decay):
    return pl.pallas_call(
        _retention_kernel,
        grid=(B, NH),
        in_specs=[
            pl.BlockSpec(memory_space=pltpu.SMEM),
            pl.BlockSpec((1, RB, 128), lambda b, h: (b, 0, h)),
            pl.BlockSpec((1, RB, 128), lambda b, h: (b, 0, NH + h)),
            pl.BlockSpec((1, RB, 128), lambda b, h: (b, 0, 2 * NH + h)),
        ],
        out_specs=pl.BlockSpec((1, T, 128), lambda b, h: (b, 0, h)),
        out_shape=jax.ShapeDtypeStruct((B, T, NH * HV), BF16),
        scratch_shapes=[pltpu.VMEM((NCH, 128, 128), F32)],
        compiler_params=_cparams(("arbitrary", "arbitrary")),
        name="retention",
    )(log_decay, proj, proj, proj)


def _softmax(s):
    m = jnp.max(s, axis=-1, keepdims=True)
    e = jnp.exp(s - m)
    return e / jnp.sum(e, axis=-1, keepdims=True)


def _diffattn_kernel(lam_ref, q_ref, k_ref, v_ref, sw_ref, o_ref, *, out_scale):
    lam = lam_ref[0]
    q = q_ref[0]
    k = k_ref[0]
    lane = lax.broadcasted_iota(I32, q.shape, 1)
    zero = jnp.zeros_like(q)
    q1 = jnp.where(lane < DK, q, zero)
    q2 = jnp.where(lane >= DK, q, zero)
    nt = (((1,), (1,)), ((), ()))
    p1 = _softmax(lax.dot_general(q1, k, nt, preferred_element_type=F32))
    p2 = _softmax(lax.dot_general(q2, k, nt, preferred_element_type=F32))
    a = (p1 - lam * p2).astype(BF16)
    o = jnp.dot(a, v_ref[0], preferred_element_type=F32)
    o_ref[0] = (_rms(o) * sw_ref[...] * out_scale).astype(BF16)


def _diffattn(proj, lam, subln_w, out_scale):
    nq = T // TM_PROJ
    off = LC // TM_PROJ
    return pl.pallas_call(
        functools.partial(_diffattn_kernel, out_scale=out_scale),
        grid=(B, NH, nq),
        in_specs=[
            pl.BlockSpec(memory_space=pltpu.SMEM),
            pl.BlockSpec((1, TM_PROJ, 128), lambda b, h, j: (b, j + off, 3 * NH + h)),
            pl.BlockSpec((1, RB, 128), lambda b, h, j: (b, 0, 4 * NH + h)),
            pl.BlockSpec((1, RB, 128), lambda b, h, j: (b, 0, 5 * NH + h)),
            pl.BlockSpec((1, HV), lambda b, h, j: (0, 0)),
        ],
        out_specs=pl.BlockSpec((1, TM_PROJ, 128), lambda b, h, j: (b, j, h)),
        out_shape=jax.ShapeDtypeStruct((B, T, NH * HV), BF16),
        compiler_params=_cparams(("arbitrary", "arbitrary", "arbitrary")),
        name="diffattn",
    )(lam, proj, proj, proj, subln_w)


def _route(logits_t, bias):
    s = [_sigmoid(logits_t[e:e + 1, :]) for e in range(NE)]
    bz = [s[e] + bias[e:e + 1, :] for e in range(NE)]
    grp = []
    for g in range(NGRP):
        m = bz[g * EPG:(g + 1) * EPG]
        best = None
        for i in range(EPG):
            for k in range(i + 1, EPG):
                pair = m[i] + m[k]
                best = pair if best is None else jnp.maximum(best, pair)
        grp.append(best)
    gbest = grp[0]
    gsel = jnp.zeros_like(gbest, dtype=I32)
    for g in range(1, NGRP):
        better = grp[g] > gbest
        gsel = jnp.where(better, g, gsel)
        gbest = jnp.where(better, grp[g], gbest)
    cb = [bz[i] for i in range(EPG)]
    cs = [s[i] for i in range(EPG)]
    for g in range(1, NGRP):
        pick = gsel == g
        cb = [jnp.where(pick, bz[g * EPG + i], cb[i]) for i in range(EPG)]
        cs = [jnp.where(pick, s[g * EPG + i], cs[i]) for i in range(EPG)]
    i1 = jnp.zeros_like(gsel)
    b1 = cb[0]
    for i in range(1, EPG):
        better = cb[i] > b1
        i1 = jnp.where(better, i, i1)
        b1 = jnp.where(better, cb[i], b1)
    neg = jnp.full_like(b1, -jnp.inf)
    rest = [jnp.where(i1 == i, neg, cb[i]) for i in range(EPG)]
    i2 = jnp.zeros_like(gsel)
    b2 = rest[0]
    for i in range(1, EPG):
        better = rest[i] > b2
        i2 = jnp.where(better, i, i2)
        b2 = jnp.where(better, rest[i], b2)
    lo = jnp.minimum(i1, i2)
    hi = jnp.maximum(i1, i2)
    code = lo * EPG + hi
    pair = jnp.full_like(gsel, len(PAIR_A) - 1)
    for p in range(len(PAIR_A) - 1):
        a, b = min(PAIR_A[p], PAIR_B[p]), max(PAIR_A[p], PAIR_B[p])
        pair = jnp.where(code == a * EPG + b, p, pair)
    s1 = cs[0]
    s2 = cs[0]
    for i in range(1, EPG):
        s1 = jnp.where(i1 == i, cs[i], s1)
        s2 = jnp.where(i2 == i, cs[i], s2)
    denom = s1 + s2
    wn = [cs[i] / denom for i in range(EPG)]
    w_a = wn[PAIR_A[-1]]
    w_b = wn[PAIR_B[-1]]
    for p in range(len(PAIR_A) - 1):
        w_a = jnp.where(pair == p, wn[PAIR_A[p]], w_a)
        w_b = jnp.where(pair == p, wn[PAIR_B[p]], w_b)
    return gsel * len(PAIR_A) + pair, w_a, w_b


def _ffn_prologue(first, x1, mod, nfw_ref, rw_ref, rb_ref, hp_ref, route_ref, cnt_ref, run_ref):
    rows = x1.shape[0]
    sh2 = mod[:, 3 * D:4 * D]
    sc2 = mod[:, 4 * D:5 * D]
    h2 = (_rms(x1) * nfw_ref[...]) * (1.0 + sc2) + sh2
    logits_t = lax.dot_general(rw_ref[...], h2, (((1,), (1,)), ((), ())),
                               preferred_element_type=F32, precision=lax.Precision.HIGHEST)
    cls, w_a, w_b = _route(logits_t, rb_ref[...])

    srow = jnp.concatenate([w_a, w_b, jnp.zeros((6, rows), F32)], axis=0)
    eye = (lax.broadcasted_iota(I32, (8, 128), 0) == lax.broadcasted_iota(I32, (8, 128), 1)).astype(F32)
    cols = lax.dot_general(srow, eye, (((0,), (0,)), ((), ())),
                           preferred_element_type=F32, precision=lax.Precision.HIGHEST)
    hp_ref[0, :, 0:D] = h2
    hp_ref[0, :, D:XW] = cols

    @pl.when(first)
    def _():
        run_ref[...] = jnp.zeros_like(run_ref)

    onehot = (lax.broadcasted_iota(I32, (CLS_PAD, rows), 0) == cls).astype(F32)
    tri = (lax.broadcasted_iota(I32, (rows, rows), 0) <= lax.broadcasted_iota(I32, (rows, rows), 1)).astype(BF16)
    prefix = jnp.dot(onehot.astype(BF16), tri, preferred_element_type=F32)
    run = run_ref[...]
    rank = jnp.sum(onehot * (prefix - 1.0 + run[:, 0:1]), axis=0, keepdims=True)
    run = run + jnp.sum(onehot, axis=1, keepdims=True)
    run_ref[...] = run
    cnt_ref[...] = run
    route_ref[...] = jnp.concatenate([cls.astype(F32), rank, jnp.zeros((6, rows), F32)], axis=0)


def _ffn_out_specs(tm, n_tiles_per_b):
    specs = [
        pl.BlockSpec((1, tm, D), lambda b, j: (b, j, 0)),
        pl.BlockSpec((1, tm, XW), lambda b, j: (b, j, 0)),
        pl.BlockSpec((8, tm), lambda b, j: (0, b * n_tiles_per_b + j)),
        pl.BlockSpec((CLS_PAD, 128), lambda b, j: (0, 0)),
    ]
    shapes = [
        jax.ShapeDtypeStruct((B, T, D), F32),
        jax.ShapeDtypeStruct((B, T, XW), F32),
        jax.ShapeDtypeStruct((8, N), F32),
        jax.ShapeDtypeStruct((CLS_PAD, 128), F32),
    ]
    return specs, shapes


def _outproj_kernel(ret_ref, dif_ref, w_ref, x_ref, mod_ref, nfw_ref, rw_ref, rb_ref,
                    x1_ref, hp_ref, route_ref, cnt_ref, run_ref):
    first = (pl.program_id(0) == 0) & (pl.program_id(1) == 0)
    mod = mod_ref[0]
    mx = (jnp.dot(ret_ref[0], w_ref[0:NH * HV, :], preferred_element_type=F32)
          + jnp.dot(dif_ref[0], w_ref[NH * HV:, :], preferred_element_type=F32))
    x1 = x_ref[0] + mod[:, 2 * D:3 * D] * mx
    x1_ref[0] = x1
    _ffn_prologue(first, x1, mod, nfw_ref, rw_ref, rb_ref, hp_ref, route_ref, cnt_ref, run_ref)


def _outproj(ret, dif, w_out, x, mod3, nfw, rw_t, rb):
    nj = T // TM_PROJ
    out_specs, out_shapes = _ffn_out_specs(TM_PROJ, nj)
    return pl.pallas_call(
        _outproj_kernel,
        grid=(B, nj),
        in_specs=[
            pl.BlockSpec((1, TM_PROJ, NH * HV), lambda b, j: (b, j, 0)),
            pl.BlockSpec((1, TM_PROJ, NH * HV), lambda b, j: (b, j, 0)),
            pl.BlockSpec((2 * NH * HV, D), lambda b, j: (0, 0)),
            pl.BlockSpec((1, TM_PROJ, D), lambda b, j: (b, j, 0)),
            pl.BlockSpec((1, 1, 6 * D), lambda b, j: (b, 0, 0)),
            pl.BlockSpec((1, D), lambda b, j: (0, 0)),
            pl.BlockSpec((NE, D), lambda b, j: (0, 0)),
            pl.BlockSpec((NE, 1), lambda b, j: (0, 0)),
        ],
        out_specs=out_specs,
        out_shape=out_shapes,
        scratch_shapes=[pltpu.VMEM((CLS_PAD, 128), F32)],
        compiler_params=_cparams(("arbitrary", "arbitrary")),
        name="outproj",
    )(ret, dif, w_out, x, mod3, nfw, rw_t, rb)


def _pool_kernel(x_ref, prev_ref, next_ref, mod_ref, nmw_ref, pw_ref, ps_ref, nfw_ref, rw_ref, rb_ref,
                 x1_ref, hp_ref, route_ref, cnt_ref, ext_ref, run_ref):
    i = pl.program_id(1)
    first = (pl.program_id(0) == 0) & (i == 0)
    last = pl.num_programs(1) - 1
    mod = mod_ref[0]
    sh1 = mod[:, 0:D]
    sc1 = mod[:, D:2 * D]

    def modnorm(v):
        return (_rms(v) * nmw_ref[...]) * (1.0 + sc1) + sh1

    x = x_ref[0]
    hc = modnorm(x)
    ext_ref[0:HALO, :] = jnp.where(i > 0, modnorm(prev_ref[0]), 0.0)
    ext_ref[HALO:HALO + TM_POOL, :] = hc
    ext_ref[HALO + TM_POOL:, :] = jnp.where(i < last, modnorm(next_ref[0]), 0.0)
    pos = i * TM_POOL + lax.broadcasted_iota(I32, (TM_POOL, 1), 0)
    mixed = []
    for gi, w in enumerate(POOL_WINDOWS):
        left = w // 2
        right = w - 1 - left
        cols = slice(gi * PG, (gi + 1) * PG)
        tot = None
        for d in range(-left, right + 1):
            part = ext_ref[HALO + d:HALO + d + TM_POOL, cols]
            tot = part if tot is None else tot + part
        cnt = (jnp.minimum(pos + right + 1, T) - jnp.maximum(pos - left, 0)).astype(F32)
        pooled = (tot / cnt - hc[:, cols]).astype(BF16)
        mixed.append(jnp.dot(pooled, pw_ref[gi], preferred_element_type=F32))
    mixed = jnp.concatenate(mixed, axis=1) * ps_ref[...]
    x1 = x + mod[:, 2 * D:3 * D] * mixed
    x1_ref[0] = x1
    _ffn_prologue(first, x1, mod, nfw_ref, rw_ref, rb_ref, hp_ref, route_ref, cnt_ref, run_ref)


def _pool_layer(x, mod3, nmw, pool_w, pool_scale, nfw, rw_t, rb):
    ni = T // TM_POOL
    hb = TM_POOL // HALO
    out_specs, out_shapes = _ffn_out_specs(TM_POOL, ni)
    return pl.pallas_call(
        _pool_kernel,
        grid=(B, ni),
        in_specs=[
            pl.BlockSpec((1, TM_POOL, D), lambda b, i: (b, i, 0)),
            pl.BlockSpec((1, HALO, D), lambda b, i: (b, jnp.maximum(i * hb - 1, 0), 0)),
            pl.BlockSpec((1, HALO, D), lambda b, i: (b, jnp.minimum((i + 1) * hb, T // HALO - 1), 0)),
            pl.BlockSpec((1, 1, 6 * D), lambda b, i: (b, 0, 0)),
            pl.BlockSpec((1, D), lambda b, i: (0, 0)),
            pl.BlockSpec((len(POOL_WINDOWS), PG, PG), lambda b, i: (0, 0, 0)),
            pl.BlockSpec((1, D), lambda b, i: (0, 0)),
            pl.BlockSpec((1, D), lambda b, i: (0, 0)),
            pl.BlockSpec((NE, D), lambda b, i: (0, 0)),
            pl.BlockSpec((NE, 1), lambda b, i: (0, 0)),
        ],
        out_specs=out_specs,
        out_shape=out_shapes,
        scratch_shapes=[pltpu.VMEM((TM_POOL + 2 * HALO, D), F32), pltpu.VMEM((CLS_PAD, 128), F32)],
        compiler_params=_cparams(("arbitrary", "arbitrary")),
        name="pool_layer",
    )(x, x, x, mod3, nmw, pool_w, pool_scale, nfw, rw_t, rb)


def _row_copy(src_ref, dst_ref, sem, s, d, rows):
    return pltpu.make_async_copy(src_ref.at[pl.ds(s, rows)], dst_ref.at[pl.ds(d, rows)], sem)


def _permute_kernel(idx_ref, src_ref, dst_ref, sem, *, scatter):
    n_chunks = N // PERM_CHUNK

    def issue_chunk(c, carry):
        def issue(i, carry):
            base = c * PERM_CHUNK + i * PERM_UNROLL
            for u in range(PERM_UNROLL):
                t = base + u
                r = idx_ref[t]
                s, d = (t, r) if scatter else (r, t)
                _row_copy(src_ref, dst_ref, sem, s, d, 1).start()
            return carry

        lax.fori_loop(0, PERM_CHUNK // PERM_UNROLL, issue, 0)

        @pl.when(c >= 2)
        def _():
            _row_copy(src_ref, dst_ref, sem, 0, 0, PERM_CHUNK).wait()

        return carry

    lax.fori_loop(0, n_chunks, issue_chunk, 0)
    for _ in range(2):
        _row_copy(src_ref, dst_ref, sem, 0, 0, PERM_CHUNK).wait()


def _permute_rows(idx, src, scatter):
    return pl.pallas_call(
        functools.partial(_permute_kernel, scatter=scatter),
        grid_spec=pltpu.PrefetchScalarGridSpec(
            num_scalar_prefetch=1,
            grid=(1,),
            in_specs=[pl.BlockSpec(memory_space=pl.ANY)],
            out_specs=pl.BlockSpec(memory_space=pl.ANY),
            scratch_shapes=[pltpu.SemaphoreType.DMA(())],
        ),
        out_shape=jax.ShapeDtypeStruct(src.shape, src.dtype),
        compiler_params=_cparams(("arbitrary",)),
        name="scatter_rows" if scatter else "gather_rows",
    )(idx, src)


def _moe_kernel(tile_ref, lo_ref, hi_ref, ea_ref, eb_ref, ca_ref, cb_ref,
                x_ref, wga_ref, wua_ref, wda_ref, wgb_ref, wub_ref, wdb_ref, o_ref,
                ga_ref, ua_ref, da_ref, gb_ref, ub_ref, db_ref):
    m = pl.program_id(0)
    lo = lo_ref[m]
    hi = hi_ref[m]
    tile0 = tile_ref[m] * TM_MOE

    @pl.when(ca_ref[m] == 1)
    def _():
        ga_ref[...] = wga_ref[0, 0].astype(BF16)
        ua_ref[...] = wua_ref[0, 0].astype(BF16)
        da_ref[...] = wda_ref[0, 0].astype(BF16)

    @pl.when(cb_ref[m] == 1)
    def _():
        gb_ref[...] = wgb_ref[0, 0].astype(BF16)
        ub_ref[...] = wub_ref[0, 0].astype(BF16)
        db_ref[...] = wdb_ref[0, 0].astype(BF16)

    def value():
        h = x_ref[:, 0:D].astype(BF16)

        def ffn(g_ref, u_ref, d_ref):
            hid = (_silu(jnp.dot(h, g_ref[...], preferred_element_type=F32))
                   * jnp.dot(h, u_ref[...], preferred_element_type=F32))
            return jnp.dot(hid.astype(BF16), d_ref[...], preferred_element_type=F32)

        val = x_ref[:, D:D + 1] * ffn(ga_ref, ua_ref, da_ref) + x_ref[:, D + 1:D + 2] * ffn(gb_ref, ub_ref, db_ref)
        row = tile0 + lax.broadcasted_iota(I32, (TM_MOE, 1), 0)
        return val, (row >= lo) & (row < hi)

    @pl.when((hi > lo) & (lo == tile0))
    def _():
        val, seg = value()
        o_ref[...] = jnp.where(seg, val, 0.0)

    @pl.when((hi > lo) & (lo != tile0))
    def _():
        val, seg = value()
        o_ref[...] = jnp.where(seg, val, o_ref[...])


def _moe_sorted(units, xs, wg, wu, wd, layer):
    def wspec(which, shape):
        return pl.BlockSpec((1, 1) + shape,
                            lambda m, t, lo, hi, ea, eb, ca, cb: (layer, (ea, eb)[which][m], 0, 0))

    row_map = lambda m, t, lo, hi, ea, eb, ca, cb: (t[m], 0)
    return pl.pallas_call(
        _moe_kernel,
        grid_spec=pltpu.PrefetchScalarGridSpec(
            num_scalar_prefetch=7,
            grid=(N_UNITS,),
            in_specs=[
                pl.BlockSpec((TM_MOE, XW), row_map),
                wspec(0, (D, DE)), wspec(0, (D, DE)), wspec(0, (DE, D)),
                wspec(1, (D, DE)), wspec(1, (D, DE)), wspec(1, (DE, D)),
            ],
            out_specs=pl.BlockSpec((TM_MOE, D), row_map),
            scratch_shapes=[pltpu.VMEM((D, DE), BF16), pltpu.VMEM((D, DE), BF16), pltpu.VMEM((DE, D), BF16),
                            pltpu.VMEM((D, DE), BF16), pltpu.VMEM((D, DE), BF16), pltpu.VMEM((DE, D), BF16)],
        ),
        out_shape=jax.ShapeDtypeStruct((N, D), F32),
        compiler_params=_cparams(("arbitrary",)),
        name="moe_sorted",
    )(*units, xs, wg, wu, wd, wg, wu, wd)


def _moe_units(route, counts):
    cls = route[0].astype(I32)
    rank = route[1].astype(I32)
    cnt = counts[:NCLS, 0].astype(I32)
    ends = jnp.cumsum(cnt)
    offs = ends - cnt
    dest = jnp.sum(jnp.where(cls[:, None] == jnp.arange(NCLS)[None, :], offs[None, :], 0), axis=1) + rank
    starts = jnp.sort(jnp.concatenate([jnp.arange(N // TM_MOE, dtype=I32) * TM_MOE, offs]))
    u_lo = starts
    u_hi = jnp.concatenate([starts[1:], jnp.full((1,), N, I32)])
    u_tile = jnp.minimum(u_lo // TM_MOE, N // TM_MOE - 1)
    u_cls = jnp.minimum(jnp.sum(ends[None, :] <= u_lo[:, None], axis=1), NCLS - 1).astype(I32)
    grp = u_cls // len(PAIR_A)
    pair = u_cls % len(PAIR_A)
    e_a = grp * EPG + jnp.take(jnp.array(PAIR_A, I32), pair)
    e_b = grp * EPG + jnp.take(jnp.array(PAIR_B, I32), pair)
    one = jnp.ones((1,), I32)
    chg_a = jnp.concatenate([one, (e_a[1:] != e_a[:-1]).astype(I32)])
    chg_b = jnp.concatenate([one, (e_b[1:] != e_b[:-1]).astype(I32)])
    return dest, (u_tile, u_lo, u_hi, e_a, e_b, chg_a, chg_b)


def _residual_kernel(x_ref, y_ref, mod_ref, fw_ref, o_ref, *, final):
    out = x_ref[0] + mod_ref[0][:, 5 * D:6 * D] * y_ref[0]
    if final:
        out = _rms(out) * fw_ref[...]
    o_ref[0] = out


def _residual(x1, y, mod3, final_w, final):
    tm = 512
    return pl.pallas_call(
        functools.partial(_residual_kernel, final=final),
        grid=(B, T // tm),
        in_specs=[
            pl.BlockSpec((1, tm, D), lambda b, j: (b, j, 0)),
            pl.BlockSpec((1, tm, D), lambda b, j: (b, j, 0)),
            pl.BlockSpec((1, 1, 6 * D), lambda b, j: (b, 0, 0)),
            pl.BlockSpec((1, D), lambda b, j: (0, 0)),
        ],
        out_specs=pl.BlockSpec((1, tm, D), lambda b, j: (b, j, 0)),
        out_shape=jax.ShapeDtypeStruct((B, T, D), F32),
        compiler_params=_cparams(("arbitrary", "arbitrary")),
        name="moe_residual",
    )(x1, y, mod3, final_w)


def _moe_layer(x1, hp, route, counts, wg, wu, wd, layer, mod3, final_w, final):
    dest, units = _moe_units(route, counts)
    xs = _permute_rows(dest, hp.reshape(N, XW), scatter=True)
    ys = _moe_sorted(units, xs, wg, wu, wd, layer)
    y = _permute_rows(dest, ys, scatter=False)
    return _residual(x1, y.reshape(B, T, D), mod3, final_w, final)


def _rope_tables():
    half = 16
    inv = ROPE_BASE ** (-jnp.arange(half, dtype=F32) / half)
    t = jnp.arange(T)
    row = (t // GRID_W).astype(F32)
    col = (t % GRID_W).astype(F32)
    ang_r = row[:, None] * inv[None, :]
    ang_c = col[:, None] * inv[None, :]
    ang = jnp.concatenate([ang_r, ang_r, ang_c, ang_c], axis=1)
    sign = jnp.tile(jnp.concatenate([-jnp.ones(half, F32), jnp.ones(half, F32)]), 2)
    cos = jnp.cos(ang)
    sin = jnp.sin(ang) * sign[None, :]
    cos = jnp.concatenate([jnp.ones((LC, 64), F32), cos], axis=0)
    sin = jnp.concatenate([jnp.zeros((LC, 64), F32), sin], axis=0)
    return jnp.tile(cos, (1, 2)), jnp.tile(sin, (1, 2))


def _permute_w_in(w):
    rq = w[:, 0:256].reshape(D, NH, DK)
    dq = w[:, 256:768]
    rg = w[:, 768:1280]
    rk = w[:, 1280:1536].reshape(D, NH, DK)
    rv = w[:, 1536:2048]
    dk = w[:, 2048:2560]
    dv = w[:, 2560:3072]
    qk = jnp.concatenate([rq, rk * (DK ** -0.5)], axis=2).reshape(D, NH * 2 * DK)
    return jnp.concatenate([qk, rv, rg, dq * (DK ** -0.5), dk, dv], axis=1).astype(BF16)


def kernel(x, c, ctx, c_ctx, ada_w, ada_b, norm_mix_w, norm_ffn_w, w_in, w_out, ret_log_decay, diff_lambda,
           diff_subln_w, pool_w, pool_scale, router_w, router_b, moe_w_gate, moe_w_up, moe_w_down, final_norm_w):
    assert x.shape == (B, T, D) and ctx.shape == (B, LC, D) and ada_w.shape[0] == 2
    cc = jnp.concatenate([c, c_ctx[None, :], jnp.zeros((16 - B - 1, D), F32)], axis=0)
    mod = _ada_mod(cc, ada_w, ada_b)
    rw_t = router_w.T
    rb = router_b.reshape(NE, 1)
    fw = final_norm_w.reshape(1, D)

    mod0 = mod[0].reshape(16, 1, 6 * D)
    cos_t, sin_t = _rope_tables()
    proj = _inproj(x, ctx, mod0[:, :, :2 * D], norm_mix_w[0:1], _permute_w_in(w_in[0]), cos_t, sin_t)
    ret = _retention(proj, ret_log_decay[0])
    lam_init = 0.8 - 0.6 * math.exp(-0.3 * 0)
    lv = diff_lambda[0]
    lam = jnp.exp(jnp.sum(lv[0] * lv[1])) - jnp.exp(jnp.sum(lv[2] * lv[3])) + lam_init
    dif = _diffattn(proj, lam.reshape(1), diff_subln_w[0:1], 1.0 - lam_init)
    x1, hp, route, counts = _outproj(ret, dif, w_out[0].astype(BF16), x, mod0, norm_ffn_w[0:1], rw_t, rb)
    x2 = _moe_layer(x1, hp, route, counts, moe_w_gate, moe_w_up, moe_w_down, 0, mod0, fw, False)

    mod1 = mod[1].reshape(16, 1, 6 * D)
    x3, hp, route, counts = _pool_layer(x2, mod1, norm_mix_w[1:2], pool_w[0].astype(BF16),
                                        pool_scale[0:1], norm_ffn_w[1:2], rw_t, rb)
    return _moe_layer(x3, hp, route, counts, moe_w_gate, moe_w_up, moe_w_down, 1, mod1, fw, True)
```

```python
import functools
import math

import jax
import jax.numpy as jnp
from jax import lax
from jax.experimental import pallas as pl
from jax.experimental.pallas import tpu as pltpu

F32 = jnp.float32
BF16 = jnp.bfloat16
I32 = jnp.int32

D = 1024
B = 8
T = 2048
N = B * T
GRID_W = 64
LC = 256
EPS = 1e-6
ROPE_BASE = 10000.0
NH = 4
DK = 64
HV = 128
CH = 256
RB = LC + T
NCH = RB // CH
POOL_WINDOWS = (2, 4, 8, 16)
PG = D // len(POOL_WINDOWS)
NE = 16
NGRP = 4
EPG = NE // NGRP
DE = 512
IN_W = 3072
HALO = 8

PAIR_A = (0, 0, 0, 1, 1, 3)
PAIR_B = (1, 2, 3, 3, 2, 2)
NCLS = NGRP * len(PAIR_A)
CLS_PAD = 32
SLAB = D // 128

TM_PROJ = 256
TM_OUT = 512
TQ_SUB = 4
TM_POOL = 512
TM_MOE = 256
N_UNITS = N // TM_MOE + NCLS
PERM_CHUNK = 512
PERM_UNROLL = 8
VMEM_LIMIT = 56 * 1024 * 1024


def _cparams(sem):
    return pltpu.CompilerParams(dimension_semantics=sem, vmem_limit_bytes=VMEM_LIMIT)


def _sigmoid(x):
    return 1.0 / (1.0 + jnp.exp(-x))


def _silu(x):
    return x * _sigmoid(x)


def _rms(x):
    return x * lax.rsqrt(jnp.mean(x * x, axis=-1, keepdims=True) + EPS)


def _load_slabs(ref, rows):
    return jnp.concatenate([ref[pl.ds(s, rows, stride=SLAB), :] for s in range(SLAB)], axis=1)


def _store_slabs(ref, val):
    rows = val.shape[0]
    for s in range(SLAB):
        ref[pl.ds(s, rows, stride=SLAB), :] = val[:, s * 128:(s + 1) * 128]


def _ada_kernel(cc_ref, w_ref, b_ref, o_ref):
    s = _silu(cc_ref[...])
    o_ref[0] = jnp.dot(s, w_ref[0], preferred_element_type=F32, precision=lax.Precision.HIGHEST) + b_ref[0]


def _ada_mod(cc, ada_w, ada_b):
    depth = ada_w.shape[0]
    tn = 1536
    return pl.pallas_call(
        _ada_kernel,
        grid=(depth, 6 * D // tn),
        in_specs=[
            pl.BlockSpec((16, D), lambda l, n: (0, 0)),
            pl.BlockSpec((1, D, tn), lambda l, n: (l, 0, n)),
            pl.BlockSpec((1, 1, tn), lambda l, n: (l, 0, n)),
        ],
        out_specs=pl.BlockSpec((1, 16, tn), lambda l, n: (l, 0, n)),
        out_shape=jax.ShapeDtypeStruct((depth, 16, 6 * D), F32),
        compiler_params=_cparams(("arbitrary", "arbitrary")),
        name="ada_mod",
    )(cc, ada_w, ada_b.reshape(depth, 1, 6 * D))


def _rope(seg, cos, sin_signed, lo_mask):
    w = seg.shape[1]
    from_hi = pltpu.roll(seg, w - 16, axis=1)
    from_lo = pltpu.roll(seg, 16, axis=1)
    partner = jnp.where(lo_mask, from_hi, from_lo)
    reps = w // cos.shape[1]
    c = jnp.concatenate([cos] * reps, axis=1)
    s = jnp.concatenate([sin_signed] * reps, axis=1)
    return seg * c + partner * s


def _inproj_kernel(x_ref, c_ref, mod_ref, nw_ref, w_ref, cos_ref, sin_ref, o_ref):
    j = pl.program_id(1)
    xt = jnp.where(j == 0, c_ref[0], x_ref[0])
    sh = mod_ref[0, :, 0:D]
    sc = mod_ref[0, :, D:2 * D]
    h = (_rms(xt) * nw_ref[...]) * (1.0 + sc) + sh
    acc = jnp.dot(h.astype(BF16), w_ref[...], preferred_element_type=F32)
    lane = lax.broadcasted_iota(I32, (TM_PROJ, 512), 1)
    lo_mask = (lane % 32) < 16
    cos = cos_ref[...]
    sin = sin_ref[...]
    for g in range(6):
        seg = acc[:, g * 512:(g + 1) * 512]
        if g in (0, 3, 4):
            seg = _rope(seg, cos, sin, lo_mask)
        o_ref[0, :, g * 512:(g + 1) * 512] = seg.astype(BF16)


def _inproj(x, ctx, mod3, norm_w, w_perm, cos_t, sin_t):
    nj = RB // TM_PROJ
    return pl.pallas_call(
        _inproj_kernel,
        grid=(B, nj),
        in_specs=[
            pl.BlockSpec((1, TM_PROJ, D), lambda b, j: (b, jnp.maximum(j - 1, 0), 0)),
            pl.BlockSpec((1, LC, D), lambda b, j: (b, 0, 0)),
            pl.BlockSpec((1, 1, 2 * D), lambda b, j: (jnp.where(j == 0, B, b), 0, 0)),
            pl.BlockSpec((1, D), lambda b, j: (0, 0)),
            pl.BlockSpec((D, IN_W), lambda b, j: (0, 0)),
            pl.BlockSpec((TM_PROJ, 128), lambda b, j: (j, 0)),
            pl.BlockSpec((TM_PROJ, 128), lambda b, j: (j, 0)),
        ],
        out_specs=pl.BlockSpec((1, TM_PROJ, IN_W), lambda b, j: (b, j, 0)),
        out_shape=jax.ShapeDtypeStruct((B, RB, IN_W), BF16),
        compiler_params=_cparams(("arbitrary", "arbitrary")),
        name="inproj",
    )(x, ctx, mod3, norm_w, w_perm, cos_t, sin_t)


def _retention_kernel(ld_ref, qk_ref, v_ref, g_ref, o_ref, st_ref):
    h = pl.program_id(1)
    lgf = ld_ref[0, h]
    lgb = ld_ref[1, h]
    lane = lax.broadcasted_iota(I32, (CH, 128), 1)
    fwd_lane = lane < DK
    pos = lax.broadcasted_iota(I32, (CH, 128), 0).astype(F32)
    kdec = jnp.where(fwd_lane, jnp.exp(lgf * (CH - 1 - pos)), jnp.exp(lgb * pos))
    qdec = jnp.where(fwd_lane, jnp.exp(lgf * (pos + 1.0)), jnp.exp(lgb * (CH - pos)))
    ii = lax.broadcasted_iota(I32, (CH, CH), 0)
    jj = lax.broadcasted_iota(I32, (CH, CH), 1)
    gap = (ii - jj).astype(F32)
    mask = (jnp.where(gap >= 0, jnp.exp(lgf * jnp.maximum(gap, 0.0)), 0.0)
            + jnp.where(gap <= 0, jnp.exp(lgb * jnp.maximum(-gap, 0.0)), 0.0))
    ones = jnp.ones((DK, 128), F32)
    cf = jnp.exp(lgf * CH * ones)
    cb = jnp.exp(lgb * CH * ones)

    def chunk(n):
        a = qk_ref[0, n * CH:(n + 1) * CH, :].astype(F32)
        swapped = pltpu.roll(a, DK, axis=1)
        return a, swapped

    kv = []
    for n in range(NCH):
        a, swapped = chunk(n)
        kk = jnp.where(fwd_lane, swapped, a)
        kb = (kk * kdec).astype(BF16)
        vn = v_ref[0, n * CH:(n + 1) * CH, :]
        kv.append(lax.dot_general(kb, vn, (((0,), (0,)), ((), ())), preferred_element_type=F32))
    sf = kv[0][:DK]
    for n in range(1, NCH):
        st_ref[n, 0:DK, :] = sf
        sf = cf * sf + kv[n][:DK]
    sb = kv[0][DK:]
    for n in range(NCH - 1, 0, -1):
        st_ref[n, DK:2 * DK, :] = sb
        sb = cb * sb + kv[n][DK:]

    for n in range(1, NCH):
        a, swapped = chunk(n)
        q = a[:, :DK].astype(BF16)
        k = swapped[:, :DK].astype(BF16)
        scores = lax.dot_general(q, k, (((1,), (1,)), ((), ())), preferred_element_type=F32)
        p = (scores * mask).astype(BF16)
        vn = v_ref[0, n * CH:(n + 1) * CH, :]
        qq = jnp.where(fwd_lane, a, swapped)
        qd = (qq * qdec).astype(BF16)
        o = (jnp.dot(p, vn, preferred_element_type=F32)
             + jnp.dot(qd, st_ref[n].astype(BF16), preferred_element_type=F32))
        gate = g_ref[0, n * CH:(n + 1) * CH, :].astype(F32)
        o_ref[0, (n - 1) * CH:n * CH, :] = (_rms(o) * _silu(gate)).astype(BF16)


def _retention(proj, log_decay):
    return pl.pallas_call(
        _retention_kernel,
        grid=(B, NH),
        in_specs=[
            pl.BlockSpec(memory_space=pltpu.SMEM),
            pl.BlockSpec((1, RB, 128), lambda b, h: (b, 0, h)),
            pl.BlockSpec((1, RB, 128), lambda b, h: (b, 0, NH + h)),
            pl.BlockSpec((1, RB, 128), lambda b, h: (b, 0, 2 * NH + h)),
        ],
        out_specs=pl.BlockSpec((1, T, 128), lambda b, h: (b, 0, h)),
        out_shape=jax.ShapeDtypeStruct((B, T, NH * HV), BF16),
        scratch_shapes=[pltpu.VMEM((NCH, 128, 128), F32)],
        compiler_params=_cparams(("arbitrary", "arbitrary")),
        name="retention",
    )(log_decay, proj, proj, proj)


def _diffattn_kernel(lam_ref, *refs, out_scale):
    q_refs = refs[:TQ_SUB]
    k_ref, v_ref, sw_ref, o_ref = refs[TQ_SUB:]
    lam = lam_ref[0]
    k = k_ref[0]
    v = v_ref[0]
    nt = (((1,), (1,)), ((), ()))

    def half(qh):
        s = lax.dot_general(qh, k, nt, preferred_element_type=F32)
        e = jnp.exp2(s - jnp.max(s, axis=-1, keepdims=True))
        return jnp.dot(e.astype(BF16), v, preferred_element_type=F32), jnp.sum(e, axis=-1, keepdims=True)

    for i in range(TQ_SUB):
        q = q_refs[i][0]
        lane = lax.broadcasted_iota(I32, q.shape, 1)
        zero = jnp.zeros_like(q)
        o1, l1 = half(jnp.where(lane < DK, q, zero))
        o2, l2 = half(jnp.where(lane >= DK, q, zero))
        o = o1 / l1 - o2 * (lam / l2)
        o_ref[0, i * TM_PROJ:(i + 1) * TM_PROJ, :] = (_rms(o) * sw_ref[...] * out_scale).astype(BF16)


def _diffattn(proj, lam, subln_w, out_scale):
    tq = TQ_SUB * TM_PROJ
    nq = T // tq

    def q_map(i, b, h, j):
        return (b, LC // TM_PROJ + j * TQ_SUB + i, 3 * NH + h)

    return pl.pallas_call(
        functools.partial(_diffattn_kernel, out_scale=out_scale),
        grid=(B, NH, nq),
        in_specs=[
            pl.BlockSpec(memory_space=pltpu.SMEM),
            *[pl.BlockSpec((1, TM_PROJ, 128), functools.partial(q_map, i)) for i in range(TQ_SUB)],
            pl.BlockSpec((1, RB, 128), lambda b, h, j: (b, 0, 4 * NH + h)),
            pl.BlockSpec((1, RB, 128), lambda b, h, j: (b, 0, 5 * NH + h)),
            pl.BlockSpec((1, HV), lambda b, h, j: (0, 0)),
        ],
        out_specs=pl.BlockSpec((1, tq, 128), lambda b, h, j: (b, j, h)),
        out_shape=jax.ShapeDtypeStruct((B, T, NH * HV), BF16),
        compiler_params=_cparams(("arbitrary", "arbitrary", "arbitrary")),
        name="diffattn",
    )(lam, *([proj] * TQ_SUB), proj, proj, subln_w)


def _route(logits_t, bias):
    bz = [_sigmoid(logits_t[e:e + 1, :]) + bias[e:e + 1, :] for e in range(NE)]
    grp = []
    for g in range(NGRP):
        m = bz[g * EPG:(g + 1) * EPG]
        best = None
        for i in range(EPG):
            for k in range(i + 1, EPG):
                pair = m[i] + m[k]
                best = pair if best is None else jnp.maximum(best, pair)
        grp.append(best)
    gbest = grp[0]
    gsel = jnp.zeros_like(gbest, dtype=I32)
    for g in range(1, NGRP):
        better = grp[g] > gbest
        gsel = jnp.where(better, g, gsel)
        gbest = jnp.where(better, grp[g], gbest)
    cb = [bz[i] for i in range(EPG)]
    for g in range(1, NGRP):
        pick = gsel == g
        cb = [jnp.where(pick, bz[g * EPG + i], cb[i]) for i in range(EPG)]
    i1 = jnp.zeros_like(gsel)
    b1 = cb[0]
    for i in range(1, EPG):
        better = cb[i] > b1
        i1 = jnp.where(better, i, i1)
        b1 = jnp.where(better, cb[i], b1)
    neg = jnp.full_like(b1, -jnp.inf)
    rest = [jnp.where(i1 == i, neg, cb[i]) for i in range(EPG)]
    i2 = jnp.zeros_like(gsel)
    b2 = rest[0]
    for i in range(1, EPG):
        better = rest[i] > b2
        i2 = jnp.where(better, i, i2)
        b2 = jnp.where(better, rest[i], b2)
    lo = jnp.minimum(i1, i2)
    hi = jnp.maximum(i1, i2)
    code = lo * EPG + hi
    pair = jnp.full_like(gsel, len(PAIR_A) - 1)
    for p in range(len(PAIR_A) - 1):
        a, b = min(PAIR_A[p], PAIR_B[p]), max(PAIR_A[p], PAIR_B[p])
        pair = jnp.where(code == a * EPG + b, p, pair)
    return gsel * len(PAIR_A) + pair


def _ffn_prologue(first, x1, mod, nfw_ref, rw_ref, rb_ref, hp_ref, route_ref, cnt_ref, run_ref):
    rows = x1.shape[0]
    sh2 = mod[:, 3 * D:4 * D]
    sc2 = mod[:, 4 * D:5 * D]
    h2 = (_rms(x1) * nfw_ref[...]) * (1.0 + sc2) + sh2
    _store_slabs(hp_ref, h2)
    logits_t = lax.dot_general(rw_ref[...], h2, (((1,), (1,)), ((), ())),
                               preferred_element_type=F32, precision=lax.Precision.HIGHEST)
    cls = _route(logits_t, rb_ref[...])

    @pl.when(first)
    def _():
        run_ref[...] = jnp.zeros_like(run_ref)

    onehot = (lax.broadcasted_iota(I32, (CLS_PAD, rows), 0) == cls).astype(F32)
    tri = (lax.broadcasted_iota(I32, (rows, rows), 0) <= lax.broadcasted_iota(I32, (rows, rows), 1)).astype(BF16)
    prefix = jnp.dot(onehot.astype(BF16), tri, preferred_element_type=F32)
    run = run_ref[...]
    rank = jnp.sum(onehot * (prefix - 1.0 + run[:, 0:1]), axis=0, keepdims=True)
    run = run + jnp.sum(onehot, axis=1, keepdims=True)
    run_ref[...] = run
    cnt_ref[...] = run
    route_ref[...] = jnp.concatenate([cls.astype(F32), rank, jnp.zeros((6, rows), F32)], axis=0)


def _ffn_out_specs(tm, n_tiles_per_b):
    specs = [
        pl.BlockSpec((1, tm, D), lambda b, j: (b, j, 0)),
        pl.BlockSpec((tm * SLAB, 128), lambda b, j: (b * n_tiles_per_b + j, 0)),
        pl.BlockSpec((8, tm), lambda b, j: (0, b * n_tiles_per_b + j)),
        pl.BlockSpec((CLS_PAD, 128), lambda b, j: (0, 0)),
    ]
    shapes = [
        jax.ShapeDtypeStruct((B, T, D), F32),
        jax.ShapeDtypeStruct((N * SLAB, 128), F32),
        jax.ShapeDtypeStruct((8, N), F32),
        jax.ShapeDtypeStruct((CLS_PAD, 128), F32),
    ]
    return specs, shapes


def _outproj_kernel(ret_ref, dif_ref, w_ref, x_ref, mod_ref, nfw_ref, rw_ref, rb_ref,
                    x1_ref, hp_ref, route_ref, cnt_ref, run_ref):
    first = (pl.program_id(0) == 0) & (pl.program_id(1) == 0)
    mod = mod_ref[0]
    mx = (jnp.dot(ret_ref[0], w_ref[0:NH * HV, :], preferred_element_type=F32)
          + jnp.dot(dif_ref[0], w_ref[NH * HV:, :], preferred_element_type=F32))
    x1 = x_ref[0] + mod[:, 2 * D:3 * D] * mx
    x1_ref[0] = x1
    _ffn_prologue(first, x1, mod, nfw_ref, rw_ref, rb_ref, hp_ref, route_ref, cnt_ref, run_ref)


def _outproj(ret, dif, w_out, x, mod3, nfw, rw_t, rb):
    nj = T // TM_OUT
    out_specs, out_shapes = _ffn_out_specs(TM_OUT, nj)
    return pl.pallas_call(
        _outproj_kernel,
        grid=(B, nj),
        in_specs=[
            pl.BlockSpec((1, TM_OUT, NH * HV), lambda b, j: (b, j, 0)),
            pl.BlockSpec((1, TM_OUT, NH * HV), lambda b, j: (b, j, 0)),
            pl.BlockSpec((2 * NH * HV, D), lambda b, j: (0, 0)),
            pl.BlockSpec((1, TM_OUT, D), lambda b, j: (b, j, 0)),
            pl.BlockSpec((1, 1, 6 * D), lambda b, j: (b, 0, 0)),
            pl.BlockSpec((1, D), lambda b, j: (0, 0)),
            pl.BlockSpec((NE, D), lambda b, j: (0, 0)),
            pl.BlockSpec((NE, 1), lambda b, j: (0, 0)),
        ],
        out_specs=out_specs,
        out_shape=out_shapes,
        scratch_shapes=[pltpu.VMEM((CLS_PAD, 128), F32)],
        compiler_params=_cparams(("arbitrary", "arbitrary")),
        name="outproj",
    )(ret, dif, w_out, x, mod3, nfw, rw_t, rb)


def _pool_kernel(x_ref, prev_ref, next_ref, mod_ref, nmw_ref, pw_ref, ps_ref, nfw_ref, rw_ref, rb_ref,
                 x1_ref, hp_ref, route_ref, cnt_ref, ext_ref, run_ref):
    i = pl.program_id(1)
    first = (pl.program_id(0) == 0) & (i == 0)
    last = pl.num_programs(1) - 1
    mod = mod_ref[0]
    sh1 = mod[:, 0:D]
    sc1 = mod[:, D:2 * D]

    def modnorm(v):
        return (_rms(v) * nmw_ref[...]) * (1.0 + sc1) + sh1

    x = x_ref[0]
    hc = modnorm(x)
    ext_ref[0:HALO, :] = jnp.where(i > 0, modnorm(prev_ref[0]), 0.0)
    ext_ref[HALO:HALO + TM_POOL, :] = hc
    ext_ref[HALO + TM_POOL:, :] = jnp.where(i < last, modnorm(next_ref[0]), 0.0)
    pos = i * TM_POOL + lax.broadcasted_iota(I32, (TM_POOL, 1), 0)
    mixed = []
    for gi, w in enumerate(POOL_WINDOWS):
        left = w // 2
        right = w - 1 - left
        cols = slice(gi * PG, (gi + 1) * PG)
        tot = None
        for d in range(-left, right + 1):
            part = ext_ref[HALO + d:HALO + d + TM_POOL, cols]
            tot = part if tot is None else tot + part
        cnt = (jnp.minimum(pos + right + 1, T) - jnp.maximum(pos - left, 0)).astype(F32)
        pooled = (tot / cnt - hc[:, cols]).astype(BF16)
        mixed.append(jnp.dot(pooled, pw_ref[gi], preferred_element_type=F32))
    mixed = jnp.concatenate(mixed, axis=1) * ps_ref[...]
    x1 = x + mod[:, 2 * D:3 * D] * mixed
    x1_ref[0] = x1
    _ffn_prologue(first, x1, mod, nfw_ref, rw_ref, rb_ref, hp_ref, route_ref, cnt_ref, run_ref)


def _pool_layer(x, mod3, nmw, pool_w, pool_scale, nfw, rw_t, rb):
    ni = T // TM_POOL
    hb = TM_POOL // HALO
    out_specs, out_shapes = _ffn_out_specs(TM_POOL, ni)
    return pl.pallas_call(
        _pool_kernel,
        grid=(B, ni),
        in_specs=[
            pl.BlockSpec((1, TM_POOL, D), lambda b, i: (b, i, 0)),
            pl.BlockSpec((1, HALO, D), lambda b, i: (b, jnp.maximum(i * hb - 1, 0), 0)),
            pl.BlockSpec((1, HALO, D), lambda b, i: (b, jnp.minimum((i + 1) * hb, T // HALO - 1), 0)),
            pl.BlockSpec((1, 1, 6 * D), lambda b, i: (b, 0, 0)),
            pl.BlockSpec((1, D), lambda b, i: (0, 0)),
            pl.BlockSpec((len(POOL_WINDOWS), PG, PG), lambda b, i: (0, 0, 0)),
            pl.BlockSpec((1, D), lambda b, i: (0, 0)),
            pl.BlockSpec((1, D), lambda b, i: (0, 0)),
            pl.BlockSpec((NE, D), lambda b, i: (0, 0)),
            pl.BlockSpec((NE, 1), lambda b, i: (0, 0)),
        ],
        out_specs=out_specs,
        out_shape=out_shapes,
        scratch_shapes=[pltpu.VMEM((TM_POOL + 2 * HALO, D), F32), pltpu.VMEM((CLS_PAD, 128), F32)],
        compiler_params=_cparams(("arbitrary", "arbitrary")),
        name="pool_layer",
    )(x, x, x, mod3, nmw, pool_w, pool_scale, nfw, rw_t, rb)


def _row_copy(src_ref, dst_ref, sem, s, d, rows):
    s0 = pl.multiple_of(s * SLAB, SLAB)
    d0 = pl.multiple_of(d * SLAB, SLAB)
    return pltpu.make_async_copy(src_ref.at[pl.ds(s0, rows * SLAB)], dst_ref.at[pl.ds(d0, rows * SLAB)], sem)


def _permute_kernel(idx_ref, src_ref, dst_ref, sem, *, scatter):
    n_chunks = N // PERM_CHUNK

    def issue_chunk(c, carry):
        def issue(i, carry):
            base = c * PERM_CHUNK + i * PERM_UNROLL
            for u in range(PERM_UNROLL):
                t = base + u
                r = idx_ref[t]
                s, d = (t, r) if scatter else (r, t)
                _row_copy(src_ref, dst_ref, sem, s, d, 1).start()
            return carry

        lax.fori_loop(0, PERM_CHUNK // PERM_UNROLL, issue, 0)

        @pl.when(c >= 2)
        def _():
            _row_copy(src_ref, dst_ref, sem, 0, 0, PERM_CHUNK).wait()

        return carry

    lax.fori_loop(0, n_chunks, issue_chunk, 0)
    for _ in range(2):
        _row_copy(src_ref, dst_ref, sem, 0, 0, PERM_CHUNK).wait()


def _permute_rows(idx, src, scatter):
    return pl.pallas_call(
        functools.partial(_permute_kernel, scatter=scatter),
        grid_spec=pltpu.PrefetchScalarGridSpec(
            num_scalar_prefetch=1,
            grid=(1,),
            in_specs=[pl.BlockSpec(memory_space=pl.ANY)],
            out_specs=pl.BlockSpec(memory_space=pl.ANY),
            scratch_shapes=[pltpu.SemaphoreType.DMA(())],
        ),
        out_shape=jax.ShapeDtypeStruct(src.shape, src.dtype),
        compiler_params=_cparams(("arbitrary",)),
        name="scatter_rows" if scatter else "gather_rows",
    )(idx, src)


def _moe_kernel(tile_ref, lo_ref, hi_ref, ea_ref, eb_ref, ca_ref, cb_ref,
                x_ref, rw_ref, wga_ref, wua_ref, wda_ref, wgb_ref, wub_ref, wdb_ref, o_ref,
                ga_ref, ua_ref, da_ref, gb_ref, ub_ref, db_ref):
    m = pl.program_id(0)
    lo = lo_ref[m]
    hi = hi_ref[m]
    tile0 = tile_ref[m] * TM_MOE

    @pl.when(ca_ref[m] == 1)
    def _():
        ga_ref[...] = wga_ref[0, 0].astype(BF16)
        ua_ref[...] = wua_ref[0, 0].astype(BF16)
        da_ref[...] = wda_ref[0, 0].astype(BF16)

    @pl.when(cb_ref[m] == 1)
    def _():
        gb_ref[...] = wgb_ref[0, 0].astype(BF16)
        ub_ref[...] = wub_ref[0, 0].astype(BF16)
        db_ref[...] = wdb_ref[0, 0].astype(BF16)

    def value():
        hf = _load_slabs(x_ref, TM_MOE)
        h = hf.astype(BF16)

        def ffn(g_ref, u_ref, d_ref):
            hid = (_silu(jnp.dot(h, g_ref[...], preferred_element_type=F32))
                   * jnp.dot(h, u_ref[...], preferred_element_type=F32))
            return jnp.dot(hid.astype(BF16), d_ref[...], preferred_element_type=F32)

        s_a = _sigmoid(jnp.sum(hf * rw_ref[pl.ds(ea_ref[m], 1), :], axis=1, keepdims=True))
        s_b = _sigmoid(jnp.sum(hf * rw_ref[pl.ds(eb_ref[m], 1), :], axis=1, keepdims=True))
        denom = s_a + s_b
        val = (s_a / denom) * ffn(ga_ref, ua_ref, da_ref) + (s_b / denom) * ffn(gb_ref, ub_ref, db_ref)
        row = tile0 + lax.broadcasted_iota(I32, (TM_MOE, 1), 0)
        return val, (row >= lo) & (row < hi)

    @pl.when((hi > lo) & (lo == tile0))
    def _():
        val, seg = value()
        _store_slabs(o_ref, jnp.where(seg, val, 0.0))

    @pl.when((hi > lo) & (lo != tile0))
    def _():
        val, seg = value()
        _store_slabs(o_ref, jnp.where(seg, val, _load_slabs(o_ref, TM_MOE)))


def _moe_sorted(units, xs, rw_t, wg, wu, wd, layer):
    def wspec(which, shape):
        return pl.BlockSpec((1, 1) + shape,
                            lambda m, t, lo, hi, ea, eb, ca, cb: (layer, (ea, eb)[which][m], 0, 0))

    row_map = lambda m, t, lo, hi, ea, eb, ca, cb: (t[m], 0)
    return pl.pallas_call(
        _moe_kernel,
        grid_spec=pltpu.PrefetchScalarGridSpec(
            num_scalar_prefetch=7,
            grid=(N_UNITS,),
            in_specs=[
                pl.BlockSpec((TM_MOE * SLAB, 128), row_map),
                pl.BlockSpec((NE, D), lambda m, t, lo, hi, ea, eb, ca, cb: (0, 0)),
                wspec(0, (D, DE)), wspec(0, (D, DE)), wspec(0, (DE, D)),
                wspec(1, (D, DE)), wspec(1, (D, DE)), wspec(1, (DE, D)),
            ],
            out_specs=pl.BlockSpec((TM_MOE * SLAB, 128), row_map),
            scratch_shapes=[pltpu.VMEM((D, DE), BF16), pltpu.VMEM((D, DE), BF16), pltpu.VMEM((DE, D), BF16),
                            pltpu.VMEM((D, DE), BF16), pltpu.VMEM((D, DE), BF16), pltpu.VMEM((DE, D), BF16)],
        ),
        out_shape=jax.ShapeDtypeStruct((N * SLAB, 128), F32),
        compiler_params=_cparams(("arbitrary",)),
        name="moe_sorted",
    )(*units, xs, rw_t, wg, wu, wd, wg, wu, wd)


def _moe_units(route, counts):
    cls = route[0].astype(I32)
    rank = route[1].astype(I32)
    cnt = counts[:NCLS, 0].astype(I32)
    ends = jnp.cumsum(cnt)
    offs = ends - cnt
    dest = jnp.sum(jnp.where(cls[:, None] == jnp.arange(NCLS)[None, :], offs[None, :], 0), axis=1) + rank
    starts = jnp.sort(jnp.concatenate([jnp.arange(N // TM_MOE, dtype=I32) * TM_MOE, offs]))
    u_lo = starts
    u_hi = jnp.concatenate([starts[1:], jnp.full((1,), N, I32)])
    u_tile = jnp.minimum(u_lo // TM_MOE, N // TM_MOE - 1)
    u_cls = jnp.minimum(jnp.sum(ends[None, :] <= u_lo[:, None], axis=1), NCLS - 1).astype(I32)
    grp = u_cls // len(PAIR_A)
    pair = u_cls % len(PAIR_A)
    e_a = grp * EPG + jnp.take(jnp.array(PAIR_A, I32), pair)
    e_b = grp * EPG + jnp.take(jnp.array(PAIR_B, I32), pair)
    one = jnp.ones((1,), I32)
    chg_a = jnp.concatenate([one, (e_a[1:] != e_a[:-1]).astype(I32)])
    chg_b = jnp.concatenate([one, (e_b[1:] != e_b[:-1]).astype(I32)])
    return dest, (u_tile, u_lo, u_hi, e_a, e_b, chg_a, chg_b)


def _residual_kernel(x_ref, y_ref, mod_ref, fw_ref, o_ref, *, final):
    out = x_ref[0] + mod_ref[0][:, 5 * D:6 * D] * _load_slabs(y_ref, x_ref.shape[1])
    if final:
        out = _rms(out) * fw_ref[...]
    o_ref[0] = out


def _residual(x1, y, mod3, final_w, final):
    tm = 512
    nj = T // tm
    return pl.pallas_call(
        functools.partial(_residual_kernel, final=final),
        grid=(B, nj),
        in_specs=[
            pl.BlockSpec((1, tm, D), lambda b, j: (b, j, 0)),
            pl.BlockSpec((tm * SLAB, 128), lambda b, j: (b * nj + j, 0)),
            pl.BlockSpec((1, 1, 6 * D), lambda b, j: (b, 0, 0)),
            pl.BlockSpec((1, D), lambda b, j: (0, 0)),
        ],
        out_specs=pl.BlockSpec((1, tm, D), lambda b, j: (b, j, 0)),
        out_shape=jax.ShapeDtypeStruct((B, T, D), F32),
        compiler_params=_cparams(("arbitrary", "arbitrary")),
        name="moe_residual",
    )(x1, y, mod3, final_w)


def _moe_layer(x1, hp, route, counts, rw_t, wg, wu, wd, layer, mod3, final_w, final):
    dest, units = _moe_units(route, counts)
    xs = _permute_rows(dest, hp, scatter=True)
    ys = _moe_sorted(units, xs, rw_t, wg, wu, wd, layer)
    y = _permute_rows(dest, ys, scatter=False)
    return _residual(x1, y, mod3, final_w, final)


def _rope_tables():
    half = 16
    inv = ROPE_BASE ** (-jnp.arange(half, dtype=F32) / half)
    t = jnp.arange(T)
    row = (t // GRID_W).astype(F32)
    col = (t % GRID_W).astype(F32)
    ang_r = row[:, None] * inv[None, :]
    ang_c = col[:, None] * inv[None, :]
    ang = jnp.concatenate([ang_r, ang_r, ang_c, ang_c], axis=1)
    sign = jnp.tile(jnp.concatenate([-jnp.ones(half, F32), jnp.ones(half, F32)]), 2)
    cos = jnp.cos(ang)
    sin = jnp.sin(ang) * sign[None, :]
    cos = jnp.concatenate([jnp.ones((LC, 64), F32), cos], axis=0)
    sin = jnp.concatenate([jnp.zeros((LC, 64), F32), sin], axis=0)
    return jnp.tile(cos, (1, 2)), jnp.tile(sin, (1, 2))


def _permute_w_in(w):
    rq = w[:, 0:256].reshape(D, NH, DK)
    dq = w[:, 256:768]
    rg = w[:, 768:1280]
    rk = w[:, 1280:1536].reshape(D, NH, DK)
    rv = w[:, 1536:2048]
    dk = w[:, 2048:2560]
    dv = w[:, 2560:3072]
    qk = jnp.concatenate([rq, rk * (DK ** -0.5)], axis=2).reshape(D, NH * 2 * DK)
    return jnp.concatenate([qk, rv, rg, dq * (DK ** -0.5 * math.log2(math.e)), dk, dv], axis=1).astype(BF16)


def kernel(x, c, ctx, c_ctx, ada_w, ada_b, norm_mix_w, norm_ffn_w, w_in, w_out, ret_log_decay, diff_lambda,
           diff_subln_w, pool_w, pool_scale, router_w, router_b, moe_w_gate, moe_w_up, moe_w_down, final_norm_w):
    assert x.shape == (B, T, D) and ctx.shape == (B, LC, D) and ada_w.shape[0] == 2
    cc = jnp.concatenate([c, c_ctx[None, :], jnp.zeros((16 - B - 1, D), F32)], axis=0)
    mod = _ada_mod(cc, ada_w, ada_b)
    rw_t = router_w.T
    rb = router_b.reshape(NE, 1)
    fw = final_norm_w.reshape(1, D)

    mod0 = mod[0].reshape(16, 1, 6 * D)
    cos_t, sin_t = _rope_tables()
    proj = _inproj(x, ctx, mod0[:, :, :2 * D], norm_mix_w[0:1], _permute_w_in(w_in[0]), cos_t, sin_t)
    ret = _retention(proj, ret_log_decay[0])
    lam_init = 0.8 - 0.6 * math.exp(-0.3 * 0)
    lv = diff_lambda[0]
    lam = jnp.exp(jnp.sum(lv[0] * lv[1])) - jnp.exp(jnp.sum(lv[2] * lv[3])) + lam_init
    dif = _diffattn(proj, lam.reshape(1), diff_subln_w[0:1], 1.0 - lam_init)
    x1, hp, route, counts = _outproj(ret, dif, w_out[0].astype(BF16), x, mod0, norm_ffn_w[0:1], rw_t, rb)
    x2 = _moe_layer(x1, hp, route, counts, rw_t, moe_w_gate, moe_w_up, moe_w_down, 0, mod0, fw, False)

    mod1 = mod[1].reshape(16, 1, 6 * D)
    x3, hp, route, counts = _pool_layer(x2, mod1, norm_mix_w[1:2], pool_w[0].astype(BF16),
                                        pool_scale[0:1], norm_ffn_w[1:2], rw_t, rb)
    out = _moe_layer(x3, hp, route, counts, rw_t, moe_w_gate, moe_w_up, moe_w_down, 1, mod1, fw, True)
    return out
```

```python
import functools
import math

import jax
import jax.numpy as jnp
from jax import lax
from jax.experimental import pallas as pl
from jax.experimental.pallas import tpu as pltpu

F32 = jnp.float32
BF16 = jnp.bfloat16
I32 = jnp.int32

D = 1024
B = 8
T = 2048
N = B * T
GRID_W = 64
LC = 256
EPS = 1e-6
ROPE_BASE = 10000.0
NH = 4
DK = 64
HV = 128
CH = 256
RB = LC + T
NCH = RB // CH
POOL_WINDOWS = (2, 4, 8, 16)
PG = D // len(POOL_WINDOWS)
NE = 16
NGRP = 4
EPG = NE // NGRP
DE = 512
IN_W = 3072
HALO = 8

PAIR_A = (0, 0, 0, 1, 1, 3)
PAIR_B = (1, 2, 3, 3, 2, 2)
NCLS = NGRP * len(PAIR_A)
CLS_PAD = 32
SLAB = D // 128

TM_PROJ = 256
TM_OUT = 512
TQ_SUB = 4
TM_POOL = 512
TM_MOE = 256
N_UNITS = N // TM_MOE + NCLS
TM_PERM = 1024
TM_COMB = 512
PERM_UNROLL = 16
VMEM_LIMIT = 56 * 1024 * 1024


def _cparams(sem):
    return pltpu.CompilerParams(dimension_semantics=sem, vmem_limit_bytes=VMEM_LIMIT)


def _sigmoid(x):
    return 1.0 / (1.0 + jnp.exp(-x))


def _silu(x):
    return x * _sigmoid(x)


def _rms(x):
    return x * lax.rsqrt(jnp.mean(x * x, axis=-1, keepdims=True) + EPS)


def _load_slabs(ref, rows):
    return jnp.concatenate([ref[pl.ds(s, rows, stride=SLAB), :] for s in range(SLAB)], axis=1)


def _store_slabs(ref, val):
    rows = val.shape[0]
    for s in range(SLAB):
        ref[pl.ds(s, rows, stride=SLAB), :] = val[:, s * 128:(s + 1) * 128]


def _ada_kernel(cc_ref, w_ref, b_ref, o_ref):
    s = _silu(cc_ref[...])
    o_ref[0] = jnp.dot(s, w_ref[0], preferred_element_type=F32, precision=lax.Precision.HIGHEST) + b_ref[0]


def _ada_mod(cc, ada_w, ada_b):
    depth = ada_w.shape[0]
    tn = 1536
    return pl.pallas_call(
        _ada_kernel,
        grid=(depth, 6 * D // tn),
        in_specs=[
            pl.BlockSpec((16, D), lambda l, n: (0, 0)),
            pl.BlockSpec((1, D, tn), lambda l, n: (l, 0, n)),
            pl.BlockSpec((1, 1, tn), lambda l, n: (l, 0, n)),
        ],
        out_specs=pl.BlockSpec((1, 16, tn), lambda l, n: (l, 0, n)),
        out_shape=jax.ShapeDtypeStruct((depth, 16, 6 * D), F32),
        compiler_params=_cparams(("arbitrary", "arbitrary")),
        name="ada_mod",
    )(cc, ada_w, ada_b.reshape(depth, 1, 6 * D))


def _rope(seg, cos, sin_signed, lo_mask):
    w = seg.shape[1]
    from_hi = pltpu.roll(seg, w - 16, axis=1)
    from_lo = pltpu.roll(seg, 16, axis=1)
    partner = jnp.where(lo_mask, from_hi, from_lo)
    reps = w // cos.shape[1]
    c = jnp.concatenate([cos] * reps, axis=1)
    s = jnp.concatenate([sin_signed] * reps, axis=1)
    return seg * c + partner * s


def _inproj_kernel(x_ref, c_ref, mod_ref, nw_ref, w_ref, cos_ref, sin_ref, o_ref):
    j = pl.program_id(1)
    xt = jnp.where(j == 0, c_ref[0], x_ref[0])
    sh = mod_ref[0, :, 0:D]
    sc = mod_ref[0, :, D:2 * D]
    h = (_rms(xt) * nw_ref[...]) * (1.0 + sc) + sh
    acc = jnp.dot(h.astype(BF16), w_ref[...], preferred_element_type=F32)
    lane = lax.broadcasted_iota(I32, (TM_PROJ, 512), 1)
    lo_mask = (lane % 32) < 16
    cos = cos_ref[...]
    sin = sin_ref[...]
    for g in range(6):
        seg = acc[:, g * 512:(g + 1) * 512]
        if g in (0, 3, 4):
            seg = _rope(seg, cos, sin, lo_mask)
        o_ref[0, :, g * 512:(g + 1) * 512] = seg.astype(BF16)


def _inproj(x, ctx, mod3, norm_w, w_perm, cos_t, sin_t):
    nj = RB // TM_PROJ
    return pl.pallas_call(
        _inproj_kernel,
        grid=(B, nj),
        in_specs=[
            pl.BlockSpec((1, TM_PROJ, D), lambda b, j: (b, jnp.maximum(j - 1, 0), 0)),
            pl.BlockSpec((1, LC, D), lambda b, j: (b, 0, 0)),
            pl.BlockSpec((1, 1, 2 * D), lambda b, j: (jnp.where(j == 0, B, b), 0, 0)),
            pl.BlockSpec((1, D), lambda b, j: (0, 0)),
            pl.BlockSpec((D, IN_W), lambda b, j: (0, 0)),
            pl.BlockSpec((TM_PROJ, 128), lambda b, j: (j, 0)),
            pl.BlockSpec((TM_PROJ, 128), lambda b, j: (j, 0)),
        ],
        out_specs=pl.BlockSpec((1, TM_PROJ, IN_W), lambda b, j: (b, j, 0)),
        out_shape=jax.ShapeDtypeStruct((B, RB, IN_W), BF16),
        compiler_params=_cparams(("arbitrary", "arbitrary")),
        name="inproj",
    )(x, ctx, mod3, norm_w, w_perm, cos_t, sin_t)


def _retention_kernel(ld_ref, qk_ref, v_ref, g_ref, o_ref, st_ref):
    h = pl.program_id(1)
    lgf = ld_ref[0, h]
    lgb = ld_ref[1, h]
    lane = lax.broadcasted_iota(I32, (CH, 128), 1)
    fwd_lane = lane < DK
    pos = lax.broadcasted_iota(I32, (CH, 128), 0).astype(F32)
    kdec = jnp.where(fwd_lane, jnp.exp(lgf * (CH - 1 - pos)), jnp.exp(lgb * pos))
    qdec = jnp.where(fwd_lane, jnp.exp(lgf * (pos + 1.0)), jnp.exp(lgb * (CH - pos)))
    ii = lax.broadcasted_iota(I32, (CH, CH), 0)
    jj = lax.broadcasted_iota(I32, (CH, CH), 1)
    gap = (ii - jj).astype(F32)
    mask = (jnp.where(gap >= 0, jnp.exp(lgf * jnp.maximum(gap, 0.0)), 0.0)
            + jnp.where(gap <= 0, jnp.exp(lgb * jnp.maximum(-gap, 0.0)), 0.0))
    ones = jnp.ones((DK, 128), F32)
    cf = jnp.exp(lgf * CH * ones)
    cb = jnp.exp(lgb * CH * ones)

    def chunk(n):
        a = qk_ref[0, n * CH:(n + 1) * CH, :].astype(F32)
        swapped = pltpu.roll(a, DK, axis=1)
        return a, swapped

    kv = []
    for n in range(NCH):
        a, swapped = chunk(n)
        kk = jnp.where(fwd_lane, swapped, a)
        kb = (kk * kdec).astype(BF16)
        vn = v_ref[0, n * CH:(n + 1) * CH, :]
        kv.append(lax.dot_general(kb, vn, (((0,), (0,)), ((), ())), preferred_element_type=F32))
    sf = kv[0][:DK]
    for n in range(1, NCH):
        st_ref[n, 0:DK, :] = sf
        sf = cf * sf + kv[n][:DK]
    sb = kv[0][DK:]
    for n in range(NCH - 1, 0, -1):
        st_ref[n, DK:2 * DK, :] = sb
        sb = cb * sb + kv[n][DK:]

    for n in range(1, NCH):
        a, swapped = chunk(n)
        q = a[:, :DK].astype(BF16)
        k = swapped[:, :DK].astype(BF16)
        scores = lax.dot_general(q, k, (((1,), (1,)), ((), ())), preferred_element_type=F32)
        p = (scores * mask).astype(BF16)
        vn = v_ref[0, n * CH:(n + 1) * CH, :]
        qq = jnp.where(fwd_lane, a, swapped)
        qd = (qq * qdec).astype(BF16)
        o = (jnp.dot(p, vn, preferred_element_type=F32)
             + jnp.dot(qd, st_ref[n].astype(BF16), preferred_element_type=F32))
        gate = g_ref[0, n * CH:(n + 1) * CH, :].astype(F32)
        o_ref[0, (n - 1) * CH:n * CH, :] = (_rms(o) * _silu(gate)).astype(BF16)


def _retention(proj, log_decay):
    return pl.pallas_call(
        _retention_kernel,
        grid=(B, NH),
        in_specs=[
            pl.BlockSpec(memory_space=pltpu.SMEM),
            pl.BlockSpec((1, RB, 128), lambda b, h: (b, 0, h)),
            pl.BlockSpec((1, RB, 128), lambda b, h: (b, 0, NH + h)),
            pl.BlockSpec((1, RB, 128), lambda b, h: (b, 0, 2 * NH + h)),
        ],
        out_specs=pl.BlockSpec((1, T, 128), lambda b, h: (b, 0, h)),
        out_shape=jax.ShapeDtypeStruct((B, T, NH * HV), BF16),
        scratch_shapes=[pltpu.VMEM((NCH, 128, 128), F32)],
        compiler_params=_cparams(("arbitrary", "arbitrary")),
        name="retention",
    )(log_decay, proj, proj, proj)


def _diffattn_kernel(lam_ref, *refs, out_scale):
    q_refs = refs[:TQ_SUB]
    k_ref, v_ref, sw_ref, o_ref = refs[TQ_SUB:]
    lam = lam_ref[0]
    k = k_ref[0]
    v = v_ref[0]
    nt = (((1,), (1,)), ((), ()))

    def half(qh):
        s = lax.dot_general(qh, k, nt, preferred_element_type=F32)
        e = jnp.exp2(s - jnp.max(s, axis=-1, keepdims=True))
        return jnp.dot(e.astype(BF16), v, preferred_element_type=F32), jnp.sum(e, axis=-1, keepdims=True)

    for i in range(TQ_SUB):
        q = q_refs[i][0]
        lane = lax.broadcasted_iota(I32, q.shape, 1)
        zero = jnp.zeros_like(q)
        o1, l1 = half(jnp.where(lane < DK, q, zero))
        o2, l2 = half(jnp.where(lane >= DK, q, zero))
        o = o1 / l1 - o2 * (lam / l2)
        o_ref[0, i * TM_PROJ:(i + 1) * TM_PROJ, :] = (_rms(o) * sw_ref[...] * out_scale).astype(BF16)


def _diffattn(proj, lam, subln_w, out_scale):
    tq = TQ_SUB * TM_PROJ
    nq = T // tq

    def q_map(i, b, h, j):
        return (b, LC // TM_PROJ + j * TQ_SUB + i, 3 * NH + h)

    return pl.pallas_call(
        functools.partial(_diffattn_kernel, out_scale=out_scale),
        grid=(B, NH, nq),
        in_specs=[
            pl.BlockSpec(memory_space=pltpu.SMEM),
            *[pl.BlockSpec((1, TM_PROJ, 128), functools.partial(q_map, i)) for i in range(TQ_SUB)],
            pl.BlockSpec((1, RB, 128), lambda b, h, j: (b, 0, 4 * NH + h)),
            pl.BlockSpec((1, RB, 128), lambda b, h, j: (b, 0, 5 * NH + h)),
            pl.BlockSpec((1, HV), lambda b, h, j: (0, 0)),
        ],
        out_specs=pl.BlockSpec((1, tq, 128), lambda b, h, j: (b, j, h)),
        out_shape=jax.ShapeDtypeStruct((B, T, NH * HV), BF16),
        compiler_params=_cparams(("arbitrary", "arbitrary", "arbitrary")),
        name="diffattn",
    )(lam, *([proj] * TQ_SUB), proj, proj, subln_w)


def _route(logits_t, bias):
    bz = [_sigmoid(logits_t[e:e + 1, :]) + bias[e:e + 1, :] for e in range(NE)]
    grp = []
    for g in range(NGRP):
        m = bz[g * EPG:(g + 1) * EPG]
        best = None
        for i in range(EPG):
            for k in range(i + 1, EPG):
                pair = m[i] + m[k]
                best = pair if best is None else jnp.maximum(best, pair)
        grp.append(best)
    gbest = grp[0]
    gsel = jnp.zeros_like(gbest, dtype=I32)
    for g in range(1, NGRP):
        better = grp[g] > gbest
        gsel = jnp.where(better, g, gsel)
        gbest = jnp.where(better, grp[g], gbest)
    cb = [bz[i] for i in range(EPG)]
    for g in range(1, NGRP):
        pick = gsel == g
        cb = [jnp.where(pick, bz[g * EPG + i], cb[i]) for i in range(EPG)]
    i1 = jnp.zeros_like(gsel)
    b1 = cb[0]
    for i in range(1, EPG):
        better = cb[i] > b1
        i1 = jnp.where(better, i, i1)
        b1 = jnp.where(better, cb[i], b1)
    neg = jnp.full_like(b1, -jnp.inf)
    rest = [jnp.where(i1 == i, neg, cb[i]) for i in range(EPG)]
    i2 = jnp.zeros_like(gsel)
    b2 = rest[0]
    for i in range(1, EPG):
        better = rest[i] > b2
        i2 = jnp.where(better, i, i2)
        b2 = jnp.where(better, rest[i], b2)
    lo = jnp.minimum(i1, i2)
    hi = jnp.maximum(i1, i2)
    code = lo * EPG + hi
    pair = jnp.full_like(gsel, len(PAIR_A) - 1)
    for p in range(len(PAIR_A) - 1):
        a, b = min(PAIR_A[p], PAIR_B[p]), max(PAIR_A[p], PAIR_B[p])
        pair = jnp.where(code == a * EPG + b, p, pair)
    return gsel * len(PAIR_A) + pair


def _ffn_prologue(first, x1, mod, nfw_ref, rw_ref, rb_ref, hp_ref, route_ref, cnt_ref, run_ref):
    rows = x1.shape[0]
    sh2 = mod[:, 3 * D:4 * D]
    sc2 = mod[:, 4 * D:5 * D]
    h2 = (_rms(x1) * nfw_ref[...]) * (1.0 + sc2) + sh2
    _store_slabs(hp_ref, h2)
    logits_t = lax.dot_general(rw_ref[...], h2, (((1,), (1,)), ((), ())),
                               preferred_element_type=F32, precision=lax.Precision.HIGHEST)
    cls = _route(logits_t, rb_ref[...])

    @pl.when(first)
    def _():
        run_ref[...] = jnp.zeros_like(run_ref)

    onehot = (lax.broadcasted_iota(I32, (CLS_PAD, rows), 0) == cls).astype(F32)
    tri = (lax.broadcasted_iota(I32, (rows, rows), 0) <= lax.broadcasted_iota(I32, (rows, rows), 1)).astype(BF16)
    prefix = jnp.dot(onehot.astype(BF16), tri, preferred_element_type=F32)
    run = run_ref[...]
    rank = jnp.sum(onehot * (prefix - 1.0 + run[:, 0:1]), axis=0, keepdims=True)
    run = run + jnp.sum(onehot, axis=1, keepdims=True)
    run_ref[...] = run
    cnt_ref[...] = run
    route_ref[...] = jnp.concatenate([cls.astype(F32), rank, jnp.zeros((6, rows), F32)], axis=0)


def _ffn_out_specs(tm, n_tiles_per_b):
    specs = [
        pl.BlockSpec((1, tm, D), lambda b, j: (b, j, 0)),
        pl.BlockSpec((tm * SLAB, 128), lambda b, j: (b * n_tiles_per_b + j, 0)),
        pl.BlockSpec((8, tm), lambda b, j: (0, b * n_tiles_per_b + j)),
        pl.BlockSpec((CLS_PAD, 128), lambda b, j: (0, 0)),
    ]
    shapes = [
        jax.ShapeDtypeStruct((B, T, D), F32),
        jax.ShapeDtypeStruct((N * SLAB, 128), F32),
        jax.ShapeDtypeStruct((8, N), F32),
        jax.ShapeDtypeStruct((CLS_PAD, 128), F32),
    ]
    return specs, shapes


def _outproj_kernel(ret_ref, dif_ref, w_ref, x_ref, mod_ref, nfw_ref, rw_ref, rb_ref,
                    x1_ref, hp_ref, route_ref, cnt_ref, run_ref):
    first = (pl.program_id(0) == 0) & (pl.program_id(1) == 0)
    mod = mod_ref[0]
    mx = (jnp.dot(ret_ref[0], w_ref[0:NH * HV, :], preferred_element_type=F32)
          + jnp.dot(dif_ref[0], w_ref[NH * HV:, :], preferred_element_type=F32))
    x1 = x_ref[0] + mod[:, 2 * D:3 * D] * mx
    x1_ref[0] = x1
    _ffn_prologue(first, x1, mod, nfw_ref, rw_ref, rb_ref, hp_ref, route_ref, cnt_ref, run_ref)


def _outproj(ret, dif, w_out, x, mod3, nfw, rw_t, rb):
    nj = T // TM_OUT
    out_specs, out_shapes = _ffn_out_specs(TM_OUT, nj)
    return pl.pallas_call(
        _outproj_kernel,
        grid=(B, nj),
        in_specs=[
            pl.BlockSpec((1, TM_OUT, NH * HV), lambda b, j: (b, j, 0)),
            pl.BlockSpec((1, TM_OUT, NH * HV), lambda b, j: (b, j, 0)),
            pl.BlockSpec((2 * NH * HV, D), lambda b, j: (0, 0)),
            pl.BlockSpec((1, TM_OUT, D), lambda b, j: (b, j, 0)),
            pl.BlockSpec((1, 1, 6 * D), lambda b, j: (b, 0, 0)),
            pl.BlockSpec((1, D), lambda b, j: (0, 0)),
            pl.BlockSpec((NE, D), lambda b, j: (0, 0)),
            pl.BlockSpec((NE, 1), lambda b, j: (0, 0)),
        ],
        out_specs=out_specs,
        out_shape=out_shapes,
        scratch_shapes=[pltpu.VMEM((CLS_PAD, 128), F32)],
        compiler_params=_cparams(("arbitrary", "arbitrary")),
        name="outproj",
    )(ret, dif, w_out, x, mod3, nfw, rw_t, rb)


def _pool_kernel(x_ref, prev_ref, next_ref, mod_ref, nmw_ref, pw_ref, ps_ref, nfw_ref, rw_ref, rb_ref,
                 x1_ref, hp_ref, route_ref, cnt_ref, ext_ref, run_ref):
    i = pl.program_id(1)
    first = (pl.program_id(0) == 0) & (i == 0)
    last = pl.num_programs(1) - 1
    mod = mod_ref[0]
    sh1 = mod[:, 0:D]
    sc1 = mod[:, D:2 * D]

    def modnorm(v):
        return (_rms(v) * nmw_ref[...]) * (1.0 + sc1) + sh1

    x = x_ref[0]
    hc = modnorm(x)
    ext_ref[0:HALO, :] = jnp.where(i > 0, modnorm(prev_ref[0]), 0.0)
    ext_ref[HALO:HALO + TM_POOL, :] = hc
    ext_ref[HALO + TM_POOL:, :] = jnp.where(i < last, modnorm(next_ref[0]), 0.0)
    pos = i * TM_POOL + lax.broadcasted_iota(I32, (TM_POOL, 1), 0)
    mixed = []
    for gi, w in enumerate(POOL_WINDOWS):
        left = w // 2
        right = w - 1 - left
        cols = slice(gi * PG, (gi + 1) * PG)
        tot = None
        for d in range(-left, right + 1):
            part = ext_ref[HALO + d:HALO + d + TM_POOL, cols]
            tot = part if tot is None else tot + part
        cnt = (jnp.minimum(pos + right + 1, T) - jnp.maximum(pos - left, 0)).astype(F32)
        pooled = (tot / cnt - hc[:, cols]).astype(BF16)
        mixed.append(jnp.dot(pooled, pw_ref[gi], preferred_element_type=F32))
    mixed = jnp.concatenate(mixed, axis=1) * ps_ref[...]
    x1 = x + mod[:, 2 * D:3 * D] * mixed
    x1_ref[0] = x1
    _ffn_prologue(first, x1, mod, nfw_ref, rw_ref, rb_ref, hp_ref, route_ref, cnt_ref, run_ref)


def _pool_layer(x, mod3, nmw, pool_w, pool_scale, nfw, rw_t, rb):
    ni = T // TM_POOL
    hb = TM_POOL // HALO
    out_specs, out_shapes = _ffn_out_specs(TM_POOL, ni)
    return pl.pallas_call(
        _pool_kernel,
        grid=(B, ni),
        in_specs=[
            pl.BlockSpec((1, TM_POOL, D), lambda b, i: (b, i, 0)),
            pl.BlockSpec((1, HALO, D), lambda b, i: (b, jnp.maximum(i * hb - 1, 0), 0)),
            pl.BlockSpec((1, HALO, D), lambda b, i: (b, jnp.minimum((i + 1) * hb, T // HALO - 1), 0)),
            pl.BlockSpec((1, 1, 6 * D), lambda b, i: (b, 0, 0)),
            pl.BlockSpec((1, D), lambda b, i: (0, 0)),
            pl.BlockSpec((len(POOL_WINDOWS), PG, PG), lambda b, i: (0, 0, 0)),
            pl.BlockSpec((1, D), lambda b, i: (0, 0)),
            pl.BlockSpec((1, D), lambda b, i: (0, 0)),
            pl.BlockSpec((NE, D), lambda b, i: (0, 0)),
            pl.BlockSpec((NE, 1), lambda b, i: (0, 0)),
        ],
        out_specs=out_specs,
        out_shape=out_shapes,
        scratch_shapes=[pltpu.VMEM((TM_POOL + 2 * HALO, D), F32), pltpu.VMEM((CLS_PAD, 128), F32)],
        compiler_params=_cparams(("arbitrary", "arbitrary")),
        name="pool_layer",
    )(x, x, x, mod3, nmw, pool_w, pool_scale, nfw, rw_t, rb)


def _tile_copy(src_ref, dst_ref, sem, s, d, rows=1):
    s0 = pl.multiple_of(s * SLAB, SLAB)
    d0 = pl.multiple_of(d * SLAB, SLAB)
    return pltpu.make_async_copy(src_ref.at[pl.ds(s0, rows * SLAB)], dst_ref.at[pl.ds(d0, rows * SLAB)], sem)


def _issue_tile_copies(idx_ref, base, rows, start_one):
    def group(g, carry):
        r0 = g * PERM_UNROLL
        ids = [idx_ref[base + r0 + u] for u in range(PERM_UNROLL)]
        for u in range(PERM_UNROLL):
            start_one(r0 + u, ids[u])
        return carry

    lax.fori_loop(0, rows // PERM_UNROLL, group, 0)


def _dispatch_kernel(dest_ref, src_ref, dst_ref, sem):
    base = pl.program_id(0) * TM_PERM

    def start_one(r, d):
        _tile_copy(src_ref, dst_ref, sem, r, d).start()

    _issue_tile_copies(dest_ref, base, TM_PERM, start_one)
    _tile_copy(src_ref, dst_ref, sem, 0, 0, TM_PERM).wait()


def _dispatch(dest, src):
    return pl.pallas_call(
        _dispatch_kernel,
        grid_spec=pltpu.PrefetchScalarGridSpec(
            num_scalar_prefetch=1,
            grid=(N // TM_PERM,),
            in_specs=[pl.BlockSpec((TM_PERM * SLAB, 128), lambda i, dest: (i, 0))],
            out_specs=pl.BlockSpec(memory_space=pl.ANY),
            scratch_shapes=[pltpu.SemaphoreType.DMA(())],
        ),
        out_shape=jax.ShapeDtypeStruct(src.shape, src.dtype),
        compiler_params=_cparams(("arbitrary",)),
        name="dispatch",
    )(dest, src)


def _combine_kernel(dest_ref, x_ref, ys_ref, mod_ref, fw_ref, o_ref, ybuf_ref, sem, *, final):
    i = pl.program_id(0)
    n = pl.num_programs(0)
    slot = i % 2

    def gather(step, to_slot):
        def start_one(r, d):
            _tile_copy(ys_ref, ybuf_ref.at[to_slot], sem.at[to_slot], d, r).start()

        _issue_tile_copies(dest_ref, step * TM_COMB, TM_COMB, start_one)

    @pl.when(i == 0)
    def _():
        gather(0, 0)

    @pl.when(i + 1 < n)
    def _():
        gather(i + 1, 1 - slot)

    _tile_copy(ys_ref, ybuf_ref.at[slot], sem.at[slot], 0, 0, TM_COMB).wait()
    out = x_ref[...] + mod_ref[0][:, 5 * D:6 * D] * _load_slabs(ybuf_ref.at[slot], TM_COMB)
    if final:
        out = _rms(out) * fw_ref[...]
    o_ref[...] = out


def _combine(dest, x1, ys, mod3, final_w, final):
    per_b = T // TM_COMB
    return pl.pallas_call(
        functools.partial(_combine_kernel, final=final),
        grid_spec=pltpu.PrefetchScalarGridSpec(
            num_scalar_prefetch=1,
            grid=(N // TM_COMB,),
            in_specs=[
                pl.BlockSpec((TM_COMB, D), lambda i, dest: (i, 0)),
                pl.BlockSpec(memory_space=pl.ANY),
                pl.BlockSpec((1, 1, 6 * D), lambda i, dest: (i // per_b, 0, 0)),
                pl.BlockSpec((1, D), lambda i, dest: (0, 0)),
            ],
            out_specs=pl.BlockSpec((TM_COMB, D), lambda i, dest: (i, 0)),
            scratch_shapes=[pltpu.VMEM((2, TM_COMB * SLAB, 128), F32), pltpu.SemaphoreType.DMA((2,))],
        ),
        out_shape=jax.ShapeDtypeStruct((N, D), F32),
        compiler_params=_cparams(("arbitrary",)),
        name="combine",
    )(dest, x1.reshape(N, D), ys, mod3, final_w)


def _moe_kernel(tile_ref, lo_ref, hi_ref, ea_ref, eb_ref, ca_ref, cb_ref,
                x_ref, rw_ref, wga_ref, wua_ref, wda_ref, wgb_ref, wub_ref, wdb_ref, o_ref,
                ga_ref, ua_ref, da_ref, gb_ref, ub_ref, db_ref):
    m = pl.program_id(0)
    lo = lo_ref[m]
    hi = hi_ref[m]
    tile0 = tile_ref[m] * TM_MOE

    @pl.when(ca_ref[m] == 1)
    def _():
        ga_ref[...] = wga_ref[0, 0].astype(BF16)
        ua_ref[...] = wua_ref[0, 0].astype(BF16)
        da_ref[...] = wda_ref[0, 0].astype(BF16)

    @pl.when(cb_ref[m] == 1)
    def _():
        gb_ref[...] = wgb_ref[0, 0].astype(BF16)
        ub_ref[...] = wub_ref[0, 0].astype(BF16)
        db_ref[...] = wdb_ref[0, 0].astype(BF16)

    def value():
        hf = _load_slabs(x_ref, TM_MOE)
        h = hf.astype(BF16)

        def ffn(g_ref, u_ref, d_ref):
            hid = (_silu(jnp.dot(h, g_ref[...], preferred_element_type=F32))
                   * jnp.dot(h, u_ref[...], preferred_element_type=F32))
            return jnp.dot(hid.astype(BF16), d_ref[...], preferred_element_type=F32)

        s_a = _sigmoid(jnp.sum(hf * rw_ref[pl.ds(ea_ref[m], 1), :], axis=1, keepdims=True))
        s_b = _sigmoid(jnp.sum(hf * rw_ref[pl.ds(eb_ref[m], 1), :], axis=1, keepdims=True))
        denom = s_a + s_b
        val = (s_a / denom) * ffn(ga_ref, ua_ref, da_ref) + (s_b / denom) * ffn(gb_ref, ub_ref, db_ref)
        row = tile0 + lax.broadcasted_iota(I32, (TM_MOE, 1), 0)
        return val, (row >= lo) & (row < hi)

    @pl.when((hi > lo) & (lo == tile0))
    def _():
        val, seg = value()
        _store_slabs(o_ref, jnp.where(seg, val, 0.0))

    @pl.when((hi > lo) & (lo != tile0))
    def _():
        val, seg = value()
        _store_slabs(o_ref, jnp.where(seg, val, _load_slabs(o_ref, TM_MOE)))


def _moe_sorted(units, xs, rw_t, wg, wu, wd, layer):
    def wspec(which, shape):
        return pl.BlockSpec((1, 1) + shape,
                            lambda m, t, lo, hi, ea, eb, ca, cb: (layer, (ea, eb)[which][m], 0, 0))

    row_map = lambda m, t, lo, hi, ea, eb, ca, cb: (t[m], 0)
    return pl.pallas_call(
        _moe_kernel,
        grid_spec=pltpu.PrefetchScalarGridSpec(
            num_scalar_prefetch=7,
            grid=(N_UNITS,),
            in_specs=[
                pl.BlockSpec((TM_MOE * SLAB, 128), row_map),
                pl.BlockSpec((NE, D), lambda m, t, lo, hi, ea, eb, ca, cb: (0, 0)),
                wspec(0, (D, DE)), wspec(0, (D, DE)), wspec(0, (DE, D)),
                wspec(1, (D, DE)), wspec(1, (D, DE)), wspec(1, (DE, D)),
            ],
            out_specs=pl.BlockSpec((TM_MOE * SLAB, 128), row_map),
            scratch_shapes=[pltpu.VMEM((D, DE), BF16), pltpu.VMEM((D, DE), BF16), pltpu.VMEM((DE, D), BF16),
                            pltpu.VMEM((D, DE), BF16), pltpu.VMEM((D, DE), BF16), pltpu.VMEM((DE, D), BF16)],
        ),
        out_shape=jax.ShapeDtypeStruct((N * SLAB, 128), F32),
        compiler_params=_cparams(("arbitrary",)),
        name="moe_sorted",
    )(*units, xs, rw_t, wg, wu, wd, wg, wu, wd)


def _moe_units(route, counts):
    cls = route[0].astype(I32)
    rank = route[1].astype(I32)
    cnt = counts[:NCLS, 0].astype(I32)
    ends = jnp.cumsum(cnt)
    offs = ends - cnt
    dest = jnp.sum(jnp.where(cls[:, None] == jnp.arange(NCLS)[None, :], offs[None, :], 0), axis=1) + rank
    starts = jnp.sort(jnp.concatenate([jnp.arange(N // TM_MOE, dtype=I32) * TM_MOE, offs]))
    u_lo = starts
    u_hi = jnp.concatenate([starts[1:], jnp.full((1,), N, I32)])
    u_tile = jnp.minimum(u_lo // TM_MOE, N // TM_MOE - 1)
    u_cls = jnp.minimum(jnp.sum(ends[None, :] <= u_lo[:, None], axis=1), NCLS - 1).astype(I32)
    grp = u_cls // len(PAIR_A)
    pair = u_cls % len(PAIR_A)
    e_a = grp * EPG + jnp.take(jnp.array(PAIR_A, I32), pair)
    e_b = grp * EPG + jnp.take(jnp.array(PAIR_B, I32), pair)
    one = jnp.ones((1,), I32)
    chg_a = jnp.concatenate([one, (e_a[1:] != e_a[:-1]).astype(I32)])
    chg_b = jnp.concatenate([one, (e_b[1:] != e_b[:-1]).astype(I32)])
    return dest, (u_tile, u_lo, u_hi, e_a, e_b, chg_a, chg_b)


def _moe_layer(x1, hp, route, counts, rw_t, wg, wu, wd, layer, mod3, final_w, final):
    dest, units = _moe_units(route, counts)
    xs = _dispatch(dest, hp)
    ys = _moe_sorted(units, xs, rw_t, wg, wu, wd, layer)
    return _combine(dest, x1, ys, mod3, final_w, final).reshape(B, T, D)


def _rope_tables():
    half = 16
    inv = ROPE_BASE ** (-jnp.arange(half, dtype=F32) / half)
    t = jnp.arange(T)
    row = (t // GRID_W).astype(F32)
    col = (t % GRID_W).astype(F32)
    ang_r = row[:, None] * inv[None, :]
    ang_c = col[:, None] * inv[None, :]
    ang = jnp.concatenate([ang_r, ang_r, ang_c, ang_c], axis=1)
    sign = jnp.tile(jnp.concatenate([-jnp.ones(half, F32), jnp.ones(half, F32)]), 2)
    cos = jnp.cos(ang)
    sin = jnp.sin(ang) * sign[None, :]
    cos = jnp.concatenate([jnp.ones((LC, 64), F32), cos], axis=0)
    sin = jnp.concatenate([jnp.zeros((LC, 64), F32), sin], axis=0)
    return jnp.tile(cos, (1, 2)), jnp.tile(sin, (1, 2))


def _permute_w_in(w):
    rq = w[:, 0:256].reshape(D, NH, DK)
    dq = w[:, 256:768]
    rg = w[:, 768:1280]
    rk = w[:, 1280:1536].reshape(D, NH, DK)
    rv = w[:, 1536:2048]
    dk = w[:, 2048:2560]
    dv = w[:, 2560:3072]
    qk = jnp.concatenate([rq, rk * (DK ** -0.5)], axis=2).reshape(D, NH * 2 * DK)
    return jnp.concatenate([qk, rv, rg, dq * (DK ** -0.5 * math.log2(math.e)), dk, dv], axis=1).astype(BF16)


def kernel(x, c, ctx, c_ctx, ada_w, ada_b, norm_mix_w, norm_ffn_w, w_in, w_out, ret_log_decay, diff_lambda,
           diff_subln_w, pool_w, pool_scale, router_w, router_b, moe_w_gate, moe_w_up, moe_w_down, final_norm_w):
    assert x.shape == (B, T, D) and ctx.shape == (B, LC, D) and ada_w.shape[0] == 2
    cc = jnp.concatenate([c, c_ctx[None, :], jnp.zeros((16 - B - 1, D), F32)], axis=0)
    mod = _ada_mod(cc, ada_w, ada_b)
    rw_t = router_w.T
    rb = router_b.reshape(NE, 1)
    fw = final_norm_w.reshape(1, D)

    mod0 = mod[0].reshape(16, 1, 6 * D)
    cos_t, sin_t = _rope_tables()
    proj = _inproj(x, ctx, mod0[:, :, :2 * D], norm_mix_w[0:1], _permute_w_in(w_in[0]), cos_t, sin_t)
    ret = _retention(proj, ret_log_decay[0])
    lam_init = 0.8 - 0.6 * math.exp(-0.3 * 0)
    lv = diff_lambda[0]
    lam = jnp.exp(jnp.sum(lv[0] * lv[1])) - jnp.exp(jnp.sum(lv[2] * lv[3])) + lam_init
    dif = _diffattn(proj, lam.reshape(1), diff_subln_w[0:1], 1.0 - lam_init)
    x1, hp, route, counts = _outproj(ret, dif, w_out[0].astype(BF16), x, mod0, norm_ffn_w[0:1], rw_t, rb)
    x2 = _moe_layer(x1, hp, route, counts, rw_t, moe_w_gate, moe_w_up, moe_w_down, 0, mod0, fw, False)

    mod1 = mod[1].reshape(16, 1, 6 * D)
    x3, hp, route, counts = _pool_layer(x2, mod1, norm_mix_w[1:2], pool_w[0].astype(BF16),
                                        pool_scale[0:1], norm_ffn_w[1:2], rw_t, rb)
    out = _moe_layer(x3, hp, route, counts, rw_t, moe_w_gate, moe_w_up, moe_w_down, 1, mod1, fw, True)
    return out
```

```python
import functools
import math

import jax
import jax.numpy as jnp
from jax import lax
from jax.experimental import pallas as pl
from jax.experimental.pallas import tpu as pltpu

F32 = jnp.float32
BF16 = jnp.bfloat16
I32 = jnp.int32

D = 1024
B = 8
T = 2048
N = B * T
GRID_W = 64
LC = 256
EPS = 1e-6
ROPE_BASE = 10000.0
NH = 4
DK = 64
HV = 128
CH = 256
RB = LC + T
NCH = RB // CH
POOL_WINDOWS = (2, 4, 8, 16)
PG = D // len(POOL_WINDOWS)
NE = 16
NGRP = 4
EPG = NE // NGRP
DE = 512
IN_W = 3072
HALO = 8

PAIR_A = (0, 0, 0, 1, 1, 3)
PAIR_B = (1, 2, 3, 3, 2, 2)
NCLS = NGRP * len(PAIR_A)
CLS_PAD = 32
SLAB = D // 128

TM_PROJ = 256
TM_OUT = 512
TQ_SUB = 4
TM_POOL = 512
TM_MOE = 256
N_UNITS = N // TM_MOE + NCLS
TM_PERM = 1024
TM_COMB = 512
PERM_UNROLL = 16
VMEM_LIMIT = 56 * 1024 * 1024


def _cparams(sem):
    return pltpu.CompilerParams(dimension_semantics=sem, vmem_limit_bytes=VMEM_LIMIT)


def _sigmoid(x):
    return 1.0 / (1.0 + jnp.exp(-x))


def _silu(x):
    return x * _sigmoid(x)


def _rms(x):
    return x * lax.rsqrt(jnp.mean(x * x, axis=-1, keepdims=True) + EPS)


def _load_slabs(ref, rows):
    return jnp.concatenate([ref[pl.ds(s, rows, stride=SLAB), :] for s in range(SLAB)], axis=1)


def _store_slabs(ref, val):
    rows = val.shape[0]
    for s in range(SLAB):
        ref[pl.ds(s, rows, stride=SLAB), :] = val[:, s * 128:(s + 1) * 128]


def _ada_kernel(cc_ref, w_ref, b_ref, o_ref):
    s = _silu(cc_ref[...])
    o_ref[0] = jnp.dot(s, w_ref[0], preferred_element_type=F32, precision=lax.Precision.HIGHEST) + b_ref[0]


def _ada_mod(cc, ada_w, ada_b):
    depth = ada_w.shape[0]
    tn = 1536
    return pl.pallas_call(
        _ada_kernel,
        grid=(depth, 6 * D // tn),
        in_specs=[
            pl.BlockSpec((16, D), lambda l, n: (0, 0)),
            pl.BlockSpec((1, D, tn), lambda l, n: (l, 0, n)),
            pl.BlockSpec((1, 1, tn), lambda l, n: (l, 0, n)),
        ],
        out_specs=pl.BlockSpec((1, 16, tn), lambda l, n: (l, 0, n)),
        out_shape=jax.ShapeDtypeStruct((depth, 16, 6 * D), F32),
        compiler_params=_cparams(("arbitrary", "arbitrary")),
        name="ada_mod",
    )(cc, ada_w, ada_b.reshape(depth, 1, 6 * D))


def _rope(seg, cos, sin_signed, lo_mask):
    w = seg.shape[1]
    from_hi = pltpu.roll(seg, w - 16, axis=1)
    from_lo = pltpu.roll(seg, 16, axis=1)
    partner = jnp.where(lo_mask, from_hi, from_lo)
    reps = w // cos.shape[1]
    c = jnp.concatenate([cos] * reps, axis=1)
    s = jnp.concatenate([sin_signed] * reps, axis=1)
    return seg * c + partner * s


def _inproj_kernel(x_ref, c_ref, mod_ref, nw_ref, w_ref, cos_ref, sin_ref, o_ref):
    j = pl.program_id(1)
    xt = jnp.where(j == 0, c_ref[0], x_ref[0])
    sh = mod_ref[0, :, 0:D]
    sc = mod_ref[0, :, D:2 * D]
    h = (_rms(xt) * nw_ref[...]) * (1.0 + sc) + sh
    acc = jnp.dot(h.astype(BF16), w_ref[...], preferred_element_type=F32)
    lane = lax.broadcasted_iota(I32, (TM_PROJ, 512), 1)
    lo_mask = (lane % 32) < 16
    cos = cos_ref[...]
    sin = sin_ref[...]
    for g in range(6):
        seg = acc[:, g * 512:(g + 1) * 512]
        if g in (0, 3, 4):
            seg = _rope(seg, cos, sin, lo_mask)
        o_ref[0, :, g * 512:(g + 1) * 512] = seg.astype(BF16)


def _inproj(x, ctx, mod3, norm_w, w_perm, cos_t, sin_t):
    nj = RB // TM_PROJ
    return pl.pallas_call(
        _inproj_kernel,
        grid=(B, nj),
        in_specs=[
            pl.BlockSpec((1, TM_PROJ, D), lambda b, j: (b, jnp.maximum(j - 1, 0), 0)),
            pl.BlockSpec((1, LC, D), lambda b, j: (b, 0, 0)),
            pl.BlockSpec((1, 1, 2 * D), lambda b, j: (jnp.where(j == 0, B, b), 0, 0)),
            pl.BlockSpec((1, D), lambda b, j: (0, 0)),
            pl.BlockSpec((D, IN_W), lambda b, j: (0, 0)),
            pl.BlockSpec((TM_PROJ, 128), lambda b, j: (j, 0)),
            pl.BlockSpec((TM_PROJ, 128), lambda b, j: (j, 0)),
        ],
        out_specs=pl.BlockSpec((1, TM_PROJ, IN_W), lambda b, j: (b, j, 0)),
        out_shape=jax.ShapeDtypeStruct((B, RB, IN_W), BF16),
        compiler_params=_cparams(("arbitrary", "arbitrary")),
        name="inproj",
    )(x, ctx, mod3, norm_w, w_perm, cos_t, sin_t)


def _retention_kernel(ld_ref, qk_ref, v_ref, g_ref, o_ref, st_ref):
    h = pl.program_id(1)
    lgf = ld_ref[0, h]
    lgb = ld_ref[1, h]
    lane = lax.broadcasted_iota(I32, (CH, 128), 1)
    fwd_lane = lane < DK
    pos = lax.broadcasted_iota(I32, (CH, 128), 0).astype(F32)
    kdec = jnp.where(fwd_lane, jnp.exp(lgf * (CH - 1 - pos)), jnp.exp(lgb * pos))
    qdec = jnp.where(fwd_lane, jnp.exp(lgf * (pos + 1.0)), jnp.exp(lgb * (CH - pos)))
    ii = lax.broadcasted_iota(I32, (CH, CH), 0)
    jj = lax.broadcasted_iota(I32, (CH, CH), 1)
    gap = (ii - jj).astype(F32)
    mask = (jnp.where(gap >= 0, jnp.exp(lgf * jnp.maximum(gap, 0.0)), 0.0)
            + jnp.where(gap <= 0, jnp.exp(lgb * jnp.maximum(-gap, 0.0)), 0.0))
    ones = jnp.ones((DK, 128), F32)
    cf = jnp.exp(lgf * CH * ones)
    cb = jnp.exp(lgb * CH * ones)

    def chunk(n):
        a = qk_ref[0, n * CH:(n + 1) * CH, :].astype(F32)
        swapped = pltpu.roll(a, DK, axis=1)
        return a, swapped

    kv = []
    for n in range(NCH):
        a, swapped = chunk(n)
        kk = jnp.where(fwd_lane, swapped, a)
        kb = (kk * kdec).astype(BF16)
        vn = v_ref[0, n * CH:(n + 1) * CH, :]
        kv.append(lax.dot_general(kb, vn, (((0,), (0,)), ((), ())), preferred_element_type=F32))
    sf = kv[0][:DK]
    for n in range(1, NCH):
        st_ref[n, 0:DK, :] = sf
        sf = cf * sf + kv[n][:DK]
    sb = kv[0][DK:]
    for n in range(NCH - 1, 0, -1):
        st_ref[n, DK:2 * DK, :] = sb
        sb = cb * sb + kv[n][DK:]

    for n in range(1, NCH):
        a, swapped = chunk(n)
        q = a[:, :DK].astype(BF16)
        k = swapped[:, :DK].astype(BF16)
        scores = lax.dot_general(q, k, (((1,), (1,)), ((), ())), preferred_element_type=F32)
        p = (scores * mask).astype(BF16)
        vn = v_ref[0, n * CH:(n + 1) * CH, :]
        qq = jnp.where(fwd_lane, a, swapped)
        qd = (qq * qdec).astype(BF16)
        o = (jnp.dot(p, vn, preferred_element_type=F32)
             + jnp.dot(qd, st_ref[n].astype(BF16), preferred_element_type=F32))
        gate = g_ref[0, n * CH:(n + 1) * CH, :].astype(F32)
        o_ref[0, (n - 1) * CH:n * CH, :] = (_rms(o) * _silu(gate)).astype(BF16)


def _retention(proj, log_decay):
    return pl.pallas_call(
        _retention_kernel,
        grid=(B, NH),
        in_specs=[
            pl.BlockSpec(memory_space=pltpu.SMEM),
            pl.BlockSpec((1, RB, 128), lambda b, h: (b, 0, h)),
            pl.BlockSpec((1, RB, 128), lambda b, h: (b, 0, NH + h)),
            pl.BlockSpec((1, RB, 128), lambda b, h: (b, 0, 2 * NH + h)),
        ],
        out_specs=pl.BlockSpec((1, T, 128), lambda b, h: (b, 0, h)),
        out_shape=jax.ShapeDtypeStruct((B, T, NH * HV), BF16),
        scratch_shapes=[pltpu.VMEM((NCH, 128, 128), F32)],
        compiler_params=_cparams(("arbitrary", "arbitrary")),
        name="retention",
    )(log_decay, proj, proj, proj)


def _diffattn_kernel(lam_ref, *refs, out_scale):
    q_refs = refs[:TQ_SUB]
    k_ref, v_ref, sw_ref, o_ref = refs[TQ_SUB:]
    lam = lam_ref[0]
    k = k_ref[0]
    v = v_ref[0]
    nt = (((1,), (1,)), ((), ()))

    def half(qh):
        s = lax.dot_general(qh, k, nt, preferred_element_type=F32)
        e = jnp.exp2(s - jnp.max(s, axis=-1, keepdims=True))
        return jnp.dot(e.astype(BF16), v, preferred_element_type=F32), jnp.sum(e, axis=-1, keepdims=True)

    for i in range(TQ_SUB):
        q = q_refs[i][0]
        lane = lax.broadcasted_iota(I32, q.shape, 1)
        zero = jnp.zeros_like(q)
        o1, l1 = half(jnp.where(lane < DK, q, zero))
        o2, l2 = half(jnp.where(lane >= DK, q, zero))
        o = o1 / l1 - o2 * (lam / l2)
        o_ref[0, i * TM_PROJ:(i + 1) * TM_PROJ, :] = (_rms(o) * sw_ref[...] * out_scale).astype(BF16)


def _diffattn(proj, lam, subln_w, out_scale):
    tq = TQ_SUB * TM_PROJ
    nq = T // tq

    def q_map(i, b, h, j):
        return (b, LC // TM_PROJ + j * TQ_SUB + i, 3 * NH + h)

    return pl.pallas_call(
        functools.partial(_diffattn_kernel, out_scale=out_scale),
        grid=(B, NH, nq),
        in_specs=[
            pl.BlockSpec(memory_space=pltpu.SMEM),
            *[pl.BlockSpec((1, TM_PROJ, 128), functools.partial(q_map, i)) for i in range(TQ_SUB)],
            pl.BlockSpec((1, RB, 128), lambda b, h, j: (b, 0, 4 * NH + h)),
            pl.BlockSpec((1, RB, 128), lambda b, h, j: (b, 0, 5 * NH + h)),
            pl.BlockSpec((1, HV), lambda b, h, j: (0, 0)),
        ],
        out_specs=pl.BlockSpec((1, tq, 128), lambda b, h, j: (b, j, h)),
        out_shape=jax.ShapeDtypeStruct((B, T, NH * HV), BF16),
        compiler_params=_cparams(("arbitrary", "arbitrary", "arbitrary")),
        name="diffattn",
    )(lam, *([proj] * TQ_SUB), proj, proj, subln_w)


def _route(bz):
    grp = []
    for g in range(NGRP):
        m = bz[g * EPG:(g + 1) * EPG]
        best = None
        for i in range(EPG):
            for k in range(i + 1, EPG):
                pair = m[i] + m[k]
                best = pair if best is None else jnp.maximum(best, pair)
        grp.append(best)
    gbest = grp[0]
    gsel = jnp.zeros_like(gbest, dtype=I32)
    for g in range(1, NGRP):
        better = grp[g] > gbest
        gsel = jnp.where(better, g, gsel)
        gbest = jnp.where(better, grp[g], gbest)
    cb = [bz[i] for i in range(EPG)]
    for g in range(1, NGRP):
        pick = gsel == g
        cb = [jnp.where(pick, bz[g * EPG + i], cb[i]) for i in range(EPG)]
    i1 = jnp.zeros_like(gsel)
    b1 = cb[0]
    for i in range(1, EPG):
        better = cb[i] > b1
        i1 = jnp.where(better, i, i1)
        b1 = jnp.where(better, cb[i], b1)
    neg = jnp.full_like(b1, -jnp.inf)
    rest = [jnp.where(i1 == i, neg, cb[i]) for i in range(EPG)]
    i2 = jnp.zeros_like(gsel)
    b2 = rest[0]
    for i in range(1, EPG):
        better = rest[i] > b2
        i2 = jnp.where(better, i, i2)
        b2 = jnp.where(better, rest[i], b2)
    lo = jnp.minimum(i1, i2)
    hi = jnp.maximum(i1, i2)
    code = lo * EPG + hi
    pair = jnp.full_like(gsel, len(PAIR_A) - 1)
    for p in range(len(PAIR_A) - 1):
        a, b = min(PAIR_A[p], PAIR_B[p]), max(PAIR_A[p], PAIR_B[p])
        pair = jnp.where(code == a * EPG + b, p, pair)
    return gsel * len(PAIR_A) + pair


def _ffn_prologue(x1, mod, nfw_ref, rw_ref, hp_ref, logit_ref):
    sh2 = mod[:, 3 * D:4 * D]
    sc2 = mod[:, 4 * D:5 * D]
    h2 = (_rms(x1) * nfw_ref[...]) * (1.0 + sc2) + sh2
    _store_slabs(hp_ref, h2)
    logit_ref[...] = lax.dot_general(rw_ref[...], h2, (((1,), (1,)), ((), ())),
                                     preferred_element_type=F32, precision=lax.Precision.HIGHEST)


def _ffn_out_specs(tm, n_tiles_per_b):
    specs = [
        pl.BlockSpec((1, tm, D), lambda b, j: (b, j, 0)),
        pl.BlockSpec((tm * SLAB, 128), lambda b, j: (b * n_tiles_per_b + j, 0)),
        pl.BlockSpec((NE, tm), lambda b, j: (0, b * n_tiles_per_b + j)),
    ]
    shapes = [
        jax.ShapeDtypeStruct((B, T, D), F32),
        jax.ShapeDtypeStruct((N * SLAB, 128), F32),
        jax.ShapeDtypeStruct((NE, N), F32),
    ]
    return specs, shapes


def _route_kernel(logit_ref, bias_ref, dest_ref, ends_ref):
    r = logit_ref.shape[1]
    cls = _route([_sigmoid(logit_ref[e]) + bias_ref[e] for e in range(NE)])
    lane_incl = (lax.broadcasted_iota(I32, (128, 128), 0) <= lax.broadcasted_iota(I32, (128, 128), 1)).astype(BF16)
    rows_before = (lax.broadcasted_iota(I32, (r, r), 1) < lax.broadcasted_iota(I32, (r, r), 0)).astype(BF16)
    dest = jnp.zeros((r, 128), F32)
    start = jnp.zeros((1, 128), F32)
    ends = []
    for c in range(NCLS):
        onehot = jnp.where(cls == c, 1.0, 0.0)
        in_row = jnp.dot(onehot.astype(BF16), lane_incl, preferred_element_type=F32)
        row_tot = jnp.broadcast_to(in_row[:, 127:128], (r, 128))
        above = jnp.dot(rows_before, row_tot.astype(BF16), preferred_element_type=F32)
        dest = dest + onehot * (start + above + in_row - 1.0)
        start = start + jnp.sum(row_tot, axis=0, keepdims=True)
        ends.append(start)
    dest_ref[...] = dest.astype(I32)
    ends_ref[...] = jnp.concatenate(ends + [jnp.zeros((CLS_PAD - NCLS, 128), F32)], axis=0)


def _route_tokens(logits_t, router_b):
    r = N // 128
    return pl.pallas_call(
        _route_kernel,
        in_specs=[pl.BlockSpec((NE, r, 128), lambda: (0, 0, 0)), pl.BlockSpec(memory_space=pltpu.SMEM)],
        out_specs=[pl.BlockSpec((r, 128), lambda: (0, 0)), pl.BlockSpec((CLS_PAD, 128), lambda: (0, 0))],
        out_shape=[jax.ShapeDtypeStruct((r, 128), I32), jax.ShapeDtypeStruct((CLS_PAD, 128), F32)],
        compiler_params=pltpu.CompilerParams(vmem_limit_bytes=VMEM_LIMIT),
        name="route",
    )(logits_t.reshape(NE, r, 128), router_b)


def _outproj_kernel(ret_ref, dif_ref, w_ref, x_ref, mod_ref, nfw_ref, rw_ref, x1_ref, hp_ref, logit_ref):
    mod = mod_ref[0]
    mx = (jnp.dot(ret_ref[0], w_ref[0:NH * HV, :], preferred_element_type=F32)
          + jnp.dot(dif_ref[0], w_ref[NH * HV:, :], preferred_element_type=F32))
    x1 = x_ref[0] + mod[:, 2 * D:3 * D] * mx
    x1_ref[0] = x1
    _ffn_prologue(x1, mod, nfw_ref, rw_ref, hp_ref, logit_ref)


def _outproj(ret, dif, w_out, x, mod3, nfw, rw_t):
    nj = T // TM_OUT
    out_specs, out_shapes = _ffn_out_specs(TM_OUT, nj)
    return pl.pallas_call(
        _outproj_kernel,
        grid=(B, nj),
        in_specs=[
            pl.BlockSpec((1, TM_OUT, NH * HV), lambda b, j: (b, j, 0)),
            pl.BlockSpec((1, TM_OUT, NH * HV), lambda b, j: (b, j, 0)),
            pl.BlockSpec((2 * NH * HV, D), lambda b, j: (0, 0)),
            pl.BlockSpec((1, TM_OUT, D), lambda b, j: (b, j, 0)),
            pl.BlockSpec((1, 1, 6 * D), lambda b, j: (b, 0, 0)),
            pl.BlockSpec((1, D), lambda b, j: (0, 0)),
            pl.BlockSpec((NE, D), lambda b, j: (0, 0)),
        ],
        out_specs=out_specs,
        out_shape=out_shapes,
        compiler_params=_cparams(("arbitrary", "arbitrary")),
        name="outproj",
    )(ret, dif, w_out, x, mod3, nfw, rw_t)


def _pool_kernel(x_ref, prev_ref, next_ref, mod_ref, nmw_ref, pw_ref, ps_ref, nfw_ref, rw_ref,
                 x1_ref, hp_ref, logit_ref, ext_ref):
    i = pl.program_id(1)
    last = pl.num_programs(1) - 1
    mod = mod_ref[0]
    sh1 = mod[:, 0:D]
    sc1 = mod[:, D:2 * D]

    def modnorm(v):
        return (_rms(v) * nmw_ref[...]) * (1.0 + sc1) + sh1

    x = x_ref[0]
    hc = modnorm(x)
    ext_ref[0:HALO, :] = jnp.where(i > 0, modnorm(prev_ref[0]), 0.0)
    ext_ref[HALO:HALO + TM_POOL, :] = hc
    ext_ref[HALO + TM_POOL:, :] = jnp.where(i < last, modnorm(next_ref[0]), 0.0)
    pos = i * TM_POOL + lax.broadcasted_iota(I32, (TM_POOL, 1), 0)
    mixed = []
    for gi, w in enumerate(POOL_WINDOWS):
        left = w // 2
        right = w - 1 - left
        cols = slice(gi * PG, (gi + 1) * PG)
        tot = None
        for d in range(-left, right + 1):
            part = ext_ref[HALO + d:HALO + d + TM_POOL, cols]
            tot = part if tot is None else tot + part
        cnt = (jnp.minimum(pos + right + 1, T) - jnp.maximum(pos - left, 0)).astype(F32)
        pooled = (tot / cnt - hc[:, cols]).astype(BF16)
        mixed.append(jnp.dot(pooled, pw_ref[gi], preferred_element_type=F32))
    mixed = jnp.concatenate(mixed, axis=1) * ps_ref[...]
    x1 = x + mod[:, 2 * D:3 * D] * mixed
    x1_ref[0] = x1
    _ffn_prologue(x1, mod, nfw_ref, rw_ref, hp_ref, logit_ref)


def _pool_layer(x, mod3, nmw, pool_w, pool_scale, nfw, rw_t):
    ni = T // TM_POOL
    hb = TM_POOL // HALO
    out_specs, out_shapes = _ffn_out_specs(TM_POOL, ni)
    return pl.pallas_call(
        _pool_kernel,
        grid=(B, ni),
        in_specs=[
            pl.BlockSpec((1, TM_POOL, D), lambda b, i: (b, i, 0)),
            pl.BlockSpec((1, HALO, D), lambda b, i: (b, jnp.maximum(i * hb - 1, 0), 0)),
            pl.BlockSpec((1, HALO, D), lambda b, i: (b, jnp.minimum((i + 1) * hb, T // HALO - 1), 0)),
            pl.BlockSpec((1, 1, 6 * D), lambda b, i: (b, 0, 0)),
            pl.BlockSpec((1, D), lambda b, i: (0, 0)),
            pl.BlockSpec((len(POOL_WINDOWS), PG, PG), lambda b, i: (0, 0, 0)),
            pl.BlockSpec((1, D), lambda b, i: (0, 0)),
            pl.BlockSpec((1, D), lambda b, i: (0, 0)),
            pl.BlockSpec((NE, D), lambda b, i: (0, 0)),
        ],
        out_specs=out_specs,
        out_shape=out_shapes,
        scratch_shapes=[pltpu.VMEM((TM_POOL + 2 * HALO, D), F32)],
        compiler_params=_cparams(("arbitrary", "arbitrary")),
        name="pool_layer",
    )(x, x, x, mod3, nmw, pool_w, pool_scale, nfw, rw_t)


def _tile_copy(src_ref, dst_ref, sem, s, d, rows=1):
    s0 = pl.multiple_of(s * SLAB, SLAB)
    d0 = pl.multiple_of(d * SLAB, SLAB)
    return pltpu.make_async_copy(src_ref.at[pl.ds(s0, rows * SLAB)], dst_ref.at[pl.ds(d0, rows * SLAB)], sem)


def _issue_tile_copies(idx_ref, base, rows, start_one):
    def group(g, carry):
        r0 = g * PERM_UNROLL
        ids = [idx_ref[base + r0 + u] for u in range(PERM_UNROLL)]
        for u in range(PERM_UNROLL):
            start_one(r0 + u, ids[u], u % 2)
        return carry

    lax.fori_loop(0, rows // PERM_UNROLL, group, 0)


def _dispatch_kernel(dest_ref, src_ref, dst_ref, sem):
    base = pl.program_id(0) * TM_PERM

    def start_one(r, d, priority):
        _tile_copy(src_ref, dst_ref, sem, r, d).start(priority=priority)

    _issue_tile_copies(dest_ref, base, TM_PERM, start_one)
    _tile_copy(src_ref, dst_ref, sem, 0, 0, TM_PERM).wait()


def _dispatch(dest, src):
    return pl.pallas_call(
        _dispatch_kernel,
        grid_spec=pltpu.PrefetchScalarGridSpec(
            num_scalar_prefetch=1,
            grid=(N // TM_PERM,),
            in_specs=[pl.BlockSpec((TM_PERM * SLAB, 128), lambda i, dest: (i, 0))],
            out_specs=pl.BlockSpec(memory_space=pl.ANY),
            scratch_shapes=[pltpu.SemaphoreType.DMA(())],
        ),
        out_shape=jax.ShapeDtypeStruct(src.shape, src.dtype),
        compiler_params=_cparams(("arbitrary",)),
        name="dispatch",
    )(dest, src)


def _combine_kernel(dest_ref, x_ref, ys_ref, mod_ref, fw_ref, o_ref, ybuf_ref, sem, *, final):
    i = pl.program_id(0)
    n = pl.num_programs(0)
    slot = i % 2

    def gather(step, to_slot):
        def start_one(r, d, priority):
            _tile_copy(ys_ref, ybuf_ref.at[to_slot], sem.at[to_slot], d, r).start(priority=priority)

        _issue_tile_copies(dest_ref, step * TM_COMB, TM_COMB, start_one)

    @pl.when(i == 0)
    def _():
        gather(0, 0)

    @pl.when(i + 1 < n)
    def _():
        gather(i + 1, 1 - slot)

    _tile_copy(ys_ref, ybuf_ref.at[slot], sem.at[slot], 0, 0, TM_COMB).wait()
    out = x_ref[...] + mod_ref[0][:, 5 * D:6 * D] * _load_slabs(ybuf_ref.at[slot], TM_COMB)
    if final:
        out = _rms(out) * fw_ref[...]
    o_ref[...] = out


def _combine(dest, x1, ys, mod3, final_w, final):
    per_b = T // TM_COMB
    return pl.pallas_call(
        functools.partial(_combine_kernel, final=final),
        grid_spec=pltpu.PrefetchScalarGridSpec(
            num_scalar_prefetch=1,
            grid=(N // TM_COMB,),
            in_specs=[
                pl.BlockSpec((TM_COMB, D), lambda i, dest: (i, 0)),
                pl.BlockSpec(memory_space=pl.ANY),
                pl.BlockSpec((1, 1, 6 * D), lambda i, dest: (i // per_b, 0, 0)),
                pl.BlockSpec((1, D), lambda i, dest: (0, 0)),
            ],
            out_specs=pl.BlockSpec((TM_COMB, D), lambda i, dest: (i, 0)),
            scratch_shapes=[pltpu.VMEM((2, TM_COMB * SLAB, 128), F32), pltpu.SemaphoreType.DMA((2,))],
        ),
        out_shape=jax.ShapeDtypeStruct((N, D), F32),
        compiler_params=_cparams(("arbitrary",)),
        name="combine",
    )(dest, x1.reshape(N, D), ys, mod3, final_w)


def _moe_kernel(tile_ref, lo_ref, hi_ref, ea_ref, eb_ref, ca_ref, cb_ref,
                x_ref, rw_ref, wga_ref, wua_ref, wda_ref, wgb_ref, wub_ref, wdb_ref, o_ref,
                ga_ref, ua_ref, da_ref, gb_ref, ub_ref, db_ref):
    m = pl.program_id(0)
    lo = lo_ref[m]
    hi = hi_ref[m]
    tile0 = tile_ref[m] * TM_MOE

    @pl.when(ca_ref[m] == 1)
    def _():
        ga_ref[...] = wga_ref[0, 0].astype(BF16)
        ua_ref[...] = wua_ref[0, 0].astype(BF16)
        da_ref[...] = wda_ref[0, 0].astype(BF16)

    @pl.when(cb_ref[m] == 1)
    def _():
        gb_ref[...] = wgb_ref[0, 0].astype(BF16)
        ub_ref[...] = wub_ref[0, 0].astype(BF16)
        db_ref[...] = wdb_ref[0, 0].astype(BF16)

    def value():
        hf = _load_slabs(x_ref, TM_MOE)
        h = hf.astype(BF16)

        def ffn(g_ref, u_ref, d_ref):
            hid = (_silu(jnp.dot(h, g_ref[...], preferred_element_type=F32))
                   * jnp.dot(h, u_ref[...], preferred_element_type=F32))
            return jnp.dot(hid.astype(BF16), d_ref[...], preferred_element_type=F32)

        s_a = _sigmoid(jnp.sum(hf * rw_ref[pl.ds(ea_ref[m], 1), :], axis=1, keepdims=True))
        s_b = _sigmoid(jnp.sum(hf * rw_ref[pl.ds(eb_ref[m], 1), :], axis=1, keepdims=True))
        denom = s_a + s_b
        val = (s_a / denom) * ffn(ga_ref, ua_ref, da_ref) + (s_b / denom) * ffn(gb_ref, ub_ref, db_ref)
        row = tile0 + lax.broadcasted_iota(I32, (TM_MOE, 1), 0)
        return val, (row >= lo) & (row < hi)

    @pl.when((hi > lo) & (lo == tile0))
    def _():
        val, seg = value()
        _store_slabs(o_ref, jnp.where(seg, val, 0.0))

    @pl.when((hi > lo) & (lo != tile0))
    def _():
        val, seg = value()
        _store_slabs(o_ref, jnp.where(seg, val, _load_slabs(o_ref, TM_MOE)))


def _moe_sorted(units, xs, rw_t, wg, wu, wd, layer):
    def wspec(which, shape):
        return pl.BlockSpec((1, 1) + shape,
                            lambda m, t, lo, hi, ea, eb, ca, cb: (layer, (ea, eb)[which][m], 0, 0))

    row_map = lambda m, t, lo, hi, ea, eb, ca, cb: (t[m], 0)
    return pl.pallas_call(
        _moe_kernel,
        grid_spec=pltpu.PrefetchScalarGridSpec(
            num_scalar_prefetch=7,
            grid=(N_UNITS,),
            in_specs=[
                pl.BlockSpec((TM_MOE * SLAB, 128), row_map),
                pl.BlockSpec((NE, D), lambda m, t, lo, hi, ea, eb, ca, cb: (0, 0)),
                wspec(0, (D, DE)), wspec(0, (D, DE)), wspec(0, (DE, D)),
                wspec(1, (D, DE)), wspec(1, (D, DE)), wspec(1, (DE, D)),
            ],
            out_specs=pl.BlockSpec((TM_MOE * SLAB, 128), row_map),
            scratch_shapes=[pltpu.VMEM((D, DE), BF16), pltpu.VMEM((D, DE), BF16), pltpu.VMEM((DE, D), BF16),
                            pltpu.VMEM((D, DE), BF16), pltpu.VMEM((D, DE), BF16), pltpu.VMEM((DE, D), BF16)],
        ),
        out_shape=jax.ShapeDtypeStruct((N * SLAB, 128), F32),
        compiler_params=_cparams(("arbitrary",)),
        name="moe_sorted",
    )(*units, xs, rw_t, wg, wu, wd, wg, wu, wd)


def _moe_units(ends_f):
    ends = ends_f[:NCLS, 0].astype(I32)
    offs = jnp.concatenate([jnp.zeros((1,), I32), ends[:-1]])
    starts = jnp.sort(jnp.concatenate([jnp.arange(N // TM_MOE, dtype=I32) * TM_MOE, offs]))
    u_lo = starts
    u_hi = jnp.concatenate([starts[1:], jnp.full((1,), N, I32)])
    u_tile = jnp.minimum(u_lo // TM_MOE, N // TM_MOE - 1)
    u_cls = jnp.minimum(jnp.sum(ends[None, :] <= u_lo[:, None], axis=1), NCLS - 1).astype(I32)
    grp = u_cls // len(PAIR_A)
    pair = u_cls % len(PAIR_A)
    e_a = grp * EPG + jnp.take(jnp.array(PAIR_A, I32), pair)
    e_b = grp * EPG + jnp.take(jnp.array(PAIR_B, I32), pair)
    one = jnp.ones((1,), I32)
    chg_a = jnp.concatenate([one, (e_a[1:] != e_a[:-1]).astype(I32)])
    chg_b = jnp.concatenate([one, (e_b[1:] != e_b[:-1]).astype(I32)])
    return u_tile, u_lo, u_hi, e_a, e_b, chg_a, chg_b


def _moe_layer(x1, hp, logits, rw_t, router_b, wg, wu, wd, layer, mod3, final_w, final):
    dest, ends = _route_tokens(logits, router_b)
    dest = dest.reshape(N)
    units = _moe_units(ends)
    xs = _dispatch(dest, hp)
    ys = _moe_sorted(units, xs, rw_t, wg, wu, wd, layer)
    return _combine(dest, x1, ys, mod3, final_w, final).reshape(B, T, D)


def _rope_tables():
    half = 16
    inv = ROPE_BASE ** (-jnp.arange(half, dtype=F32) / half)
    t = jnp.arange(T)
    row = (t // GRID_W).astype(F32)
    col = (t % GRID_W).astype(F32)
    ang_r = row[:, None] * inv[None, :]
    ang_c = col[:, None] * inv[None, :]
    ang = jnp.concatenate([ang_r, ang_r, ang_c, ang_c], axis=1)
    sign = jnp.tile(jnp.concatenate([-jnp.ones(half, F32), jnp.ones(half, F32)]), 2)
    cos = jnp.cos(ang)
    sin = jnp.sin(ang) * sign[None, :]
    cos = jnp.concatenate([jnp.ones((LC, 64), F32), cos], axis=0)
    sin = jnp.concatenate([jnp.zeros((LC, 64), F32), sin], axis=0)
    return jnp.tile(cos, (1, 2)), jnp.tile(sin, (1, 2))


def _permute_w_in(w):
    rq = w[:, 0:256].reshape(D, NH, DK)
    dq = w[:, 256:768]
    rg = w[:, 768:1280]
    rk = w[:, 1280:1536].reshape(D, NH, DK)
    rv = w[:, 1536:2048]
    dk = w[:, 2048:2560]
    dv = w[:, 2560:3072]
    qk = jnp.concatenate([rq, rk * (DK ** -0.5)], axis=2).reshape(D, NH * 2 * DK)
    return jnp.concatenate([qk, rv, rg, dq * (DK ** -0.5 * math.log2(math.e)), dk, dv], axis=1).astype(BF16)


def kernel(x, c, ctx, c_ctx, ada_w, ada_b, norm_mix_w, norm_ffn_w, w_in, w_out, ret_log_decay, diff_lambda,
           diff_subln_w, pool_w, pool_scale, router_w, router_b, moe_w_gate, moe_w_up, moe_w_down, final_norm_w):
    assert x.shape == (B, T, D) and ctx.shape == (B, LC, D) and ada_w.shape[0] == 2
    cc = jnp.concatenate([c, c_ctx[None, :], jnp.zeros((16 - B - 1, D), F32)], axis=0)
    mod = _ada_mod(cc, ada_w, ada_b)
    rw_t = router_w.T
    fw = final_norm_w.reshape(1, D)
    experts = (moe_w_gate, moe_w_up, moe_w_down)

    mod0 = mod[0].reshape(16, 1, 6 * D)
    cos_t, sin_t = _rope_tables()
    proj = _inproj(x, ctx, mod0[:, :, :2 * D], norm_mix_w[0:1], _permute_w_in(w_in[0]), cos_t, sin_t)
    ret = _retention(proj, ret_log_decay[0])
    lam_init = 0.8 - 0.6 * math.exp(-0.3 * 0)
    lv = diff_lambda[0]
    lam = jnp.exp(jnp.sum(lv[0] * lv[1])) - jnp.exp(jnp.sum(lv[2] * lv[3])) + lam_init
    dif = _diffattn(proj, lam.reshape(1), diff_subln_w[0:1], 1.0 - lam_init)
    x1, hp, logits = _outproj(ret, dif, w_out[0].astype(BF16), x, mod0, norm_ffn_w[0:1], rw_t)
    x2 = _moe_layer(x1, hp, logits, rw_t, router_b, *experts, 0, mod0, fw, False)

    mod1 = mod[1].reshape(16, 1, 6 * D)
    x3, hp, logits = _pool_layer(x2, mod1, norm_mix_w[1:2], pool_w[0].astype(BF16), pool_scale[0:1],
                                 norm_ffn_w[1:2], rw_t)
    return _moe_layer(x3, hp, logits, rw_t, router_b, *experts, 1, mod1, fw, True)
```

```python
import functools
import math

import jax
import jax.numpy as jnp
from jax import lax
from jax.experimental import pallas as pl
from jax.experimental.pallas import tpu as pltpu

F32 = jnp.float32
BF16 = jnp.bfloat16
I32 = jnp.int32

D = 1024
B = 8
T = 2048
N = B * T
GRID_W = 64
LC = 256
EPS = 1e-6
ROPE_BASE = 10000.0
NH = 4
DK = 64
HV = 128
CH = 256
RB = LC + T
NCH = RB // CH
POOL_WINDOWS = (2, 4, 8, 16)
PG = D // len(POOL_WINDOWS)
NE = 16
NGRP = 4
EPG = NE // NGRP
DE = 512
IN_W = 3072
HALO = 8

PAIR_A = (0, 0, 0, 1, 1, 3)
PAIR_B = (1, 2, 3, 3, 2, 2)
NCLS = NGRP * len(PAIR_A)
CLS_PAD = 32
SLAB = D // 128

TM_PROJ = 256
TM_OUT = 512
TQ_SUB = 4
SCORE_AHEAD = 1
TM_POOL = 512
TM_MOE = 256
N_UNITS = N // TM_MOE + NCLS
TM_PERM = 2048
TM_COMB = 512
PERM_UNROLL = 16
VMEM_LIMIT = 56 * 1024 * 1024


def _cparams(sem):
    return pltpu.CompilerParams(dimension_semantics=sem, vmem_limit_bytes=VMEM_LIMIT)


def _sigmoid(x):
    return 1.0 / (1.0 + jnp.exp(-x))


def _silu(x):
    return x * _sigmoid(x)


def _rms(x):
    return x * lax.rsqrt(jnp.mean(x * x, axis=-1, keepdims=True) + EPS)


def _dot_3pass(a, b, dims):
    a_hi = a.astype(BF16)
    b_hi = b.astype(BF16)
    a_lo = (a - a_hi.astype(F32)).astype(BF16)
    b_lo = (b - b_hi.astype(F32)).astype(BF16)

    def dot(x, y):
        return lax.dot_general(x, y, dims, preferred_element_type=F32)

    return dot(a_hi, b_hi) + (dot(a_lo, b_hi) + dot(a_hi, b_lo))


def _load_slabs(ref, rows):
    return jnp.concatenate([ref[pl.ds(s, rows, stride=SLAB), :] for s in range(SLAB)], axis=1)


def _store_slabs(ref, val, row0=0):
    rows = val.shape[0]
    for s in range(SLAB):
        ref[pl.ds(row0 * SLAB + s, rows, stride=SLAB), :] = val[:, s * 128:(s + 1) * 128]


def _ada_kernel(cc_ref, w_ref, b_ref, o_ref):
    s = _silu(cc_ref[...])
    o_ref[0] = _dot_3pass(s, w_ref[0], (((1,), (0,)), ((), ()))) + b_ref[0]


def _ada_mod(cc, ada_w, ada_b):
    depth = ada_w.shape[0]
    tn = 1536
    return pl.pallas_call(
        _ada_kernel,
        grid=(depth, 6 * D // tn),
        in_specs=[
            pl.BlockSpec((16, D), lambda l, n: (0, 0)),
            pl.BlockSpec((1, D, tn), lambda l, n: (l, 0, n)),
            pl.BlockSpec((1, 1, tn), lambda l, n: (l, 0, n)),
        ],
        out_specs=pl.BlockSpec((1, 16, tn), lambda l, n: (l, 0, n)),
        out_shape=jax.ShapeDtypeStruct((depth, 16, 6 * D), F32),
        compiler_params=_cparams(("arbitrary", "arbitrary")),
        name="ada_mod",
    )(cc, ada_w, ada_b.reshape(depth, 1, 6 * D))


def _rope(seg, cos, sin_signed, lo_mask):
    w = seg.shape[1]
    from_hi = pltpu.roll(seg, w - 16, axis=1)
    from_lo = pltpu.roll(seg, 16, axis=1)
    partner = jnp.where(lo_mask, from_hi, from_lo)
    reps = w // cos.shape[1]
    c = jnp.concatenate([cos] * reps, axis=1)
    s = jnp.concatenate([sin_signed] * reps, axis=1)
    return seg * c + partner * s


def _inproj_kernel(x_ref, c_ref, mod_ref, nw_ref, w_ref, cos_ref, sin_ref, o_ref):
    j = pl.program_id(1)
    xt = jnp.where(j == 0, c_ref[0], x_ref[0])
    sh = mod_ref[0, :, 0:D]
    sc = mod_ref[0, :, D:2 * D]
    h = (_rms(xt) * nw_ref[...]) * (1.0 + sc) + sh
    hb = h.astype(BF16)
    lane = lax.broadcasted_iota(I32, (TM_PROJ, 512), 1)
    lo_mask = (lane % 32) < 16
    cos = cos_ref[...]
    sin = sin_ref[...]

    def project(g):
        return jnp.dot(hb, w_ref[:, g * 512:(g + 1) * 512], preferred_element_type=F32)

    seg = project(0)
    for g in range(6):
        seg_next = project(g + 1) if g + 1 < 6 else None
        if g in (0, 3, 4):
            seg = _rope(seg, cos, sin, lo_mask)
        o_ref[0, :, g * 512:(g + 1) * 512] = seg.astype(BF16)
        seg = seg_next


def _inproj(x, ctx, mod3, norm_w, w_perm, cos_t, sin_t):
    nj = RB // TM_PROJ
    return pl.pallas_call(
        _inproj_kernel,
        grid=(B, nj),
        in_specs=[
            pl.BlockSpec((1, TM_PROJ, D), lambda b, j: (b, jnp.maximum(j - 1, 0), 0)),
            pl.BlockSpec((1, LC, D), lambda b, j: (b, 0, 0)),
            pl.BlockSpec((1, 1, 2 * D), lambda b, j: (jnp.where(j == 0, B, b), 0, 0)),
            pl.BlockSpec((1, D), lambda b, j: (0, 0)),
            pl.BlockSpec((D, IN_W), lambda b, j: (0, 0)),
            pl.BlockSpec((TM_PROJ, 128), lambda b, j: (j, 0)),
            pl.BlockSpec((TM_PROJ, 128), lambda b, j: (j, 0)),
        ],
        out_specs=pl.BlockSpec((1, TM_PROJ, IN_W), lambda b, j: (b, j, 0)),
        out_shape=jax.ShapeDtypeStruct((B, RB, IN_W), BF16),
        compiler_params=_cparams(("arbitrary", "arbitrary")),
        name="inproj",
    )(x, ctx, mod3, norm_w, w_perm, cos_t, sin_t)


def _retention_kernel(ld_ref, qk_ref, v_ref, g_ref, o_ref, st_ref):
    h = pl.program_id(1)
    lgf = ld_ref[0, h]
    lgb = ld_ref[1, h]
    lane = lax.broadcasted_iota(I32, (CH, 128), 1)
    fwd_lane = lane < DK
    pos = lax.broadcasted_iota(I32, (CH, 128), 0).astype(F32)
    kdec = jnp.where(fwd_lane, jnp.exp(lgf * (CH - 1 - pos)), jnp.exp(lgb * pos))
    qdec = jnp.where(fwd_lane, jnp.exp(lgf * (pos + 1.0)), jnp.exp(lgb * (CH - pos)))
    ii = lax.broadcasted_iota(I32, (CH, CH), 0)
    jj = lax.broadcasted_iota(I32, (CH, CH), 1)
    gap = (ii - jj).astype(F32)
    mask = (jnp.where(gap >= 0, jnp.exp(lgf * jnp.maximum(gap, 0.0)), 0.0)
            + jnp.where(gap <= 0, jnp.exp(lgb * jnp.maximum(-gap, 0.0)), 0.0))
    ones = jnp.ones((DK, 128), F32)
    cf = jnp.exp(lgf * CH * ones)
    cb = jnp.exp(lgb * CH * ones)

    def chunk(n):
        a = qk_ref[0, n * CH:(n + 1) * CH, :].astype(F32)
        swapped = pltpu.roll(a, DK, axis=1)
        return a, swapped

    kv = []
    for n in range(NCH):
        a, swapped = chunk(n)
        kk = jnp.where(fwd_lane, swapped, a)
        kb = (kk * kdec).astype(BF16)
        vn = v_ref[0, n * CH:(n + 1) * CH, :]
        kv.append(lax.dot_general(kb, vn, (((0,), (0,)), ((), ())), preferred_element_type=F32))
    sf = kv[0][:DK]
    for n in range(1, NCH):
        st_ref[n, 0:DK, :] = sf
        sf = cf * sf + kv[n][:DK]
    sb = kv[0][DK:]
    for n in range(NCH - 1, 0, -1):
        st_ref[n, DK:2 * DK, :] = sb
        sb = cb * sb + kv[n][DK:]

    for n in range(1, NCH):
        a, swapped = chunk(n)
        q = a[:, :DK].astype(BF16)
        k = swapped[:, :DK].astype(BF16)
        scores = lax.dot_general(q, k, (((1,), (1,)), ((), ())), preferred_element_type=F32)
        p = (scores * mask).astype(BF16)
        vn = v_ref[0, n * CH:(n + 1) * CH, :]
        qq = jnp.where(fwd_lane, a, swapped)
        qd = (qq * qdec).astype(BF16)
        o = (jnp.dot(p, vn, preferred_element_type=F32)
             + jnp.dot(qd, st_ref[n].astype(BF16), preferred_element_type=F32))
        gate = g_ref[0, n * CH:(n + 1) * CH, :].astype(F32)
        o_ref[0, (n - 1) * CH:n * CH, :] = (_rms(o) * _silu(gate)).astype(BF16)


def _retention(proj, log_decay):
    return pl.pallas_call(
        _retention_kernel,
        grid=(B, NH),
        in_specs=[
            pl.BlockSpec(memory_space=pltpu.SMEM),
            pl.BlockSpec((1, RB, 128), lambda b, h: (b, 0, h)),
            pl.BlockSpec((1, RB, 128), lambda b, h: (b, 0, NH + h)),
            pl.BlockSpec((1, RB, 128), lambda b, h: (b, 0, 2 * NH + h)),
        ],
        out_specs=pl.BlockSpec((1, T, 128), lambda b, h: (b, 0, h)),
        out_shape=jax.ShapeDtypeStruct((B, T, NH * HV), BF16),
        scratch_shapes=[pltpu.VMEM((NCH, 128, 128), F32)],
        compiler_params=_cparams(("arbitrary", "arbitrary")),
        name="retention",
    )(log_decay, proj, proj, proj)


def _diffattn_kernel(lam_ref, *refs, out_scale):
    q_refs = refs[:TQ_SUB]
    k_ref, v_ref, sw_ref, o_ref = refs[TQ_SUB:]
    lam = lam_ref[0]
    k = k_ref[0]
    v = v_ref[0]
    nt = (((1,), (1,)), ((), ()))

    def scores(qh):
        return lax.dot_general(qh, k, nt, preferred_element_type=F32)

    def values(s):
        e = jnp.exp2(s - jnp.max(s, axis=-1, keepdims=True))
        return jnp.dot(e.astype(BF16), v, preferred_element_type=F32), jnp.sum(e, axis=-1, keepdims=True)

    halves = []
    for i in range(TQ_SUB):
        q = q_refs[i][0]
        lane = lax.broadcasted_iota(I32, q.shape, 1)
        zero = jnp.zeros_like(q)
        halves += [jnp.where(lane < DK, q, zero), jnp.where(lane >= DK, q, zero)]
    outs = []
    ahead = [scores(h) for h in halves[:SCORE_AHEAD]]
    for c in range(len(halves)):
        if c + SCORE_AHEAD < len(halves):
            ahead.append(scores(halves[c + SCORE_AHEAD]))
        outs.append(values(ahead.pop(0)))
    for i in range(TQ_SUB):
        (o1, l1), (o2, l2) = outs[2 * i], outs[2 * i + 1]
        o = o1 / l1 - o2 * (lam / l2)
        o_ref[0, i * TM_PROJ:(i + 1) * TM_PROJ, :] = (_rms(o) * sw_ref[...] * out_scale).astype(BF16)


def _diffattn(proj, lam, subln_w, out_scale):
    tq = TQ_SUB * TM_PROJ
    nq = T // tq

    def q_map(i, b, h, j):
        return (b, LC // TM_PROJ + j * TQ_SUB + i, 3 * NH + h)

    return pl.pallas_call(
        functools.partial(_diffattn_kernel, out_scale=out_scale),
        grid=(B, NH, nq),
        in_specs=[
            pl.BlockSpec(memory_space=pltpu.SMEM),
            *[pl.BlockSpec((1, TM_PROJ, 128), functools.partial(q_map, i)) for i in range(TQ_SUB)],
            pl.BlockSpec((1, RB, 128), lambda b, h, j: (b, 0, 4 * NH + h)),
            pl.BlockSpec((1, RB, 128), lambda b, h, j: (b, 0, 5 * NH + h)),
            pl.BlockSpec((1, HV), lambda b, h, j: (0, 0)),
        ],
        out_specs=pl.BlockSpec((1, tq, 128), lambda b, h, j: (b, j, h)),
        out_shape=jax.ShapeDtypeStruct((B, T, NH * HV), BF16),
        compiler_params=_cparams(("arbitrary", "arbitrary", "arbitrary")),
        name="diffattn",
    )(lam, *([proj] * TQ_SUB), proj, proj, subln_w)


def _route(bz):
    grp = []
    for g in range(NGRP):
        m = bz[g * EPG:(g + 1) * EPG]
        best = None
        for i in range(EPG):
            for k in range(i + 1, EPG):
                pair = m[i] + m[k]
                best = pair if best is None else jnp.maximum(best, pair)
        grp.append(best)
    gbest = grp[0]
    gsel = jnp.zeros_like(gbest, dtype=I32)
    for g in range(1, NGRP):
        better = grp[g] > gbest
        gsel = jnp.where(better, g, gsel)
        gbest = jnp.where(better, grp[g], gbest)
    cb = [bz[i] for i in range(EPG)]
    for g in range(1, NGRP):
        pick = gsel == g
        cb = [jnp.where(pick, bz[g * EPG + i], cb[i]) for i in range(EPG)]
    i1 = jnp.zeros_like(gsel)
    b1 = cb[0]
    for i in range(1, EPG):
        better = cb[i] > b1
        i1 = jnp.where(better, i, i1)
        b1 = jnp.where(better, cb[i], b1)
    neg = jnp.full_like(b1, -jnp.inf)
    rest = [jnp.where(i1 == i, neg, cb[i]) for i in range(EPG)]
    i2 = jnp.zeros_like(gsel)
    b2 = rest[0]
    for i in range(1, EPG):
        better = rest[i] > b2
        i2 = jnp.where(better, i, i2)
        b2 = jnp.where(better, rest[i], b2)
    lo = jnp.minimum(i1, i2)
    hi = jnp.maximum(i1, i2)
    code = lo * EPG + hi
    pair = jnp.full_like(gsel, len(PAIR_A) - 1)
    for p in range(len(PAIR_A) - 1):
        a, b = min(PAIR_A[p], PAIR_B[p]), max(PAIR_A[p], PAIR_B[p])
        pair = jnp.where(code == a * EPG + b, p, pair)
    return gsel * len(PAIR_A) + pair


def _ffn_prologue(x1, row0, mod, nfw_ref, rw_ref, hp_ref, logit_ref):
    rows = x1.shape[0]
    sh2 = mod[:, 3 * D:4 * D]
    sc2 = mod[:, 4 * D:5 * D]
    h2 = (_rms(x1) * nfw_ref[...]) * (1.0 + sc2) + sh2
    _store_slabs(hp_ref, h2, row0)
    logit_ref[:, row0:row0 + rows] = _dot_3pass(rw_ref[...], h2, (((1,), (1,)), ((), ())))


def _ffn_out_specs(tm, n_tiles_per_b):
    specs = [
        pl.BlockSpec((1, tm, D), lambda b, j: (b, j, 0)),
        pl.BlockSpec((tm * SLAB, 128), lambda b, j: (b * n_tiles_per_b + j, 0)),
        pl.BlockSpec((NE, tm), lambda b, j: (0, b * n_tiles_per_b + j)),
    ]
    shapes = [
        jax.ShapeDtypeStruct((B, T, D), F32),
        jax.ShapeDtypeStruct((N * SLAB, 128), F32),
        jax.ShapeDtypeStruct((NE, N), F32),
    ]
    return specs, shapes


def _route_kernel(logit_ref, bias_ref, dest_ref, ends_ref):
    r = logit_ref.shape[1]
    cls = _route([_sigmoid(logit_ref[e]) + bias_ref[e] for e in range(NE)])
    lane_incl = (lax.broadcasted_iota(I32, (128, 128), 0) <= lax.broadcasted_iota(I32, (128, 128), 1)).astype(BF16)
    rows_before = (lax.broadcasted_iota(I32, (r, r), 1) < lax.broadcasted_iota(I32, (r, r), 0)).astype(BF16)
    dest = jnp.zeros((r, 128), F32)
    start = jnp.zeros((1, 128), F32)
    ends = []
    for c in range(NCLS):
        onehot = jnp.where(cls == c, 1.0, 0.0)
        in_row = jnp.dot(onehot.astype(BF16), lane_incl, preferred_element_type=F32)
        row_tot = jnp.broadcast_to(in_row[:, 127:128], (r, 128))
        above = jnp.dot(rows_before, row_tot.astype(BF16), preferred_element_type=F32)
        dest = dest + onehot * (start + above + in_row - 1.0)
        start = start + jnp.sum(row_tot, axis=0, keepdims=True)
        ends.append(start)
    dest_ref[...] = dest.astype(I32)
    ends_ref[...] = jnp.concatenate(ends + [jnp.zeros((CLS_PAD - NCLS, 128), F32)], axis=0)


def _route_tokens(logits_t, router_b):
    r = N // 128
    return pl.pallas_call(
        _route_kernel,
        in_specs=[pl.BlockSpec((NE, r, 128), lambda: (0, 0, 0)), pl.BlockSpec(memory_space=pltpu.SMEM)],
        out_specs=[pl.BlockSpec((r, 128), lambda: (0, 0)), pl.BlockSpec((CLS_PAD, 128), lambda: (0, 0))],
        out_shape=[jax.ShapeDtypeStruct((r, 128), I32), jax.ShapeDtypeStruct((CLS_PAD, 128), F32)],
        compiler_params=pltpu.CompilerParams(vmem_limit_bytes=VMEM_LIMIT),
        name="route",
    )(logits_t.reshape(NE, r, 128), router_b)


def _outproj_kernel(ret_ref, dif_ref, w_ref, x_ref, mod_ref, nfw_ref, rw_ref, x1_ref, hp_ref, logit_ref):
    mod = mod_ref[0]

    mx = (jnp.dot(ret_ref[0], w_ref[0:NH * HV, :], preferred_element_type=F32)
          + jnp.dot(dif_ref[0], w_ref[NH * HV:, :], preferred_element_type=F32))
    x1 = x_ref[0] + mod[:, 2 * D:3 * D] * mx
    x1_ref[0] = x1
    _ffn_prologue(x1, 0, mod, nfw_ref, rw_ref, hp_ref, logit_ref)


def _outproj(ret, dif, w_out, x, mod3, nfw, rw_t):
    nj = T // TM_OUT
    out_specs, out_shapes = _ffn_out_specs(TM_OUT, nj)
    return pl.pallas_call(
        _outproj_kernel,
        grid=(B, nj),
        in_specs=[
            pl.BlockSpec((1, TM_OUT, NH * HV), lambda b, j: (b, j, 0)),
            pl.BlockSpec((1, TM_OUT, NH * HV), lambda b, j: (b, j, 0)),
            pl.BlockSpec((2 * NH * HV, D), lambda b, j: (0, 0)),
            pl.BlockSpec((1, TM_OUT, D), lambda b, j: (b, j, 0)),
            pl.BlockSpec((1, 1, 6 * D), lambda b, j: (b, 0, 0)),
            pl.BlockSpec((1, D), lambda b, j: (0, 0)),
            pl.BlockSpec((NE, D), lambda b, j: (0, 0)),
        ],
        out_specs=out_specs,
        out_shape=out_shapes,
        compiler_params=_cparams(("arbitrary", "arbitrary")),
        name="outproj",
    )(ret, dif, w_out, x, mod3, nfw, rw_t)


def _pool_kernel(x_ref, prev_ref, next_ref, mod_ref, nmw_ref, pw_ref, ps_ref, nfw_ref, rw_ref,
                 x1_ref, hp_ref, logit_ref, ext_ref):
    i = pl.program_id(1)
    last = pl.num_programs(1) - 1
    mod = mod_ref[0]
    sh1 = mod[:, 0:D]
    sc1 = mod[:, D:2 * D]

    def modnorm(v):
        return (_rms(v) * nmw_ref[...]) * (1.0 + sc1) + sh1

    x = x_ref[0]
    hc = modnorm(x)
    ext_ref[0:HALO, :] = jnp.where(i > 0, modnorm(prev_ref[0]), 0.0)
    ext_ref[HALO:HALO + TM_POOL, :] = hc
    ext_ref[HALO + TM_POOL:, :] = jnp.where(i < last, modnorm(next_ref[0]), 0.0)
    pos = i * TM_POOL + lax.broadcasted_iota(I32, (TM_POOL, 1), 0)
    mixed = []
    for gi, w in enumerate(POOL_WINDOWS):
        left = w // 2
        right = w - 1 - left
        cols = slice(gi * PG, (gi + 1) * PG)
        tot = None
        for d in range(-left, right + 1):
            part = ext_ref[HALO + d:HALO + d + TM_POOL, cols]
            tot = part if tot is None else tot + part
        cnt = (jnp.minimum(pos + right + 1, T) - jnp.maximum(pos - left, 0)).astype(F32)
        pooled = (tot / cnt - hc[:, cols]).astype(BF16)
        mixed.append(jnp.dot(pooled, pw_ref[gi], preferred_element_type=F32))
    mixed = jnp.concatenate(mixed, axis=1) * ps_ref[...]
    x1 = x + mod[:, 2 * D:3 * D] * mixed
    x1_ref[0] = x1
    _ffn_prologue(x1, 0, mod, nfw_ref, rw_ref, hp_ref, logit_ref)


def _pool_layer(x, mod3, nmw, pool_w, pool_scale, nfw, rw_t):
    ni = T // TM_POOL
    hb = TM_POOL // HALO
    out_specs, out_shapes = _ffn_out_specs(TM_POOL, ni)
    return pl.pallas_call(
        _pool_kernel,
        grid=(B, ni),
        in_specs=[
            pl.BlockSpec((1, TM_POOL, D), lambda b, i: (b, i, 0)),
            pl.BlockSpec((1, HALO, D), lambda b, i: (b, jnp.maximum(i * hb - 1, 0), 0)),
            pl.BlockSpec((1, HALO, D), lambda b, i: (b, jnp.minimum((i + 1) * hb, T // HALO - 1), 0)),
            pl.BlockSpec((1, 1, 6 * D), lambda b, i: (b, 0, 0)),
            pl.BlockSpec((1, D), lambda b, i: (0, 0)),
            pl.BlockSpec((len(POOL_WINDOWS), PG, PG), lambda b, i: (0, 0, 0)),
            pl.BlockSpec((1, D), lambda b, i: (0, 0)),
            pl.BlockSpec((1, D), lambda b, i: (0, 0)),
            pl.BlockSpec((NE, D), lambda b, i: (0, 0)),
        ],
        out_specs=out_specs,
        out_shape=out_shapes,
        scratch_shapes=[pltpu.VMEM((TM_POOL + 2 * HALO, D), F32)],
        compiler_params=_cparams(("arbitrary", "arbitrary")),
        name="pool_layer",
    )(x, x, x, mod3, nmw, pool_w, pool_scale, nfw, rw_t)


def _tile_copy(src_ref, dst_ref, sem, s, d, rows=1):
    s0 = pl.multiple_of(s * SLAB, SLAB)
    d0 = pl.multiple_of(d * SLAB, SLAB)
    return pltpu.make_async_copy(src_ref.at[pl.ds(s0, rows * SLAB)], dst_ref.at[pl.ds(d0, rows * SLAB)], sem)


def _issue_tile_copies(idx_ref, base, rows, start_one):
    def group(g, carry):
        r0 = g * PERM_UNROLL
        ids = [idx_ref[base + r0 + u] for u in range(PERM_UNROLL)]
        for u in range(PERM_UNROLL):
            start_one(r0 + u, ids[u], u % 2)
        return carry

    lax.fori_loop(0, rows // PERM_UNROLL, group, 0)


def _dispatch_kernel(dest_ref, src_ref, dst_ref, sem):
    base = pl.program_id(0) * TM_PERM

    def start_one(r, d, priority):
        _tile_copy(src_ref, dst_ref, sem, r, d).start(priority=priority)

    _issue_tile_copies(dest_ref, base, TM_PERM, start_one)
    _tile_copy(src_ref, dst_ref, sem, 0, 0, TM_PERM).wait()


def _dispatch(dest, src):
    return pl.pallas_call(
        _dispatch_kernel,
        grid_spec=pltpu.PrefetchScalarGridSpec(
            num_scalar_prefetch=1,
            grid=(N // TM_PERM,),
            in_specs=[pl.BlockSpec((TM_PERM * SLAB, 128), lambda i, dest: (i, 0))],
            out_specs=pl.BlockSpec(memory_space=pl.ANY),
            scratch_shapes=[pltpu.SemaphoreType.DMA(())],
        ),
        out_shape=jax.ShapeDtypeStruct(src.shape, src.dtype),
        compiler_params=_cparams(("arbitrary",)),
        name="dispatch",
    )(dest, src)


def _combine_kernel(dest_ref, x_ref, ys_ref, mod_ref, fw_ref, o_ref, ybuf_ref, sem, *, final):
    i = pl.program_id(0)
    n = pl.num_programs(0)
    slot = i % 2

    def gather(step, to_slot):
        def start_one(r, d, priority):
            _tile_copy(ys_ref, ybuf_ref.at[to_slot], sem.at[to_slot], d, r).start(priority=priority)

        _issue_tile_copies(dest_ref, step * TM_COMB, TM_COMB, start_one)

    @pl.when(i == 0)
    def _():
        gather(0, 0)

    @pl.when(i + 1 < n)
    def _():
        gather(i + 1, 1 - slot)

    _tile_copy(ys_ref, ybuf_ref.at[slot], sem.at[slot], 0, 0, TM_COMB).wait()
    out = x_ref[...] + mod_ref[0][:, 5 * D:6 * D] * _load_slabs(ybuf_ref.at[slot], TM_COMB)
    if final:
        out = _rms(out) * fw_ref[...]
    o_ref[...] = out


def _combine(dest, x1, ys, mod3, final_w, final):
    per_b = T // TM_COMB
    return pl.pallas_call(
        functools.partial(_combine_kernel, final=final),
        grid_spec=pltpu.PrefetchScalarGridSpec(
            num_scalar_prefetch=1,
            grid=(N // TM_COMB,),
            in_specs=[
                pl.BlockSpec((TM_COMB, D), lambda i, dest: (i, 0)),
                pl.BlockSpec(memory_space=pl.ANY),
                pl.BlockSpec((1, 1, 6 * D), lambda i, dest: (i // per_b, 0, 0)),
                pl.BlockSpec((1, D), lambda i, dest: (0, 0)),
            ],
            out_specs=pl.BlockSpec((TM_COMB, D), lambda i, dest: (i, 0)),
            scratch_shapes=[pltpu.VMEM((2, TM_COMB * SLAB, 128), F32), pltpu.SemaphoreType.DMA((2,))],
        ),
        out_shape=jax.ShapeDtypeStruct((N, D), F32),
        compiler_params=_cparams(("arbitrary",)),
        name="combine",
    )(dest, x1.reshape(N, D), ys, mod3, final_w)


def _moe_kernel(tile_ref, lo_ref, hi_ref, ea_ref, eb_ref, ca_ref, cb_ref,
                x_ref, rw_ref, wga_ref, wua_ref, wda_ref, wgb_ref, wub_ref, wdb_ref, o_ref,
                ga_ref, ua_ref, da_ref, gb_ref, ub_ref, db_ref):
    m = pl.program_id(0)
    lo = lo_ref[m]
    hi = hi_ref[m]
    tile0 = tile_ref[m] * TM_MOE

    @pl.when(ca_ref[m] == 1)
    def _():
        ga_ref[...] = wga_ref[0, 0].astype(BF16)
        ua_ref[...] = wua_ref[0, 0].astype(BF16)
        da_ref[...] = wda_ref[0, 0].astype(BF16)

    @pl.when(cb_ref[m] == 1)
    def _():
        gb_ref[...] = wgb_ref[0, 0].astype(BF16)
        ub_ref[...] = wub_ref[0, 0].astype(BF16)
        db_ref[...] = wdb_ref[0, 0].astype(BF16)

    def value():
        hf = _load_slabs(x_ref, TM_MOE)
        h = hf.astype(BF16)

        def up(w_ref):
            return jnp.dot(h, w_ref[...], preferred_element_type=F32)

        def down(g, u, d_ref):
            return jnp.dot((_silu(g) * u).astype(BF16), d_ref[...], preferred_element_type=F32)

        g_a, u_a, g_b, u_b = up(ga_ref), up(ua_ref), up(gb_ref), up(ub_ref)
        y_a = down(g_a, u_a, da_ref)
        y_b = down(g_b, u_b, db_ref)
        s_a = _sigmoid(jnp.sum(hf * rw_ref[pl.ds(ea_ref[m], 1), :], axis=1, keepdims=True))
        s_b = _sigmoid(jnp.sum(hf * rw_ref[pl.ds(eb_ref[m], 1), :], axis=1, keepdims=True))
        denom = s_a + s_b
        val = (s_a / denom) * y_a + (s_b / denom) * y_b
        row = tile0 + lax.broadcasted_iota(I32, (TM_MOE, 1), 0)
        return val, (row >= lo) & (row < hi)

    @pl.when((hi > lo) & (lo == tile0))
    def _():
        val, seg = value()
        _store_slabs(o_ref, jnp.where(seg, val, 0.0))

    @pl.when((hi > lo) & (lo != tile0))
    def _():
        val, seg = value()
        _store_slabs(o_ref, jnp.where(seg, val, _load_slabs(o_ref, TM_MOE)))


def _moe_sorted(units, xs, rw_t, wg, wu, wd, layer):
    def wspec(which, shape):
        return pl.BlockSpec((1, 1) + shape,
                            lambda m, t, lo, hi, ea, eb, ca, cb: (layer, (ea, eb)[which][m], 0, 0))

    row_map = lambda m, t, lo, hi, ea, eb, ca, cb: (t[m], 0)
    return pl.pallas_call(
        _moe_kernel,
        grid_spec=pltpu.PrefetchScalarGridSpec(
            num_scalar_prefetch=7,
            grid=(N_UNITS,),
            in_specs=[
                pl.BlockSpec((TM_MOE * SLAB, 128), row_map),
                pl.BlockSpec((NE, D), lambda m, t, lo, hi, ea, eb, ca, cb: (0, 0)),
                wspec(0, (D, DE)), wspec(0, (D, DE)), wspec(0, (DE, D)),
                wspec(1, (D, DE)), wspec(1, (D, DE)), wspec(1, (DE, D)),
            ],
            out_specs=pl.BlockSpec((TM_MOE * SLAB, 128), row_map),
            scratch_shapes=[pltpu.VMEM((D, DE), BF16), pltpu.VMEM((D, DE), BF16), pltpu.VMEM((DE, D), BF16),
                            pltpu.VMEM((D, DE), BF16), pltpu.VMEM((D, DE), BF16), pltpu.VMEM((DE, D), BF16)],
        ),
        out_shape=jax.ShapeDtypeStruct((N * SLAB, 128), F32),
        compiler_params=_cparams(("arbitrary",)),
        name="moe_sorted",
    )(*units, xs, rw_t, wg, wu, wd, wg, wu, wd)


def _moe_units(ends_f):
    ends = ends_f[:NCLS, 0].astype(I32)
    offs = jnp.concatenate([jnp.zeros((1,), I32), ends[:-1]])
    starts = jnp.sort(jnp.concatenate([jnp.arange(N // TM_MOE, dtype=I32) * TM_MOE, offs]))
    u_lo = starts
    u_hi = jnp.concatenate([starts[1:], jnp.full((1,), N, I32)])
    u_tile = jnp.minimum(u_lo // TM_MOE, N // TM_MOE - 1)
    u_cls = jnp.minimum(jnp.sum(ends[None, :] <= u_lo[:, None], axis=1), NCLS - 1).astype(I32)
    grp = u_cls // len(PAIR_A)
    pair = u_cls % len(PAIR_A)
    e_a = grp * EPG + jnp.take(jnp.array(PAIR_A, I32), pair)
    e_b = grp * EPG + jnp.take(jnp.array(PAIR_B, I32), pair)
    one = jnp.ones((1,), I32)
    chg_a = jnp.concatenate([one, (e_a[1:] != e_a[:-1]).astype(I32)])
    chg_b = jnp.concatenate([one, (e_b[1:] != e_b[:-1]).astype(I32)])
    return u_tile, u_lo, u_hi, e_a, e_b, chg_a, chg_b


def _moe_layer(x1, hp, logits, rw_t, router_b, wg, wu, wd, layer, mod3, final_w, final):
    dest, ends = _route_tokens(logits, router_b)
    dest = dest.reshape(N)
    units = _moe_units(ends)
    xs = _dispatch(dest, hp)
    ys = _moe_sorted(units, xs, rw_t, wg, wu, wd, layer)
    return _combine(dest, x1, ys, mod3, final_w, final).reshape(B, T, D)


def _rope_tables():
    half = 16
    inv = ROPE_BASE ** (-jnp.arange(half, dtype=F32) / half)
    t = jnp.arange(T)
    row = (t // GRID_W).astype(F32)
    col = (t % GRID_W).astype(F32)
    ang_r = row[:, None] * inv[None, :]
    ang_c = col[:, None] * inv[None, :]
    ang = jnp.concatenate([ang_r, ang_r, ang_c, ang_c], axis=1)
    sign = jnp.tile(jnp.concatenate([-jnp.ones(half, F32), jnp.ones(half, F32)]), 2)
    cos = jnp.cos(ang)
    sin = jnp.sin(ang) * sign[None, :]
    cos = jnp.concatenate([jnp.ones((LC, 64), F32), cos], axis=0)
    sin = jnp.concatenate([jnp.zeros((LC, 64), F32), sin], axis=0)
    return jnp.tile(cos, (1, 2)), jnp.tile(sin, (1, 2))


def _permute_w_in(w):
    rq = w[:, 0:256].reshape(D, NH, DK)
    dq = w[:, 256:768]
    rg = w[:, 768:1280]
    rk = w[:, 1280:1536].reshape(D, NH, DK)
    rv = w[:, 1536:2048]
    dk = w[:, 2048:2560]
    dv = w[:, 2560:3072]
    qk = jnp.concatenate([rq, rk * (DK ** -0.5)], axis=2).reshape(D, NH * 2 * DK)
    return jnp.concatenate([qk, rv, rg, dq * (DK ** -0.5 * math.log2(math.e)), dk, dv], axis=1).astype(BF16)


def kernel(x, c, ctx, c_ctx, ada_w, ada_b, norm_mix_w, norm_ffn_w, w_in, w_out, ret_log_decay, diff_lambda,
           diff_subln_w, pool_w, pool_scale, router_w, router_b, moe_w_gate, moe_w_up, moe_w_down, final_norm_w):
    assert x.shape == (B, T, D) and ctx.shape == (B, LC, D) and ada_w.shape[0] == 2
    cc = jnp.concatenate([c, c_ctx[None, :], jnp.zeros((16 - B - 1, D), F32)], axis=0)
    mod = _ada_mod(cc, ada_w, ada_b)
    rw_t = router_w.T
    fw = final_norm_w.reshape(1, D)
    experts = (moe_w_gate, moe_w_up, moe_w_down)

    mod0 = mod[0].reshape(16, 1, 6 * D)
    cos_t, sin_t = _rope_tables()
    proj = _inproj(x, ctx, mod0[:, :, :2 * D], norm_mix_w[0:1], _permute_w_in(w_in[0]), cos_t, sin_t)
    ret = _retention(proj, ret_log_decay[0])
    lam_init = 0.8 - 0.6 * math.exp(-0.3 * 0)
    lv = diff_lambda[0]
    lam = jnp.exp(jnp.sum(lv[0] * lv[1])) - jnp.exp(jnp.sum(lv[2] * lv[3])) + lam_init
    dif = _diffattn(proj, lam.reshape(1), diff_subln_w[0:1], 1.0 - lam_init)
    x1, hp, logits = _outproj(ret, dif, w_out[0].astype(BF16), x, mod0, norm_ffn_w[0:1], rw_t)
    x2 = _moe_layer(x1, hp, logits, rw_t, router_b, *experts, 0, mod0, fw, False)

    mod1 = mod[1].reshape(16, 1, 6 * D)
    x3, hp, logits = _pool_layer(x2, mod1, norm_mix_w[1:2], pool_w[0].astype(BF16), pool_scale[0:1],
                                 norm_ffn_w[1:2], rw_t)
    return _moe_layer(x3, hp, logits, rw_t, router_b, *experts, 1, mod1, fw, True)
```

```python
import functools
import math

import jax
import jax.numpy as jnp
import numpy as np
from jax import lax
from jax.experimental import pallas as pl
from jax.experimental.pallas import tpu as pltpu

F32 = jnp.float32
BF16 = jnp.bfloat16
I32 = jnp.int32

D = 1024
B = 8
T = 2048
N = B * T
GRID_W = 64
LC = 256
EPS = 1e-6
ROPE_BASE = 10000.0
NH = 4
DK = 64
HV = 128
CH = 256
RB = LC + T
NCH = RB // CH
POOL_WINDOWS = (2, 4, 8, 16)
PG = D // len(POOL_WINDOWS)
NE = 16
NGRP = 4
EPG = NE // NGRP
DE = 512
IN_W = 3072
HALO = 8

PAIR_A = (0, 0, 0, 1, 1, 3)
PAIR_B = (1, 2, 3, 3, 2, 2)
NCLS = NGRP * len(PAIR_A)
CLS_PAD = 32
SLAB = D // 128

TM_PROJ = 256
TM_OUT = 512
TQ_SUB = 4
SCORE_AHEAD = 1
TM_POOL = 512
TM_MOE = 256
N_UNITS = N // TM_MOE + NCLS
TM_PERM = 2048
TM_COMB = 512
PERM_UNROLL = 16
VMEM_LIMIT = 56 * 1024 * 1024


def _cparams(sem):
    return pltpu.CompilerParams(dimension_semantics=sem, vmem_limit_bytes=VMEM_LIMIT)


def _sigmoid(x):
    return 1.0 / (1.0 + jnp.exp(-x))


def _silu(x):
    return x * _sigmoid(x)


def _rms(x):
    return x * lax.rsqrt(jnp.mean(x * x, axis=-1, keepdims=True) + EPS)


def _dot_3pass(a, b, dims):
    a_hi = a.astype(BF16)
    b_hi = b.astype(BF16)
    a_lo = (a - a_hi.astype(F32)).astype(BF16)
    b_lo = (b - b_hi.astype(F32)).astype(BF16)

    def dot(x, y):
        return lax.dot_general(x, y, dims, preferred_element_type=F32)

    return dot(a_hi, b_hi) + (dot(a_lo, b_hi) + dot(a_hi, b_lo))


def _load_slabs(ref, rows):
    return jnp.concatenate([ref[pl.ds(s, rows, stride=SLAB), :] for s in range(SLAB)], axis=1)


def _store_slabs(ref, val, row0=0):
    rows = val.shape[0]
    for s in range(SLAB):
        ref[pl.ds(row0 * SLAB + s, rows, stride=SLAB), :] = val[:, s * 128:(s + 1) * 128]


def _ada_kernel(cc_ref, w_ref, b_ref, o_ref):
    s = _silu(cc_ref[...])
    o_ref[0] = _dot_3pass(s, w_ref[0], (((1,), (0,)), ((), ()))) + b_ref[0]


def _ada_mod(cc, ada_w, ada_b):
    depth = ada_w.shape[0]
    tn = 1536
    return pl.pallas_call(
        _ada_kernel,
        grid=(depth, 6 * D // tn),
        in_specs=[
            pl.BlockSpec((16, D), lambda l, n: (0, 0)),
            pl.BlockSpec((1, D, tn), lambda l, n: (l, 0, n)),
            pl.BlockSpec((1, 1, tn), lambda l, n: (l, 0, n)),
        ],
        out_specs=pl.BlockSpec((1, 16, tn), lambda l, n: (l, 0, n)),
        out_shape=jax.ShapeDtypeStruct((depth, 16, 6 * D), F32),
        compiler_params=_cparams(("arbitrary", "arbitrary")),
        name="ada_mod",
    )(cc, ada_w, ada_b.reshape(depth, 1, 6 * D))


def _rope(seg, cos, sin_signed, lo_mask):
    w = seg.shape[1]
    from_hi = pltpu.roll(seg, w - 16, axis=1)
    from_lo = pltpu.roll(seg, 16, axis=1)
    partner = jnp.where(lo_mask, from_hi, from_lo)
    reps = w // cos.shape[1]
    c = jnp.concatenate([cos] * reps, axis=1)
    s = jnp.concatenate([sin_signed] * reps, axis=1)
    return seg * c + partner * s


def _inproj_kernel(x_ref, c_ref, mod_ref, nw_ref, w_ref, cos_ref, sin_ref, o_ref):
    j = pl.program_id(1)
    xt = jnp.where(j == 0, c_ref[0], x_ref[0])
    sh = mod_ref[0, :, 0:D]
    sc = mod_ref[0, :, D:2 * D]
    h = (_rms(xt) * nw_ref[...]) * (1.0 + sc) + sh
    hb = h.astype(BF16)
    lane = lax.broadcasted_iota(I32, (TM_PROJ, 512), 1)
    lo_mask = (lane % 32) < 16
    cos = cos_ref[...]
    sin = sin_ref[...]

    def project(g):
        return jnp.dot(hb, w_ref[:, g * 512:(g + 1) * 512], preferred_element_type=F32)

    seg = project(0)
    for g in range(6):
        seg_next = project(g + 1) if g + 1 < 6 else None
        if g in (0, 3, 4):
            seg = _rope(seg, cos, sin, lo_mask)
        o_ref[0, :, g * 512:(g + 1) * 512] = seg.astype(BF16)
        seg = seg_next


def _inproj(x, ctx, mod3, norm_w, w_perm, cos_t, sin_t):
    nj = RB // TM_PROJ
    return pl.pallas_call(
        _inproj_kernel,
        grid=(B, nj),
        in_specs=[
            pl.BlockSpec((1, TM_PROJ, D), lambda b, j: (b, jnp.maximum(j - 1, 0), 0)),
            pl.BlockSpec((1, LC, D), lambda b, j: (b, 0, 0)),
            pl.BlockSpec((1, 1, 2 * D), lambda b, j: (jnp.where(j == 0, B, b), 0, 0)),
            pl.BlockSpec((1, D), lambda b, j: (0, 0)),
            pl.BlockSpec((D, IN_W), lambda b, j: (0, 0)),
            pl.BlockSpec((TM_PROJ, 128), lambda b, j: (j, 0)),
            pl.BlockSpec((TM_PROJ, 128), lambda b, j: (j, 0)),
        ],
        out_specs=pl.BlockSpec((1, TM_PROJ, IN_W), lambda b, j: (b, j, 0)),
        out_shape=jax.ShapeDtypeStruct((B, RB, IN_W), BF16),
        compiler_params=_cparams(("arbitrary", "arbitrary")),
        name="inproj",
    )(x, ctx, mod3, norm_w, w_perm, cos_t, sin_t)


def _retention_kernel(ld_ref, qk_ref, v_ref, g_ref, o_ref, st_ref):
    h = pl.program_id(1)
    lgf = ld_ref[0, h]
    lgb = ld_ref[1, h]
    lane = lax.broadcasted_iota(I32, (CH, 128), 1)
    fwd_lane = lane < DK
    pos = lax.broadcasted_iota(I32, (CH, 128), 0).astype(F32)
    kdec = jnp.where(fwd_lane, jnp.exp(lgf * (CH - 1 - pos)), jnp.exp(lgb * pos))
    qdec = jnp.where(fwd_lane, jnp.exp(lgf * (pos + 1.0)), jnp.exp(lgb * (CH - pos)))
    ii = lax.broadcasted_iota(I32, (CH, CH), 0)
    jj = lax.broadcasted_iota(I32, (CH, CH), 1)
    gap = (ii - jj).astype(F32)
    mask = (jnp.where(gap >= 0, jnp.exp(lgf * jnp.maximum(gap, 0.0)), 0.0)
            + jnp.where(gap <= 0, jnp.exp(lgb * jnp.maximum(-gap, 0.0)), 0.0))
    ones = jnp.ones((DK, 128), F32)
    cf = jnp.exp(lgf * CH * ones)
    cb = jnp.exp(lgb * CH * ones)

    def chunk(n):
        a = qk_ref[0, n * CH:(n + 1) * CH, :].astype(F32)
        swapped = pltpu.roll(a, DK, axis=1)
        return a, swapped

    kv = []
    for n in range(NCH):
        a, swapped = chunk(n)
        kk = jnp.where(fwd_lane, swapped, a)
        kb = (kk * kdec).astype(BF16)
        vn = v_ref[0, n * CH:(n + 1) * CH, :]
        kv.append(lax.dot_general(kb, vn, (((0,), (0,)), ((), ())), preferred_element_type=F32))
    sf = kv[0][:DK]
    for n in range(1, NCH):
        st_ref[n, 0:DK, :] = sf
        sf = cf * sf + kv[n][:DK]
    sb = kv[0][DK:]
    for n in range(NCH - 1, 0, -1):
        st_ref[n, DK:2 * DK, :] = sb
        sb = cb * sb + kv[n][DK:]

    for n in range(1, NCH):
        a, swapped = chunk(n)
        q = a[:, :DK].astype(BF16)
        k = swapped[:, :DK].astype(BF16)
        scores = lax.dot_general(q, k, (((1,), (1,)), ((), ())), preferred_element_type=F32)
        p = (scores * mask).astype(BF16)
        vn = v_ref[0, n * CH:(n + 1) * CH, :]
        qq = jnp.where(fwd_lane, a, swapped)
        qd = (qq * qdec).astype(BF16)
        o = (jnp.dot(p, vn, preferred_element_type=F32)
             + jnp.dot(qd, st_ref[n].astype(BF16), preferred_element_type=F32))
        gate = g_ref[0, n * CH:(n + 1) * CH, :].astype(F32)
        o_ref[0, (n - 1) * CH:n * CH, :] = (_rms(o) * _silu(gate)).astype(BF16)


def _retention(proj, log_decay):
    return pl.pallas_call(
        _retention_kernel,
        grid=(B, NH),
        in_specs=[
            pl.BlockSpec(memory_space=pltpu.SMEM),
            pl.BlockSpec((1, RB, 128), lambda b, h: (b, 0, h)),
            pl.BlockSpec((1, RB, 128), lambda b, h: (b, 0, NH + h)),
            pl.BlockSpec((1, RB, 128), lambda b, h: (b, 0, 2 * NH + h)),
        ],
        out_specs=pl.BlockSpec((1, T, 128), lambda b, h: (b, 0, h)),
        out_shape=jax.ShapeDtypeStruct((B, T, NH * HV), BF16),
        scratch_shapes=[pltpu.VMEM((NCH, 128, 128), F32)],
        compiler_params=_cparams(("arbitrary", "arbitrary")),
        name="retention",
    )(log_decay, proj, proj, proj)


def _diffattn_kernel(lam_ref, *refs, out_scale):
    q_refs = refs[:TQ_SUB]
    k_ref, v_ref, sw_ref, o_ref = refs[TQ_SUB:]
    lam = lam_ref[0]
    k = k_ref[0]
    v = v_ref[0]
    nt = (((1,), (1,)), ((), ()))

    def scores(qh):
        return lax.dot_general(qh, k, nt, preferred_element_type=F32)

    def values(s):
        e = jnp.exp2(s - jnp.max(s, axis=-1, keepdims=True))
        return jnp.dot(e.astype(BF16), v, preferred_element_type=F32), jnp.sum(e, axis=-1, keepdims=True)

    halves = []
    for i in range(TQ_SUB):
        q = q_refs[i][0]
        lane = lax.broadcasted_iota(I32, q.shape, 1)
        zero = jnp.zeros_like(q)
        halves += [jnp.where(lane < DK, q, zero), jnp.where(lane >= DK, q, zero)]
    outs = []
    ahead = [scores(h) for h in halves[:SCORE_AHEAD]]
    for c in range(len(halves)):
        if c + SCORE_AHEAD < len(halves):
            ahead.append(scores(halves[c + SCORE_AHEAD]))
        outs.append(values(ahead.pop(0)))
    for i in range(TQ_SUB):
        (o1, l1), (o2, l2) = outs[2 * i], outs[2 * i + 1]
        o = o1 / l1 - o2 * (lam / l2)
        o_ref[0, i * TM_PROJ:(i + 1) * TM_PROJ, :] = (_rms(o) * sw_ref[...] * out_scale).astype(BF16)


def _diffattn(proj, lam, subln_w, out_scale):
    tq = TQ_SUB * TM_PROJ
    nq = T // tq

    def q_map(i, b, h, j):
        return (b, LC // TM_PROJ + j * TQ_SUB + i, 3 * NH + h)

    return pl.pallas_call(
        functools.partial(_diffattn_kernel, out_scale=out_scale),
        grid=(B, NH, nq),
        in_specs=[
            pl.BlockSpec(memory_space=pltpu.SMEM),
            *[pl.BlockSpec((1, TM_PROJ, 128), functools.partial(q_map, i)) for i in range(TQ_SUB)],
            pl.BlockSpec((1, RB, 128), lambda b, h, j: (b, 0, 4 * NH + h)),
            pl.BlockSpec((1, RB, 128), lambda b, h, j: (b, 0, 5 * NH + h)),
            pl.BlockSpec((1, HV), lambda b, h, j: (0, 0)),
        ],
        out_specs=pl.BlockSpec((1, tq, 128), lambda b, h, j: (b, j, h)),
        out_shape=jax.ShapeDtypeStruct((B, T, NH * HV), BF16),
        compiler_params=_cparams(("arbitrary", "arbitrary", "arbitrary")),
        name="diffattn",
    )(lam, *([proj] * TQ_SUB), proj, proj, subln_w)


def _route(bz):
    grp = []
    for g in range(NGRP):
        m = bz[g * EPG:(g + 1) * EPG]
        best = None
        for i in range(EPG):
            for k in range(i + 1, EPG):
                pair = m[i] + m[k]
                best = pair if best is None else jnp.maximum(best, pair)
        grp.append(best)
    gbest = grp[0]
    gsel = jnp.zeros_like(gbest, dtype=I32)
    for g in range(1, NGRP):
        better = grp[g] > gbest
        gsel = jnp.where(better, g, gsel)
        gbest = jnp.where(better, grp[g], gbest)
    cb = [bz[i] for i in range(EPG)]
    for g in range(1, NGRP):
        pick = gsel == g
        cb = [jnp.where(pick, bz[g * EPG + i], cb[i]) for i in range(EPG)]
    i1 = jnp.zeros_like(gsel)
    b1 = cb[0]
    for i in range(1, EPG):
        better = cb[i] > b1
        i1 = jnp.where(better, i, i1)
        b1 = jnp.where(better, cb[i], b1)
    neg = jnp.full_like(b1, -jnp.inf)
    rest = [jnp.where(i1 == i, neg, cb[i]) for i in range(EPG)]
    i2 = jnp.zeros_like(gsel)
    b2 = rest[0]
    for i in range(1, EPG):
        better = rest[i] > b2
        i2 = jnp.where(better, i, i2)
        b2 = jnp.where(better, rest[i], b2)
    lo = jnp.minimum(i1, i2)
    hi = jnp.maximum(i1, i2)
    code = lo * EPG + hi
    pair = jnp.full_like(gsel, len(PAIR_A) - 1)
    for p in range(len(PAIR_A) - 1):
        a, b = min(PAIR_A[p], PAIR_B[p]), max(PAIR_A[p], PAIR_B[p])
        pair = jnp.where(code == a * EPG + b, p, pair)
    return gsel * len(PAIR_A) + pair


def _ffn_prologue(x1, row0, mod, nfw_ref, rw_ref, hp_ref, logit_ref):
    rows = x1.shape[0]
    sh2 = mod[:, 3 * D:4 * D]
    sc2 = mod[:, 4 * D:5 * D]
    h2 = (_rms(x1) * nfw_ref[...]) * (1.0 + sc2) + sh2
    _store_slabs(hp_ref, h2, row0)
    logit_ref[:, row0:row0 + rows] = _dot_3pass(rw_ref[...], h2, (((1,), (1,)), ((), ())))


def _ffn_out_specs(tm, n_tiles_per_b):
    specs = [
        pl.BlockSpec((1, tm, D), lambda b, j: (b, j, 0)),
        pl.BlockSpec((tm * SLAB, 128), lambda b, j: (b * n_tiles_per_b + j, 0)),
        pl.BlockSpec((NE, tm), lambda b, j: (0, b * n_tiles_per_b + j)),
    ]
    shapes = [
        jax.ShapeDtypeStruct((B, T, D), F32),
        jax.ShapeDtypeStruct((N * SLAB, 128), F32),
        jax.ShapeDtypeStruct((NE, N), F32),
    ]
    return specs, shapes


def _route_kernel(logit_ref, bias_ref, dest_ref, ends_ref):
    r = logit_ref.shape[1]
    cls = _route([_sigmoid(logit_ref[e]) + bias_ref[e] for e in range(NE)])
    lane_incl = (lax.broadcasted_iota(I32, (128, 128), 0) <= lax.broadcasted_iota(I32, (128, 128), 1)).astype(BF16)
    rows_before = (lax.broadcasted_iota(I32, (r, r), 1) < lax.broadcasted_iota(I32, (r, r), 0)).astype(BF16)
    dest = jnp.zeros((r, 128), F32)
    start = jnp.zeros((1, 128), F32)
    ends = []
    for c in range(NCLS):
        onehot = jnp.where(cls == c, 1.0, 0.0)
        in_row = jnp.dot(onehot.astype(BF16), lane_incl, preferred_element_type=F32)
        row_tot = jnp.broadcast_to(in_row[:, 127:128], (r, 128))
        above = jnp.dot(rows_before, row_tot.astype(BF16), preferred_element_type=F32)
        dest = dest + onehot * (start + above + in_row - 1.0)
        start = start + jnp.sum(row_tot, axis=0, keepdims=True)
        ends.append(start)
    dest_ref[...] = dest.astype(I32)
    ends_ref[...] = jnp.concatenate(ends + [jnp.zeros((CLS_PAD - NCLS, 128), F32)], axis=0)


def _route_tokens(logits_t, router_b):
    r = N // 128
    return pl.pallas_call(
        _route_kernel,
        in_specs=[pl.BlockSpec((NE, r, 128), lambda: (0, 0, 0)), pl.BlockSpec(memory_space=pltpu.SMEM)],
        out_specs=[pl.BlockSpec((r, 128), lambda: (0, 0)), pl.BlockSpec((CLS_PAD, 128), lambda: (0, 0))],
        out_shape=[jax.ShapeDtypeStruct((r, 128), I32), jax.ShapeDtypeStruct((CLS_PAD, 128), F32)],
        compiler_params=pltpu.CompilerParams(vmem_limit_bytes=VMEM_LIMIT),
        name="route",
    )(logits_t.reshape(NE, r, 128), router_b)


def _outproj_kernel(ret_ref, dif_ref, w_ref, x_ref, mod_ref, nfw_ref, rw_ref, x1_ref, hp_ref, logit_ref):
    mod = mod_ref[0]

    mx = (jnp.dot(ret_ref[0], w_ref[0:NH * HV, :], preferred_element_type=F32)
          + jnp.dot(dif_ref[0], w_ref[NH * HV:, :], preferred_element_type=F32))
    x1 = x_ref[0] + mod[:, 2 * D:3 * D] * mx
    x1_ref[0] = x1
    _ffn_prologue(x1, 0, mod, nfw_ref, rw_ref, hp_ref, logit_ref)


def _outproj(ret, dif, w_out, x, mod3, nfw, rw_t):
    nj = T // TM_OUT
    out_specs, out_shapes = _ffn_out_specs(TM_OUT, nj)
    return pl.pallas_call(
        _outproj_kernel,
        grid=(B, nj),
        in_specs=[
            pl.BlockSpec((1, TM_OUT, NH * HV), lambda b, j: (b, j, 0)),
            pl.BlockSpec((1, TM_OUT, NH * HV), lambda b, j: (b, j, 0)),
            pl.BlockSpec((2 * NH * HV, D), lambda b, j: (0, 0)),
            pl.BlockSpec((1, TM_OUT, D), lambda b, j: (b, j, 0)),
            pl.BlockSpec((1, 1, 6 * D), lambda b, j: (b, 0, 0)),
            pl.BlockSpec((1, D), lambda b, j: (0, 0)),
            pl.BlockSpec((NE, D), lambda b, j: (0, 0)),
        ],
        out_specs=out_specs,
        out_shape=out_shapes,
        compiler_params=_cparams(("arbitrary", "arbitrary")),
        name="outproj",
    )(ret, dif, w_out, x, mod3, nfw, rw_t)


def _pool_kernel(x_ref, prev_ref, next_ref, mod_ref, nmw_ref, pw_ref, ps_ref, nfw_ref, rw_ref,
                 x1_ref, hp_ref, logit_ref, ext_ref):
    i = pl.program_id(1)
    last = pl.num_programs(1) - 1
    mod = mod_ref[0]
    sh1 = mod[:, 0:D]
    sc1 = mod[:, D:2 * D]

    def modnorm(v):
        return (_rms(v) * nmw_ref[...]) * (1.0 + sc1) + sh1

    x = x_ref[0]
    hc = modnorm(x)
    ext_ref[0:HALO, :] = jnp.where(i > 0, modnorm(prev_ref[0]), 0.0)
    ext_ref[HALO:HALO + TM_POOL, :] = hc
    ext_ref[HALO + TM_POOL:, :] = jnp.where(i < last, modnorm(next_ref[0]), 0.0)
    pos = i * TM_POOL + lax.broadcasted_iota(I32, (TM_POOL, 1), 0)
    mixed = []
    for gi, w in enumerate(POOL_WINDOWS):
        left = w // 2
        right = w - 1 - left
        cols = slice(gi * PG, (gi + 1) * PG)
        tot = None
        for d in range(-left, right + 1):
            part = ext_ref[HALO + d:HALO + d + TM_POOL, cols]
            tot = part if tot is None else tot + part
        cnt = (jnp.minimum(pos + right + 1, T) - jnp.maximum(pos - left, 0)).astype(F32)
        pooled = (tot / cnt - hc[:, cols]).astype(BF16)
        mixed.append(jnp.dot(pooled, pw_ref[gi], preferred_element_type=F32))
    mixed = jnp.concatenate(mixed, axis=1) * ps_ref[...]
    x1 = x + mod[:, 2 * D:3 * D] * mixed
    x1_ref[0] = x1
    _ffn_prologue(x1, 0, mod, nfw_ref, rw_ref, hp_ref, logit_ref)


def _pool_layer(x, mod3, nmw, pool_w, pool_scale, nfw, rw_t):
    ni = T // TM_POOL
    hb = TM_POOL // HALO
    out_specs, out_shapes = _ffn_out_specs(TM_POOL, ni)
    return pl.pallas_call(
        _pool_kernel,
        grid=(B, ni),
        in_specs=[
            pl.BlockSpec((1, TM_POOL, D), lambda b, i: (b, i, 0)),
            pl.BlockSpec((1, HALO, D), lambda b, i: (b, jnp.maximum(i * hb - 1, 0), 0)),
            pl.BlockSpec((1, HALO, D), lambda b, i: (b, jnp.minimum((i + 1) * hb, T // HALO - 1), 0)),
            pl.BlockSpec((1, 1, 6 * D), lambda b, i: (b, 0, 0)),
            pl.BlockSpec((1, D), lambda b, i: (0, 0)),
            pl.BlockSpec((len(POOL_WINDOWS), PG, PG), lambda b, i: (0, 0, 0)),
            pl.BlockSpec((1, D), lambda b, i: (0, 0)),
            pl.BlockSpec((1, D), lambda b, i: (0, 0)),
            pl.BlockSpec((NE, D), lambda b, i: (0, 0)),
        ],
        out_specs=out_specs,
        out_shape=out_shapes,
        scratch_shapes=[pltpu.VMEM((TM_POOL + 2 * HALO, D), F32)],
        compiler_params=_cparams(("arbitrary", "arbitrary")),
        name="pool_layer",
    )(x, x, x, mod3, nmw, pool_w, pool_scale, nfw, rw_t)


def _tile_copy(src_ref, dst_ref, sem, s, d, rows=1):
    s0 = pl.multiple_of(s * SLAB, SLAB)
    d0 = pl.multiple_of(d * SLAB, SLAB)
    return pltpu.make_async_copy(src_ref.at[pl.ds(s0, rows * SLAB)], dst_ref.at[pl.ds(d0, rows * SLAB)], sem)


def _issue_tile_copies(idx_ref, base, rows, start_one):
    def group(g, carry):
        r0 = g * PERM_UNROLL
        ids = [idx_ref[base + r0 + u] for u in range(PERM_UNROLL)]
        for u in range(PERM_UNROLL):
            start_one(r0 + u, ids[u], u % 2)
        return carry

    lax.fori_loop(0, rows // PERM_UNROLL, group, 0)


def _dispatch_kernel(dest_ref, src_ref, dst_ref, sem):
    base = pl.program_id(0) * TM_PERM

    def start_one(r, d, priority):
        _tile_copy(src_ref, dst_ref, sem, r, d).start(priority=priority)

    _issue_tile_copies(dest_ref, base, TM_PERM, start_one)
    _tile_copy(src_ref, dst_ref, sem, 0, 0, TM_PERM).wait()


def _dispatch(dest, src):
    return pl.pallas_call(
        _dispatch_kernel,
        grid_spec=pltpu.PrefetchScalarGridSpec(
            num_scalar_prefetch=1,
            grid=(N // TM_PERM,),
            in_specs=[pl.BlockSpec((TM_PERM * SLAB, 128), lambda i, dest: (i, 0))],
            out_specs=pl.BlockSpec(memory_space=pl.ANY),
            scratch_shapes=[pltpu.SemaphoreType.DMA(())],
        ),
        out_shape=jax.ShapeDtypeStruct(src.shape, src.dtype),
        compiler_params=_cparams(("arbitrary",)),
        name="dispatch",
    )(dest, src)


def _combine_kernel(dest_ref, x_ref, ys_ref, mod_ref, fw_ref, o_ref, ybuf_ref, sem, *, final):
    i = pl.program_id(0)
    n = pl.num_programs(0)
    slot = i % 2

    def gather(step, to_slot):
        def start_one(r, d, priority):
            _tile_copy(ys_ref, ybuf_ref.at[to_slot], sem.at[to_slot], d, r).start(priority=priority)

        _issue_tile_copies(dest_ref, step * TM_COMB, TM_COMB, start_one)

    @pl.when(i == 0)
    def _():
        gather(0, 0)

    @pl.when(i + 1 < n)
    def _():
        gather(i + 1, 1 - slot)

    _tile_copy(ys_ref, ybuf_ref.at[slot], sem.at[slot], 0, 0, TM_COMB).wait()
    out = x_ref[...] + mod_ref[0][:, 5 * D:6 * D] * _load_slabs(ybuf_ref.at[slot], TM_COMB)
    if final:
        out = _rms(out) * fw_ref[...]
    o_ref[...] = out


def _combine(dest, x1, ys, mod3, final_w, final):
    per_b = T // TM_COMB
    return pl.pallas_call(
        functools.partial(_combine_kernel, final=final),
        grid_spec=pltpu.PrefetchScalarGridSpec(
            num_scalar_prefetch=1,
            grid=(N // TM_COMB,),
            in_specs=[
                pl.BlockSpec((TM_COMB, D), lambda i, dest: (i, 0)),
                pl.BlockSpec(memory_space=pl.ANY),
                pl.BlockSpec((1, 1, 6 * D), lambda i, dest: (i // per_b, 0, 0)),
                pl.BlockSpec((1, D), lambda i, dest: (0, 0)),
            ],
            out_specs=pl.BlockSpec((TM_COMB, D), lambda i, dest: (i, 0)),
            scratch_shapes=[pltpu.VMEM((2, TM_COMB * SLAB, 128), F32), pltpu.SemaphoreType.DMA((2,))],
        ),
        out_shape=jax.ShapeDtypeStruct((N, D), F32),
        compiler_params=_cparams(("arbitrary",)),
        name="combine",
    )(dest, x1.reshape(N, D), ys, mod3, final_w)


def _moe_kernel(tile_ref, lo_ref, hi_ref, ea_ref, eb_ref, ca_ref, cb_ref, na_ref, nb_ref,
                x_ref, rw_ref, wg_hbm, wu_hbm, wd_hbm, o_ref,
                sga_ref, sua_ref, sda_ref, sgb_ref, sub_ref, sdb_ref,
                ga_ref, ua_ref, da_ref, gb_ref, ub_ref, db_ref, sem, *, layer):
    m = pl.program_id(0)
    lo = lo_ref[m]
    hi = hi_ref[m]
    tile0 = tile_ref[m] * TM_MOE
    slots = ((ea_ref, ca_ref, na_ref, (sga_ref, sua_ref, sda_ref), (ga_ref, ua_ref, da_ref)),
             (eb_ref, cb_ref, nb_ref, (sgb_ref, sub_ref, sdb_ref), (gb_ref, ub_ref, db_ref)))

    def fetch(slot, expert):
        stage = slots[slot][3]
        return [pltpu.make_async_copy(w.at[layer, expert], s, sem.at[slot])
                for w, s in zip((wg_hbm, wu_hbm, wd_hbm), stage)]

    for slot, (e_ref, c_ref, n_ref, stage, work) in enumerate(slots):
        @pl.when(m == 0)
        def _():
            for cp in fetch(slot, e_ref[0]):
                cp.start()

        @pl.when(c_ref[m] == 1)
        def _():
            for cp in fetch(slot, 0):
                cp.wait()
            for s, w in zip(stage, work):
                w[...] = s[...].astype(BF16)

            @pl.when(n_ref[m] >= 0)
            def _():
                for cp in fetch(slot, n_ref[m]):
                    cp.start()

    def value():
        hf = _load_slabs(x_ref, TM_MOE)
        h = hf.astype(BF16)

        def up(w_ref):
            return jnp.dot(h, w_ref[...], preferred_element_type=F32)

        def down(g, u, d_ref):
            return jnp.dot((_silu(g) * u).astype(BF16), d_ref[...], preferred_element_type=F32)

        g_a, u_a, g_b, u_b = up(ga_ref), up(ua_ref), up(gb_ref), up(ub_ref)
        y_a = down(g_a, u_a, da_ref)
        y_b = down(g_b, u_b, db_ref)
        s_a = _sigmoid(jnp.sum(hf * rw_ref[pl.ds(ea_ref[m], 1), :], axis=1, keepdims=True))
        s_b = _sigmoid(jnp.sum(hf * rw_ref[pl.ds(eb_ref[m], 1), :], axis=1, keepdims=True))
        denom = s_a + s_b
        val = (s_a / denom) * y_a + (s_b / denom) * y_b
        row = tile0 + lax.broadcasted_iota(I32, (TM_MOE, 1), 0)
        return val, (row >= lo) & (row < hi)

    @pl.when((hi > lo) & (lo == tile0))
    def _():
        val, seg = value()
        _store_slabs(o_ref, jnp.where(seg, val, 0.0))

    @pl.when((hi > lo) & (lo != tile0))
    def _():
        val, seg = value()
        _store_slabs(o_ref, jnp.where(seg, val, _load_slabs(o_ref, TM_MOE)))


def _moe_sorted(units, xs, rw_t, wg, wu, wd, layer):
    row_map = lambda m, t, *_: (t[m], 0)
    mats = lambda dt: [pltpu.VMEM((D, DE), dt), pltpu.VMEM((D, DE), dt), pltpu.VMEM((DE, D), dt)]
    return pl.pallas_call(
        functools.partial(_moe_kernel, layer=layer),
        grid_spec=pltpu.PrefetchScalarGridSpec(
            num_scalar_prefetch=len(units),
            grid=(N_UNITS,),
            in_specs=[
                pl.BlockSpec((TM_MOE * SLAB, 128), row_map),
                pl.BlockSpec((NE, D), lambda m, *_: (0, 0)),
                pl.BlockSpec(memory_space=pl.ANY),
                pl.BlockSpec(memory_space=pl.ANY),
                pl.BlockSpec(memory_space=pl.ANY),
            ],
            out_specs=pl.BlockSpec((TM_MOE * SLAB, 128), row_map),
            scratch_shapes=mats(F32) + mats(F32) + mats(BF16) + mats(BF16) + [pltpu.SemaphoreType.DMA((2,))],
        ),
        out_shape=jax.ShapeDtypeStruct((N * SLAB, 128), F32),
        compiler_params=_cparams(("arbitrary",)),
        name="moe_sorted",
    )(*units, xs, rw_t, wg, wu, wd)


def _moe_units(ends_f):
    ends = ends_f[:NCLS, 0].astype(I32)
    offs = jnp.concatenate([jnp.zeros((1,), I32), ends[:-1]])
    starts = jnp.sort(jnp.concatenate([jnp.arange(N // TM_MOE, dtype=I32) * TM_MOE, offs]))
    u_lo = starts
    u_hi = jnp.concatenate([starts[1:], jnp.full((1,), N, I32)])
    u_tile = jnp.minimum(u_lo // TM_MOE, N // TM_MOE - 1)
    u_cls = jnp.minimum(jnp.sum(ends[None, :] <= u_lo[:, None], axis=1), NCLS - 1).astype(I32)
    grp = u_cls // len(PAIR_A)
    pair = u_cls % len(PAIR_A)
    e_a = grp * EPG + jnp.take(jnp.array(PAIR_A, I32), pair)
    e_b = grp * EPG + jnp.take(jnp.array(PAIR_B, I32), pair)
    one = jnp.ones((1,), I32)
    idx = jnp.arange(N_UNITS, dtype=I32)

    def changes(e):
        chg = jnp.concatenate([one, (e[1:] != e[:-1]).astype(I32)])
        at = jnp.where(chg == 1, idx, N_UNITS)
        nxt_at = jnp.concatenate([lax.cummin(at, reverse=True)[1:], jnp.full((1,), N_UNITS, I32)])
        nxt = jnp.where(nxt_at < N_UNITS, jnp.take(e, jnp.minimum(nxt_at, N_UNITS - 1)), -1)
        return chg, nxt

    chg_a, nxt_a = changes(e_a)
    chg_b, nxt_b = changes(e_b)
    return u_tile, u_lo, u_hi, e_a, e_b, chg_a, chg_b, nxt_a, nxt_b


def _moe_layer(x1, hp, logits, rw_t, router_b, wg, wu, wd, layer, mod3, final_w, final):
    dest, ends = _route_tokens(logits, router_b)
    dest = dest.reshape(N)
    units = _moe_units(ends)
    xs = _dispatch(dest, hp)
    ys = _moe_sorted(units, xs, rw_t, wg, wu, wd, layer)
    return _combine(dest, x1, ys, mod3, final_w, final).reshape(B, T, D)


def _rope_tables():
    half = 16
    inv = ROPE_BASE ** (-np.arange(half, dtype=np.float64) / half)
    t = np.arange(T)
    ang_r = (t // GRID_W)[:, None] * inv[None, :]
    ang_c = (t % GRID_W)[:, None] * inv[None, :]
    ang = np.concatenate([ang_r, ang_r, ang_c, ang_c], axis=1)
    sign = np.tile(np.concatenate([-np.ones(half), np.ones(half)]), 2)
    cos = np.concatenate([np.ones((LC, 64)), np.cos(ang)], axis=0)
    sin = np.concatenate([np.zeros((LC, 64)), np.sin(ang) * sign[None, :]], axis=0)
    return (jnp.asarray(np.tile(cos, (1, 2)), dtype=F32), jnp.asarray(np.tile(sin, (1, 2)), dtype=F32))


def _permute_w_in(w):
    rq = w[:, 0:256].reshape(D, NH, DK)
    dq = w[:, 256:768]
    rg = w[:, 768:1280]
    rk = w[:, 1280:1536].reshape(D, NH, DK)
    rv = w[:, 1536:2048]
    dk = w[:, 2048:2560]
    dv = w[:, 2560:3072]
    qk = jnp.concatenate([rq, rk * (DK ** -0.5)], axis=2).reshape(D, NH * 2 * DK)
    return jnp.concatenate([qk, rv, rg, dq * (DK ** -0.5 * math.log2(math.e)), dk, dv], axis=1).astype(BF16)


def kernel(x, c, ctx, c_ctx, ada_w, ada_b, norm_mix_w, norm_ffn_w, w_in, w_out, ret_log_decay, diff_lambda,
           diff_subln_w, pool_w, pool_scale, router_w, router_b, moe_w_gate, moe_w_up, moe_w_down, final_norm_w):
    assert x.shape == (B, T, D) and ctx.shape == (B, LC, D) and ada_w.shape[0] == 2
    cc = jnp.concatenate([c, c_ctx[None, :], jnp.zeros((16 - B - 1, D), F32)], axis=0)
    mod = _ada_mod(cc, ada_w, ada_b)
    rw_t = router_w.T
    fw = final_norm_w.reshape(1, D)
    experts = (moe_w_gate, moe_w_up, moe_w_down)

    mod0 = mod[0].reshape(16, 1, 6 * D)
    cos_t, sin_t = _rope_tables()
    proj = _inproj(x, ctx, mod0, norm_mix_w[0:1], _permute_w_in(w_in[0]), cos_t, sin_t)
    ret = _retention(proj, ret_log_decay[0])
    lam_init = 0.8 - 0.6 * math.exp(-0.3 * 0)
    lv = diff_lambda[0]
    lam = jnp.exp(jnp.sum(lv[0] * lv[1])) - jnp.exp(jnp.sum(lv[2] * lv[3])) + lam_init
    dif = _diffattn(proj, lam.reshape(1), diff_subln_w[0:1], 1.0 - lam_init)
    x1, hp, logits = _outproj(ret, dif, w_out[0].astype(BF16), x, mod0, norm_ffn_w[0:1], rw_t)
    x2 = _moe_layer(x1, hp, logits, rw_t, router_b, *experts, 0, mod0, fw, False)

    mod1 = mod[1].reshape(16, 1, 6 * D)
    x3, hp, logits = _pool_layer(x2, mod1, norm_mix_w[1:2], pool_w[0].astype(BF16), pool_scale[0:1],
                                 norm_ffn_w[1:2], rw_t)
    return _moe_layer(x3, hp, logits, rw_t, router_b, *experts, 1, mod1, fw, True)
```

```python
import functools
import math

import jax
import jax.numpy as jnp
import numpy as np
from jax import lax
from jax.experimental import pallas as pl
from jax.experimental.pallas import tpu as pltpu

F32 = jnp.float32
BF16 = jnp.bfloat16
I32 = jnp.int32

D = 1024
B = 8
T = 2048
N = B * T
GRID_W = 64
LC = 256
EPS = 1e-6
ROPE_BASE = 10000.0
NH = 4
DK = 64
HV = 128
CH = 256
RB = LC + T
NCH = RB // CH
POOL_WINDOWS = (2, 4, 8, 16)
PG = D // len(POOL_WINDOWS)
NE = 16
NGRP = 4
EPG = NE // NGRP
DE = 512
IN_W = 3072
HALO = 8

PAIR_A = (0, 0, 0, 1, 1, 3)
PAIR_B = (1, 2, 3, 3, 2, 2)
NCLS = NGRP * len(PAIR_A)
CLS_PAD = 32
SLAB = D // 128

TM_PROJ = 256
PROJ_SUB = 3
TM_OUT = 512
TQ_SUB = 8
SCORE_AHEAD = 1
TM_POOL = 512
TM_MOE = 256
N_UNITS = N // TM_MOE + NCLS
TM_PERM = 2048
TM_COMB = 512
PERM_UNROLL = 16
VMEM_LIMIT = 56 * 1024 * 1024


def _cparams(sem):
    return pltpu.CompilerParams(dimension_semantics=sem, vmem_limit_bytes=VMEM_LIMIT)


def _sigmoid(x):
    return 1.0 / (1.0 + jnp.exp(-x))


def _silu(x):
    return x * _sigmoid(x)


def _rms(x):
    return x * lax.rsqrt(jnp.mean(x * x, axis=-1, keepdims=True) + EPS)


def _dot_3pass(a, b, dims):
    a_hi = a.astype(BF16)
    b_hi = b.astype(BF16)
    a_lo = (a - a_hi.astype(F32)).astype(BF16)
    b_lo = (b - b_hi.astype(F32)).astype(BF16)

    def dot(x, y):
        return lax.dot_general(x, y, dims, preferred_element_type=F32)

    return dot(a_hi, b_hi) + (dot(a_lo, b_hi) + dot(a_hi, b_lo))


def _load_slabs(ref, rows):
    return jnp.concatenate([ref[pl.ds(s, rows, stride=SLAB), :] for s in range(SLAB)], axis=1)


def _store_slabs(ref, val, row0=0):
    rows = val.shape[0]
    for s in range(SLAB):
        ref[pl.ds(row0 * SLAB + s, rows, stride=SLAB), :] = val[:, s * 128:(s + 1) * 128]


def _ada_kernel(cc_ref, w_ref, b_ref, o_ref):
    s = _silu(cc_ref[...])
    o_ref[0] = _dot_3pass(s, w_ref[0], (((1,), (0,)), ((), ()))) + b_ref[0]


def _ada_mod(cc, ada_w, ada_b):
    depth = ada_w.shape[0]
    tn = 1536
    return pl.pallas_call(
        _ada_kernel,
        grid=(depth, 6 * D // tn),
        in_specs=[
            pl.BlockSpec((16, D), lambda l, n: (0, 0)),
            pl.BlockSpec((1, D, tn), lambda l, n: (l, 0, n)),
            pl.BlockSpec((1, 1, tn), lambda l, n: (l, 0, n)),
        ],
        out_specs=pl.BlockSpec((1, 16, tn), lambda l, n: (l, 0, n)),
        out_shape=jax.ShapeDtypeStruct((depth, 16, 6 * D), F32),
        compiler_params=_cparams(("arbitrary", "arbitrary")),
        name="ada_mod",
    )(cc, ada_w, ada_b.reshape(depth, 1, 6 * D))


def _rope(seg, cos, sin_signed, lo_mask):
    w = seg.shape[1]
    from_hi = pltpu.roll(seg, w - 16, axis=1)
    from_lo = pltpu.roll(seg, 16, axis=1)
    partner = jnp.where(lo_mask, from_hi, from_lo)
    reps = w // cos.shape[1]
    c = jnp.concatenate([cos] * reps, axis=1)
    s = jnp.concatenate([sin_signed] * reps, axis=1)
    return seg * c + partner * s


def _inproj_kernel(*refs):
    x_refs = refs[:PROJ_SUB]
    c_ref, mod_ref, cmod_ref, nw_ref, w_ref, cos_ref, sin_ref, o_ref = refs[PROJ_SUB:]
    is_ctx = pl.program_id(1) == 0
    parts = []
    for s in range(PROJ_SUB):
        xt = x_refs[s][0]
        sh = mod_ref[0, :, 0:D]
        sc = mod_ref[0, :, D:2 * D]
        if s == 0:
            xt = jnp.where(is_ctx, c_ref[0], xt)
            sh = jnp.where(is_ctx, cmod_ref[0, :, 0:D], sh)
            sc = jnp.where(is_ctx, cmod_ref[0, :, D:2 * D], sc)
        parts.append(((_rms(xt) * nw_ref[...]) * (1.0 + sc) + sh).astype(BF16))
    hb = jnp.concatenate(parts, axis=0)
    lane = lax.broadcasted_iota(I32, (PROJ_SUB * TM_PROJ, 512), 1)
    lo_mask = (lane % 32) < 16
    cos = cos_ref[...]
    sin = sin_ref[...]

    def project(g):
        return jnp.dot(hb, w_ref[:, g * 512:(g + 1) * 512], preferred_element_type=F32)

    seg = project(0)
    for g in range(6):
        seg_next = project(g + 1) if g + 1 < 6 else None
        if g in (0, 3, 4):
            seg = _rope(seg, cos, sin, lo_mask)
        o_ref[0, :, g * 512:(g + 1) * 512] = seg.astype(BF16)
        seg = seg_next


def _inproj(x, ctx, mod3, norm_w, w_perm, cos_t, sin_t):
    tm = PROJ_SUB * TM_PROJ
    nj = RB // tm

    def x_map(s, b, j):
        return (b, jnp.maximum(PROJ_SUB * j + s - LC // TM_PROJ, 0), 0)

    return pl.pallas_call(
        _inproj_kernel,
        grid=(B, nj),
        in_specs=[
            *[pl.BlockSpec((1, TM_PROJ, D), functools.partial(x_map, s)) for s in range(PROJ_SUB)],
            pl.BlockSpec((1, LC, D), lambda b, j: (b, 0, 0)),
            pl.BlockSpec((1, 1, 2 * D), lambda b, j: (b, 0, 0)),
            pl.BlockSpec((1, 1, 2 * D), lambda b, j: (B, 0, 0)),
            pl.BlockSpec((1, D), lambda b, j: (0, 0)),
            pl.BlockSpec((D, IN_W), lambda b, j: (0, 0)),
            pl.BlockSpec((tm, 128), lambda b, j: (j, 0)),
            pl.BlockSpec((tm, 128), lambda b, j: (j, 0)),
        ],
        out_specs=pl.BlockSpec((1, tm, IN_W), lambda b, j: (b, j, 0)),
        out_shape=jax.ShapeDtypeStruct((B, RB, IN_W), BF16),
        compiler_params=_cparams(("arbitrary", "arbitrary")),
        name="inproj",
    )(*([x] * PROJ_SUB), ctx, mod3, mod3, norm_w, w_perm, cos_t, sin_t)


def _retention_kernel(ld_ref, qk_ref, v_ref, g_ref, o_ref, st_ref, kdec_ref, qdec_ref, mask_ref, cdec_ref):
    h = pl.program_id(0)
    lane = lax.broadcasted_iota(I32, (CH, 128), 1)
    fwd_lane = lane < DK

    @pl.when(pl.program_id(1) == 0)
    def _():
        lgf = ld_ref[0, h]
        lgb = ld_ref[1, h]
        pos = lax.broadcasted_iota(I32, (CH, 128), 0).astype(F32)
        kdec_ref[...] = jnp.where(fwd_lane, jnp.exp(lgf * (CH - 1 - pos)), jnp.exp(lgb * pos))
        qdec_ref[...] = jnp.where(fwd_lane, jnp.exp(lgf * (pos + 1.0)), jnp.exp(lgb * (CH - pos)))
        ii = lax.broadcasted_iota(I32, (CH, CH), 0)
        jj = lax.broadcasted_iota(I32, (CH, CH), 1)
        gap = (ii - jj).astype(F32)
        mask_ref[...] = (jnp.where(gap >= 0, jnp.exp(lgf * jnp.maximum(gap, 0.0)), 0.0)
                         + jnp.where(gap <= 0, jnp.exp(lgb * jnp.maximum(-gap, 0.0)), 0.0))
        ones = jnp.ones((DK, 128), F32)
        cdec_ref[0:DK, :] = jnp.exp(lgf * CH * ones)
        cdec_ref[DK:, :] = jnp.exp(lgb * CH * ones)

    kdec = kdec_ref[...]
    qdec = qdec_ref[...]
    mask = mask_ref[...]
    cf = cdec_ref[0:DK, :]
    cb = cdec_ref[DK:, :]

    def chunk(n):
        a = qk_ref[0, n * CH:(n + 1) * CH, :].astype(F32)
        swapped = pltpu.roll(a, DK, axis=1)
        return a, swapped

    kv = []
    for n in range(NCH):
        a, swapped = chunk(n)
        kk = jnp.where(fwd_lane, swapped, a)
        kb = (kk * kdec).astype(BF16)
        vn = v_ref[0, n * CH:(n + 1) * CH, :]
        kv.append(lax.dot_general(kb, vn, (((0,), (0,)), ((), ())), preferred_element_type=F32))
    sf = kv[0][:DK]
    for n in range(1, NCH):
        st_ref[n, 0:DK, :] = sf
        sf = cf * sf + kv[n][:DK]
    sb = kv[0][DK:]
    for n in range(NCH - 1, 0, -1):
        st_ref[n, DK:2 * DK, :] = sb
        sb = cb * sb + kv[n][DK:]

    for n in range(1, NCH):
        a, swapped = chunk(n)
        q = a[:, :DK].astype(BF16)
        k = swapped[:, :DK].astype(BF16)
        scores = lax.dot_general(q, k, (((1,), (1,)), ((), ())), preferred_element_type=F32)
        p = (scores * mask).astype(BF16)
        vn = v_ref[0, n * CH:(n + 1) * CH, :]
        qq = jnp.where(fwd_lane, a, swapped)
        qd = (qq * qdec).astype(BF16)
        o = (jnp.dot(p, vn, preferred_element_type=F32)
             + jnp.dot(qd, st_ref[n].astype(BF16), preferred_element_type=F32))
        gate = g_ref[0, n * CH:(n + 1) * CH, :].astype(F32)
        o_ref[0, (n - 1) * CH:n * CH, :] = (_rms(o) * _silu(gate)).astype(BF16)


def _retention(proj, log_decay):
    return pl.pallas_call(
        _retention_kernel,
        grid=(NH, B),
        in_specs=[
            pl.BlockSpec(memory_space=pltpu.SMEM),
            pl.BlockSpec((1, RB, 128), lambda h, b: (b, 0, h)),
            pl.BlockSpec((1, RB, 128), lambda h, b: (b, 0, NH + h)),
            pl.BlockSpec((1, RB, 128), lambda h, b: (b, 0, 2 * NH + h)),
        ],
        out_specs=pl.BlockSpec((1, T, 128), lambda h, b: (b, 0, h)),
        out_shape=jax.ShapeDtypeStruct((B, T, NH * HV), BF16),
        scratch_shapes=[pltpu.VMEM((NCH, 128, 128), F32), pltpu.VMEM((CH, 128), F32), pltpu.VMEM((CH, 128), F32),
                        pltpu.VMEM((CH, CH), F32), pltpu.VMEM((2 * DK, 128), F32)],
        compiler_params=_cparams(("arbitrary", "arbitrary")),
        name="retention",
    )(log_decay, proj, proj, proj)


def _diffattn_kernel(lam_ref, *refs, out_scale):
    q_refs = refs[:TQ_SUB]
    k_ref, v_ref, sw_ref, o_ref = refs[TQ_SUB:]
    lam = lam_ref[0]
    k = k_ref[0]
    v = v_ref[0]
    nt = (((1,), (1,)), ((), ()))

    def scores(qh):
        return lax.dot_general(qh, k, nt, preferred_element_type=F32)

    def values(s):
        e = jnp.exp2(s - jnp.max(s, axis=-1, keepdims=True))
        return jnp.dot(e.astype(BF16), v, preferred_element_type=F32), jnp.sum(e, axis=-1, keepdims=True)

    halves = []
    for i in range(TQ_SUB):
        q = q_refs[i][0]
        lane = lax.broadcasted_iota(I32, q.shape, 1)
        zero = jnp.zeros_like(q)
        halves += [jnp.where(lane < DK, q, zero), jnp.where(lane >= DK, q, zero)]
    outs = []
    ahead = [scores(h) for h in halves[:SCORE_AHEAD]]
    for c in range(len(halves)):
        if c + SCORE_AHEAD < len(halves):
            ahead.append(scores(halves[c + SCORE_AHEAD]))
        outs.append(values(ahead.pop(0)))
    for i in range(TQ_SUB):
        (o1, l1), (o2, l2) = outs[2 * i], outs[2 * i + 1]
        o = o1 / l1 - o2 * (lam / l2)
        o_ref[0, i * TM_PROJ:(i + 1) * TM_PROJ, :] = (_rms(o) * sw_ref[...] * out_scale).astype(BF16)


def _diffattn(proj, lam, subln_w, out_scale):
    tq = TQ_SUB * TM_PROJ
    nq = T // tq

    def q_map(i, b, h, j):
        return (b, LC // TM_PROJ + j * TQ_SUB + i, 3 * NH + h)

    return pl.pallas_call(
        functools.partial(_diffattn_kernel, out_scale=out_scale),
        grid=(B, NH, nq),
        in_specs=[
            pl.BlockSpec(memory_space=pltpu.SMEM),
            *[pl.BlockSpec((1, TM_PROJ, 128), functools.partial(q_map, i)) for i in range(TQ_SUB)],
            pl.BlockSpec((1, RB, 128), lambda b, h, j: (b, 0, 4 * NH + h)),
            pl.BlockSpec((1, RB, 128), lambda b, h, j: (b, 0, 5 * NH + h)),
            pl.BlockSpec((1, HV), lambda b, h, j: (0, 0)),
        ],
        out_specs=pl.BlockSpec((1, tq, 128), lambda b, h, j: (b, j, h)),
        out_shape=jax.ShapeDtypeStruct((B, T, NH * HV), BF16),
        compiler_params=_cparams(("arbitrary", "arbitrary", "arbitrary")),
        name="diffattn",
    )(lam, *([proj] * TQ_SUB), proj, proj, subln_w)


def _route(bz):
    grp = []
    for g in range(NGRP):
        m = bz[g * EPG:(g + 1) * EPG]
        best = None
        for i in range(EPG):
            for k in range(i + 1, EPG):
                pair = m[i] + m[k]
                best = pair if best is None else jnp.maximum(best, pair)
        grp.append(best)
    gbest = grp[0]
    gsel = jnp.zeros_like(gbest, dtype=I32)
    for g in range(1, NGRP):
        better = grp[g] > gbest
        gsel = jnp.where(better, g, gsel)
        gbest = jnp.where(better, grp[g], gbest)
    cb = [bz[i] for i in range(EPG)]
    for g in range(1, NGRP):
        pick = gsel == g
        cb = [jnp.where(pick, bz[g * EPG + i], cb[i]) for i in range(EPG)]
    i1 = jnp.zeros_like(gsel)
    b1 = cb[0]
    for i in range(1, EPG):
        better = cb[i] > b1
        i1 = jnp.where(better, i, i1)
        b1 = jnp.where(better, cb[i], b1)
    neg = jnp.full_like(b1, -jnp.inf)
    rest = [jnp.where(i1 == i, neg, cb[i]) for i in range(EPG)]
    i2 = jnp.zeros_like(gsel)
    b2 = rest[0]
    for i in range(1, EPG):
        better = rest[i] > b2
        i2 = jnp.where(better, i, i2)
        b2 = jnp.where(better, rest[i], b2)
    lo = jnp.minimum(i1, i2)
    hi = jnp.maximum(i1, i2)
    code = lo * EPG + hi
    pair = jnp.full_like(gsel, len(PAIR_A) - 1)
    for p in range(len(PAIR_A) - 1):
        a, b = min(PAIR_A[p], PAIR_B[p]), max(PAIR_A[p], PAIR_B[p])
        pair = jnp.where(code == a * EPG + b, p, pair)
    return gsel * len(PAIR_A) + pair


def _ffn_prologue(x1, row0, mod, nfw_ref, rw_ref, hp_ref, logit_ref):
    rows = x1.shape[0]
    sh2 = mod[:, 3 * D:4 * D]
    sc2 = mod[:, 4 * D:5 * D]
    h2 = (_rms(x1) * nfw_ref[...]) * (1.0 + sc2) + sh2
    _store_slabs(hp_ref, h2, row0)
    logit_ref[:, row0:row0 + rows] = _dot_3pass(rw_ref[...], h2, (((1,), (1,)), ((), ())))


def _ffn_out_specs(tm, n_tiles_per_b):
    specs = [
        pl.BlockSpec((1, tm, D), lambda b, j: (b, j, 0)),
        pl.BlockSpec((tm * SLAB, 128), lambda b, j: (b * n_tiles_per_b + j, 0)),
        pl.BlockSpec((NE, tm), lambda b, j: (0, b * n_tiles_per_b + j)),
    ]
    shapes = [
        jax.ShapeDtypeStruct((B, T, D), F32),
        jax.ShapeDtypeStruct((N * SLAB, 128), F32),
        jax.ShapeDtypeStruct((NE, N), F32),
    ]
    return specs, shapes


def _route_kernel(logit_ref, bias_ref, dest_ref, ends_ref):
    r = logit_ref.shape[1]
    cls = _route([_sigmoid(logit_ref[e]) + bias_ref[e] for e in range(NE)])
    lane_incl = (lax.broadcasted_iota(I32, (128, 128), 0) <= lax.broadcasted_iota(I32, (128, 128), 1)).astype(BF16)
    rows_before = (lax.broadcasted_iota(I32, (r, r), 1) < lax.broadcasted_iota(I32, (r, r), 0)).astype(BF16)
    dest = jnp.zeros((r, 128), F32)
    start = jnp.zeros((1, 128), F32)
    ends = []
    for c in range(NCLS):
        onehot = jnp.where(cls == c, 1.0, 0.0)
        in_row = jnp.dot(onehot.astype(BF16), lane_incl, preferred_element_type=F32)
        row_tot = jnp.broadcast_to(in_row[:, 127:128], (r, 128))
        above = jnp.dot(rows_before, row_tot.astype(BF16), preferred_element_type=F32)
        dest = dest + onehot * (start + above + in_row - 1.0)
        start = start + jnp.sum(row_tot, axis=0, keepdims=True)
        ends.append(start)
    dest_ref[...] = dest.astype(I32)
    ends_ref[...] = jnp.concatenate(ends + [jnp.zeros((CLS_PAD - NCLS, 128), F32)], axis=0)


def _route_tokens(logits_t, router_b):
    r = N // 128
    return pl.pallas_call(
        _route_kernel,
        in_specs=[pl.BlockSpec((NE, r, 128), lambda: (0, 0, 0)), pl.BlockSpec(memory_space=pltpu.SMEM)],
        out_specs=[pl.BlockSpec((r, 128), lambda: (0, 0)), pl.BlockSpec((CLS_PAD, 128), lambda: (0, 0))],
        out_shape=[jax.ShapeDtypeStruct((r, 128), I32), jax.ShapeDtypeStruct((CLS_PAD, 128), F32)],
        compiler_params=pltpu.CompilerParams(vmem_limit_bytes=VMEM_LIMIT),
        name="route",
    )(logits_t.reshape(NE, r, 128), router_b)


def _outproj_kernel(ret_ref, dif_ref, w_ref, x_ref, mod_ref, nfw_ref, rw_ref, x1_ref, hp_ref, logit_ref):
    mod = mod_ref[0]

    mx = (jnp.dot(ret_ref[0], w_ref[0:NH * HV, :], preferred_element_type=F32)
          + jnp.dot(dif_ref[0], w_ref[NH * HV:, :], preferred_element_type=F32))
    x1 = x_ref[0] + mod[:, 2 * D:3 * D] * mx
    x1_ref[0] = x1
    _ffn_prologue(x1, 0, mod, nfw_ref, rw_ref, hp_ref, logit_ref)


def _outproj(ret, dif, w_out, x, mod3, nfw, rw_t):
    nj = T // TM_OUT
    out_specs, out_shapes = _ffn_out_specs(TM_OUT, nj)
    return pl.pallas_call(
        _outproj_kernel,
        grid=(B, nj),
        in_specs=[
            pl.BlockSpec((1, TM_OUT, NH * HV), lambda b, j: (b, j, 0)),
            pl.BlockSpec((1, TM_OUT, NH * HV), lambda b, j: (b, j, 0)),
            pl.BlockSpec((2 * NH * HV, D), lambda b, j: (0, 0)),
            pl.BlockSpec((1, TM_OUT, D), lambda b, j: (b, j, 0)),
            pl.BlockSpec((1, 1, 6 * D), lambda b, j: (b, 0, 0)),
            pl.BlockSpec((1, D), lambda b, j: (0, 0)),
            pl.BlockSpec((NE, D), lambda b, j: (0, 0)),
        ],
        out_specs=out_specs,
        out_shape=out_shapes,
        compiler_params=_cparams(("arbitrary", "arbitrary")),
        name="outproj",
    )(ret, dif, w_out, x, mod3, nfw, rw_t)


def _pool_kernel(x_ref, prev_ref, next_ref, mod_ref, nmw_ref, pw_ref, ps_ref, nfw_ref, rw_ref,
                 x1_ref, hp_ref, logit_ref, ext_ref):
    i = pl.program_id(1)
    last = pl.num_programs(1) - 1
    mod = mod_ref[0]
    sh1 = mod[:, 0:D]
    sc1 = mod[:, D:2 * D]

    def modnorm(v):
        return (_rms(v) * nmw_ref[...]) * (1.0 + sc1) + sh1

    x = x_ref[0]
    hc = modnorm(x)
    ext_ref[0:HALO, :] = jnp.where(i > 0, modnorm(prev_ref[0]), 0.0)
    ext_ref[HALO:HALO + TM_POOL, :] = hc
    ext_ref[HALO + TM_POOL:, :] = jnp.where(i < last, modnorm(next_ref[0]), 0.0)
    pos = i * TM_POOL + lax.broadcasted_iota(I32, (TM_POOL, 1), 0)
    mixed = []
    for gi, w in enumerate(POOL_WINDOWS):
        left = w // 2
        right = w - 1 - left
        cols = slice(gi * PG, (gi + 1) * PG)
        tot = None
        for d in range(-left, right + 1):
            part = ext_ref[HALO + d:HALO + d + TM_POOL, cols]
            tot = part if tot is None else tot + part
        cnt = (jnp.minimum(pos + right + 1, T) - jnp.maximum(pos - left, 0)).astype(F32)
        pooled = (tot * (1.0 / cnt) - hc[:, cols]).astype(BF16)
        mixed.append(jnp.dot(pooled, pw_ref[gi], preferred_element_type=F32))
    mixed = jnp.concatenate(mixed, axis=1) * ps_ref[...]
    x1 = x + mod[:, 2 * D:3 * D] * mixed
    x1_ref[0] = x1
    _ffn_prologue(x1, 0, mod, nfw_ref, rw_ref, hp_ref, logit_ref)


def _pool_layer(x, mod3, nmw, pool_w, pool_scale, nfw, rw_t):
    ni = T // TM_POOL
    hb = TM_POOL // HALO
    out_specs, out_shapes = _ffn_out_specs(TM_POOL, ni)
    return pl.pallas_call(
        _pool_kernel,
        grid=(B, ni),
        in_specs=[
            pl.BlockSpec((1, TM_POOL, D), lambda b, i: (b, i, 0)),
            pl.BlockSpec((1, HALO, D), lambda b, i: (b, jnp.maximum(i * hb - 1, 0), 0)),
            pl.BlockSpec((1, HALO, D), lambda b, i: (b, jnp.minimum((i + 1) * hb, T // HALO - 1), 0)),
            pl.BlockSpec((1, 1, 6 * D), lambda b, i: (b, 0, 0)),
            pl.BlockSpec((1, D), lambda b, i: (0, 0)),
            pl.BlockSpec((len(POOL_WINDOWS), PG, PG), lambda b, i: (0, 0, 0)),
            pl.BlockSpec((1, D), lambda b, i: (0, 0)),
            pl.BlockSpec((1, D), lambda b, i: (0, 0)),
            pl.BlockSpec((NE, D), lambda b, i: (0, 0)),
        ],
        out_specs=out_specs,
        out_shape=out_shapes,
        scratch_shapes=[pltpu.VMEM((TM_POOL + 2 * HALO, D), F32)],
        compiler_params=_cparams(("arbitrary", "arbitrary")),
        name="pool_layer",
    )(x, x, x, mod3, nmw, pool_w, pool_scale, nfw, rw_t)


def _tile_copy(src_ref, dst_ref, sem, s, d, rows=1):
    s0 = pl.multiple_of(s * SLAB, SLAB)
    d0 = pl.multiple_of(d * SLAB, SLAB)
    return pltpu.make_async_copy(src_ref.at[pl.ds(s0, rows * SLAB)], dst_ref.at[pl.ds(d0, rows * SLAB)], sem)


def _issue_tile_copies(idx_ref, base, rows, start_one):
    def group(g, carry):
        r0 = g * PERM_UNROLL
        ids = [idx_ref[base + r0 + u] for u in range(PERM_UNROLL)]
        for u in range(PERM_UNROLL):
            start_one(r0 + u, ids[u], u % 2)
        return carry

    lax.fori_loop(0, rows // PERM_UNROLL, group, 0)


def _dispatch_kernel(dest_ref, src_ref, dst_ref, sem):
    base = pl.program_id(0) * TM_PERM

    def start_one(r, d, priority):
        _tile_copy(src_ref, dst_ref, sem, r, d).start(priority=priority)

    _issue_tile_copies(dest_ref, base, TM_PERM, start_one)
    _tile_copy(src_ref, dst_ref, sem, 0, 0, TM_PERM).wait()


def _dispatch(dest, src):
    return pl.pallas_call(
        _dispatch_kernel,
        grid_spec=pltpu.PrefetchScalarGridSpec(
            num_scalar_prefetch=1,
            grid=(N // TM_PERM,),
            in_specs=[pl.BlockSpec((TM_PERM * SLAB, 128), lambda i, dest: (i, 0))],
            out_specs=pl.BlockSpec(memory_space=pl.ANY),
            scratch_shapes=[pltpu.SemaphoreType.DMA(())],
        ),
        out_shape=jax.ShapeDtypeStruct(src.shape, src.dtype),
        compiler_params=_cparams(("arbitrary",)),
        name="dispatch",
    )(dest, src)


def _combine_kernel(dest_ref, x_ref, ys_ref, mod_ref, fw_ref, o_ref, ybuf_ref, sem, *, final):
    i = pl.program_id(0)
    n = pl.num_programs(0)
    slot = i % 2

    def gather(step, to_slot):
        def start_one(r, d, priority):
            _tile_copy(ys_ref, ybuf_ref.at[to_slot], sem.at[to_slot], d, r).start(priority=priority)

        _issue_tile_copies(dest_ref, step * TM_COMB, TM_COMB, start_one)

    @pl.when(i == 0)
    def _():
        gather(0, 0)

    @pl.when(i + 1 < n)
    def _():
        gather(i + 1, 1 - slot)

    _tile_copy(ys_ref, ybuf_ref.at[slot], sem.at[slot], 0, 0, TM_COMB).wait()
    out = x_ref[...] + mod_ref[0][:, 5 * D:6 * D] * _load_slabs(ybuf_ref.at[slot], TM_COMB)
    if final:
        out = _rms(out) * fw_ref[...]
    o_ref[...] = out


def _combine(dest, x1, ys, mod3, final_w, final):
    per_b = T // TM_COMB
    return pl.pallas_call(
        functools.partial(_combine_kernel, final=final),
        grid_spec=pltpu.PrefetchScalarGridSpec(
            num_scalar_prefetch=1,
            grid=(N // TM_COMB,),
            in_specs=[
                pl.BlockSpec((TM_COMB, D), lambda i, dest: (i, 0)),
                pl.BlockSpec(memory_space=pl.ANY),
                pl.BlockSpec((1, 1, 6 * D), lambda i, dest: (i // per_b, 0, 0)),
                pl.BlockSpec((1, D), lambda i, dest: (0, 0)),
            ],
            out_specs=pl.BlockSpec((TM_COMB, D), lambda i, dest: (i, 0)),
            scratch_shapes=[pltpu.VMEM((2, TM_COMB * SLAB, 128), F32), pltpu.SemaphoreType.DMA((2,))],
        ),
        out_shape=jax.ShapeDtypeStruct((N, D), F32),
        compiler_params=_cparams(("arbitrary",)),
        name="combine",
    )(dest, x1.reshape(N, D), ys, mod3, final_w)


def _moe_kernel(tile_ref, lo_ref, hi_ref, ea_ref, eb_ref, ca_ref, cb_ref, na_ref, nb_ref,
                x_ref, rw_ref, wg_hbm, wu_hbm, wd_hbm, o_ref,
                sga_ref, sua_ref, sda_ref, sgb_ref, sub_ref, sdb_ref,
                ga_ref, ua_ref, da_ref, gb_ref, ub_ref, db_ref, sem, *, layer):
    m = pl.program_id(0)
    lo = lo_ref[m]
    hi = hi_ref[m]
    tile0 = tile_ref[m] * TM_MOE
    slots = ((ea_ref, ca_ref, na_ref, (sga_ref, sua_ref, sda_ref), (ga_ref, ua_ref, da_ref)),
             (eb_ref, cb_ref, nb_ref, (sgb_ref, sub_ref, sdb_ref), (gb_ref, ub_ref, db_ref)))

    def fetch(slot, expert):
        stage = slots[slot][3]
        return [pltpu.make_async_copy(w.at[layer, expert], s, sem.at[slot])
                for w, s in zip((wg_hbm, wu_hbm, wd_hbm), stage)]

    for slot, (e_ref, c_ref, n_ref, stage, work) in enumerate(slots):
        @pl.when(m == 0)
        def _():
            for cp in fetch(slot, e_ref[0]):
                cp.start()

        @pl.when(c_ref[m] == 1)
        def _():
            for cp in fetch(slot, 0):
                cp.wait()
            for s, w in zip(stage, work):
                w[...] = s[...].astype(BF16)

            @pl.when(n_ref[m] >= 0)
            def _():
                for cp in fetch(slot, n_ref[m]):
                    cp.start()

    def value():
        hf = _load_slabs(x_ref, TM_MOE)
        h = hf.astype(BF16)

        def up(w_ref):
            return jnp.dot(h, w_ref[...], preferred_element_type=F32)

        def down(g, u, d_ref):
            return jnp.dot((_silu(g) * u).astype(BF16), d_ref[...], preferred_element_type=F32)

        g_a, u_a, g_b, u_b = up(ga_ref), up(ua_ref), up(gb_ref), up(ub_ref)
        y_a = down(g_a, u_a, da_ref)
        y_b = down(g_b, u_b, db_ref)
        s_a = _sigmoid(jnp.sum(hf * rw_ref[pl.ds(ea_ref[m], 1), :], axis=1, keepdims=True))
        s_b = _sigmoid(jnp.sum(hf * rw_ref[pl.ds(eb_ref[m], 1), :], axis=1, keepdims=True))
        denom = s_a + s_b
        val = (s_a / denom) * y_a + (s_b / denom) * y_b
        row = tile0 + lax.broadcasted_iota(I32, (TM_MOE, 1), 0)
        return val, (row >= lo) & (row < hi)

    @pl.when((hi > lo) & (lo == tile0))
    def _():
        val, seg = value()
        _store_slabs(o_ref, jnp.where(seg, val, 0.0))

    @pl.when((hi > lo) & (lo != tile0))
    def _():
        val, seg = value()
        _store_slabs(o_ref, jnp.where(seg, val, _load_slabs(o_ref, TM_MOE)))


def _moe_sorted(units, xs, rw_t, wg, wu, wd, layer):
    row_map = lambda m, t, *_: (t[m], 0)
    mats = lambda dt: [pltpu.VMEM((D, DE), dt), pltpu.VMEM((D, DE), dt), pltpu.VMEM((DE, D), dt)]
    return pl.pallas_call(
        functools.partial(_moe_kernel, layer=layer),
        grid_spec=pltpu.PrefetchScalarGridSpec(
            num_scalar_prefetch=len(units),
            grid=(N_UNITS,),
            in_specs=[
                pl.BlockSpec((TM_MOE * SLAB, 128), row_map),
                pl.BlockSpec((NE, D), lambda m, *_: (0, 0)),
                pl.BlockSpec(memory_space=pl.ANY),
                pl.BlockSpec(memory_space=pl.ANY),
                pl.BlockSpec(memory_space=pl.ANY),
            ],
            out_specs=pl.BlockSpec((TM_MOE * SLAB, 128), row_map),
            scratch_shapes=mats(F32) + mats(F32) + mats(BF16) + mats(BF16) + [pltpu.SemaphoreType.DMA((2,))],
        ),
        out_shape=jax.ShapeDtypeStruct((N * SLAB, 128), F32),
        compiler_params=_cparams(("arbitrary",)),
        name="moe_sorted",
    )(*units, xs, rw_t, wg, wu, wd)


def _moe_units(ends_f):
    ends = ends_f[:NCLS, 0].astype(I32)
    offs = jnp.concatenate([jnp.zeros((1,), I32), ends[:-1]])
    starts = jnp.sort(jnp.concatenate([jnp.arange(N // TM_MOE, dtype=I32) * TM_MOE, offs]))
    u_lo = starts
    u_hi = jnp.concatenate([starts[1:], jnp.full((1,), N, I32)])
    u_tile = jnp.minimum(u_lo // TM_MOE, N // TM_MOE - 1)
    u_cls = jnp.minimum(jnp.sum(ends[None, :] <= u_lo[:, None], axis=1), NCLS - 1).astype(I32)
    grp = u_cls // len(PAIR_A)
    pair = u_cls % len(PAIR_A)
    e_a = grp * EPG + jnp.take(jnp.array(PAIR_A, I32), pair)
    e_b = grp * EPG + jnp.take(jnp.array(PAIR_B, I32), pair)
    one = jnp.ones((1,), I32)
    idx = jnp.arange(N_UNITS, dtype=I32)

    def changes(e):
        chg = jnp.concatenate([one, (e[1:] != e[:-1]).astype(I32)])
        at = jnp.where(chg == 1, idx, N_UNITS)
        nxt_at = jnp.concatenate([lax.cummin(at, reverse=True)[1:], jnp.full((1,), N_UNITS, I32)])
        nxt = jnp.where(nxt_at < N_UNITS, jnp.take(e, jnp.minimum(nxt_at, N_UNITS - 1)), -1)
        return chg, nxt

    chg_a, nxt_a = changes(e_a)
    chg_b, nxt_b = changes(e_b)
    return u_tile, u_lo, u_hi, e_a, e_b, chg_a, chg_b, nxt_a, nxt_b


def _moe_layer(x1, hp, logits, rw_t, router_b, wg, wu, wd, layer, mod3, final_w, final):
    dest, ends = _route_tokens(logits, router_b)
    dest = dest.reshape(N)
    units = _moe_units(ends)
    xs = _dispatch(dest, hp)
    ys = _moe_sorted(units, xs, rw_t, wg, wu, wd, layer)
    return _combine(dest, x1, ys, mod3, final_w, final).reshape(B, T, D)


def _rope_tables():
    half = 16
    inv = ROPE_BASE ** (-np.arange(half, dtype=np.float64) / half)
    t = np.arange(T)
    ang_r = (t // GRID_W)[:, None] * inv[None, :]
    ang_c = (t % GRID_W)[:, None] * inv[None, :]
    ang = np.concatenate([ang_r, ang_r, ang_c, ang_c], axis=1)
    sign = np.tile(np.concatenate([-np.ones(half), np.ones(half)]), 2)
    cos = np.concatenate([np.ones((LC, 64)), np.cos(ang)], axis=0)
    sin = np.concatenate([np.zeros((LC, 64)), np.sin(ang) * sign[None, :]], axis=0)
    return (jnp.asarray(np.tile(cos, (1, 2)), dtype=F32), jnp.asarray(np.tile(sin, (1, 2)), dtype=F32))


def _permute_w_in(w):
    rq = w[:, 0:256].reshape(D, NH, DK)
    dq = w[:, 256:768]
    rg = w[:, 768:1280]
    rk = w[:, 1280:1536].reshape(D, NH, DK)
    rv = w[:, 1536:2048]
    dk = w[:, 2048:2560]
    dv = w[:, 2560:3072]
    qk = jnp.concatenate([rq, rk * (DK ** -0.5)], axis=2).reshape(D, NH * 2 * DK)
    return jnp.concatenate([qk, rv, rg, dq * (DK ** -0.5 * math.log2(math.e)), dk, dv], axis=1).astype(BF16)


def kernel(x, c, ctx, c_ctx, ada_w, ada_b, norm_mix_w, norm_ffn_w, w_in, w_out, ret_log_decay, diff_lambda,
           diff_subln_w, pool_w, pool_scale, router_w, router_b, moe_w_gate, moe_w_up, moe_w_down, final_norm_w):
    assert x.shape == (B, T, D) and ctx.shape == (B, LC, D) and ada_w.shape[0] == 2
    cc = jnp.concatenate([c, c_ctx[None, :], jnp.zeros((16 - B - 1, D), F32)], axis=0)
    mod = _ada_mod(cc, ada_w, ada_b)
    rw_t = router_w.T
    fw = final_norm_w.reshape(1, D)
    experts = (moe_w_gate, moe_w_up, moe_w_down)

    mod0 = mod[0].reshape(16, 1, 6 * D)
    cos_t, sin_t = _rope_tables()
    proj = _inproj(x, ctx, mod0, norm_mix_w[0:1], _permute_w_in(w_in[0]), cos_t, sin_t)
    ret = _retention(proj, ret_log_decay[0])
    lam_init = 0.8 - 0.6 * math.exp(-0.3 * 0)
    lv = diff_lambda[0]
    lam = jnp.exp(jnp.sum(lv[0] * lv[1])) - jnp.exp(jnp.sum(lv[2] * lv[3])) + lam_init
    dif = _diffattn(proj, lam.reshape(1), diff_subln_w[0:1], 1.0 - lam_init)
    x1, hp, logits = _outproj(ret, dif, w_out[0].astype(BF16), x, mod0, norm_ffn_w[0:1], rw_t)
    x2 = _moe_layer(x1, hp, logits, rw_t, router_b, *experts, 0, mod0, fw, False)

    mod1 = mod[1].reshape(16, 1, 6 * D)
    x3, hp, logits = _pool_layer(x2, mod1, norm_mix_w[1:2], pool_w[0].astype(BF16), pool_scale[0:1],
                                 norm_ffn_w[1:2], rw_t)
    return _moe_layer(x3, hp, logits, rw_t, router_b, *experts, 1, mod1, fw, True)
```

```python
import functools
import math

import jax
import jax.numpy as jnp
import numpy as np
from jax import lax
from jax.experimental import pallas as pl
from jax.experimental.pallas import tpu as pltpu

F32 = jnp.float32
BF16 = jnp.bfloat16
I32 = jnp.int32

D = 1024
B = 8
T = 2048
N = B * T
GRID_W = 64
LC = 256
EPS = 1e-6
ROPE_BASE = 10000.0
NH = 4
DK = 64
HV = 128
CH = 256
RB = LC + T
NCH = RB // CH
POOL_WINDOWS = (2, 4, 8, 16)
PG = D // len(POOL_WINDOWS)
NE = 16
NGRP = 4
EPG = NE // NGRP
DE = 512
IN_W = 3072
HALO = 8

PAIR_A = (0, 0, 0, 1, 1, 3)
PAIR_B = (1, 2, 3, 3, 2, 2)
NCLS = NGRP * len(PAIR_A)
CLS_PAD = 32
SLAB = D // 128

TM_PROJ = 256
PROJ_SUB = 3
TM_OUT = 1024
SUB_OUT = 512
TQ_SUB = 8
SCORE_AHEAD = 1
TM_POOL = 1024
TM_MOE = 256
N_UNITS = N // TM_MOE + NCLS
TM_PERM = 2048
TM_COMB = 1024
PERM_UNROLL = 16
VMEM_LIMIT = 56 * 1024 * 1024


def _cparams(sem):
    return pltpu.CompilerParams(dimension_semantics=sem, vmem_limit_bytes=VMEM_LIMIT)


def _sigmoid(x):
    return 1.0 / (1.0 + jnp.exp(-x))


def _silu(x):
    return x * _sigmoid(x)


def _rms(x):
    return x * lax.rsqrt(jnp.mean(x * x, axis=-1, keepdims=True) + EPS)


def _dot_3pass(a, b, dims):
    a_hi = a.astype(BF16)
    b_hi = b.astype(BF16)
    a_lo = (a - a_hi.astype(F32)).astype(BF16)
    b_lo = (b - b_hi.astype(F32)).astype(BF16)

    def dot(x, y):
        return lax.dot_general(x, y, dims, preferred_element_type=F32)

    return dot(a_hi, b_hi) + (dot(a_lo, b_hi) + dot(a_hi, b_lo))


def _load_slabs(ref, rows):
    return jnp.concatenate([ref[pl.ds(s, rows, stride=SLAB), :] for s in range(SLAB)], axis=1)


def _store_slabs(ref, val, row0=0):
    rows = val.shape[0]
    for s in range(SLAB):
        ref[pl.ds(row0 * SLAB + s, rows, stride=SLAB), :] = val[:, s * 128:(s + 1) * 128]


def _ada_kernel(cc_ref, w_ref, b_ref, o_ref):
    s = _silu(cc_ref[...])
    o_ref[0] = _dot_3pass(s, w_ref[0], (((1,), (0,)), ((), ()))) + b_ref[0]


def _ada_mod(cc, ada_w, ada_b):
    depth = ada_w.shape[0]
    tn = 1536
    return pl.pallas_call(
        _ada_kernel,
        grid=(depth, 6 * D // tn),
        in_specs=[
            pl.BlockSpec((16, D), lambda l, n: (0, 0)),
            pl.BlockSpec((1, D, tn), lambda l, n: (l, 0, n)),
            pl.BlockSpec((1, 1, tn), lambda l, n: (l, 0, n)),
        ],
        out_specs=pl.BlockSpec((1, 16, tn), lambda l, n: (l, 0, n)),
        out_shape=jax.ShapeDtypeStruct((depth, 16, 6 * D), F32),
        compiler_params=_cparams(("arbitrary", "arbitrary")),
        name="ada_mod",
    )(cc, ada_w, ada_b.reshape(depth, 1, 6 * D))


def _rope(seg, cos, sin_signed, lo_mask):
    w = seg.shape[1]
    from_hi = pltpu.roll(seg, w - 16, axis=1)
    from_lo = pltpu.roll(seg, 16, axis=1)
    partner = jnp.where(lo_mask, from_hi, from_lo)
    reps = w // cos.shape[1]
    c = jnp.concatenate([cos] * reps, axis=1)
    s = jnp.concatenate([sin_signed] * reps, axis=1)
    return seg * c + partner * s


def _inproj_kernel(*refs):
    x_refs = refs[:PROJ_SUB]
    c_ref, mod_ref, cmod_ref, nw_ref, w_ref, cos_ref, sin_ref, o_ref = refs[PROJ_SUB:]
    is_ctx = pl.program_id(1) == 0
    parts = []
    for s in range(PROJ_SUB):
        xt = x_refs[s][0]
        sh = mod_ref[0, :, 0:D]
        sc = mod_ref[0, :, D:2 * D]
        if s == 0:
            xt = jnp.where(is_ctx, c_ref[0], xt)
            sh = jnp.where(is_ctx, cmod_ref[0, :, 0:D], sh)
            sc = jnp.where(is_ctx, cmod_ref[0, :, D:2 * D], sc)
        parts.append(((_rms(xt) * nw_ref[...]) * (1.0 + sc) + sh).astype(BF16))
    hb = jnp.concatenate(parts, axis=0)
    lane = lax.broadcasted_iota(I32, (PROJ_SUB * TM_PROJ, 512), 1)
    lo_mask = (lane % 32) < 16
    cos = cos_ref[...]
    sin = sin_ref[...]

    def project(g):
        return jnp.dot(hb, w_ref[:, g * 512:(g + 1) * 512], preferred_element_type=F32)

    seg = project(0)
    for g in range(6):
        seg_next = project(g + 1) if g + 1 < 6 else None
        if g in (0, 3, 4):
            seg = _rope(seg, cos, sin, lo_mask)
        o_ref[0, :, g * 512:(g + 1) * 512] = seg.astype(BF16)
        seg = seg_next


def _inproj(x, ctx, mod3, norm_w, w_perm, cos_t, sin_t):
    tm = PROJ_SUB * TM_PROJ
    nj = RB // tm

    def x_map(s, b, j):
        return (b, jnp.maximum(PROJ_SUB * j + s - LC // TM_PROJ, 0), 0)

    return pl.pallas_call(
        _inproj_kernel,
        grid=(B, nj),
        in_specs=[
            *[pl.BlockSpec((1, TM_PROJ, D), functools.partial(x_map, s)) for s in range(PROJ_SUB)],
            pl.BlockSpec((1, LC, D), lambda b, j: (b, 0, 0)),
            pl.BlockSpec((1, 1, 2 * D), lambda b, j: (b, 0, 0)),
            pl.BlockSpec((1, 1, 2 * D), lambda b, j: (B, 0, 0)),
            pl.BlockSpec((1, D), lambda b, j: (0, 0)),
            pl.BlockSpec((D, IN_W), lambda b, j: (0, 0)),
            pl.BlockSpec((tm, 128), lambda b, j: (j, 0)),
            pl.BlockSpec((tm, 128), lambda b, j: (j, 0)),
        ],
        out_specs=pl.BlockSpec((1, tm, IN_W), lambda b, j: (b, j, 0)),
        out_shape=jax.ShapeDtypeStruct((B, RB, IN_W), BF16),
        compiler_params=_cparams(("arbitrary", "arbitrary")),
        name="inproj",
    )(*([x] * PROJ_SUB), ctx, mod3, mod3, norm_w, w_perm, cos_t, sin_t)


def _retention_kernel(ld_ref, qk_ref, v_ref, g_ref, o_ref, st_ref, kdec_ref, qdec_ref, mask_ref, cdec_ref):
    h = pl.program_id(0)
    lane = lax.broadcasted_iota(I32, (CH, 128), 1)
    fwd_lane = lane < DK

    @pl.when(pl.program_id(1) == 0)
    def _():
        lgf = ld_ref[0, h]
        lgb = ld_ref[1, h]
        pos = lax.broadcasted_iota(I32, (CH, 128), 0).astype(F32)
        kdec_ref[...] = jnp.where(fwd_lane, jnp.exp(lgf * (CH - 1 - pos)), jnp.exp(lgb * pos))
        qdec_ref[...] = jnp.where(fwd_lane, jnp.exp(lgf * (pos + 1.0)), jnp.exp(lgb * (CH - pos)))
        ii = lax.broadcasted_iota(I32, (CH, CH), 0)
        jj = lax.broadcasted_iota(I32, (CH, CH), 1)
        gap = (ii - jj).astype(F32)
        mask_ref[...] = (jnp.where(gap >= 0, jnp.exp(lgf * jnp.maximum(gap, 0.0)), 0.0)
                         + jnp.where(gap <= 0, jnp.exp(lgb * jnp.maximum(-gap, 0.0)), 0.0))
        ones = jnp.ones((DK, 128), F32)
        cdec_ref[0:DK, :] = jnp.exp(lgf * CH * ones)
        cdec_ref[DK:, :] = jnp.exp(lgb * CH * ones)

    kdec = kdec_ref[...]
    qdec = qdec_ref[...]
    mask = mask_ref[...]
    cf = cdec_ref[0:DK, :]
    cb = cdec_ref[DK:, :]

    def chunk(n):
        a = qk_ref[0, n * CH:(n + 1) * CH, :].astype(F32)
        swapped = pltpu.roll(a, DK, axis=1)
        return a, swapped

    kv = []
    for n in range(NCH):
        a, swapped = chunk(n)
        kk = jnp.where(fwd_lane, swapped, a)
        kb = (kk * kdec).astype(BF16)
        vn = v_ref[0, n * CH:(n + 1) * CH, :]
        kv.append(lax.dot_general(kb, vn, (((0,), (0,)), ((), ())), preferred_element_type=F32))
    sf = kv[0][:DK]
    for n in range(1, NCH):
        st_ref[n, 0:DK, :] = sf
        sf = cf * sf + kv[n][:DK]
    sb = kv[0][DK:]
    for n in range(NCH - 1, 0, -1):
        st_ref[n, DK:2 * DK, :] = sb
        sb = cb * sb + kv[n][DK:]

    for n in range(1, NCH):
        a, swapped = chunk(n)
        q = a[:, :DK].astype(BF16)
        k = swapped[:, :DK].astype(BF16)
        scores = lax.dot_general(q, k, (((1,), (1,)), ((), ())), preferred_element_type=F32)
        p = (scores * mask).astype(BF16)
        vn = v_ref[0, n * CH:(n + 1) * CH, :]
        qq = jnp.where(fwd_lane, a, swapped)
        qd = (qq * qdec).astype(BF16)
        o = (jnp.dot(p, vn, preferred_element_type=F32)
             + jnp.dot(qd, st_ref[n].astype(BF16), preferred_element_type=F32))
        gate = g_ref[0, n * CH:(n + 1) * CH, :].astype(F32)
        o_ref[0, (n - 1) * CH:n * CH, :] = (_rms(o) * _silu(gate)).astype(BF16)


def _retention(proj, log_decay):
    return pl.pallas_call(
        _retention_kernel,
        grid=(NH, B),
        in_specs=[
            pl.BlockSpec(memory_space=pltpu.SMEM),
            pl.BlockSpec((1, RB, 128), lambda h, b: (b, 0, h)),
            pl.BlockSpec((1, RB, 128), lambda h, b: (b, 0, NH + h)),
            pl.BlockSpec((1, RB, 128), lambda h, b: (b, 0, 2 * NH + h)),
        ],
        out_specs=pl.BlockSpec((1, T, 128), lambda h, b: (b, 0, h)),
        out_shape=jax.ShapeDtypeStruct((B, T, NH * HV), BF16),
        scratch_shapes=[pltpu.VMEM((NCH, 128, 128), F32), pltpu.VMEM((CH, 128), F32), pltpu.VMEM((CH, 128), F32),
                        pltpu.VMEM((CH, CH), F32), pltpu.VMEM((2 * DK, 128), F32)],
        compiler_params=_cparams(("arbitrary", "arbitrary")),
        name="retention",
    )(log_decay, proj, proj, proj)


def _diffattn_kernel(lam_ref, *refs, out_scale):
    q_refs = refs[:TQ_SUB]
    k_ref, v_ref, sw_ref, o_ref = refs[TQ_SUB:]
    lam = lam_ref[0]
    k = k_ref[0]
    v = v_ref[0]
    nt = (((1,), (1,)), ((), ()))

    def scores(qh):
        return lax.dot_general(qh, k, nt, preferred_element_type=F32)

    def values(s):
        e = jnp.exp2(s - jnp.max(s, axis=-1, keepdims=True))
        return jnp.dot(e.astype(BF16), v, preferred_element_type=F32), jnp.sum(e, axis=-1, keepdims=True)

    halves = []
    for i in range(TQ_SUB):
        q = q_refs[i][0]
        lane = lax.broadcasted_iota(I32, q.shape, 1)
        zero = jnp.zeros_like(q)
        halves += [jnp.where(lane < DK, q, zero), jnp.where(lane >= DK, q, zero)]
    outs = []
    ahead = [scores(h) for h in halves[:SCORE_AHEAD]]
    for c in range(len(halves)):
        if c + SCORE_AHEAD < len(halves):
            ahead.append(scores(halves[c + SCORE_AHEAD]))
        outs.append(values(ahead.pop(0)))
    for i in range(TQ_SUB):
        (o1, l1), (o2, l2) = outs[2 * i], outs[2 * i + 1]
        o = o1 / l1 - o2 * (lam / l2)
        o_ref[0, i * TM_PROJ:(i + 1) * TM_PROJ, :] = (_rms(o) * sw_ref[...] * out_scale).astype(BF16)


def _diffattn(proj, lam, subln_w, out_scale):
    tq = TQ_SUB * TM_PROJ
    nq = T // tq

    def q_map(i, b, h, j):
        return (b, LC // TM_PROJ + j * TQ_SUB + i, 3 * NH + h)

    return pl.pallas_call(
        functools.partial(_diffattn_kernel, out_scale=out_scale),
        grid=(B, NH, nq),
        in_specs=[
            pl.BlockSpec(memory_space=pltpu.SMEM),
            *[pl.BlockSpec((1, TM_PROJ, 128), functools.partial(q_map, i)) for i in range(TQ_SUB)],
            pl.BlockSpec((1, RB, 128), lambda b, h, j: (b, 0, 4 * NH + h)),
            pl.BlockSpec((1, RB, 128), lambda b, h, j: (b, 0, 5 * NH + h)),
            pl.BlockSpec((1, HV), lambda b, h, j: (0, 0)),
        ],
        out_specs=pl.BlockSpec((1, tq, 128), lambda b, h, j: (b, j, h)),
        out_shape=jax.ShapeDtypeStruct((B, T, NH * HV), BF16),
        compiler_params=_cparams(("arbitrary", "arbitrary", "arbitrary")),
        name="diffattn",
    )(lam, *([proj] * TQ_SUB), proj, proj, subln_w)


def _route(bz):
    grp = []
    for g in range(NGRP):
        m = bz[g * EPG:(g + 1) * EPG]
        best = None
        for i in range(EPG):
            for k in range(i + 1, EPG):
                pair = m[i] + m[k]
                best = pair if best is None else jnp.maximum(best, pair)
        grp.append(best)
    gbest = grp[0]
    gsel = jnp.zeros_like(gbest, dtype=I32)
    for g in range(1, NGRP):
        better = grp[g] > gbest
        gsel = jnp.where(better, g, gsel)
        gbest = jnp.where(better, grp[g], gbest)
    cb = [bz[i] for i in range(EPG)]
    for g in range(1, NGRP):
        pick = gsel == g
        cb = [jnp.where(pick, bz[g * EPG + i], cb[i]) for i in range(EPG)]
    i1 = jnp.zeros_like(gsel)
    b1 = cb[0]
    for i in range(1, EPG):
        better = cb[i] > b1
        i1 = jnp.where(better, i, i1)
        b1 = jnp.where(better, cb[i], b1)
    neg = jnp.full_like(b1, -jnp.inf)
    rest = [jnp.where(i1 == i, neg, cb[i]) for i in range(EPG)]
    i2 = jnp.zeros_like(gsel)
    b2 = rest[0]
    for i in range(1, EPG):
        better = rest[i] > b2
        i2 = jnp.where(better, i, i2)
        b2 = jnp.where(better, rest[i], b2)
    lo = jnp.minimum(i1, i2)
    hi = jnp.maximum(i1, i2)
    code = lo * EPG + hi
    pair = jnp.full_like(gsel, len(PAIR_A) - 1)
    for p in range(len(PAIR_A) - 1):
        a, b = min(PAIR_A[p], PAIR_B[p]), max(PAIR_A[p], PAIR_B[p])
        pair = jnp.where(code == a * EPG + b, p, pair)
    return gsel * len(PAIR_A) + pair


def _ffn_prologue(x1, row0, mod, nfw_ref, rw_ref, hp_ref, logit_ref):
    rows = x1.shape[0]
    sh2 = mod[:, 3 * D:4 * D]
    sc2 = mod[:, 4 * D:5 * D]
    h2 = (_rms(x1) * nfw_ref[...]) * (1.0 + sc2) + sh2
    _store_slabs(hp_ref, h2, row0)
    logit_ref[:, row0:row0 + rows] = _dot_3pass(rw_ref[...], h2, (((1,), (1,)), ((), ())))


def _ffn_out_specs(tm, n_tiles_per_b):
    specs = [
        pl.BlockSpec((1, tm, D), lambda b, j: (b, j, 0)),
        pl.BlockSpec((tm * SLAB, 128), lambda b, j: (b * n_tiles_per_b + j, 0)),
        pl.BlockSpec((NE, tm), lambda b, j: (0, b * n_tiles_per_b + j)),
    ]
    shapes = [
        jax.ShapeDtypeStruct((B, T, D), F32),
        jax.ShapeDtypeStruct((N * SLAB, 128), F32),
        jax.ShapeDtypeStruct((NE, N), F32),
    ]
    return specs, shapes


def _route_kernel(logit_ref, bias_ref, dest_ref, ends_ref):
    r = logit_ref.shape[1]
    cls = _route([_sigmoid(logit_ref[e]) + bias_ref[e] for e in range(NE)])
    lane_incl = (lax.broadcasted_iota(I32, (128, 128), 0) <= lax.broadcasted_iota(I32, (128, 128), 1)).astype(BF16)
    rows_before = (lax.broadcasted_iota(I32, (r, r), 1) < lax.broadcasted_iota(I32, (r, r), 0)).astype(BF16)
    dest = jnp.zeros((r, 128), F32)
    start = jnp.zeros((1, 128), F32)
    ends = []
    for c in range(NCLS):
        onehot = jnp.where(cls == c, 1.0, 0.0)
        in_row = jnp.dot(onehot.astype(BF16), lane_incl, preferred_element_type=F32)
        row_tot = jnp.broadcast_to(in_row[:, 127:128], (r, 128))
        above = jnp.dot(rows_before, row_tot.astype(BF16), preferred_element_type=F32)
        dest = dest + onehot * (start + above + in_row - 1.0)
        start = start + jnp.sum(row_tot, axis=0, keepdims=True)
        ends.append(start)
    dest_ref[...] = dest.astype(I32)
    ends_ref[...] = jnp.concatenate(ends + [jnp.zeros((CLS_PAD - NCLS, 128), F32)], axis=0)


def _route_tokens(logits_t, router_b):
    r = N // 128
    return pl.pallas_call(
        _route_kernel,
        in_specs=[pl.BlockSpec((NE, r, 128), lambda: (0, 0, 0)), pl.BlockSpec(memory_space=pltpu.SMEM)],
        out_specs=[pl.BlockSpec((r, 128), lambda: (0, 0)), pl.BlockSpec((CLS_PAD, 128), lambda: (0, 0))],
        out_shape=[jax.ShapeDtypeStruct((r, 128), I32), jax.ShapeDtypeStruct((CLS_PAD, 128), F32)],
        compiler_params=pltpu.CompilerParams(vmem_limit_bytes=VMEM_LIMIT),
        name="route",
    )(logits_t.reshape(NE, r, 128), router_b)


def _outproj_kernel(ret_ref, dif_ref, w_ref, x_ref, mod_ref, nfw_ref, rw_ref, x1_ref, hp_ref, logit_ref):
    mod = mod_ref[0]

    def mix(r0):
        rows = slice(r0, r0 + SUB_OUT)
        return (jnp.dot(ret_ref[0, rows, :], w_ref[0:NH * HV, :], preferred_element_type=F32)
                + jnp.dot(dif_ref[0, rows, :], w_ref[NH * HV:, :], preferred_element_type=F32))

    mx = mix(0)
    for r0 in range(0, TM_OUT, SUB_OUT):
        mx_next = mix(r0 + SUB_OUT) if r0 + SUB_OUT < TM_OUT else None
        x1 = x_ref[0, r0:r0 + SUB_OUT, :] + mod[:, 2 * D:3 * D] * mx
        x1_ref[0, r0:r0 + SUB_OUT, :] = x1
        _ffn_prologue(x1, r0, mod, nfw_ref, rw_ref, hp_ref, logit_ref)
        mx = mx_next


def _outproj(ret, dif, w_out, x, mod3, nfw, rw_t):
    nj = T // TM_OUT
    out_specs, out_shapes = _ffn_out_specs(TM_OUT, nj)
    return pl.pallas_call(
        _outproj_kernel,
        grid=(B, nj),
        in_specs=[
            pl.BlockSpec((1, TM_OUT, NH * HV), lambda b, j: (b, j, 0)),
            pl.BlockSpec((1, TM_OUT, NH * HV), lambda b, j: (b, j, 0)),
            pl.BlockSpec((2 * NH * HV, D), lambda b, j: (0, 0)),
            pl.BlockSpec((1, TM_OUT, D), lambda b, j: (b, j, 0)),
            pl.BlockSpec((1, 1, 6 * D), lambda b, j: (b, 0, 0)),
            pl.BlockSpec((1, D), lambda b, j: (0, 0)),
            pl.BlockSpec((NE, D), lambda b, j: (0, 0)),
        ],
        out_specs=out_specs,
        out_shape=out_shapes,
        compiler_params=_cparams(("arbitrary", "arbitrary")),
        name="outproj",
    )(ret, dif, w_out, x, mod3, nfw, rw_t)


def _pool_kernel(x_ref, prev_ref, next_ref, mod_ref, nmw_ref, pw_ref, ps_ref, nfw_ref, rw_ref,
                 x1_ref, hp_ref, logit_ref, ext_ref):
    i = pl.program_id(1)
    last = pl.num_programs(1) - 1
    mod = mod_ref[0]
    sh1 = mod[:, 0:D]
    sc1 = mod[:, D:2 * D]

    def modnorm(v):
        return (_rms(v) * nmw_ref[...]) * (1.0 + sc1) + sh1

    x = x_ref[0]
    hc = modnorm(x)
    ext_ref[0:HALO, :] = jnp.where(i > 0, modnorm(prev_ref[0]), 0.0)
    ext_ref[HALO:HALO + TM_POOL, :] = hc
    ext_ref[HALO + TM_POOL:, :] = jnp.where(i < last, modnorm(next_ref[0]), 0.0)
    pos = i * TM_POOL + lax.broadcasted_iota(I32, (TM_POOL, 1), 0)
    mixed = []
    for gi, w in enumerate(POOL_WINDOWS):
        left = w // 2
        right = w - 1 - left
        cols = slice(gi * PG, (gi + 1) * PG)
        tot = None
        for d in range(-left, right + 1):
            part = ext_ref[HALO + d:HALO + d + TM_POOL, cols]
            tot = part if tot is None else tot + part
        cnt = (jnp.minimum(pos + right + 1, T) - jnp.maximum(pos - left, 0)).astype(F32)
        pooled = (tot * (1.0 / cnt) - hc[:, cols]).astype(BF16)
        mixed.append(jnp.dot(pooled, pw_ref[gi], preferred_element_type=F32))
    mixed = jnp.concatenate(mixed, axis=1) * ps_ref[...]
    x1 = x + mod[:, 2 * D:3 * D] * mixed
    x1_ref[0] = x1
    _ffn_prologue(x1, 0, mod, nfw_ref, rw_ref, hp_ref, logit_ref)


def _pool_layer(x, mod3, nmw, pool_w, pool_scale, nfw, rw_t):
    ni = T // TM_POOL
    hb = TM_POOL // HALO
    out_specs, out_shapes = _ffn_out_specs(TM_POOL, ni)
    return pl.pallas_call(
        _pool_kernel,
        grid=(B, ni),
        in_specs=[
            pl.BlockSpec((1, TM_POOL, D), lambda b, i: (b, i, 0)),
            pl.BlockSpec((1, HALO, D), lambda b, i: (b, jnp.maximum(i * hb - 1, 0), 0)),
            pl.BlockSpec((1, HALO, D), lambda b, i: (b, jnp.minimum((i + 1) * hb, T // HALO - 1), 0)),
            pl.BlockSpec((1, 1, 6 * D), lambda b, i: (b, 0, 0)),
            pl.BlockSpec((1, D), lambda b, i: (0, 0)),
            pl.BlockSpec((len(POOL_WINDOWS), PG, PG), lambda b, i: (0, 0, 0)),
            pl.BlockSpec((1, D), lambda b, i: (0, 0)),
            pl.BlockSpec((1, D), lambda b, i: (0, 0)),
            pl.BlockSpec((NE, D), lambda b, i: (0, 0)),
        ],
        out_specs=out_specs,
        out_shape=out_shapes,
        scratch_shapes=[pltpu.VMEM((TM_POOL + 2 * HALO, D), F32)],
        compiler_params=_cparams(("arbitrary", "arbitrary")),
        name="pool_layer",
    )(x, x, x, mod3, nmw, pool_w, pool_scale, nfw, rw_t)


def _tile_copy(src_ref, dst_ref, sem, s, d, rows=1):
    s0 = pl.multiple_of(s * SLAB, SLAB)
    d0 = pl.multiple_of(d * SLAB, SLAB)
    return pltpu.make_async_copy(src_ref.at[pl.ds(s0, rows * SLAB)], dst_ref.at[pl.ds(d0, rows * SLAB)], sem)


def _issue_tile_copies(idx_ref, base, rows, start_one):
    def group(g, carry):
        r0 = g * PERM_UNROLL
        ids = [idx_ref[base + r0 + u] for u in range(PERM_UNROLL)]
        for u in range(PERM_UNROLL):
            start_one(r0 + u, ids[u], u % 2)
        return carry

    lax.fori_loop(0, rows // PERM_UNROLL, group, 0)


def _dispatch_kernel(dest_ref, src_ref, dst_ref, sem):
    base = pl.program_id(0) * TM_PERM

    def start_one(r, d, priority):
        _tile_copy(src_ref, dst_ref, sem, r, d).start(priority=priority)

    _issue_tile_copies(dest_ref, base, TM_PERM, start_one)
    _tile_copy(src_ref, dst_ref, sem, 0, 0, TM_PERM).wait()


def _dispatch(dest, src):
    return pl.pallas_call(
        _dispatch_kernel,
        grid_spec=pltpu.PrefetchScalarGridSpec(
            num_scalar_prefetch=1,
            grid=(N // TM_PERM,),
            in_specs=[pl.BlockSpec((TM_PERM * SLAB, 128), lambda i, dest: (i, 0))],
            out_specs=pl.BlockSpec(memory_space=pl.ANY),
            scratch_shapes=[pltpu.SemaphoreType.DMA(())],
        ),
        out_shape=jax.ShapeDtypeStruct(src.shape, src.dtype),
        compiler_params=_cparams(("arbitrary",)),
        name="dispatch",
    )(dest, src)


def _combine_kernel(dest_ref, x_ref, ys_ref, mod_ref, fw_ref, o_ref, ybuf_ref, sem, *, final):
    i = pl.program_id(0)
    n = pl.num_programs(0)
    slot = i % 2

    def gather(step, to_slot):
        def start_one(r, d, priority):
            _tile_copy(ys_ref, ybuf_ref.at[to_slot], sem.at[to_slot], d, r).start(priority=priority)

        _issue_tile_copies(dest_ref, step * TM_COMB, TM_COMB, start_one)

    @pl.when(i == 0)
    def _():
        gather(0, 0)

    @pl.when(i + 1 < n)
    def _():
        gather(i + 1, 1 - slot)

    _tile_copy(ys_ref, ybuf_ref.at[slot], sem.at[slot], 0, 0, TM_COMB).wait()
    out = x_ref[...] + mod_ref[0][:, 5 * D:6 * D] * _load_slabs(ybuf_ref.at[slot], TM_COMB)
    if final:
        out = _rms(out) * fw_ref[...]
    o_ref[...] = out


def _combine(dest, x1, ys, mod3, final_w, final):
    per_b = T // TM_COMB
    return pl.pallas_call(
        functools.partial(_combine_kernel, final=final),
        grid_spec=pltpu.PrefetchScalarGridSpec(
            num_scalar_prefetch=1,
            grid=(N // TM_COMB,),
            in_specs=[
                pl.BlockSpec((TM_COMB, D), lambda i, dest: (i, 0)),
                pl.BlockSpec(memory_space=pl.ANY),
                pl.BlockSpec((1, 1, 6 * D), lambda i, dest: (i // per_b, 0, 0)),
                pl.BlockSpec((1, D), lambda i, dest: (0, 0)),
            ],
            out_specs=pl.BlockSpec((TM_COMB, D), lambda i, dest: (i, 0)),
            scratch_shapes=[pltpu.VMEM((2, TM_COMB * SLAB, 128), F32), pltpu.SemaphoreType.DMA((2,))],
        ),
        out_shape=jax.ShapeDtypeStruct((N, D), F32),
        compiler_params=_cparams(("arbitrary",)),
        name="combine",
    )(dest, x1.reshape(N, D), ys, mod3, final_w)


def _moe_kernel(tile_ref, lo_ref, hi_ref, ea_ref, eb_ref, ca_ref, cb_ref, na_ref, nb_ref,
                x_ref, rw_ref, wg_hbm, wu_hbm, wd_hbm, o_ref,
                sga_ref, sua_ref, sda_ref, sgb_ref, sub_ref, sdb_ref,
                ga_ref, ua_ref, da_ref, gb_ref, ub_ref, db_ref, sem, *, layer):
    m = pl.program_id(0)
    lo = lo_ref[m]
    hi = hi_ref[m]
    tile0 = tile_ref[m] * TM_MOE
    slots = ((ea_ref, ca_ref, na_ref, (sga_ref, sua_ref, sda_ref), (ga_ref, ua_ref, da_ref)),
             (eb_ref, cb_ref, nb_ref, (sgb_ref, sub_ref, sdb_ref), (gb_ref, ub_ref, db_ref)))

    def fetch(slot, expert):
        stage = slots[slot][3]
        return [pltpu.make_async_copy(w.at[layer, expert], s, sem.at[slot])
                for w, s in zip((wg_hbm, wu_hbm, wd_hbm), stage)]

    for slot, (e_ref, c_ref, n_ref, stage, work) in enumerate(slots):
        @pl.when(m == 0)
        def _():
            for cp in fetch(slot, e_ref[0]):
                cp.start()

        @pl.when(c_ref[m] == 1)
        def _():
            for cp in fetch(slot, 0):
                cp.wait()
            for s, w in zip(stage, work):
                w[...] = s[...].astype(BF16)

            @pl.when(n_ref[m] >= 0)
            def _():
                for cp in fetch(slot, n_ref[m]):
                    cp.start()

    def value():
        hf = _load_slabs(x_ref, TM_MOE)
        h = hf.astype(BF16)

        def up(w_ref):
            return jnp.dot(h, w_ref[...], preferred_element_type=F32)

        def down(g, u, d_ref):
            return jnp.dot((_silu(g) * u).astype(BF16), d_ref[...], preferred_element_type=F32)

        g_a, u_a, g_b, u_b = up(ga_ref), up(ua_ref), up(gb_ref), up(ub_ref)
        y_a = down(g_a, u_a, da_ref)
        y_b = down(g_b, u_b, db_ref)
        s_a = _sigmoid(jnp.sum(hf * rw_ref[pl.ds(ea_ref[m], 1), :], axis=1, keepdims=True))
        s_b = _sigmoid(jnp.sum(hf * rw_ref[pl.ds(eb_ref[m], 1), :], axis=1, keepdims=True))
        denom = s_a + s_b
        val = (s_a / denom) * y_a + (s_b / denom) * y_b
        row = tile0 + lax.broadcasted_iota(I32, (TM_MOE, 1), 0)
        return val, (row >= lo) & (row < hi)

    @pl.when((hi > lo) & (lo == tile0))
    def _():
        val, seg = value()
        _store_slabs(o_ref, jnp.where(seg, val, 0.0))

    @pl.when((hi > lo) & (lo != tile0))
    def _():
        val, seg = value()
        _store_slabs(o_ref, jnp.where(seg, val, _load_slabs(o_ref, TM_MOE)))


def _moe_sorted(units, xs, rw_t, wg, wu, wd, layer):
    row_map = lambda m, t, *_: (t[m], 0)
    mats = lambda dt: [pltpu.VMEM((D, DE), dt), pltpu.VMEM((D, DE), dt), pltpu.VMEM((DE, D), dt)]
    return pl.pallas_call(
        functools.partial(_moe_kernel, layer=layer),
        grid_spec=pltpu.PrefetchScalarGridSpec(
            num_scalar_prefetch=len(units),
            grid=(N_UNITS,),
            in_specs=[
                pl.BlockSpec((TM_MOE * SLAB, 128), row_map),
                pl.BlockSpec((NE, D), lambda m, *_: (0, 0)),
                pl.BlockSpec(memory_space=pl.ANY),
                pl.BlockSpec(memory_space=pl.ANY),
                pl.BlockSpec(memory_space=pl.ANY),
            ],
            out_specs=pl.BlockSpec((TM_MOE * SLAB, 128), row_map),
            scratch_shapes=mats(F32) + mats(F32) + mats(BF16) + mats(BF16) + [pltpu.SemaphoreType.DMA((2,))],
        ),
        out_shape=jax.ShapeDtypeStruct((N * SLAB, 128), F32),
        compiler_params=_cparams(("arbitrary",)),
        name="moe_sorted",
    )(*units, xs, rw_t, wg, wu, wd)


def _moe_units(ends_f):
    ends = ends_f[:NCLS, 0].astype(I32)
    offs = jnp.concatenate([jnp.zeros((1,), I32), ends[:-1]])
    starts = jnp.sort(jnp.concatenate([jnp.arange(N // TM_MOE, dtype=I32) * TM_MOE, offs]))
    u_lo = starts
    u_hi = jnp.concatenate([starts[1:], jnp.full((1,), N, I32)])
    u_tile = jnp.minimum(u_lo // TM_MOE, N // TM_MOE - 1)
    u_cls = jnp.minimum(jnp.sum(ends[None, :] <= u_lo[:, None], axis=1), NCLS - 1).astype(I32)
    grp = u_cls // len(PAIR_A)
    pair = u_cls % len(PAIR_A)
    e_a = grp * EPG + jnp.take(jnp.array(PAIR_A, I32), pair)
    e_b = grp * EPG + jnp.take(jnp.array(PAIR_B, I32), pair)
    one = jnp.ones((1,), I32)
    idx = jnp.arange(N_UNITS, dtype=I32)

    def changes(e):
        chg = jnp.concatenate([one, (e[1:] != e[:-1]).astype(I32)])
        at = jnp.where(chg == 1, idx, N_UNITS)
        nxt_at = jnp.concatenate([lax.cummin(at, reverse=True)[1:], jnp.full((1,), N_UNITS, I32)])
        nxt = jnp.where(nxt_at < N_UNITS, jnp.take(e, jnp.minimum(nxt_at, N_UNITS - 1)), -1)
        return chg, nxt

    chg_a, nxt_a = changes(e_a)
    chg_b, nxt_b = changes(e_b)
    return u_tile, u_lo, u_hi, e_a, e_b, chg_a, chg_b, nxt_a, nxt_b


def _moe_layer(x1, hp, logits, rw_t, router_b, wg, wu, wd, layer, mod3, final_w, final):
    dest, ends = _route_tokens(logits, router_b)
    dest = dest.reshape(N)
    units = _moe_units(ends)
    xs = _dispatch(dest, hp)
    ys = _moe_sorted(units, xs, rw_t, wg, wu, wd, layer)
    return _combine(dest, x1, ys, mod3, final_w, final).reshape(B, T, D)


def _rope_tables():
    half = 16
    inv = ROPE_BASE ** (-np.arange(half, dtype=np.float64) / half)
    t = np.arange(T)
    ang_r = (t // GRID_W)[:, None] * inv[None, :]
    ang_c = (t % GRID_W)[:, None] * inv[None, :]
    ang = np.concatenate([ang_r, ang_r, ang_c, ang_c], axis=1)
    sign = np.tile(np.concatenate([-np.ones(half), np.ones(half)]), 2)
    cos = np.concatenate([np.ones((LC, 64)), np.cos(ang)], axis=0)
    sin = np.concatenate([np.zeros((LC, 64)), np.sin(ang) * sign[None, :]], axis=0)
    return (jnp.asarray(np.tile(cos, (1, 2)), dtype=F32), jnp.asarray(np.tile(sin, (1, 2)), dtype=F32))


def _permute_w_in(w):
    rq = w[:, 0:256].reshape(D, NH, DK)
    dq = w[:, 256:768]
    rg = w[:, 768:1280]
    rk = w[:, 1280:1536].reshape(D, NH, DK)
    rv = w[:, 1536:2048]
    dk = w[:, 2048:2560]
    dv = w[:, 2560:3072]
    qk = jnp.concatenate([rq, rk * (DK ** -0.5)], axis=2).reshape(D, NH * 2 * DK)
    return jnp.concatenate([qk, rv, rg, dq * (DK ** -0.5 * math.log2(math.e)), dk, dv], axis=1).astype(BF16)


def kernel(x, c, ctx, c_ctx, ada_w, ada_b, norm_mix_w, norm_ffn_w, w_in, w_out, ret_log_decay, diff_lambda,
           diff_subln_w, pool_w, pool_scale, router_w, router_b, moe_w_gate, moe_w_up, moe_w_down, final_norm_w):
    assert x.shape == (B, T, D) and ctx.shape == (B, LC, D) and ada_w.shape[0] == 2
    cc = jnp.concatenate([c, c_ctx[None, :], jnp.zeros((16 - B - 1, D), F32)], axis=0)
    mod = _ada_mod(cc, ada_w, ada_b)
    rw_t = router_w.T
    fw = final_norm_w.reshape(1, D)
    experts = (moe_w_gate, moe_w_up, moe_w_down)

    mod0 = mod[0].reshape(16, 1, 6 * D)
    cos_t, sin_t = _rope_tables()
    proj = _inproj(x, ctx, mod0, norm_mix_w[0:1], _permute_w_in(w_in[0]), cos_t, sin_t)
    ret = _retention(proj, ret_log_decay[0])
    lam_init = 0.8 - 0.6 * math.exp(-0.3 * 0)
    lv = diff_lambda[0]
    lam = jnp.exp(jnp.sum(lv[0] * lv[1])) - jnp.exp(jnp.sum(lv[2] * lv[3])) + lam_init
    dif = _diffattn(proj, lam.reshape(1), diff_subln_w[0:1], 1.0 - lam_init)
    x1, hp, logits = _outproj(ret, dif, w_out[0].astype(BF16), x, mod0, norm_ffn_w[0:1], rw_t)
    x2 = _moe_layer(x1, hp, logits, rw_t, router_b, *experts, 0, mod0, fw, False)

    mod1 = mod[1].reshape(16, 1, 6 * D)
    x3, hp, logits = _pool_layer(x2, mod1, norm_mix_w[1:2], pool_w[0].astype(BF16), pool_scale[0:1],
                                 norm_ffn_w[1:2], rw_t)
    return _moe_layer(x3, hp, logits, rw_t, router_b, *experts, 1, mod1, fw, True)
```

```python
import functools
import math

import jax
import jax.numpy as jnp
import numpy as np
from jax import lax
from jax.experimental import pallas as pl
from jax.experimental.pallas import tpu as pltpu

F32 = jnp.float32
BF16 = jnp.bfloat16
I32 = jnp.int32

D = 1024
B = 8
T = 2048
N = B * T
GRID_W = 64
LC = 256
EPS = 1e-6
ROPE_BASE = 10000.0
NH = 4
DK = 64
HV = 128
CH = 256
RB = LC + T
NCH = RB // CH
POOL_WINDOWS = (2, 4, 8, 16)
PG = D // len(POOL_WINDOWS)
NE = 16
NGRP = 4
EPG = NE // NGRP
DE = 512
IN_W = 3072
HALO = 8

PAIR_A = (0, 0, 0, 1, 1, 3)
PAIR_B = (1, 2, 3, 3, 2, 2)
NCLS = NGRP * len(PAIR_A)
CLS_PAD = 32
SLAB = D // 128

TM_PROJ = 256
PROJ_SUB = 3
TM_OUT = 1024
SUB_OUT = 512
TQ_SUB = 8
SCORE_AHEAD = 1
TM_POOL = 512
TM_MOE = 256
N_UNITS = N // TM_MOE + NCLS
TM_PERM = 2048
TM_COMB = 512
PERM_UNROLL = 16
VMEM_LIMIT = 56 * 1024 * 1024


def _cparams(sem):
    return pltpu.CompilerParams(dimension_semantics=sem, vmem_limit_bytes=VMEM_LIMIT)


def _sigmoid(x):
    return 1.0 / (1.0 + jnp.exp(-x))


def _silu(x):
    return x * _sigmoid(x)


def _rms(x):
    return x * lax.rsqrt(jnp.mean(x * x, axis=-1, keepdims=True) + EPS)


def _dot_3pass(a, b, dims):
    a_hi = a.astype(BF16)
    b_hi = b.astype(BF16)
    a_lo = (a - a_hi.astype(F32)).astype(BF16)
    b_lo = (b - b_hi.astype(F32)).astype(BF16)

    def dot(x, y):
        return lax.dot_general(x, y, dims, preferred_element_type=F32)

    return dot(a_hi, b_hi) + (dot(a_lo, b_hi) + dot(a_hi, b_lo))


def _load_slabs(ref, rows):
    return jnp.concatenate([ref[pl.ds(s, rows, stride=SLAB), :] for s in range(SLAB)], axis=1)


def _store_slabs(ref, val, row0=0):
    rows = val.shape[0]
    for s in range(SLAB):
        ref[pl.ds(row0 * SLAB + s, rows, stride=SLAB), :] = val[:, s * 128:(s + 1) * 128]


def _ada_kernel(cc_ref, w_ref, b_ref, o_ref):
    s = _silu(cc_ref[...])
    o_ref[0] = _dot_3pass(s, w_ref[0], (((1,), (0,)), ((), ()))) + b_ref[0]


def _ada_mod(cc, ada_w, ada_b):
    depth = ada_w.shape[0]
    tn = 1536
    return pl.pallas_call(
        _ada_kernel,
        grid=(depth, 6 * D // tn),
        in_specs=[
            pl.BlockSpec((16, D), lambda l, n: (0, 0)),
            pl.BlockSpec((1, D, tn), lambda l, n: (l, 0, n)),
            pl.BlockSpec((1, 1, tn), lambda l, n: (l, 0, n)),
        ],
        out_specs=pl.BlockSpec((1, 16, tn), lambda l, n: (l, 0, n)),
        out_shape=jax.ShapeDtypeStruct((depth, 16, 6 * D), F32),
        compiler_params=_cparams(("arbitrary", "arbitrary")),
        name="ada_mod",
    )(cc, ada_w, ada_b.reshape(depth, 1, 6 * D))


def _rope(seg, cos, sin_signed, lo_mask):
    w = seg.shape[1]
    from_hi = pltpu.roll(seg, w - 16, axis=1)
    from_lo = pltpu.roll(seg, 16, axis=1)
    partner = jnp.where(lo_mask, from_hi, from_lo)
    reps = w // cos.shape[1]
    c = jnp.concatenate([cos] * reps, axis=1)
    s = jnp.concatenate([sin_signed] * reps, axis=1)
    return seg * c + partner * s


def _inproj_kernel(*refs):
    x_refs = refs[:PROJ_SUB]
    c_ref, mod_ref, cmod_ref, nw_ref, w_ref, cos_ref, sin_ref, o_ref = refs[PROJ_SUB:]
    is_ctx = pl.program_id(1) == 0
    parts = []
    for s in range(PROJ_SUB):
        xt = x_refs[s][0]
        sh = mod_ref[0, :, 0:D]
        sc = mod_ref[0, :, D:2 * D]
        if s == 0:
            xt = jnp.where(is_ctx, c_ref[0], xt)
            sh = jnp.where(is_ctx, cmod_ref[0, :, 0:D], sh)
            sc = jnp.where(is_ctx, cmod_ref[0, :, D:2 * D], sc)
        parts.append(((_rms(xt) * nw_ref[...]) * (1.0 + sc) + sh).astype(BF16))
    hb = jnp.concatenate(parts, axis=0)
    lane = lax.broadcasted_iota(I32, (PROJ_SUB * TM_PROJ, 512), 1)
    lo_mask = (lane % 32) < 16
    cos = cos_ref[...]
    sin = sin_ref[...]

    def project(g):
        return jnp.dot(hb, w_ref[:, g * 512:(g + 1) * 512], preferred_element_type=F32)

    seg = project(0)
    for g in range(6):
        seg_next = project(g + 1) if g + 1 < 6 else None
        if g in (0, 3, 4):
            seg = _rope(seg, cos, sin, lo_mask)
        o_ref[0, :, g * 512:(g + 1) * 512] = seg.astype(BF16)
        seg = seg_next


def _inproj(x, ctx, mod3, norm_w, w_perm, cos_t, sin_t):
    tm = PROJ_SUB * TM_PROJ
    nj = RB // tm

    def x_map(s, b, j):
        return (b, jnp.maximum(PROJ_SUB * j + s - LC // TM_PROJ, 0), 0)

    return pl.pallas_call(
        _inproj_kernel,
        grid=(B, nj),
        in_specs=[
            *[pl.BlockSpec((1, TM_PROJ, D), functools.partial(x_map, s)) for s in range(PROJ_SUB)],
            pl.BlockSpec((1, LC, D), lambda b, j: (b, 0, 0)),
            pl.BlockSpec((1, 1, 2 * D), lambda b, j: (b, 0, 0)),
            pl.BlockSpec((1, 1, 2 * D), lambda b, j: (B, 0, 0)),
            pl.BlockSpec((1, D), lambda b, j: (0, 0)),
            pl.BlockSpec((D, IN_W), lambda b, j: (0, 0)),
            pl.BlockSpec((tm, 128), lambda b, j: (j, 0)),
            pl.BlockSpec((tm, 128), lambda b, j: (j, 0)),
        ],
        out_specs=pl.BlockSpec((1, tm, IN_W), lambda b, j: (b, j, 0)),
        out_shape=jax.ShapeDtypeStruct((B, RB, IN_W), BF16),
        compiler_params=_cparams(("arbitrary", "arbitrary")),
        name="inproj",
    )(*([x] * PROJ_SUB), ctx, mod3, mod3, norm_w, w_perm, cos_t, sin_t)


def _retention_kernel(ld_ref, qk_ref, v_ref, g_ref, o_ref, st_ref, kdec_ref, qdec_ref, mask_ref, cdec_ref):
    h = pl.program_id(0)
    lane = lax.broadcasted_iota(I32, (CH, 128), 1)
    fwd_lane = lane < DK

    @pl.when(pl.program_id(1) == 0)
    def _():
        lgf = ld_ref[0, h]
        lgb = ld_ref[1, h]
        pos = lax.broadcasted_iota(I32, (CH, 128), 0).astype(F32)
        kdec_ref[...] = jnp.where(fwd_lane, jnp.exp(lgf * (CH - 1 - pos)), jnp.exp(lgb * pos))
        qdec_ref[...] = jnp.where(fwd_lane, jnp.exp(lgf * (pos + 1.0)), jnp.exp(lgb * (CH - pos)))
        ii = lax.broadcasted_iota(I32, (CH, CH), 0)
        jj = lax.broadcasted_iota(I32, (CH, CH), 1)
        gap = (ii - jj).astype(F32)
        mask_ref[...] = (jnp.where(gap >= 0, jnp.exp(lgf * jnp.maximum(gap, 0.0)), 0.0)
                         + jnp.where(gap <= 0, jnp.exp(lgb * jnp.maximum(-gap, 0.0)), 0.0))
        ones = jnp.ones((DK, 128), F32)
        cdec_ref[0:DK, :] = jnp.exp(lgf * CH * ones)
        cdec_ref[DK:, :] = jnp.exp(lgb * CH * ones)

    kdec = kdec_ref[...]
    qdec = qdec_ref[...]
    mask = mask_ref[...]
    cf = cdec_ref[0:DK, :]
    cb = cdec_ref[DK:, :]

    def chunk(n):
        a = qk_ref[0, n * CH:(n + 1) * CH, :].astype(F32)
        swapped = pltpu.roll(a, DK, axis=1)
        return a, swapped

    kv = []
    for n in range(NCH):
        a, swapped = chunk(n)
        kk = jnp.where(fwd_lane, swapped, a)
        kb = (kk * kdec).astype(BF16)
        vn = v_ref[0, n * CH:(n + 1) * CH, :]
        kv.append(lax.dot_general(kb, vn, (((0,), (0,)), ((), ())), preferred_element_type=F32))
    sf = kv[0][:DK]
    for n in range(1, NCH):
        st_ref[n, 0:DK, :] = sf
        sf = cf * sf + kv[n][:DK]
    sb = kv[0][DK:]
    for n in range(NCH - 1, 0, -1):
        st_ref[n, DK:2 * DK, :] = sb
        sb = cb * sb + kv[n][DK:]

    for n in range(1, NCH):
        a, swapped = chunk(n)
        q = a[:, :DK].astype(BF16)
        k = swapped[:, :DK].astype(BF16)
        scores = lax.dot_general(q, k, (((1,), (1,)), ((), ())), preferred_element_type=F32)
        p = (scores * mask).astype(BF16)
        vn = v_ref[0, n * CH:(n + 1) * CH, :]
        qq = jnp.where(fwd_lane, a, swapped)
        qd = (qq * qdec).astype(BF16)
        o = (jnp.dot(p, vn, preferred_element_type=F32)
             + jnp.dot(qd, st_ref[n].astype(BF16), preferred_element_type=F32))
        gate = g_ref[0, n * CH:(n + 1) * CH, :].astype(F32)
        o_ref[0, (n - 1) * CH:n * CH, :] = (_rms(o) * _silu(gate)).astype(BF16)


def _retention(proj, log_decay):
    return pl.pallas_call(
        _retention_kernel,
        grid=(NH, B),
        in_specs=[
            pl.BlockSpec(memory_space=pltpu.SMEM),
            pl.BlockSpec((1, RB, 128), lambda h, b: (b, 0, h)),
            pl.BlockSpec((1, RB, 128), lambda h, b: (b, 0, NH + h)),
            pl.BlockSpec((1, RB, 128), lambda h, b: (b, 0, 2 * NH + h)),
        ],
        out_specs=pl.BlockSpec((1, T, 128), lambda h, b: (b, 0, h)),
        out_shape=jax.ShapeDtypeStruct((B, T, NH * HV), BF16),
        scratch_shapes=[pltpu.VMEM((NCH, 128, 128), F32), pltpu.VMEM((CH, 128), F32), pltpu.VMEM((CH, 128), F32),
                        pltpu.VMEM((CH, CH), F32), pltpu.VMEM((2 * DK, 128), F32)],
        compiler_params=_cparams(("arbitrary", "arbitrary")),
        name="retention",
    )(log_decay, proj, proj, proj)


def _diffattn_kernel(lam_ref, *refs, out_scale):
    q_refs = refs[:TQ_SUB]
    k_ref, v_ref, sw_ref, o_ref = refs[TQ_SUB:]
    lam = lam_ref[0]
    k = k_ref[0]
    v = v_ref[0]
    nt = (((1,), (1,)), ((), ()))

    def scores(qh):
        return lax.dot_general(qh, k, nt, preferred_element_type=F32)

    def values(s):
        e = jnp.exp2(s - jnp.max(s, axis=-1, keepdims=True))
        return jnp.dot(e.astype(BF16), v, preferred_element_type=F32), jnp.sum(e, axis=-1, keepdims=True)

    halves = []
    for i in range(TQ_SUB):
        q = q_refs[i][0]
        lane = lax.broadcasted_iota(I32, q.shape, 1)
        zero = jnp.zeros_like(q)
        halves += [jnp.where(lane < DK, q, zero), jnp.where(lane >= DK, q, zero)]
    outs = []
    ahead = [scores(h) for h in halves[:SCORE_AHEAD]]
    for c in range(len(halves)):
        if c + SCORE_AHEAD < len(halves):
            ahead.append(scores(halves[c + SCORE_AHEAD]))
        outs.append(values(ahead.pop(0)))
    for i in range(TQ_SUB):
        (o1, l1), (o2, l2) = outs[2 * i], outs[2 * i + 1]
        o = o1 / l1 - o2 * (lam / l2)
        o_ref[0, i * TM_PROJ:(i + 1) * TM_PROJ, :] = (_rms(o) * sw_ref[...] * out_scale).astype(BF16)


def _diffattn(proj, lam, subln_w, out_scale):
    tq = TQ_SUB * TM_PROJ
    nq = T // tq

    def q_map(i, b, h, j):
        return (b, LC // TM_PROJ + j * TQ_SUB + i, 3 * NH + h)

    return pl.pallas_call(
        functools.partial(_diffattn_kernel, out_scale=out_scale),
        grid=(B, NH, nq),
        in_specs=[
            pl.BlockSpec(memory_space=pltpu.SMEM),
            *[pl.BlockSpec((1, TM_PROJ, 128), functools.partial(q_map, i)) for i in range(TQ_SUB)],
            pl.BlockSpec((1, RB, 128), lambda b, h, j: (b, 0, 4 * NH + h)),
            pl.BlockSpec((1, RB, 128), lambda b, h, j: (b, 0, 5 * NH + h)),
            pl.BlockSpec((1, HV), lambda b, h, j: (0, 0)),
        ],
        out_specs=pl.BlockSpec((1, tq, 128), lambda b, h, j: (b, j, h)),
        out_shape=jax.ShapeDtypeStruct((B, T, NH * HV), BF16),
        compiler_params=_cparams(("arbitrary", "arbitrary", "arbitrary")),
        name="diffattn",
    )(lam, *([proj] * TQ_SUB), proj, proj, subln_w)


def _route(bz):
    grp = []
    for g in range(NGRP):
        m = bz[g * EPG:(g + 1) * EPG]
        best = None
        for i in range(EPG):
            for k in range(i + 1, EPG):
                pair = m[i] + m[k]
                best = pair if best is None else jnp.maximum(best, pair)
        grp.append(best)
    gbest = grp[0]
    gsel = jnp.zeros_like(gbest, dtype=I32)
    for g in range(1, NGRP):
        better = grp[g] > gbest
        gsel = jnp.where(better, g, gsel)
        gbest = jnp.where(better, grp[g], gbest)
    cb = [bz[i] for i in range(EPG)]
    for g in range(1, NGRP):
        pick = gsel == g
        cb = [jnp.where(pick, bz[g * EPG + i], cb[i]) for i in range(EPG)]
    i1 = jnp.zeros_like(gsel)
    b1 = cb[0]
    for i in range(1, EPG):
        better = cb[i] > b1
        i1 = jnp.where(better, i, i1)
        b1 = jnp.where(better, cb[i], b1)
    neg = jnp.full_like(b1, -jnp.inf)
    rest = [jnp.where(i1 == i, neg, cb[i]) for i in range(EPG)]
    i2 = jnp.zeros_like(gsel)
    b2 = rest[0]
    for i in range(1, EPG):
        better = rest[i] > b2
        i2 = jnp.where(better, i, i2)
        b2 = jnp.where(better, rest[i], b2)
    lo = jnp.minimum(i1, i2)
    hi = jnp.maximum(i1, i2)
    code = lo * EPG + hi
    pair = jnp.full_like(gsel, len(PAIR_A) - 1)
    for p in range(len(PAIR_A) - 1):
        a, b = min(PAIR_A[p], PAIR_B[p]), max(PAIR_A[p], PAIR_B[p])
        pair = jnp.where(code == a * EPG + b, p, pair)
    return gsel * len(PAIR_A) + pair


def _ffn_prologue(x1, row0, mod, nfw_ref, rw_ref, hp_ref, logit_ref):
    rows = x1.shape[0]
    sh2 = mod[:, 3 * D:4 * D]
    sc2 = mod[:, 4 * D:5 * D]
    h2 = (_rms(x1) * nfw_ref[...]) * (1.0 + sc2) + sh2
    _store_slabs(hp_ref, h2, row0)
    logit_ref[:, row0:row0 + rows] = _dot_3pass(rw_ref[...], h2, (((1,), (1,)), ((), ())))


def _ffn_out_specs(tm, n_tiles_per_b):
    specs = [
        pl.BlockSpec((1, tm, D), lambda b, j: (b, j, 0)),
        pl.BlockSpec((tm * SLAB, 128), lambda b, j: (b * n_tiles_per_b + j, 0)),
        pl.BlockSpec((NE, tm), lambda b, j: (0, b * n_tiles_per_b + j)),
    ]
    shapes = [
        jax.ShapeDtypeStruct((B, T, D), F32),
        jax.ShapeDtypeStruct((N * SLAB, 128), F32),
        jax.ShapeDtypeStruct((NE, N), F32),
    ]
    return specs, shapes


def _route_kernel(logit_ref, bias_ref, dest_ref, ends_ref):
    r = logit_ref.shape[1]
    cls = _route([_sigmoid(logit_ref[e]) + bias_ref[e] for e in range(NE)])
    lane_incl = (lax.broadcasted_iota(I32, (128, 128), 0) <= lax.broadcasted_iota(I32, (128, 128), 1)).astype(BF16)
    rows_before = (lax.broadcasted_iota(I32, (r, r), 1) < lax.broadcasted_iota(I32, (r, r), 0)).astype(BF16)
    dest = jnp.zeros((r, 128), F32)
    start = jnp.zeros((1, 128), F32)
    ends = []
    for c in range(NCLS):
        onehot = jnp.where(cls == c, 1.0, 0.0)
        in_row = jnp.dot(onehot.astype(BF16), lane_incl, preferred_element_type=F32)
        row_tot = jnp.broadcast_to(in_row[:, 127:128], (r, 128))
        above = jnp.dot(rows_before, row_tot.astype(BF16), preferred_element_type=F32)
        dest = dest + onehot * (start + above + in_row - 1.0)
        start = start + jnp.sum(row_tot, axis=0, keepdims=True)
        ends.append(start)
    dest_ref[...] = dest.astype(I32)
    ends_ref[...] = jnp.concatenate(ends + [jnp.zeros((CLS_PAD - NCLS, 128), F32)], axis=0)


def _route_tokens(logits_t, router_b):
    r = N // 128
    return pl.pallas_call(
        _route_kernel,
        in_specs=[pl.BlockSpec((NE, r, 128), lambda: (0, 0, 0)), pl.BlockSpec(memory_space=pltpu.SMEM)],
        out_specs=[pl.BlockSpec((r, 128), lambda: (0, 0)), pl.BlockSpec((CLS_PAD, 128), lambda: (0, 0))],
        out_shape=[jax.ShapeDtypeStruct((r, 128), I32), jax.ShapeDtypeStruct((CLS_PAD, 128), F32)],
        compiler_params=pltpu.CompilerParams(vmem_limit_bytes=VMEM_LIMIT),
        name="route",
    )(logits_t.reshape(NE, r, 128), router_b)


def _outproj_kernel(ret_ref, dif_ref, w_ref, x_ref, mod_ref, nfw_ref, rw_ref, x1_ref, hp_ref, logit_ref):
    mod = mod_ref[0]

    def mix(r0):
        rows = slice(r0, r0 + SUB_OUT)
        return (jnp.dot(ret_ref[0, rows, :], w_ref[0:NH * HV, :], preferred_element_type=F32)
                + jnp.dot(dif_ref[0, rows, :], w_ref[NH * HV:, :], preferred_element_type=F32))

    mx = mix(0)
    for r0 in range(0, TM_OUT, SUB_OUT):
        mx_next = mix(r0 + SUB_OUT) if r0 + SUB_OUT < TM_OUT else None
        x1 = x_ref[0, r0:r0 + SUB_OUT, :] + mod[:, 2 * D:3 * D] * mx
        x1_ref[0, r0:r0 + SUB_OUT, :] = x1
        _ffn_prologue(x1, r0, mod, nfw_ref, rw_ref, hp_ref, logit_ref)
        mx = mx_next


def _outproj(ret, dif, w_out, x, mod3, nfw, rw_t):
    nj = T // TM_OUT
    out_specs, out_shapes = _ffn_out_specs(TM_OUT, nj)
    return pl.pallas_call(
        _outproj_kernel,
        grid=(B, nj),
        in_specs=[
            pl.BlockSpec((1, TM_OUT, NH * HV), lambda b, j: (b, j, 0)),
            pl.BlockSpec((1, TM_OUT, NH * HV), lambda b, j: (b, j, 0)),
            pl.BlockSpec((2 * NH * HV, D), lambda b, j: (0, 0)),
            pl.BlockSpec((1, TM_OUT, D), lambda b, j: (b, j, 0)),
            pl.BlockSpec((1, 1, 6 * D), lambda b, j: (b, 0, 0)),
            pl.BlockSpec((1, D), lambda b, j: (0, 0)),
            pl.BlockSpec((NE, D), lambda b, j: (0, 0)),
        ],
        out_specs=out_specs,
        out_shape=out_shapes,
        compiler_params=_cparams(("arbitrary", "arbitrary")),
        name="outproj",
    )(ret, dif, w_out, x, mod3, nfw, rw_t)


def _pool_kernel(x_ref, prev_ref, next_ref, mod_ref, nmw_ref, pw_ref, ps_ref, nfw_ref, rw_ref,
                 x1_ref, hp_ref, logit_ref, ext_ref):
    i = pl.program_id(1)
    last = pl.num_programs(1) - 1
    mod = mod_ref[0]
    sh1 = mod[:, 0:D]
    sc1 = mod[:, D:2 * D]

    def modnorm(v):
        return (_rms(v) * nmw_ref[...]) * (1.0 + sc1) + sh1

    x = x_ref[0]
    hc = modnorm(x)
    ext_ref[0:HALO, :] = jnp.where(i > 0, modnorm(prev_ref[0]), 0.0)
    ext_ref[HALO:HALO + TM_POOL, :] = hc
    ext_ref[HALO + TM_POOL:, :] = jnp.where(i < last, modnorm(next_ref[0]), 0.0)
    pos = i * TM_POOL + lax.broadcasted_iota(I32, (TM_POOL, 1), 0)
    mixed = []
    for gi, w in enumerate(POOL_WINDOWS):
        left = w // 2
        right = w - 1 - left
        cols = slice(gi * PG, (gi + 1) * PG)
        tot = None
        for d in range(-left, right + 1):
            part = ext_ref[HALO + d:HALO + d + TM_POOL, cols]
            tot = part if tot is None else tot + part
        cnt = (jnp.minimum(pos + right + 1, T) - jnp.maximum(pos - left, 0)).astype(F32)
        pooled = (tot * (1.0 / cnt) - hc[:, cols]).astype(BF16)
        mixed.append(jnp.dot(pooled, pw_ref[gi], preferred_element_type=F32))
    mixed = jnp.concatenate(mixed, axis=1) * ps_ref[...]
    x1 = x + mod[:, 2 * D:3 * D] * mixed
    x1_ref[0] = x1
    _ffn_prologue(x1, 0, mod, nfw_ref, rw_ref, hp_ref, logit_ref)


def _pool_layer(x, mod3, nmw, pool_w, pool_scale, nfw, rw_t):
    ni = T // TM_POOL
    hb = TM_POOL // HALO
    out_specs, out_shapes = _ffn_out_specs(TM_POOL, ni)
    return pl.pallas_call(
        _pool_kernel,
        grid=(B, ni),
        in_specs=[
            pl.BlockSpec((1, TM_POOL, D), lambda b, i: (b, i, 0)),
            pl.BlockSpec((1, HALO, D), lambda b, i: (b, jnp.maximum(i * hb - 1, 0), 0)),
            pl.BlockSpec((1, HALO, D), lambda b, i: (b, jnp.minimum((i + 1) * hb, T // HALO - 1), 0)),
            pl.BlockSpec((1, 1, 6 * D), lambda b, i: (b, 0, 0)),
            pl.BlockSpec((1, D), lambda b, i: (0, 0)),
            pl.BlockSpec((len(POOL_WINDOWS), PG, PG), lambda b, i: (0, 0, 0)),
            pl.BlockSpec((1, D), lambda b, i: (0, 0)),
            pl.BlockSpec((1, D), lambda b, i: (0, 0)),
            pl.BlockSpec((NE, D), lambda b, i: (0, 0)),
        ],
        out_specs=out_specs,
        out_shape=out_shapes,
        scratch_shapes=[pltpu.VMEM((TM_POOL + 2 * HALO, D), F32)],
        compiler_params=_cparams(("arbitrary", "arbitrary")),
        name="pool_layer",
    )(x, x, x, mod3, nmw, pool_w, pool_scale, nfw, rw_t)


def _tile_copy(src_ref, dst_ref, sem, s, d, rows=1):
    s0 = pl.multiple_of(s * SLAB, SLAB)
    d0 = pl.multiple_of(d * SLAB, SLAB)
    return pltpu.make_async_copy(src_ref.at[pl.ds(s0, rows * SLAB)], dst_ref.at[pl.ds(d0, rows * SLAB)], sem)


def _issue_tile_copies(idx_ref, base, rows, start_one):
    def group(g, carry):
        r0 = g * PERM_UNROLL
        ids = [idx_ref[base + r0 + u] for u in range(PERM_UNROLL)]
        for u in range(PERM_UNROLL):
            start_one(r0 + u, ids[u], u % 2)
        return carry

    lax.fori_loop(0, rows // PERM_UNROLL, group, 0)


def _dispatch_kernel(dest_ref, src_ref, dst_ref, zero_ref, sem, pad_sem):
    base = pl.program_id(0) * TM_PERM

    @pl.when(pl.program_id(0) == 0)
    def _():
        zero_ref[...] = jnp.zeros_like(zero_ref)
        pad = _tile_copy(zero_ref, dst_ref, pad_sem, 0, N, TM_MOE)
        pad.start()
        pad.wait()

    def start_one(r, d, priority):
        _tile_copy(src_ref, dst_ref, sem, r, d).start(priority=priority)

    _issue_tile_copies(dest_ref, base, TM_PERM, start_one)
    _tile_copy(src_ref, dst_ref, sem, 0, 0, TM_PERM).wait()


def _dispatch(dest, src):
    return pl.pallas_call(
        _dispatch_kernel,
        grid_spec=pltpu.PrefetchScalarGridSpec(
            num_scalar_prefetch=1,
            grid=(N // TM_PERM,),
            in_specs=[pl.BlockSpec((TM_PERM * SLAB, 128), lambda i, dest: (i, 0))],
            out_specs=pl.BlockSpec(memory_space=pl.ANY),
            scratch_shapes=[pltpu.VMEM((TM_MOE * SLAB, 128), F32), pltpu.SemaphoreType.DMA(()),
                            pltpu.SemaphoreType.DMA(())],
        ),
        out_shape=jax.ShapeDtypeStruct(((N + TM_MOE) * SLAB, 128), src.dtype),
        compiler_params=_cparams(("arbitrary",)),
        name="dispatch",
    )(dest, src)


def _combine_kernel(dest_ref, x_ref, ys_ref, mod_ref, fw_ref, o_ref, ybuf_ref, sem, *, final):
    i = pl.program_id(0)
    n = pl.num_programs(0)
    slot = i % 2

    def gather(step, to_slot):
        def start_one(r, d, priority):
            _tile_copy(ys_ref, ybuf_ref.at[to_slot], sem.at[to_slot], d, r).start(priority=priority)

        _issue_tile_copies(dest_ref, step * TM_COMB, TM_COMB, start_one)

    @pl.when(i == 0)
    def _():
        gather(0, 0)

    @pl.when(i + 1 < n)
    def _():
        gather(i + 1, 1 - slot)

    _tile_copy(ys_ref, ybuf_ref.at[slot], sem.at[slot], 0, 0, TM_COMB).wait()
    out = x_ref[...] + mod_ref[0][:, 5 * D:6 * D] * _load_slabs(ybuf_ref.at[slot], TM_COMB)
    if final:
        out = _rms(out) * fw_ref[...]
    o_ref[...] = out


def _combine(dest, x1, ys, mod3, final_w, final):
    per_b = T // TM_COMB
    return pl.pallas_call(
        functools.partial(_combine_kernel, final=final),
        grid_spec=pltpu.PrefetchScalarGridSpec(
            num_scalar_prefetch=1,
            grid=(N // TM_COMB,),
            in_specs=[
                pl.BlockSpec((TM_COMB, D), lambda i, dest: (i, 0)),
                pl.BlockSpec(memory_space=pl.ANY),
                pl.BlockSpec((1, 1, 6 * D), lambda i, dest: (i // per_b, 0, 0)),
                pl.BlockSpec((1, D), lambda i, dest: (0, 0)),
            ],
            out_specs=pl.BlockSpec((TM_COMB, D), lambda i, dest: (i, 0)),
            scratch_shapes=[pltpu.VMEM((2, TM_COMB * SLAB, 128), F32), pltpu.SemaphoreType.DMA((2,))],
        ),
        out_shape=jax.ShapeDtypeStruct((N, D), F32),
        compiler_params=_cparams(("arbitrary",)),
        name="combine",
    )(dest, x1.reshape(N, D), ys, mod3, final_w)


def _moe_kernel(start_ref, ea_ref, eb_ref, ca_ref, cb_ref, na_ref, nb_ref, n_ref,
                rw_ref, xs_hbm, wg_hbm, wu_hbm, wd_hbm, ys_hbm,
                xbuf_ref, obuf_ref,
                sga_ref, sua_ref, sda_ref, sgb_ref, sub_ref, sdb_ref,
                ga_ref, ua_ref, da_ref, gb_ref, ub_ref, db_ref, wsem, xsem, osem, *, layer):
    m = pl.program_id(0)
    n = n_ref[0]
    slot = m % 2
    slots = ((ea_ref, ca_ref, na_ref, (sga_ref, sua_ref, sda_ref), (ga_ref, ua_ref, da_ref)),
             (eb_ref, cb_ref, nb_ref, (sgb_ref, sub_ref, sdb_ref), (gb_ref, ub_ref, db_ref)))

    def fetch(w_slot, expert):
        stage = slots[w_slot][3]
        return [pltpu.make_async_copy(w.at[layer, expert], s, wsem.at[w_slot])
                for w, s in zip((wg_hbm, wu_hbm, wd_hbm), stage)]

    def x_copy(step, buf):
        return _tile_copy(xs_hbm, xbuf_ref.at[buf], xsem.at[buf], start_ref[step], 0, TM_MOE)

    def o_copy(step, buf):
        return _tile_copy(obuf_ref.at[buf], ys_hbm, osem.at[buf], 0, start_ref[step], TM_MOE)

    @pl.when(m == 0)
    def _():
        obuf_ref[1] = jnp.zeros_like(obuf_ref[1])
        pad = _tile_copy(obuf_ref.at[1], ys_hbm, osem.at[1], 0, N, TM_MOE)
        pad.start()
        pad.wait()
        x_copy(0, 0).start()

    @pl.when(m < n)
    def _():
        for w_slot, (e_ref, c_ref, nx_ref, stage, work) in enumerate(slots):
            @pl.when(m == 0)
            def _():
                for cp in fetch(w_slot, e_ref[0]):
                    cp.start()

            @pl.when(c_ref[m] == 1)
            def _():
                for cp in fetch(w_slot, 0):
                    cp.wait()
                for s, w in zip(stage, work):
                    w[...] = s[...].astype(BF16)

                @pl.when(nx_ref[m] >= 0)
                def _():
                    for cp in fetch(w_slot, nx_ref[m]):
                        cp.start()

        @pl.when(m + 1 < n)
        def _():
            x_copy(m + 1, 1 - slot).start()

        x_copy(m, slot).wait()
        hf = _load_slabs(xbuf_ref.at[slot], TM_MOE)
        h = hf.astype(BF16)

        def up(w_ref):
            return jnp.dot(h, w_ref[...], preferred_element_type=F32)

        def down(g, u, d_ref):
            return jnp.dot((_silu(g) * u).astype(BF16), d_ref[...], preferred_element_type=F32)

        g_a, u_a, g_b, u_b = up(ga_ref), up(ua_ref), up(gb_ref), up(ub_ref)
        y_a = down(g_a, u_a, da_ref)
        y_b = down(g_b, u_b, db_ref)
        s_a = _sigmoid(jnp.sum(hf * rw_ref[pl.ds(ea_ref[m], 1), :], axis=1, keepdims=True))
        s_b = _sigmoid(jnp.sum(hf * rw_ref[pl.ds(eb_ref[m], 1), :], axis=1, keepdims=True))
        denom = s_a + s_b
        val = (s_a / denom) * y_a + (s_b / denom) * y_b

        @pl.when(m >= 1)
        def _():
            o_copy(m - 1, 1 - slot).wait()

        _store_slabs(obuf_ref.at[slot], val)
        o_copy(m, slot).start()

        @pl.when(m == n - 1)
        def _():
            o_copy(m, slot).wait()


def _moe_sorted(chunks, xs, rw_t, wg, wu, wd, layer):
    mats = lambda dt: [pltpu.VMEM((D, DE), dt), pltpu.VMEM((D, DE), dt), pltpu.VMEM((DE, D), dt)]
    hbm = pl.BlockSpec(memory_space=pl.ANY)
    return pl.pallas_call(
        functools.partial(_moe_kernel, layer=layer),
        grid_spec=pltpu.PrefetchScalarGridSpec(
            num_scalar_prefetch=len(chunks),
            grid=(N_UNITS,),
            in_specs=[pl.BlockSpec((NE, D), lambda m, *_: (0, 0)), hbm, hbm, hbm, hbm],
            out_specs=hbm,
            scratch_shapes=([pltpu.VMEM((2, TM_MOE * SLAB, 128), F32), pltpu.VMEM((2, TM_MOE * SLAB, 128), F32)]
                            + mats(F32) + mats(F32) + mats(BF16) + mats(BF16)
                            + [pltpu.SemaphoreType.DMA((2,))] * 3),
        ),
        out_shape=jax.ShapeDtypeStruct(xs.shape, F32),
        compiler_params=_cparams(("arbitrary",)),
        name="moe_sorted",
    )(*chunks, rw_t, xs, wg, wu, wd)


def _moe_chunks(ends_f):
    ends = ends_f[:NCLS, 0].astype(I32)
    offs = jnp.concatenate([jnp.zeros((1,), I32), ends[:-1]])
    per_cls = (ends - offs + (TM_MOE - 1)) // TM_MOE
    cum = jnp.cumsum(per_cls)
    count = cum[-1:]
    idx = jnp.arange(N_UNITS, dtype=I32)
    m = jnp.minimum(idx, count - 1)
    cls = jnp.minimum(jnp.sum(cum[None, :] <= m[:, None], axis=1), NCLS - 1).astype(I32)
    of_cls = cls[:, None] == jnp.arange(NCLS, dtype=I32)[None, :]

    def pick(per_class):
        return jnp.sum(jnp.where(of_cls, per_class[None, :], 0), axis=1)

    start = pick(offs) + TM_MOE * (m - (pick(cum) - pick(per_cls)))
    grp = cls // len(PAIR_A)
    pair = cls % len(PAIR_A)
    e_a = grp * EPG + jnp.take(jnp.array(PAIR_A, I32), pair)
    e_b = grp * EPG + jnp.take(jnp.array(PAIR_B, I32), pair)
    one = jnp.ones((1,), I32)

    def changes(e):
        chg = jnp.concatenate([one, (e[1:] != e[:-1]).astype(I32)])
        at = jnp.where(chg == 1, idx, N_UNITS)
        nxt_at = jnp.concatenate([lax.cummin(at, reverse=True)[1:], jnp.full((1,), N_UNITS, I32)])
        nxt = jnp.where(nxt_at < N_UNITS, jnp.take(e, jnp.minimum(nxt_at, N_UNITS - 1)), -1)
        return chg, nxt

    chg_a, nxt_a = changes(e_a)
    chg_b, nxt_b = changes(e_b)
    return start, e_a, e_b, chg_a, chg_b, nxt_a, nxt_b, count


def _moe_layer(x1, hp, logits, rw_t, router_b, wg, wu, wd, layer, mod3, final_w, final):
    dest, ends = _route_tokens(logits, router_b)
    dest = dest.reshape(N)
    xs = _dispatch(dest, hp)
    ys = _moe_sorted(_moe_chunks(ends), xs, rw_t, wg, wu, wd, layer)
    return _combine(dest, x1, ys, mod3, final_w, final).reshape(B, T, D)


def _rope_tables():
    half = 16
    inv = ROPE_BASE ** (-np.arange(half, dtype=np.float64) / half)
    t = np.arange(T)
    ang_r = (t // GRID_W)[:, None] * inv[None, :]
    ang_c = (t % GRID_W)[:, None] * inv[None, :]
    ang = np.concatenate([ang_r, ang_r, ang_c, ang_c], axis=1)
    sign = np.tile(np.concatenate([-np.ones(half), np.ones(half)]), 2)
    cos = np.concatenate([np.ones((LC, 64)), np.cos(ang)], axis=0)
    sin = np.concatenate([np.zeros((LC, 64)), np.sin(ang) * sign[None, :]], axis=0)
    return (jnp.asarray(np.tile(cos, (1, 2)), dtype=F32), jnp.asarray(np.tile(sin, (1, 2)), dtype=F32))


def _permute_w_in(w):
    rq = w[:, 0:256].reshape(D, NH, DK)
    dq = w[:, 256:768]
    rg = w[:, 768:1280]
    rk = w[:, 1280:1536].reshape(D, NH, DK)
    rv = w[:, 1536:2048]
    dk = w[:, 2048:2560]
    dv = w[:, 2560:3072]
    qk = jnp.concatenate([rq, rk * (DK ** -0.5)], axis=2).reshape(D, NH * 2 * DK)
    return jnp.concatenate([qk, rv, rg, dq * (DK ** -0.5 * math.log2(math.e)), dk, dv], axis=1).astype(BF16)


def kernel(x, c, ctx, c_ctx, ada_w, ada_b, norm_mix_w, norm_ffn_w, w_in, w_out, ret_log_decay, diff_lambda,
           diff_subln_w, pool_w, pool_scale, router_w, router_b, moe_w_gate, moe_w_up, moe_w_down, final_norm_w):
    assert x.shape == (B, T, D) and ctx.shape == (B, LC, D) and ada_w.shape[0] == 2
    cc = jnp.concatenate([c, c_ctx[None, :], jnp.zeros((16 - B - 1, D), F32)], axis=0)
    mod = _ada_mod(cc, ada_w, ada_b)
    rw_t = router_w.T
    fw = final_norm_w.reshape(1, D)
    experts = (moe_w_gate, moe_w_up, moe_w_down)

    mod0 = mod[0].reshape(16, 1, 6 * D)
    cos_t, sin_t = _rope_tables()
    proj = _inproj(x, ctx, mod0, norm_mix_w[0:1], _permute_w_in(w_in[0]), cos_t, sin_t)
    ret = _retention(proj, ret_log_decay[0])
    lam_init = 0.8 - 0.6 * math.exp(-0.3 * 0)
    lv = diff_lambda[0]
    lam = jnp.exp(jnp.sum(lv[0] * lv[1])) - jnp.exp(jnp.sum(lv[2] * lv[3])) + lam_init
    dif = _diffattn(proj, lam.reshape(1), diff_subln_w[0:1], 1.0 - lam_init)
    x1, hp, logits = _outproj(ret, dif, w_out[0].astype(BF16), x, mod0, norm_ffn_w[0:1], rw_t)
    x2 = _moe_layer(x1, hp, logits, rw_t, router_b, *experts, 0, mod0, fw, False)

    mod1 = mod[1].reshape(16, 1, 6 * D)
    x3, hp, logits = _pool_layer(x2, mod1, norm_mix_w[1:2], pool_w[0].astype(BF16), pool_scale[0:1],
                                 norm_ffn_w[1:2], rw_t)
    return _moe_layer(x3, hp, logits, rw_t, router_b, *experts, 1, mod1, fw, True)
```

```python
import functools
import math

import jax
import jax.numpy as jnp
import numpy as np
from jax import lax
from jax.experimental import pallas as pl
from jax.experimental.pallas import tpu as pltpu

F32 = jnp.float32
BF16 = jnp.bfloat16
I32 = jnp.int32

D = 1024
B = 8
T = 2048
N = B * T
GRID_W = 64
LC = 256
EPS = 1e-6
ROPE_BASE = 10000.0
NH = 4
DK = 64
HV = 128
CH = 256
RB = LC + T
NCH = RB // CH
POOL_WINDOWS = (2, 4, 8, 16)
PG = D // len(POOL_WINDOWS)
NE = 16
NGRP = 4
EPG = NE // NGRP
DE = 512
IN_W = 3072
HALO = 8

PAIR_A = (0, 0, 0, 1, 1, 3)
PAIR_B = (1, 2, 3, 3, 2, 2)
NCLS = NGRP * len(PAIR_A)
CLS_PAD = 32
SLAB = D // 128

TM_PROJ = 256
PROJ_SUB = 3
TM_OUT = 1024
SUB_OUT = 512
TQ_SUB = 8
SCORE_AHEAD = 1
TM_POOL = 512
TM_MOE = 256
N_UNITS = N // TM_MOE + NCLS
TM_PERM = 2048
TM_COMB = 512
PERM_UNROLL = 16
VMEM_LIMIT = 56 * 1024 * 1024


def _cparams(sem):
    return pltpu.CompilerParams(dimension_semantics=sem, vmem_limit_bytes=VMEM_LIMIT)


def _sigmoid(x):
    return 1.0 / (1.0 + jnp.exp(-x))


def _silu(x):
    return x * _sigmoid(x)


def _rms(x):
    return x * lax.rsqrt(jnp.mean(x * x, axis=-1, keepdims=True) + EPS)


def _dot_3pass(a, b, dims):
    a_hi = a.astype(BF16)
    b_hi = b.astype(BF16)
    a_lo = (a - a_hi.astype(F32)).astype(BF16)
    b_lo = (b - b_hi.astype(F32)).astype(BF16)

    def dot(x, y):
        return lax.dot_general(x, y, dims, preferred_element_type=F32)

    return dot(a_hi, b_hi) + (dot(a_lo, b_hi) + dot(a_hi, b_lo))


def _load_slabs(ref, rows):
    return jnp.concatenate([ref[pl.ds(s, rows, stride=SLAB), :] for s in range(SLAB)], axis=1)


def _store_slabs(ref, val, row0=0):
    rows = val.shape[0]
    for s in range(SLAB):
        ref[pl.ds(row0 * SLAB + s, rows, stride=SLAB), :] = val[:, s * 128:(s + 1) * 128]


def _ada_kernel(cc_ref, w_ref, b_ref, o_ref):
    s = _silu(cc_ref[...])
    o_ref[0] = _dot_3pass(s, w_ref[0], (((1,), (0,)), ((), ()))) + b_ref[0]


def _ada_mod(cc, ada_w, ada_b):
    depth = ada_w.shape[0]
    tn = 1536
    return pl.pallas_call(
        _ada_kernel,
        grid=(depth, 6 * D // tn),
        in_specs=[
            pl.BlockSpec((16, D), lambda l, n: (0, 0)),
            pl.BlockSpec((1, D, tn), lambda l, n: (l, 0, n)),
            pl.BlockSpec((1, 1, tn), lambda l, n: (l, 0, n)),
        ],
        out_specs=pl.BlockSpec((1, 16, tn), lambda l, n: (l, 0, n)),
        out_shape=jax.ShapeDtypeStruct((depth, 16, 6 * D), F32),
        compiler_params=_cparams(("arbitrary", "arbitrary")),
        name="ada_mod",
    )(cc, ada_w, ada_b.reshape(depth, 1, 6 * D))


def _rope(seg, cos, sin_signed, lo_mask):
    w = seg.shape[1]
    from_hi = pltpu.roll(seg, w - 16, axis=1)
    from_lo = pltpu.roll(seg, 16, axis=1)
    partner = jnp.where(lo_mask, from_hi, from_lo)
    reps = w // cos.shape[1]
    c = jnp.concatenate([cos] * reps, axis=1)
    s = jnp.concatenate([sin_signed] * reps, axis=1)
    return seg * c + partner * s


def _inproj_kernel(*refs):
    x_refs = refs[:PROJ_SUB]
    c_ref, mod_ref, cmod_ref, nw_ref, w_ref, cos_ref, sin_ref, o_ref = refs[PROJ_SUB:]
    is_ctx = pl.program_id(1) == 0
    parts = []
    for s in range(PROJ_SUB):
        xt = x_refs[s][0]
        sh = mod_ref[0, :, 0:D]
        sc = mod_ref[0, :, D:2 * D]
        if s == 0:
            xt = jnp.where(is_ctx, c_ref[0], xt)
            sh = jnp.where(is_ctx, cmod_ref[0, :, 0:D], sh)
            sc = jnp.where(is_ctx, cmod_ref[0, :, D:2 * D], sc)
        parts.append(((_rms(xt) * nw_ref[...]) * (1.0 + sc) + sh).astype(BF16))
    hb = jnp.concatenate(parts, axis=0)
    lane = lax.broadcasted_iota(I32, (PROJ_SUB * TM_PROJ, 512), 1)
    lo_mask = (lane % 32) < 16
    cos = cos_ref[...]
    sin = sin_ref[...]

    def project(g):
        return jnp.dot(hb, w_ref[:, g * 512:(g + 1) * 512], preferred_element_type=F32)

    seg = project(0)
    for g in range(6):
        seg_next = project(g + 1) if g + 1 < 6 else None
        if g in (0, 3, 4):
            seg = _rope(seg, cos, sin, lo_mask)
        o_ref[0, :, g * 512:(g + 1) * 512] = seg.astype(BF16)
        seg = seg_next


def _inproj(x, ctx, mod3, norm_w, w_perm, cos_t, sin_t):
    tm = PROJ_SUB * TM_PROJ
    nj = RB // tm

    def x_map(s, b, j):
        return (b, jnp.maximum(PROJ_SUB * j + s - LC // TM_PROJ, 0), 0)

    return pl.pallas_call(
        _inproj_kernel,
        grid=(B, nj),
        in_specs=[
            *[pl.BlockSpec((1, TM_PROJ, D), functools.partial(x_map, s)) for s in range(PROJ_SUB)],
            pl.BlockSpec((1, LC, D), lambda b, j: (b, 0, 0)),
            pl.BlockSpec((1, 1, 2 * D), lambda b, j: (b, 0, 0)),
            pl.BlockSpec((1, 1, 2 * D), lambda b, j: (B, 0, 0)),
            pl.BlockSpec((1, D), lambda b, j: (0, 0)),
            pl.BlockSpec((D, IN_W), lambda b, j: (0, 0)),
            pl.BlockSpec((tm, 128), lambda b, j: (j, 0)),
            pl.BlockSpec((tm, 128), lambda b, j: (j, 0)),
        ],
        out_specs=pl.BlockSpec((1, tm, IN_W), lambda b, j: (b, j, 0)),
        out_shape=jax.ShapeDtypeStruct((B, RB, IN_W), BF16),
        compiler_params=_cparams(("arbitrary", "arbitrary")),
        name="inproj",
    )(*([x] * PROJ_SUB), ctx, mod3, mod3, norm_w, w_perm, cos_t, sin_t)


def _retention_kernel(ld_ref, qk_ref, v_ref, g_ref, o_ref, st_ref, kdec_ref, qdec_ref, mask_ref, cdec_ref):
    h = pl.program_id(0)
    lane = lax.broadcasted_iota(I32, (CH, 128), 1)
    fwd_lane = lane < DK

    @pl.when(pl.program_id(1) == 0)
    def _():
        lgf = ld_ref[0, h]
        lgb = ld_ref[1, h]
        pos = lax.broadcasted_iota(I32, (CH, 128), 0).astype(F32)
        kdec_ref[...] = jnp.where(fwd_lane, jnp.exp(lgf * (CH - 1 - pos)), jnp.exp(lgb * pos))
        qdec_ref[...] = jnp.where(fwd_lane, jnp.exp(lgf * (pos + 1.0)), jnp.exp(lgb * (CH - pos)))
        ii = lax.broadcasted_iota(I32, (CH, CH), 0)
        jj = lax.broadcasted_iota(I32, (CH, CH), 1)
        gap = (ii - jj).astype(F32)
        mask_ref[...] = (jnp.where(gap >= 0, jnp.exp(lgf * jnp.maximum(gap, 0.0)), 0.0)
                         + jnp.where(gap <= 0, jnp.exp(lgb * jnp.maximum(-gap, 0.0)), 0.0))
        ones = jnp.ones((DK, 128), F32)
        cdec_ref[0:DK, :] = jnp.exp(lgf * CH * ones)
        cdec_ref[DK:, :] = jnp.exp(lgb * CH * ones)

    kdec = kdec_ref[...]
    qdec = qdec_ref[...]
    mask = mask_ref[...]
    cf = cdec_ref[0:DK, :]
    cb = cdec_ref[DK:, :]

    def chunk(n):
        a = qk_ref[0, n * CH:(n + 1) * CH, :].astype(F32)
        swapped = pltpu.roll(a, DK, axis=1)
        return a, swapped

    kv = []
    for n in range(NCH):
        a, swapped = chunk(n)
        kk = jnp.where(fwd_lane, swapped, a)
        kb = (kk * kdec).astype(BF16)
        vn = v_ref[0, n * CH:(n + 1) * CH, :]
        kv.append(lax.dot_general(kb, vn, (((0,), (0,)), ((), ())), preferred_element_type=F32))
    sf = kv[0][:DK]
    for n in range(1, NCH):
        st_ref[n, 0:DK, :] = sf
        sf = cf * sf + kv[n][:DK]
    sb = kv[0][DK:]
    for n in range(NCH - 1, 0, -1):
        st_ref[n, DK:2 * DK, :] = sb
        sb = cb * sb + kv[n][DK:]

    for n in range(1, NCH):
        a, swapped = chunk(n)
        q = a[:, :DK].astype(BF16)
        k = swapped[:, :DK].astype(BF16)
        scores = lax.dot_general(q, k, (((1,), (1,)), ((), ())), preferred_element_type=F32)
        p = (scores * mask).astype(BF16)
        vn = v_ref[0, n * CH:(n + 1) * CH, :]
        qq = jnp.where(fwd_lane, a, swapped)
        qd = (qq * qdec).astype(BF16)
        o = (jnp.dot(p, vn, preferred_element_type=F32)
             + jnp.dot(qd, st_ref[n].astype(BF16), preferred_element_type=F32))
        gate = g_ref[0, n * CH:(n + 1) * CH, :].astype(F32)
        o_ref[0, (n - 1) * CH:n * CH, :] = (_rms(o) * _silu(gate)).astype(BF16)


def _retention(proj, log_decay):
    return pl.pallas_call(
        _retention_kernel,
        grid=(NH, B),
        in_specs=[
            pl.BlockSpec(memory_space=pltpu.SMEM),
            pl.BlockSpec((1, RB, 128), lambda h, b: (b, 0, h)),
            pl.BlockSpec((1, RB, 128), lambda h, b: (b, 0, NH + h)),
            pl.BlockSpec((1, RB, 128), lambda h, b: (b, 0, 2 * NH + h)),
        ],
        out_specs=pl.BlockSpec((1, T, 128), lambda h, b: (b, 0, h)),
        out_shape=jax.ShapeDtypeStruct((B, T, NH * HV), BF16),
        scratch_shapes=[pltpu.VMEM((NCH, 128, 128), F32), pltpu.VMEM((CH, 128), F32), pltpu.VMEM((CH, 128), F32),
                        pltpu.VMEM((CH, CH), F32), pltpu.VMEM((2 * DK, 128), F32)],
        compiler_params=_cparams(("arbitrary", "arbitrary")),
        name="retention",
    )(log_decay, proj, proj, proj)


def _diffattn_kernel(lam_ref, *refs, out_scale):
    q_refs = refs[:TQ_SUB]
    k_ref, v_ref, sw_ref, o_ref = refs[TQ_SUB:]
    lam = lam_ref[0]
    k = k_ref[0]
    v = v_ref[0]
    nt = (((1,), (1,)), ((), ()))

    def scores(qh):
        return lax.dot_general(qh, k, nt, preferred_element_type=F32)

    def values(s):
        e = jnp.exp2(s - jnp.max(s, axis=-1, keepdims=True))
        return jnp.dot(e.astype(BF16), v, preferred_element_type=F32), jnp.sum(e, axis=-1, keepdims=True)

    halves = []
    for i in range(TQ_SUB):
        q = q_refs[i][0]
        lane = lax.broadcasted_iota(I32, q.shape, 1)
        zero = jnp.zeros_like(q)
        halves += [jnp.where(lane < DK, q, zero), jnp.where(lane >= DK, q, zero)]
    outs = []
    ahead = [scores(h) for h in halves[:SCORE_AHEAD]]
    for c in range(len(halves)):
        if c + SCORE_AHEAD < len(halves):
            ahead.append(scores(halves[c + SCORE_AHEAD]))
        outs.append(values(ahead.pop(0)))
    for i in range(TQ_SUB):
        (o1, l1), (o2, l2) = outs[2 * i], outs[2 * i + 1]
        o = o1 / l1 - o2 * (lam / l2)
        o_ref[0, i * TM_PROJ:(i + 1) * TM_PROJ, :] = (_rms(o) * sw_ref[...] * out_scale).astype(BF16)


def _diffattn(proj, lam, subln_w, out_scale):
    tq = TQ_SUB * TM_PROJ
    nq = T // tq

    def q_map(i, b, h, j):
        return (b, LC // TM_PROJ + j * TQ_SUB + i, 3 * NH + h)

    return pl.pallas_call(
        functools.partial(_diffattn_kernel, out_scale=out_scale),
        grid=(B, NH, nq),
        in_specs=[
            pl.BlockSpec(memory_space=pltpu.SMEM),
            *[pl.BlockSpec((1, TM_PROJ, 128), functools.partial(q_map, i)) for i in range(TQ_SUB)],
            pl.BlockSpec((1, RB, 128), lambda b, h, j: (b, 0, 4 * NH + h)),
            pl.BlockSpec((1, RB, 128), lambda b, h, j: (b, 0, 5 * NH + h)),
            pl.BlockSpec((1, HV), lambda b, h, j: (0, 0)),
        ],
        out_specs=pl.BlockSpec((1, tq, 128), lambda b, h, j: (b, j, h)),
        out_shape=jax.ShapeDtypeStruct((B, T, NH * HV), BF16),
        compiler_params=_cparams(("arbitrary", "arbitrary", "arbitrary")),
        name="diffattn",
    )(lam, *([proj] * TQ_SUB), proj, proj, subln_w)


def _route(bz):
    grp = []
    for g in range(NGRP):
        m = bz[g * EPG:(g + 1) * EPG]
        best = None
        for i in range(EPG):
            for k in range(i + 1, EPG):
                pair = m[i] + m[k]
                best = pair if best is None else jnp.maximum(best, pair)
        grp.append(best)
    gbest = grp[0]
    gsel = jnp.zeros_like(gbest, dtype=I32)
    for g in range(1, NGRP):
        better = grp[g] > gbest
        gsel = jnp.where(better, g, gsel)
        gbest = jnp.where(better, grp[g], gbest)
    cb = [bz[i] for i in range(EPG)]
    for g in range(1, NGRP):
        pick = gsel == g
        cb = [jnp.where(pick, bz[g * EPG + i], cb[i]) for i in range(EPG)]
    i1 = jnp.zeros_like(gsel)
    b1 = cb[0]
    for i in range(1, EPG):
        better = cb[i] > b1
        i1 = jnp.where(better, i, i1)
        b1 = jnp.where(better, cb[i], b1)
    neg = jnp.full_like(b1, -jnp.inf)
    rest = [jnp.where(i1 == i, neg, cb[i]) for i in range(EPG)]
    i2 = jnp.zeros_like(gsel)
    b2 = rest[0]
    for i in range(1, EPG):
        better = rest[i] > b2
        i2 = jnp.where(better, i, i2)
        b2 = jnp.where(better, rest[i], b2)
    lo = jnp.minimum(i1, i2)
    hi = jnp.maximum(i1, i2)
    code = lo * EPG + hi
    pair = jnp.full_like(gsel, len(PAIR_A) - 1)
    for p in range(len(PAIR_A) - 1):
        a, b = min(PAIR_A[p], PAIR_B[p]), max(PAIR_A[p], PAIR_B[p])
        pair = jnp.where(code == a * EPG + b, p, pair)
    return gsel * len(PAIR_A) + pair


def _ffn_prologue(x1, row0, mod, nfw_ref, rw_ref, hp_ref, logit_ref):
    rows = x1.shape[0]
    sh2 = mod[:, 3 * D:4 * D]
    sc2 = mod[:, 4 * D:5 * D]
    h2 = (_rms(x1) * nfw_ref[...]) * (1.0 + sc2) + sh2
    _store_slabs(hp_ref, h2, row0)
    logit_ref[:, row0:row0 + rows] = _dot_3pass(rw_ref[...], h2, (((1,), (1,)), ((), ())))


def _ffn_out_specs(tm, n_tiles_per_b):
    specs = [
        pl.BlockSpec((1, tm, D), lambda b, j: (b, j, 0)),
        pl.BlockSpec((tm * SLAB, 128), lambda b, j: (b * n_tiles_per_b + j, 0)),
        pl.BlockSpec((NE, tm), lambda b, j: (0, b * n_tiles_per_b + j)),
    ]
    shapes = [
        jax.ShapeDtypeStruct((B, T, D), F32),
        jax.ShapeDtypeStruct((N * SLAB, 128), F32),
        jax.ShapeDtypeStruct((NE, N), F32),
    ]
    return specs, shapes


def _route_kernel(logit_ref, bias_ref, dest_ref, ends_ref):
    r = logit_ref.shape[1]
    cls = _route([_sigmoid(logit_ref[e]) + bias_ref[e] for e in range(NE)])
    lane_incl = (lax.broadcasted_iota(I32, (128, 128), 0) <= lax.broadcasted_iota(I32, (128, 128), 1)).astype(BF16)
    rows_before = (lax.broadcasted_iota(I32, (r, r), 1) < lax.broadcasted_iota(I32, (r, r), 0)).astype(BF16)
    dest = jnp.zeros((r, 128), F32)
    start = jnp.zeros((1, 128), F32)
    ends = []
    for c in range(NCLS):
        onehot = jnp.where(cls == c, 1.0, 0.0)
        in_row = jnp.dot(onehot.astype(BF16), lane_incl, preferred_element_type=F32)
        row_tot = jnp.broadcast_to(in_row[:, 127:128], (r, 128))
        above = jnp.dot(rows_before, row_tot.astype(BF16), preferred_element_type=F32)
        dest = dest + onehot * (start + above + in_row - 1.0)
        start = start + jnp.sum(row_tot, axis=0, keepdims=True)
        ends.append(start)
    dest_ref[...] = dest.astype(I32)
    ends_ref[...] = jnp.concatenate(ends + [jnp.zeros((CLS_PAD - NCLS, 128), F32)], axis=0)


def _route_tokens(logits_t, router_b):
    r = N // 128
    return pl.pallas_call(
        _route_kernel,
        in_specs=[pl.BlockSpec((NE, r, 128), lambda: (0, 0, 0)), pl.BlockSpec(memory_space=pltpu.SMEM)],
        out_specs=[pl.BlockSpec((r, 128), lambda: (0, 0)), pl.BlockSpec((CLS_PAD, 128), lambda: (0, 0))],
        out_shape=[jax.ShapeDtypeStruct((r, 128), I32), jax.ShapeDtypeStruct((CLS_PAD, 128), F32)],
        compiler_params=pltpu.CompilerParams(vmem_limit_bytes=VMEM_LIMIT),
        name="route",
    )(logits_t.reshape(NE, r, 128), router_b)


def _outproj_kernel(ret_ref, dif_ref, w_ref, x_ref, mod_ref, nfw_ref, rw_ref, x1_ref, hp_ref, logit_ref):
    mod = mod_ref[0]

    def mix(r0):
        rows = slice(r0, r0 + SUB_OUT)
        return (jnp.dot(ret_ref[0, rows, :], w_ref[0:NH * HV, :], preferred_element_type=F32)
                + jnp.dot(dif_ref[0, rows, :], w_ref[NH * HV:, :], preferred_element_type=F32))

    mx = mix(0)
    for r0 in range(0, TM_OUT, SUB_OUT):
        mx_next = mix(r0 + SUB_OUT) if r0 + SUB_OUT < TM_OUT else None
        x1 = x_ref[0, r0:r0 + SUB_OUT, :] + mod[:, 2 * D:3 * D] * mx
        x1_ref[0, r0:r0 + SUB_OUT, :] = x1
        _ffn_prologue(x1, r0, mod, nfw_ref, rw_ref, hp_ref, logit_ref)
        mx = mx_next


def _outproj(ret, dif, w_out, x, mod3, nfw, rw_t):
    nj = T // TM_OUT
    out_specs, out_shapes = _ffn_out_specs(TM_OUT, nj)
    return pl.pallas_call(
        _outproj_kernel,
        grid=(B, nj),
        in_specs=[
            pl.BlockSpec((1, TM_OUT, NH * HV), lambda b, j: (b, j, 0)),
            pl.BlockSpec((1, TM_OUT, NH * HV), lambda b, j: (b, j, 0)),
            pl.BlockSpec((2 * NH * HV, D), lambda b, j: (0, 0)),
            pl.BlockSpec((1, TM_OUT, D), lambda b, j: (b, j, 0)),
            pl.BlockSpec((1, 1, 6 * D), lambda b, j: (b, 0, 0)),
            pl.BlockSpec((1, D), lambda b, j: (0, 0)),
            pl.BlockSpec((NE, D), lambda b, j: (0, 0)),
        ],
        out_specs=out_specs,
        out_shape=out_shapes,
        compiler_params=_cparams(("arbitrary", "arbitrary")),
        name="outproj",
    )(ret, dif, w_out, x, mod3, nfw, rw_t)


def _pool_kernel(x_ref, prev_ref, next_ref, mod_ref, nmw_ref, pw_ref, ps_ref, nfw_ref, rw_ref,
                 x1_ref, hp_ref, logit_ref, ext_ref):
    i = pl.program_id(1)
    last = pl.num_programs(1) - 1
    mod = mod_ref[0]
    sh1 = mod[:, 0:D]
    sc1 = mod[:, D:2 * D]

    def modnorm(v):
        return (_rms(v) * nmw_ref[...]) * (1.0 + sc1) + sh1

    x = x_ref[0]
    hc = modnorm(x)
    ext_ref[0:HALO, :] = jnp.where(i > 0, modnorm(prev_ref[0]), 0.0)
    ext_ref[HALO:HALO + TM_POOL, :] = hc
    ext_ref[HALO + TM_POOL:, :] = jnp.where(i < last, modnorm(next_ref[0]), 0.0)
    pos = i * TM_POOL + lax.broadcasted_iota(I32, (TM_POOL, 1), 0)
    mixed = []
    for gi, w in enumerate(POOL_WINDOWS):
        left = w // 2
        right = w - 1 - left
        cols = slice(gi * PG, (gi + 1) * PG)
        tot = None
        for d in range(-left, right + 1):
            part = ext_ref[HALO + d:HALO + d + TM_POOL, cols]
            tot = part if tot is None else tot + part
        cnt = (jnp.minimum(pos + right + 1, T) - jnp.maximum(pos - left, 0)).astype(F32)
        pooled = (tot * (1.0 / cnt) - hc[:, cols]).astype(BF16)
        mixed.append(jnp.dot(pooled, pw_ref[gi], preferred_element_type=F32))
    mixed = jnp.concatenate(mixed, axis=1) * ps_ref[...]
    x1 = x + mod[:, 2 * D:3 * D] * mixed
    x1_ref[0] = x1
    _ffn_prologue(x1, 0, mod, nfw_ref, rw_ref, hp_ref, logit_ref)


def _pool_layer(x, mod3, nmw, pool_w, pool_scale, nfw, rw_t):
    ni = T // TM_POOL
    hb = TM_POOL // HALO
    out_specs, out_shapes = _ffn_out_specs(TM_POOL, ni)
    return pl.pallas_call(
        _pool_kernel,
        grid=(B, ni),
        in_specs=[
            pl.BlockSpec((1, TM_POOL, D), lambda b, i: (b, i, 0)),
            pl.BlockSpec((1, HALO, D), lambda b, i: (b, jnp.maximum(i * hb - 1, 0), 0)),
            pl.BlockSpec((1, HALO, D), lambda b, i: (b, jnp.minimum((i + 1) * hb, T // HALO - 1), 0)),
            pl.BlockSpec((1, 1, 6 * D), lambda b, i: (b, 0, 0)),
            pl.BlockSpec((1, D), lambda b, i: (0, 0)),
            pl.BlockSpec((len(POOL_WINDOWS), PG, PG), lambda b, i: (0, 0, 0)),
            pl.BlockSpec((1, D), lambda b, i: (0, 0)),
            pl.BlockSpec((1, D), lambda b, i: (0, 0)),
            pl.BlockSpec((NE, D), lambda b, i: (0, 0)),
        ],
        out_specs=out_specs,
        out_shape=out_shapes,
        scratch_shapes=[pltpu.VMEM((TM_POOL + 2 * HALO, D), F32)],
        compiler_params=_cparams(("arbitrary", "arbitrary")),
        name="pool_layer",
    )(x, x, x, mod3, nmw, pool_w, pool_scale, nfw, rw_t)


def _tile_copy(src_ref, dst_ref, sem, s, d, rows=1):
    s0 = pl.multiple_of(s * SLAB, SLAB)
    d0 = pl.multiple_of(d * SLAB, SLAB)
    return pltpu.make_async_copy(src_ref.at[pl.ds(s0, rows * SLAB)], dst_ref.at[pl.ds(d0, rows * SLAB)], sem)


def _issue_tile_copies(idx_ref, base, rows, start_one):
    def group(g, carry):
        r0 = g * PERM_UNROLL
        ids = [idx_ref[base + r0 + u] for u in range(PERM_UNROLL)]
        for u in range(PERM_UNROLL):
            start_one(r0 + u, ids[u], u % 2)
        return carry

    lax.fori_loop(0, rows // PERM_UNROLL, group, 0)


def _dispatch_kernel(dest_ref, src_ref, dst_ref, zero_ref, sem, pad_sem):
    base = pl.program_id(0) * TM_PERM

    @pl.when(pl.program_id(0) == 0)
    def _():
        zero_ref[...] = jnp.zeros_like(zero_ref)
        pad = _tile_copy(zero_ref, dst_ref, pad_sem, 0, N, TM_MOE)
        pad.start()
        pad.wait()

    def start_one(r, d, priority):
        _tile_copy(src_ref, dst_ref, sem, r, d).start(priority=priority)

    _issue_tile_copies(dest_ref, base, TM_PERM, start_one)
    _tile_copy(src_ref, dst_ref, sem, 0, 0, TM_PERM).wait()


def _dispatch(dest, src):
    return pl.pallas_call(
        _dispatch_kernel,
        grid_spec=pltpu.PrefetchScalarGridSpec(
            num_scalar_prefetch=1,
            grid=(N // TM_PERM,),
            in_specs=[pl.BlockSpec((TM_PERM * SLAB, 128), lambda i, dest: (i, 0))],
            out_specs=pl.BlockSpec(memory_space=pl.ANY),
            scratch_shapes=[pltpu.VMEM((TM_MOE * SLAB, 128), F32), pltpu.SemaphoreType.DMA(()),
                            pltpu.SemaphoreType.DMA(())],
        ),
        out_shape=jax.ShapeDtypeStruct(((N + TM_MOE) * SLAB, 128), src.dtype),
        compiler_params=_cparams(("arbitrary",)),
        name="dispatch",
    )(dest, src)


def _combine_kernel(dest_ref, x_ref, ys_ref, mod_ref, fw_ref, o_ref, ybuf_ref, sem, *, final):
    i = pl.program_id(0)
    n = pl.num_programs(0)
    slot = i % 2

    def gather(step, to_slot):
        def start_one(r, d, priority):
            _tile_copy(ys_ref, ybuf_ref.at[to_slot], sem.at[to_slot], d, r).start(priority=priority)

        _issue_tile_copies(dest_ref, step * TM_COMB, TM_COMB, start_one)

    @pl.when(i == 0)
    def _():
        gather(0, 0)

    @pl.when(i + 1 < n)
    def _():
        gather(i + 1, 1 - slot)

    _tile_copy(ys_ref, ybuf_ref.at[slot], sem.at[slot], 0, 0, TM_COMB).wait()
    out = x_ref[...] + mod_ref[0][:, 5 * D:6 * D] * _load_slabs(ybuf_ref.at[slot], TM_COMB)
    if final:
        out = _rms(out) * fw_ref[...]
    o_ref[...] = out


def _combine(dest, x1, ys, mod3, final_w, final):
    per_b = T // TM_COMB
    return pl.pallas_call(
        functools.partial(_combine_kernel, final=final),
        grid_spec=pltpu.PrefetchScalarGridSpec(
            num_scalar_prefetch=1,
            grid=(N // TM_COMB,),
            in_specs=[
                pl.BlockSpec((TM_COMB, D), lambda i, dest: (i, 0)),
                pl.BlockSpec(memory_space=pl.ANY),
                pl.BlockSpec((1, 1, 6 * D), lambda i, dest: (i // per_b, 0, 0)),
                pl.BlockSpec((1, D), lambda i, dest: (0, 0)),
            ],
            out_specs=pl.BlockSpec((TM_COMB, D), lambda i, dest: (i, 0)),
            scratch_shapes=[pltpu.VMEM((2, TM_COMB * SLAB, 128), F32), pltpu.SemaphoreType.DMA((2,))],
        ),
        out_shape=jax.ShapeDtypeStruct((N, D), F32),
        compiler_params=_cparams(("arbitrary",)),
        name="combine",
    )(dest, x1.reshape(N, D), ys, mod3, final_w)


def _moe_kernel(start_ref, ea_ref, eb_ref, ca_ref, cb_ref, na_ref, nb_ref, n_ref,
                rw_ref, xs_hbm, wg_hbm, wu_hbm, wd_hbm, ys_hbm,
                xbuf_ref, obuf_ref,
                sga_ref, sua_ref, sda_ref, sgb_ref, sub_ref, sdb_ref,
                ga_ref, ua_ref, da_ref, gb_ref, ub_ref, db_ref, wsem, xsem, osem, *, layer):
    m = pl.program_id(0)
    n = n_ref[0]
    slot = m % 2
    slots = ((ea_ref, ca_ref, na_ref, (sga_ref, sua_ref, sda_ref), (ga_ref, ua_ref, da_ref)),
             (eb_ref, cb_ref, nb_ref, (sgb_ref, sub_ref, sdb_ref), (gb_ref, ub_ref, db_ref)))

    def fetch(w_slot, expert):
        stage = slots[w_slot][3]
        return [pltpu.make_async_copy(w.at[layer, expert], s, wsem.at[w_slot])
                for w, s in zip((wg_hbm, wu_hbm, wd_hbm), stage)]

    def x_copy(step, buf):
        return _tile_copy(xs_hbm, xbuf_ref.at[buf], xsem.at[buf], start_ref[step], 0, TM_MOE)

    def o_copy(step, buf):
        return _tile_copy(obuf_ref.at[buf], ys_hbm, osem.at[buf], 0, start_ref[step], TM_MOE)

    @pl.when(m == 0)
    def _():
        obuf_ref[1] = jnp.zeros_like(obuf_ref[1])
        pad = _tile_copy(obuf_ref.at[1], ys_hbm, osem.at[1], 0, N, TM_MOE)
        pad.start()
        pad.wait()
        x_copy(0, 0).start()

    @pl.when(m < n)
    def _():
        @pl.when(m + 1 < n)
        def _():
            x_copy(m + 1, 1 - slot).start()

        for w_slot, (e_ref, c_ref, nx_ref, stage, work) in enumerate(slots):
            @pl.when(m == 0)
            def _():
                for cp in fetch(w_slot, e_ref[0]):
                    cp.start(priority=1)

            @pl.when(c_ref[m] == 1)
            def _():
                for cp in fetch(w_slot, 0):
                    cp.wait()
                for s, w in zip(stage, work):
                    w[...] = s[...].astype(BF16)

                @pl.when(nx_ref[m] >= 0)
                def _():
                    for cp in fetch(w_slot, nx_ref[m]):
                        cp.start(priority=1)

        x_copy(m, slot).wait()
        hf = _load_slabs(xbuf_ref.at[slot], TM_MOE)
        h = hf.astype(BF16)

        def up(w_ref):
            return jnp.dot(h, w_ref[...], preferred_element_type=F32)

        def down(g, u, d_ref):
            return jnp.dot((_silu(g) * u).astype(BF16), d_ref[...], preferred_element_type=F32)

        g_a, u_a, g_b, u_b = up(ga_ref), up(ua_ref), up(gb_ref), up(ub_ref)
        y_a = down(g_a, u_a, da_ref)
        y_b = down(g_b, u_b, db_ref)
        s_a = _sigmoid(jnp.sum(hf * rw_ref[pl.ds(ea_ref[m], 1), :], axis=1, keepdims=True))
        s_b = _sigmoid(jnp.sum(hf * rw_ref[pl.ds(eb_ref[m], 1), :], axis=1, keepdims=True))
        denom = s_a + s_b
        val = (s_a / denom) * y_a + (s_b / denom) * y_b

        @pl.when(m >= 1)
        def _():
            o_copy(m - 1, 1 - slot).wait()

        _store_slabs(obuf_ref.at[slot], val)
        o_copy(m, slot).start()

        @pl.when(m == n - 1)
        def _():
            o_copy(m, slot).wait()


def _moe_sorted(chunks, xs, rw_t, wg, wu, wd, layer):
    mats = lambda dt: [pltpu.VMEM((D, DE), dt), pltpu.VMEM((D, DE), dt), pltpu.VMEM((DE, D), dt)]
    hbm = pl.BlockSpec(memory_space=pl.ANY)
    return pl.pallas_call(
        functools.partial(_moe_kernel, layer=layer),
        grid_spec=pltpu.PrefetchScalarGridSpec(
            num_scalar_prefetch=len(chunks),
            grid=(N_UNITS,),
            in_specs=[pl.BlockSpec((NE, D), lambda m, *_: (0, 0)), hbm, hbm, hbm, hbm],
            out_specs=hbm,
            scratch_shapes=([pltpu.VMEM((2, TM_MOE * SLAB, 128), F32), pltpu.VMEM((2, TM_MOE * SLAB, 128), F32)]
                            + mats(F32) + mats(F32) + mats(BF16) + mats(BF16)
                            + [pltpu.SemaphoreType.DMA((2,))] * 3),
        ),
        out_shape=jax.ShapeDtypeStruct(xs.shape, F32),
        compiler_params=_cparams(("arbitrary",)),
        name="moe_sorted",
    )(*chunks, rw_t, xs, wg, wu, wd)


def _moe_chunks(ends_f):
    ends = ends_f[:NCLS, 0].astype(I32)
    offs = jnp.concatenate([jnp.zeros((1,), I32), ends[:-1]])
    per_cls = (ends - offs + (TM_MOE - 1)) // TM_MOE
    cum = jnp.cumsum(per_cls)
    count = cum[-1:]
    idx = jnp.arange(N_UNITS, dtype=I32)
    m = jnp.minimum(idx, count - 1)
    cls = jnp.minimum(jnp.sum(cum[None, :] <= m[:, None], axis=1), NCLS - 1).astype(I32)
    of_cls = cls[:, None] == jnp.arange(NCLS, dtype=I32)[None, :]

    def pick(per_class):
        return jnp.sum(jnp.where(of_cls, per_class[None, :], 0), axis=1)

    start = pick(offs) + TM_MOE * (m - (pick(cum) - pick(per_cls)))
    grp = cls // len(PAIR_A)
    pair = cls % len(PAIR_A)
    e_a = grp * EPG + jnp.take(jnp.array(PAIR_A, I32), pair)
    e_b = grp * EPG + jnp.take(jnp.array(PAIR_B, I32), pair)
    one = jnp.ones((1,), I32)

    def changes(e):
        chg = jnp.concatenate([one, (e[1:] != e[:-1]).astype(I32)])
        at = jnp.where(chg == 1, idx, N_UNITS)
        nxt_at = jnp.concatenate([lax.cummin(at, reverse=True)[1:], jnp.full((1,), N_UNITS, I32)])
        nxt = jnp.where(nxt_at < N_UNITS, jnp.take(e, jnp.minimum(nxt_at, N_UNITS - 1)), -1)
        return chg, nxt

    chg_a, nxt_a = changes(e_a)
    chg_b, nxt_b = changes(e_b)
    return start, e_a, e_b, chg_a, chg_b, nxt_a, nxt_b, count


def _moe_layer(x1, hp, logits, rw_t, router_b, wg, wu, wd, layer, mod3, final_w, final):
    dest, ends = _route_tokens(logits, router_b)
    dest = dest.reshape(N)
    xs = _dispatch(dest, hp)
    ys = _moe_sorted(_moe_chunks(ends), xs, rw_t, wg, wu, wd, layer)
    return _combine(dest, x1, ys, mod3, final_w, final).reshape(B, T, D)


def _rope_tables():
    half = 16
    inv = ROPE_BASE ** (-np.arange(half, dtype=np.float64) / half)
    t = np.arange(T)
    ang_r = (t // GRID_W)[:, None] * inv[None, :]
    ang_c = (t % GRID_W)[:, None] * inv[None, :]
    ang = np.concatenate([ang_r, ang_r, ang_c, ang_c], axis=1)
    sign = np.tile(np.concatenate([-np.ones(half), np.ones(half)]), 2)
    cos = np.concatenate([np.ones((LC, 64)), np.cos(ang)], axis=0)
    sin = np.concatenate([np.zeros((LC, 64)), np.sin(ang) * sign[None, :]], axis=0)
    return (jnp.asarray(np.tile(cos, (1, 2)), dtype=F32), jnp.asarray(np.tile(sin, (1, 2)), dtype=F32))


def _permute_w_in(w):
    rq = w[:, 0:256].reshape(D, NH, DK)
    dq = w[:, 256:768]
    rg = w[:, 768:1280]
    rk = w[:, 1280:1536].reshape(D, NH, DK)
    rv = w[:, 1536:2048]
    dk = w[:, 2048:2560]
    dv = w[:, 2560:3072]
    qk = jnp.concatenate([rq, rk * (DK ** -0.5)], axis=2).reshape(D, NH * 2 * DK)
    return jnp.concatenate([qk, rv, rg, dq * (DK ** -0.5 * math.log2(math.e)), dk, dv], axis=1).astype(BF16)


def kernel(x, c, ctx, c_ctx, ada_w, ada_b, norm_mix_w, norm_ffn_w, w_in, w_out, ret_log_decay, diff_lambda,
           diff_subln_w, pool_w, pool_scale, router_w, router_b, moe_w_gate, moe_w_up, moe_w_down, final_norm_w):
    assert x.shape == (B, T, D) and ctx.shape == (B, LC, D) and ada_w.shape[0] == 2
    cc = jnp.concatenate([c, c_ctx[None, :], jnp.zeros((16 - B - 1, D), F32)], axis=0)
    mod = _ada_mod(cc, ada_w, ada_b)
    rw_t = router_w.T
    fw = final_norm_w.reshape(1, D)
    experts = (moe_w_gate, moe_w_up, moe_w_down)

    mod0 = mod[0].reshape(16, 1, 6 * D)
    cos_t, sin_t = _rope_tables()
    proj = _inproj(x, ctx, mod0, norm_mix_w[0:1], _permute_w_in(w_in[0]), cos_t, sin_t)
    ret = _retention(proj, ret_log_decay[0])
    lam_init = 0.8 - 0.6 * math.exp(-0.3 * 0)
    lv = diff_lambda[0]
    lam = jnp.exp(jnp.sum(lv[0] * lv[1])) - jnp.exp(jnp.sum(lv[2] * lv[3])) + lam_init
    dif = _diffattn(proj, lam.reshape(1), diff_subln_w[0:1], 1.0 - lam_init)
    x1, hp, logits = _outproj(ret, dif, w_out[0].astype(BF16), x, mod0, norm_ffn_w[0:1], rw_t)
    x2 = _moe_layer(x1, hp, logits, rw_t, router_b, *experts, 0, mod0, fw, False)

    mod1 = mod[1].reshape(16, 1, 6 * D)
    x3, hp, logits = _pool_layer(x2, mod1, norm_mix_w[1:2], pool_w[0].astype(BF16), pool_scale[0:1],
                                 norm_ffn_w[1:2], rw_t)
    return _moe_layer(x3, hp, logits, rw_t, router_b, *experts, 1, mod1, fw, True)
```

```python
import functools
import math

import jax
import jax.numpy as jnp
import numpy as np
from jax import lax
from jax.experimental import pallas as pl
from jax.experimental.pallas import tpu as pltpu

F32 = jnp.float32
BF16 = jnp.bfloat16
I32 = jnp.int32

D = 1024
B = 8
T = 2048
N = B * T
GRID_W = 64
LC = 256
EPS = 1e-6
ROPE_BASE = 10000.0
NH = 4
DK = 64
HV = 128
CH = 256
RB = LC + T
NCH = RB // CH
POOL_WINDOWS = (2, 4, 8, 16)
PG = D // len(POOL_WINDOWS)
NE = 16
NGRP = 4
EPG = NE // NGRP
DE = 512
IN_W = 3072
HALO = 8

PAIR_A = (0, 0, 0, 1, 1, 3)
PAIR_B = (1, 2, 3, 3, 2, 2)
NCLS = NGRP * len(PAIR_A)
CLS_PAD = 32
SLAB = D // 128

TM_PROJ = 256
PROJ_SUB = 3
TM_OUT = 1024
SUB_OUT = 512
TQ_SUB = 8
SCORE_AHEAD = 1
TM_POOL = 512
TM_MOE = 256
N_UNITS = N // TM_MOE + NCLS
TM_PERM = 2048
TM_COMB = 512
PERM_UNROLL = 16
VMEM_LIMIT = 56 * 1024 * 1024


def _cparams(sem):
    return pltpu.CompilerParams(dimension_semantics=sem, vmem_limit_bytes=VMEM_LIMIT)


def _sigmoid(x):
    return 1.0 / (1.0 + jnp.exp(-x))


def _silu(x):
    return x * _sigmoid(x)


def _rms(x):
    return x * lax.rsqrt(jnp.mean(x * x, axis=-1, keepdims=True) + EPS)


def _dot_3pass(a, b, dims):
    a_hi = a.astype(BF16)
    b_hi = b.astype(BF16)
    a_lo = (a - a_hi.astype(F32)).astype(BF16)
    b_lo = (b - b_hi.astype(F32)).astype(BF16)

    def dot(x, y):
        return lax.dot_general(x, y, dims, preferred_element_type=F32)

    return dot(a_hi, b_hi) + (dot(a_lo, b_hi) + dot(a_hi, b_lo))


def _load_slabs(ref, rows):
    return jnp.concatenate([ref[pl.ds(s, rows, stride=SLAB), :] for s in range(SLAB)], axis=1)


def _store_slabs(ref, val, row0=0):
    rows = val.shape[0]
    for s in range(SLAB):
        ref[pl.ds(row0 * SLAB + s, rows, stride=SLAB), :] = val[:, s * 128:(s + 1) * 128]


def _ada_kernel(cc_ref, w_ref, b_ref, o_ref):
    s = _silu(cc_ref[...])
    o_ref[0] = _dot_3pass(s, w_ref[0], (((1,), (0,)), ((), ()))) + b_ref[0]


def _ada_mod(cc, ada_w, ada_b):
    depth = ada_w.shape[0]
    tn = 1536
    return pl.pallas_call(
        _ada_kernel,
        grid=(depth, 6 * D // tn),
        in_specs=[
            pl.BlockSpec((16, D), lambda l, n: (0, 0)),
            pl.BlockSpec((1, D, tn), lambda l, n: (l, 0, n)),
            pl.BlockSpec((1, 1, tn), lambda l, n: (l, 0, n)),
        ],
        out_specs=pl.BlockSpec((1, 16, tn), lambda l, n: (l, 0, n)),
        out_shape=jax.ShapeDtypeStruct((depth, 16, 6 * D), F32),
        compiler_params=_cparams(("arbitrary", "arbitrary")),
        name="ada_mod",
    )(cc, ada_w, ada_b.reshape(depth, 1, 6 * D))


def _rope(seg, cos, sin_signed, lo_mask):
    w = seg.shape[1]
    from_hi = pltpu.roll(seg, w - 16, axis=1)
    from_lo = pltpu.roll(seg, 16, axis=1)
    partner = jnp.where(lo_mask, from_hi, from_lo)
    reps = w // cos.shape[1]
    c = jnp.concatenate([cos] * reps, axis=1)
    s = jnp.concatenate([sin_signed] * reps, axis=1)
    return seg * c + partner * s


def _inproj_kernel(*refs):
    x_refs = refs[:PROJ_SUB]
    c_ref, mod_ref, cmod_ref, nw_ref, w_ref, cos_ref, sin_ref, o_ref = refs[PROJ_SUB:]
    is_ctx = pl.program_id(1) == 0
    parts = []
    for s in range(PROJ_SUB):
        xt = x_refs[s][0]
        sh = mod_ref[0, :, 0:D]
        sc = mod_ref[0, :, D:2 * D]
        if s == 0:
            xt = jnp.where(is_ctx, c_ref[0], xt)
            sh = jnp.where(is_ctx, cmod_ref[0, :, 0:D], sh)
            sc = jnp.where(is_ctx, cmod_ref[0, :, D:2 * D], sc)
        parts.append(((_rms(xt) * nw_ref[...]) * (1.0 + sc) + sh).astype(BF16))
    hb = jnp.concatenate(parts, axis=0)
    lane = lax.broadcasted_iota(I32, (PROJ_SUB * TM_PROJ, 512), 1)
    lo_mask = (lane % 32) < 16
    cos = cos_ref[...]
    sin = sin_ref[...]

    def project(g):
        return jnp.dot(hb, w_ref[:, g * 512:(g + 1) * 512], preferred_element_type=F32)

    seg = project(0)
    for g in range(6):
        seg_next = project(g + 1) if g + 1 < 6 else None
        if g in (0, 3, 4):
            seg = _rope(seg, cos, sin, lo_mask)
        o_ref[0, :, g * 512:(g + 1) * 512] = seg.astype(BF16)
        seg = seg_next


def _inproj(x, ctx, mod3, norm_w, w_perm, cos_t, sin_t):
    tm = PROJ_SUB * TM_PROJ
    nj = RB // tm

    def x_map(s, b, j):
        return (b, jnp.maximum(PROJ_SUB * j + s - LC // TM_PROJ, 0), 0)

    return pl.pallas_call(
        _inproj_kernel,
        grid=(B, nj),
        in_specs=[
            *[pl.BlockSpec((1, TM_PROJ, D), functools.partial(x_map, s)) for s in range(PROJ_SUB)],
            pl.BlockSpec((1, LC, D), lambda b, j: (b, 0, 0)),
            pl.BlockSpec((1, 1, 2 * D), lambda b, j: (b, 0, 0)),
            pl.BlockSpec((1, 1, 2 * D), lambda b, j: (B, 0, 0)),
            pl.BlockSpec((1, D), lambda b, j: (0, 0)),
            pl.BlockSpec((D, IN_W), lambda b, j: (0, 0)),
            pl.BlockSpec((tm, 128), lambda b, j: (j, 0)),
            pl.BlockSpec((tm, 128), lambda b, j: (j, 0)),
        ],
        out_specs=pl.BlockSpec((1, tm, IN_W), lambda b, j: (b, j, 0)),
        out_shape=jax.ShapeDtypeStruct((B, RB, IN_W), BF16),
        compiler_params=_cparams(("arbitrary", "arbitrary")),
        name="inproj",
    )(*([x] * PROJ_SUB), ctx, mod3, mod3, norm_w, w_perm, cos_t, sin_t)


def _retention_kernel(ld_ref, qk_ref, v_ref, g_ref, o_ref, st_ref, kdec_ref, qdec_ref, mask_ref, cdec_ref):
    h = pl.program_id(0)
    lane = lax.broadcasted_iota(I32, (CH, 128), 1)
    fwd_lane = lane < DK

    @pl.when(pl.program_id(1) == 0)
    def _():
        lgf = ld_ref[0, h]
        lgb = ld_ref[1, h]
        pos = lax.broadcasted_iota(I32, (CH, 128), 0).astype(F32)
        kdec_ref[...] = jnp.where(fwd_lane, jnp.exp(lgf * (CH - 1 - pos)), jnp.exp(lgb * pos))
        qdec_ref[...] = jnp.where(fwd_lane, jnp.exp(lgf * (pos + 1.0)), jnp.exp(lgb * (CH - pos)))
        ii = lax.broadcasted_iota(I32, (CH, CH), 0)
        jj = lax.broadcasted_iota(I32, (CH, CH), 1)
        gap = (ii - jj).astype(F32)
        mask_ref[...] = (jnp.where(gap >= 0, jnp.exp(lgf * jnp.maximum(gap, 0.0)), 0.0)
                         + jnp.where(gap <= 0, jnp.exp(lgb * jnp.maximum(-gap, 0.0)), 0.0))
        ones = jnp.ones((DK, 128), F32)
        cdec_ref[0:DK, :] = jnp.exp(lgf * CH * ones)
        cdec_ref[DK:, :] = jnp.exp(lgb * CH * ones)

    kdec = kdec_ref[...]
    qdec = qdec_ref[...]
    mask = mask_ref[...]
    cf = cdec_ref[0:DK, :]
    cb = cdec_ref[DK:, :]

    def chunk(n):
        a = qk_ref[0, n * CH:(n + 1) * CH, :].astype(F32)
        swapped = pltpu.roll(a, DK, axis=1)
        return a, swapped

    kv = []
    for n in range(NCH):
        a, swapped = chunk(n)
        kk = jnp.where(fwd_lane, swapped, a)
        kb = (kk * kdec).astype(BF16)
        vn = v_ref[0, n * CH:(n + 1) * CH, :]
        kv.append(lax.dot_general(kb, vn, (((0,), (0,)), ((), ())), preferred_element_type=F32))
    sf = kv[0][:DK]
    for n in range(1, NCH):
        st_ref[n, 0:DK, :] = sf
        sf = cf * sf + kv[n][:DK]
    sb = kv[0][DK:]
    for n in range(NCH - 1, 0, -1):
        st_ref[n, DK:2 * DK, :] = sb
        sb = cb * sb + kv[n][DK:]

    for n in range(1, NCH):
        a, swapped = chunk(n)
        q = a[:, :DK].astype(BF16)
        k = swapped[:, :DK].astype(BF16)
        scores = lax.dot_general(q, k, (((1,), (1,)), ((), ())), preferred_element_type=F32)
        p = (scores * mask).astype(BF16)
        vn = v_ref[0, n * CH:(n + 1) * CH, :]
        qq = jnp.where(fwd_lane, a, swapped)
        qd = (qq * qdec).astype(BF16)
        o = (jnp.dot(p, vn, preferred_element_type=F32)
             + jnp.dot(qd, st_ref[n].astype(BF16), preferred_element_type=F32))
        gate = g_ref[0, n * CH:(n + 1) * CH, :].astype(F32)
        o_ref[0, (n - 1) * CH:n * CH, :] = (_rms(o) * _silu(gate)).astype(BF16)


def _retention(proj, log_decay):
    return pl.pallas_call(
        _retention_kernel,
        grid=(NH, B),
        in_specs=[
            pl.BlockSpec(memory_space=pltpu.SMEM),
            pl.BlockSpec((1, RB, 128), lambda h, b: (b, 0, h)),
            pl.BlockSpec((1, RB, 128), lambda h, b: (b, 0, NH + h)),
            pl.BlockSpec((1, RB, 128), lambda h, b: (b, 0, 2 * NH + h)),
        ],
        out_specs=pl.BlockSpec((1, T, 128), lambda h, b: (b, 0, h)),
        out_shape=jax.ShapeDtypeStruct((B, T, NH * HV), BF16),
        scratch_shapes=[pltpu.VMEM((NCH, 128, 128), F32), pltpu.VMEM((CH, 128), F32), pltpu.VMEM((CH, 128), F32),
                        pltpu.VMEM((CH, CH), F32), pltpu.VMEM((2 * DK, 128), F32)],
        compiler_params=_cparams(("arbitrary", "arbitrary")),
        name="retention",
    )(log_decay, proj, proj, proj)


def _diffattn_kernel(lam_ref, *refs, out_scale):
    q_refs = refs[:TQ_SUB]
    k_ref, v_ref, sw_ref, o_ref = refs[TQ_SUB:]
    lam = lam_ref[0]
    k = k_ref[0]
    v = v_ref[0]
    nt = (((1,), (1,)), ((), ()))

    def scores(qh):
        return lax.dot_general(qh, k, nt, preferred_element_type=F32)

    v_ones = jnp.concatenate([v, jnp.ones_like(v)], axis=1)

    def values(s):
        e = jnp.exp2(s - jnp.max(s, axis=-1, keepdims=True))
        ol = jnp.dot(e.astype(BF16), v_ones, preferred_element_type=F32)
        return ol[:, :HV], ol[:, HV:HV + 1]

    halves = []
    for i in range(TQ_SUB):
        q = q_refs[i][0]
        lane = lax.broadcasted_iota(I32, q.shape, 1)
        zero = jnp.zeros_like(q)
        halves += [jnp.where(lane < DK, q, zero), jnp.where(lane >= DK, q, zero)]
    outs = []
    ahead = [scores(h) for h in halves[:SCORE_AHEAD]]
    for c in range(len(halves)):
        if c + SCORE_AHEAD < len(halves):
            ahead.append(scores(halves[c + SCORE_AHEAD]))
        outs.append(values(ahead.pop(0)))
    for i in range(TQ_SUB):
        (o1, l1), (o2, l2) = outs[2 * i], outs[2 * i + 1]
        o = o1 / l1 - o2 * (lam / l2)
        o_ref[0, i * TM_PROJ:(i + 1) * TM_PROJ, :] = (_rms(o) * sw_ref[...] * out_scale).astype(BF16)


def _diffattn(proj, lam, subln_w, out_scale):
    tq = TQ_SUB * TM_PROJ
    nq = T // tq

    def q_map(i, b, h, j):
        return (b, LC // TM_PROJ + j * TQ_SUB + i, 3 * NH + h)

    return pl.pallas_call(
        functools.partial(_diffattn_kernel, out_scale=out_scale),
        grid=(B, NH, nq),
        in_specs=[
            pl.BlockSpec(memory_space=pltpu.SMEM),
            *[pl.BlockSpec((1, TM_PROJ, 128), functools.partial(q_map, i)) for i in range(TQ_SUB)],
            pl.BlockSpec((1, RB, 128), lambda b, h, j: (b, 0, 4 * NH + h)),
            pl.BlockSpec((1, RB, 128), lambda b, h, j: (b, 0, 5 * NH + h)),
            pl.BlockSpec((1, HV), lambda b, h, j: (0, 0)),
        ],
        out_specs=pl.BlockSpec((1, tq, 128), lambda b, h, j: (b, j, h)),
        out_shape=jax.ShapeDtypeStruct((B, T, NH * HV), BF16),
        compiler_params=_cparams(("arbitrary", "arbitrary", "arbitrary")),
        name="diffattn",
    )(lam, *([proj] * TQ_SUB), proj, proj, subln_w)


def _route(bz):
    grp = []
    for g in range(NGRP):
        m = bz[g * EPG:(g + 1) * EPG]
        best = None
        for i in range(EPG):
            for k in range(i + 1, EPG):
                pair = m[i] + m[k]
                best = pair if best is None else jnp.maximum(best, pair)
        grp.append(best)
    gbest = grp[0]
    gsel = jnp.zeros_like(gbest, dtype=I32)
    for g in range(1, NGRP):
        better = grp[g] > gbest
        gsel = jnp.where(better, g, gsel)
        gbest = jnp.where(better, grp[g], gbest)
    cb = [bz[i] for i in range(EPG)]
    for g in range(1, NGRP):
        pick = gsel == g
        cb = [jnp.where(pick, bz[g * EPG + i], cb[i]) for i in range(EPG)]
    i1 = jnp.zeros_like(gsel)
    b1 = cb[0]
    for i in range(1, EPG):
        better = cb[i] > b1
        i1 = jnp.where(better, i, i1)
        b1 = jnp.where(better, cb[i], b1)
    neg = jnp.full_like(b1, -jnp.inf)
    rest = [jnp.where(i1 == i, neg, cb[i]) for i in range(EPG)]
    i2 = jnp.zeros_like(gsel)
    b2 = rest[0]
    for i in range(1, EPG):
        better = rest[i] > b2
        i2 = jnp.where(better, i, i2)
        b2 = jnp.where(better, rest[i], b2)
    lo = jnp.minimum(i1, i2)
    hi = jnp.maximum(i1, i2)
    code = lo * EPG + hi
    pair = jnp.full_like(gsel, len(PAIR_A) - 1)
    for p in range(len(PAIR_A) - 1):
        a, b = min(PAIR_A[p], PAIR_B[p]), max(PAIR_A[p], PAIR_B[p])
        pair = jnp.where(code == a * EPG + b, p, pair)
    return gsel * len(PAIR_A) + pair


def _ffn_prologue(x1, row0, mod, nfw_ref, rw_ref, hp_ref, logit_ref):
    rows = x1.shape[0]
    sh2 = mod[:, 3 * D:4 * D]
    sc2 = mod[:, 4 * D:5 * D]
    h2 = (_rms(x1) * nfw_ref[...]) * (1.0 + sc2) + sh2
    _store_slabs(hp_ref, h2, row0)
    logit_ref[:, row0:row0 + rows] = _dot_3pass(rw_ref[...], h2, (((1,), (1,)), ((), ())))


def _ffn_out_specs(tm, n_tiles_per_b):
    specs = [
        pl.BlockSpec((1, tm, D), lambda b, j: (b, j, 0)),
        pl.BlockSpec((tm * SLAB, 128), lambda b, j: (b * n_tiles_per_b + j, 0)),
        pl.BlockSpec((NE, tm), lambda b, j: (0, b * n_tiles_per_b + j)),
    ]
    shapes = [
        jax.ShapeDtypeStruct((B, T, D), F32),
        jax.ShapeDtypeStruct((N * SLAB, 128), F32),
        jax.ShapeDtypeStruct((NE, N), F32),
    ]
    return specs, shapes


def _route_kernel(logit_ref, bias_ref, dest_ref, ends_ref):
    r = logit_ref.shape[1]
    cls = _route([_sigmoid(logit_ref[e]) + bias_ref[e] for e in range(NE)])
    lane_incl = (lax.broadcasted_iota(I32, (128, 128), 0) <= lax.broadcasted_iota(I32, (128, 128), 1)).astype(BF16)
    rows_before = (lax.broadcasted_iota(I32, (r, r), 1) < lax.broadcasted_iota(I32, (r, r), 0)).astype(BF16)
    dest = jnp.zeros((r, 128), F32)
    start = jnp.zeros((1, 128), F32)
    ends = []
    for c in range(NCLS):
        onehot = jnp.where(cls == c, 1.0, 0.0)
        in_row = jnp.dot(onehot.astype(BF16), lane_incl, preferred_element_type=F32)
        row_tot = jnp.broadcast_to(in_row[:, 127:128], (r, 128))
        above = jnp.dot(rows_before, row_tot.astype(BF16), preferred_element_type=F32)
        dest = dest + onehot * (start + above + in_row - 1.0)
        start = start + jnp.sum(row_tot, axis=0, keepdims=True)
        ends.append(start)
    dest_ref[...] = dest.astype(I32)
    ends_ref[...] = jnp.concatenate(ends + [jnp.zeros((CLS_PAD - NCLS, 128), F32)], axis=0)


def _route_tokens(logits_t, router_b):
    r = N // 128
    return pl.pallas_call(
        _route_kernel,
        in_specs=[pl.BlockSpec((NE, r, 128), lambda: (0, 0, 0)), pl.BlockSpec(memory_space=pltpu.SMEM)],
        out_specs=[pl.BlockSpec((r, 128), lambda: (0, 0)), pl.BlockSpec((CLS_PAD, 128), lambda: (0, 0))],
        out_shape=[jax.ShapeDtypeStruct((r, 128), I32), jax.ShapeDtypeStruct((CLS_PAD, 128), F32)],
        compiler_params=pltpu.CompilerParams(vmem_limit_bytes=VMEM_LIMIT),
        name="route",
    )(logits_t.reshape(NE, r, 128), router_b)


def _outproj_kernel(ret_ref, dif_ref, w_ref, x_ref, mod_ref, nfw_ref, rw_ref, x1_ref, hp_ref, logit_ref):
    mod = mod_ref[0]

    def mix(r0):
        rows = slice(r0, r0 + SUB_OUT)
        return (jnp.dot(ret_ref[0, rows, :], w_ref[0:NH * HV, :], preferred_element_type=F32)
                + jnp.dot(dif_ref[0, rows, :], w_ref[NH * HV:, :], preferred_element_type=F32))

    mx = mix(0)
    for r0 in range(0, TM_OUT, SUB_OUT):
        mx_next = mix(r0 + SUB_OUT) if r0 + SUB_OUT < TM_OUT else None
        x1 = x_ref[0, r0:r0 + SUB_OUT, :] + mod[:, 2 * D:3 * D] * mx
        x1_ref[0, r0:r0 + SUB_OUT, :] = x1
        _ffn_prologue(x1, r0, mod, nfw_ref, rw_ref, hp_ref, logit_ref)
        mx = mx_next


def _outproj(ret, dif, w_out, x, mod3, nfw, rw_t):
    nj = T // TM_OUT
    out_specs, out_shapes = _ffn_out_specs(TM_OUT, nj)
    return pl.pallas_call(
        _outproj_kernel,
        grid=(B, nj),
        in_specs=[
            pl.BlockSpec((1, TM_OUT, NH * HV), lambda b, j: (b, j, 0)),
            pl.BlockSpec((1, TM_OUT, NH * HV), lambda b, j: (b, j, 0)),
            pl.BlockSpec((2 * NH * HV, D), lambda b, j: (0, 0)),
            pl.BlockSpec((1, TM_OUT, D), lambda b, j: (b, j, 0)),
            pl.BlockSpec((1, 1, 6 * D), lambda b, j: (b, 0, 0)),
            pl.BlockSpec((1, D), lambda b, j: (0, 0)),
            pl.BlockSpec((NE, D), lambda b, j: (0, 0)),
        ],
        out_specs=out_specs,
        out_shape=out_shapes,
        compiler_params=_cparams(("arbitrary", "arbitrary")),
        name="outproj",
    )(ret, dif, w_out, x, mod3, nfw, rw_t)


def _pool_kernel(x_ref, prev_ref, next_ref, mod_ref, nmw_ref, pw_ref, ps_ref, nfw_ref, rw_ref,
                 x1_ref, hp_ref, logit_ref, ext_ref):
    i = pl.program_id(1)
    last = pl.num_programs(1) - 1
    mod = mod_ref[0]
    sh1 = mod[:, 0:D]
    sc1 = mod[:, D:2 * D]

    def modnorm(v):
        return (_rms(v) * nmw_ref[...]) * (1.0 + sc1) + sh1

    x = x_ref[0]
    hc = modnorm(x)
    ext_ref[0:HALO, :] = jnp.where(i > 0, modnorm(prev_ref[0]), 0.0)
    ext_ref[HALO:HALO + TM_POOL, :] = hc
    ext_ref[HALO + TM_POOL:, :] = jnp.where(i < last, modnorm(next_ref[0]), 0.0)
    pos = i * TM_POOL + lax.broadcasted_iota(I32, (TM_POOL, 1), 0)
    mixed = []
    for gi, w in enumerate(POOL_WINDOWS):
        left = w // 2
        right = w - 1 - left
        cols = slice(gi * PG, (gi + 1) * PG)
        tot = None
        for d in range(-left, right + 1):
            part = ext_ref[HALO + d:HALO + d + TM_POOL, cols]
            tot = part if tot is None else tot + part
        cnt = (jnp.minimum(pos + right + 1, T) - jnp.maximum(pos - left, 0)).astype(F32)
        pooled = (tot * (1.0 / cnt) - hc[:, cols]).astype(BF16)
        mixed.append(jnp.dot(pooled, pw_ref[gi], preferred_element_type=F32))
    mixed = jnp.concatenate(mixed, axis=1) * ps_ref[...]
    x1 = x + mod[:, 2 * D:3 * D] * mixed
    x1_ref[0] = x1
    _ffn_prologue(x1, 0, mod, nfw_ref, rw_ref, hp_ref, logit_ref)


def _pool_layer(x, mod3, nmw, pool_w, pool_scale, nfw, rw_t):
    ni = T // TM_POOL
    hb = TM_POOL // HALO
    out_specs, out_shapes = _ffn_out_specs(TM_POOL, ni)
    return pl.pallas_call(
        _pool_kernel,
        grid=(B, ni),
        in_specs=[
            pl.BlockSpec((1, TM_POOL, D), lambda b, i: (b, i, 0)),
            pl.BlockSpec((1, HALO, D), lambda b, i: (b, jnp.maximum(i * hb - 1, 0), 0)),
            pl.BlockSpec((1, HALO, D), lambda b, i: (b, jnp.minimum((i + 1) * hb, T // HALO - 1), 0)),
            pl.BlockSpec((1, 1, 6 * D), lambda b, i: (b, 0, 0)),
            pl.BlockSpec((1, D), lambda b, i: (0, 0)),
            pl.BlockSpec((len(POOL_WINDOWS), PG, PG), lambda b, i: (0, 0, 0)),
            pl.BlockSpec((1, D), lambda b, i: (0, 0)),
            pl.BlockSpec((1, D), lambda b, i: (0, 0)),
            pl.BlockSpec((NE, D), lambda b, i: (0, 0)),
        ],
        out_specs=out_specs,
        out_shape=out_shapes,
        scratch_shapes=[pltpu.VMEM((TM_POOL + 2 * HALO, D), F32)],
        compiler_params=_cparams(("arbitrary", "arbitrary")),
        name="pool_layer",
    )(x, x, x, mod3, nmw, pool_w, pool_scale, nfw, rw_t)


def _tile_copy(src_ref, dst_ref, sem, s, d, rows=1):
    s0 = pl.multiple_of(s * SLAB, SLAB)
    d0 = pl.multiple_of(d * SLAB, SLAB)
    return pltpu.make_async_copy(src_ref.at[pl.ds(s0, rows * SLAB)], dst_ref.at[pl.ds(d0, rows * SLAB)], sem)


def _issue_tile_copies(idx_ref, base, rows, start_one):
    def group(g, carry):
        r0 = g * PERM_UNROLL
        ids = [idx_ref[base + r0 + u] for u in range(PERM_UNROLL)]
        for u in range(PERM_UNROLL):
            start_one(r0 + u, ids[u], u % 2)
        return carry

    lax.fori_loop(0, rows // PERM_UNROLL, group, 0)


def _dispatch_kernel(dest_ref, src_ref, dst_ref, zero_ref, sem, pad_sem):
    base = pl.program_id(0) * TM_PERM

    @pl.when(pl.program_id(0) == 0)
    def _():
        zero_ref[...] = jnp.zeros_like(zero_ref)
        pad = _tile_copy(zero_ref, dst_ref, pad_sem, 0, N, TM_MOE)
        pad.start()
        pad.wait()

    def start_one(r, d, priority):
        _tile_copy(src_ref, dst_ref, sem, r, d).start(priority=priority)

    _issue_tile_copies(dest_ref, base, TM_PERM, start_one)
    _tile_copy(src_ref, dst_ref, sem, 0, 0, TM_PERM).wait()


def _dispatch(dest, src):
    return pl.pallas_call(
        _dispatch_kernel,
        grid_spec=pltpu.PrefetchScalarGridSpec(
            num_scalar_prefetch=1,
            grid=(N // TM_PERM,),
            in_specs=[pl.BlockSpec((TM_PERM * SLAB, 128), lambda i, dest: (i, 0))],
            out_specs=pl.BlockSpec(memory_space=pl.ANY),
            scratch_shapes=[pltpu.VMEM((TM_MOE * SLAB, 128), F32), pltpu.SemaphoreType.DMA(()),
                            pltpu.SemaphoreType.DMA(())],
        ),
        out_shape=jax.ShapeDtypeStruct(((N + TM_MOE) * SLAB, 128), src.dtype),
        compiler_params=_cparams(("arbitrary",)),
        name="dispatch",
    )(dest, src)


def _combine_kernel(dest_ref, x_ref, ys_ref, mod_ref, fw_ref, o_ref, ybuf_ref, sem, *, final):
    i = pl.program_id(0)
    n = pl.num_programs(0)
    slot = i % 2

    def gather(step, to_slot):
        def start_one(r, d, priority):
            _tile_copy(ys_ref, ybuf_ref.at[to_slot], sem.at[to_slot], d, r).start(priority=priority)

        _issue_tile_copies(dest_ref, step * TM_COMB, TM_COMB, start_one)

    @pl.when(i == 0)
    def _():
        gather(0, 0)

    @pl.when(i + 1 < n)
    def _():
        gather(i + 1, 1 - slot)

    _tile_copy(ys_ref, ybuf_ref.at[slot], sem.at[slot], 0, 0, TM_COMB).wait()
    out = x_ref[...] + mod_ref[0][:, 5 * D:6 * D] * _load_slabs(ybuf_ref.at[slot], TM_COMB)
    if final:
        out = _rms(out) * fw_ref[...]
    o_ref[...] = out


def _combine(dest, x1, ys, mod3, final_w, final):
    per_b = T // TM_COMB
    return pl.pallas_call(
        functools.partial(_combine_kernel, final=final),
        grid_spec=pltpu.PrefetchScalarGridSpec(
            num_scalar_prefetch=1,
            grid=(N // TM_COMB,),
            in_specs=[
                pl.BlockSpec((TM_COMB, D), lambda i, dest: (i, 0)),
                pl.BlockSpec(memory_space=pl.ANY),
                pl.BlockSpec((1, 1, 6 * D), lambda i, dest: (i // per_b, 0, 0)),
                pl.BlockSpec((1, D), lambda i, dest: (0, 0)),
            ],
            out_specs=pl.BlockSpec((TM_COMB, D), lambda i, dest: (i, 0)),
            scratch_shapes=[pltpu.VMEM((2, TM_COMB * SLAB, 128), F32), pltpu.SemaphoreType.DMA((2,))],
        ),
        out_shape=jax.ShapeDtypeStruct((N, D), F32),
        compiler_params=_cparams(("arbitrary",)),
        name="combine",
    )(dest, x1.reshape(N, D), ys, mod3, final_w)


def _moe_kernel(start_ref, ea_ref, eb_ref, ca_ref, cb_ref, na_ref, nb_ref, n_ref,
                rw_ref, xs_hbm, wg_hbm, wu_hbm, wd_hbm, ys_hbm,
                xbuf_ref, obuf_ref,
                sga_ref, sua_ref, sda_ref, sgb_ref, sub_ref, sdb_ref,
                ga_ref, ua_ref, da_ref, gb_ref, ub_ref, db_ref, wsem, xsem, osem, *, layer):
    n = n_ref[0]
    slots = ((ea_ref, ca_ref, na_ref, (sga_ref, sua_ref, sda_ref), (ga_ref, ua_ref, da_ref)),
             (eb_ref, cb_ref, nb_ref, (sgb_ref, sub_ref, sdb_ref), (gb_ref, ub_ref, db_ref)))

    def fetch(w_slot, expert):
        stage = slots[w_slot][3]
        return [pltpu.make_async_copy(w.at[layer, expert], s, wsem.at[w_slot])
                for w, s in zip((wg_hbm, wu_hbm, wd_hbm), stage)]

    def x_copy(step, buf):
        return _tile_copy(xs_hbm, xbuf_ref.at[buf], xsem.at[buf], start_ref[step], 0, TM_MOE)

    def o_copy(step, buf):
        return _tile_copy(obuf_ref.at[buf], ys_hbm, osem.at[buf], 0, start_ref[step], TM_MOE)

    @pl.when(pl.program_id(0) == 0)
    def _():
        obuf_ref[1] = jnp.zeros_like(obuf_ref[1])
        pad = _tile_copy(obuf_ref.at[1], ys_hbm, osem.at[1], 0, N, TM_MOE)
        pad.start()
        pad.wait()
        x_copy(0, 0).start()

    def chunk(m, slot):
        @pl.when(m + 1 < n)
        def _():
            x_copy(m + 1, 1 - slot).start()

        for w_slot, (e_ref, c_ref, nx_ref, stage, work) in enumerate(slots):
            @pl.when(m == 0)
            def _():
                for cp in fetch(w_slot, e_ref[0]):
                    cp.start(priority=1)

            @pl.when(c_ref[m] == 1)
            def _():
                for cp in fetch(w_slot, 0):
                    cp.wait()
                for s, w in zip(stage, work):
                    w[...] = s[...].astype(BF16)

                @pl.when(nx_ref[m] >= 0)
                def _():
                    for cp in fetch(w_slot, nx_ref[m]):
                        cp.start(priority=1)

        x_copy(m, slot).wait()
        hf = _load_slabs(xbuf_ref.at[slot], TM_MOE)
        h = hf.astype(BF16)

        def up(w_ref):
            return jnp.dot(h, w_ref[...], preferred_element_type=F32)

        def down(g, u, d_ref):
            return jnp.dot((_silu(g) * u).astype(BF16), d_ref[...], preferred_element_type=F32)

        g_a, u_a, g_b, u_b = up(ga_ref), up(ua_ref), up(gb_ref), up(ub_ref)
        y_a = down(g_a, u_a, da_ref)
        y_b = down(g_b, u_b, db_ref)
        s_a = _sigmoid(jnp.sum(hf * rw_ref[pl.ds(ea_ref[m], 1), :], axis=1, keepdims=True))
        s_b = _sigmoid(jnp.sum(hf * rw_ref[pl.ds(eb_ref[m], 1), :], axis=1, keepdims=True))
        denom = s_a + s_b
        val = (s_a / denom) * y_a + (s_b / denom) * y_b

        @pl.when(m >= 1)
        def _():
            o_copy(m - 1, 1 - slot).wait()

        _store_slabs(obuf_ref.at[slot], val)
        o_copy(m, slot).start()

        @pl.when(m == n - 1)
        def _():
            o_copy(m, slot).wait()

    for j in range(2):
        m = pl.program_id(0) * 2 + j
        pl.when(m < n)(functools.partial(chunk, m, j))


def _moe_sorted(chunks, xs, rw_t, wg, wu, wd, layer):
    mats = lambda dt: [pltpu.VMEM((D, DE), dt), pltpu.VMEM((D, DE), dt), pltpu.VMEM((DE, D), dt)]
    hbm = pl.BlockSpec(memory_space=pl.ANY)
    return pl.pallas_call(
        functools.partial(_moe_kernel, layer=layer),
        grid_spec=pltpu.PrefetchScalarGridSpec(
            num_scalar_prefetch=len(chunks),
            grid=(N_UNITS // 2,),
            in_specs=[pl.BlockSpec((NE, D), lambda m, *_: (0, 0)), hbm, hbm, hbm, hbm],
            out_specs=hbm,
            scratch_shapes=([pltpu.VMEM((2, TM_MOE * SLAB, 128), F32), pltpu.VMEM((2, TM_MOE * SLAB, 128), F32)]
                            + mats(F32) + mats(F32) + mats(BF16) + mats(BF16)
                            + [pltpu.SemaphoreType.DMA((2,))] * 3),
        ),
        out_shape=jax.ShapeDtypeStruct(xs.shape, F32),
        compiler_params=_cparams(("arbitrary",)),
        name="moe_sorted",
    )(*chunks, rw_t, xs, wg, wu, wd)


def _moe_chunks(ends_f):
    ends = ends_f[:NCLS, 0].astype(I32)
    offs = jnp.concatenate([jnp.zeros((1,), I32), ends[:-1]])
    per_cls = (ends - offs + (TM_MOE - 1)) // TM_MOE
    cum = jnp.cumsum(per_cls)
    count = cum[-1:]
    idx = jnp.arange(N_UNITS, dtype=I32)
    m = jnp.minimum(idx, count - 1)
    cls = jnp.minimum(jnp.sum(cum[None, :] <= m[:, None], axis=1), NCLS - 1).astype(I32)
    of_cls = cls[:, None] == jnp.arange(NCLS, dtype=I32)[None, :]

    def pick(per_class):
        return jnp.sum(jnp.where(of_cls, per_class[None, :], 0), axis=1)

    start = pick(offs) + TM_MOE * (m - (pick(cum) - pick(per_cls)))
    grp = cls // len(PAIR_A)
    pair = cls % len(PAIR_A)
    e_a = grp * EPG + jnp.take(jnp.array(PAIR_A, I32), pair)
    e_b = grp * EPG + jnp.take(jnp.array(PAIR_B, I32), pair)
    one = jnp.ones((1,), I32)

    def changes(e):
        chg = jnp.concatenate([one, (e[1:] != e[:-1]).astype(I32)])
        at = jnp.where(chg == 1, idx, N_UNITS)
        nxt_at = jnp.concatenate([lax.cummin(at, reverse=True)[1:], jnp.full((1,), N_UNITS, I32)])
        nxt = jnp.where(nxt_at < N_UNITS, jnp.take(e, jnp.minimum(nxt_at, N_UNITS - 1)), -1)
        return chg, nxt

    chg_a, nxt_a = changes(e_a)
    chg_b, nxt_b = changes(e_b)
    return start, e_a, e_b, chg_a, chg_b, nxt_a, nxt_b, count


def _moe_layer(x1, hp, logits, rw_t, router_b, wg, wu, wd, layer, mod3, final_w, final):
    dest, ends = _route_tokens(logits, router_b)
    dest = dest.reshape(N)
    xs = _dispatch(dest, hp)
    ys = _moe_sorted(_moe_chunks(ends), xs, rw_t, wg, wu, wd, layer)
    return _combine(dest, x1, ys, mod3, final_w, final).reshape(B, T, D)


def _rope_tables():
    half = 16
    inv = ROPE_BASE ** (-np.arange(half, dtype=np.float64) / half)
    t = np.arange(T)
    ang_r = (t // GRID_W)[:, None] * inv[None, :]
    ang_c = (t % GRID_W)[:, None] * inv[None, :]
    ang = np.concatenate([ang_r, ang_r, ang_c, ang_c], axis=1)
    sign = np.tile(np.concatenate([-np.ones(half), np.ones(half)]), 2)
    cos = np.concatenate([np.ones((LC, 64)), np.cos(ang)], axis=0)
    sin = np.concatenate([np.zeros((LC, 64)), np.sin(ang) * sign[None, :]], axis=0)
    return (jnp.asarray(np.tile(cos, (1, 2)), dtype=F32), jnp.asarray(np.tile(sin, (1, 2)), dtype=F32))


def _permute_w_in(w):
    rq = w[:, 0:256].reshape(D, NH, DK)
    dq = w[:, 256:768]
    rg = w[:, 768:1280]
    rk = w[:, 1280:1536].reshape(D, NH, DK)
    rv = w[:, 1536:2048]
    dk = w[:, 2048:2560]
    dv = w[:, 2560:3072]
    qk = jnp.concatenate([rq, rk * (DK ** -0.5)], axis=2).reshape(D, NH * 2 * DK)
    return jnp.concatenate([qk, rv, rg, dq * (DK ** -0.5 * math.log2(math.e)), dk, dv], axis=1).astype(BF16)


def kernel(x, c, ctx, c_ctx, ada_w, ada_b, norm_mix_w, norm_ffn_w, w_in, w_out, ret_log_decay, diff_lambda,
           diff_subln_w, pool_w, pool_scale, router_w, router_b, moe_w_gate, moe_w_up, moe_w_down, final_norm_w):
    assert x.shape == (B, T, D) and ctx.shape == (B, LC, D) and ada_w.shape[0] == 2
    cc = jnp.concatenate([c, c_ctx[None, :], jnp.zeros((16 - B - 1, D), F32)], axis=0)
    mod = _ada_mod(cc, ada_w, ada_b)
    rw_t = router_w.T
    fw = final_norm_w.reshape(1, D)
    experts = (moe_w_gate, moe_w_up, moe_w_down)

    mod0 = mod[0].reshape(16, 1, 6 * D)
    cos_t, sin_t = _rope_tables()
    proj = _inproj(x, ctx, mod0, norm_mix_w[0:1], _permute_w_in(w_in[0]), cos_t, sin_t)
    ret = _retention(proj, ret_log_decay[0])
    lam_init = 0.8 - 0.6 * math.exp(-0.3 * 0)
    lv = diff_lambda[0]
    lam = jnp.exp(jnp.sum(lv[0] * lv[1])) - jnp.exp(jnp.sum(lv[2] * lv[3])) + lam_init
    dif = _diffattn(proj, lam.reshape(1), diff_subln_w[0:1], 1.0 - lam_init)
    x1, hp, logits = _outproj(ret, dif, w_out[0].astype(BF16), x, mod0, norm_ffn_w[0:1], rw_t)
    x2 = _moe_layer(x1, hp, logits, rw_t, router_b, *experts, 0, mod0, fw, False)

    mod1 = mod[1].reshape(16, 1, 6 * D)
    x3, hp, logits = _pool_layer(x2, mod1, norm_mix_w[1:2], pool_w[0].astype(BF16), pool_scale[0:1],
                                 norm_ffn_w[1:2], rw_t)
    return _moe_layer(x3, hp, logits, rw_t, router_b, *experts, 1, mod1, fw, True)
```

```python
import functools
import math

import jax
import jax.numpy as jnp
import numpy as np
from jax import lax
from jax.experimental import pallas as pl
from jax.experimental.pallas import tpu as pltpu

F32 = jnp.float32
BF16 = jnp.bfloat16
I32 = jnp.int32

D = 1024
B = 8
T = 2048
N = B * T
GRID_W = 64
LC = 256
EPS = 1e-6
ROPE_BASE = 10000.0
NH = 4
DK = 64
HV = 128
CH = 256
RB = LC + T
NCH = RB // CH
POOL_WINDOWS = (2, 4, 8, 16)
PG = D // len(POOL_WINDOWS)
NE = 16
NGRP = 4
EPG = NE // NGRP
DE = 512
IN_W = 3072
HALO = 8

PAIR_A = (0, 0, 0, 1, 1, 3)
PAIR_B = (1, 2, 3, 3, 2, 2)
NCLS = NGRP * len(PAIR_A)
SLAB = D // 128

TM_PROJ = 256
PROJ_SUB = 3
TM_OUT = 1024
SUB_OUT = 512
TQ_SUB = 8
SCORE_AHEAD = 1
TM_POOL = 512
TM_MOE = 256
N_UNITS = N // TM_MOE + NCLS
TM_PERM = 2048
TM_COMB = 512
PERM_UNROLL = 16
VMEM_LIMIT = 56 * 1024 * 1024


def _cparams(sem):
    return pltpu.CompilerParams(dimension_semantics=sem, vmem_limit_bytes=VMEM_LIMIT)


def _sigmoid(x):
    return 1.0 / (1.0 + jnp.exp(-x))


def _silu(x):
    return x * _sigmoid(x)


def _rms(x):
    return x * lax.rsqrt(jnp.mean(x * x, axis=-1, keepdims=True) + EPS)


def _dot_3pass(a, b, dims):
    a_hi = a.astype(BF16)
    b_hi = b.astype(BF16)
    a_lo = (a - a_hi.astype(F32)).astype(BF16)
    b_lo = (b - b_hi.astype(F32)).astype(BF16)

    def dot(x, y):
        return lax.dot_general(x, y, dims, preferred_element_type=F32)

    return dot(a_hi, b_hi) + (dot(a_lo, b_hi) + dot(a_hi, b_lo))


def _load_slabs(ref, rows):
    return jnp.concatenate([ref[pl.ds(s, rows, stride=SLAB), :] for s in range(SLAB)], axis=1)


def _store_slabs(ref, val, row0=0):
    rows = val.shape[0]
    for s in range(SLAB):
        ref[pl.ds(row0 * SLAB + s, rows, stride=SLAB), :] = val[:, s * 128:(s + 1) * 128]


def _ada_kernel(cc_ref, w_ref, b_ref, o_ref):
    s = _silu(cc_ref[...])
    o_ref[0] = _dot_3pass(s, w_ref[0], (((1,), (0,)), ((), ()))) + b_ref[0]


def _ada_mod(cc, ada_w, ada_b):
    depth = ada_w.shape[0]
    tn = 1536
    return pl.pallas_call(
        _ada_kernel,
        grid=(depth, 6 * D // tn),
        in_specs=[
            pl.BlockSpec((16, D), lambda l, n: (0, 0)),
            pl.BlockSpec((1, D, tn), lambda l, n: (l, 0, n)),
            pl.BlockSpec((1, 1, tn), lambda l, n: (l, 0, n)),
        ],
        out_specs=pl.BlockSpec((1, 16, tn), lambda l, n: (l, 0, n)),
        out_shape=jax.ShapeDtypeStruct((depth, 16, 6 * D), F32),
        compiler_params=_cparams(("arbitrary", "arbitrary")),
        name="ada_mod",
    )(cc, ada_w, ada_b.reshape(depth, 1, 6 * D))


def _rope(seg, cos, sin_signed, lo_mask):
    w = seg.shape[1]
    from_hi = pltpu.roll(seg, w - 16, axis=1)
    from_lo = pltpu.roll(seg, 16, axis=1)
    partner = jnp.where(lo_mask, from_hi, from_lo)
    reps = w // cos.shape[1]
    c = jnp.concatenate([cos] * reps, axis=1)
    s = jnp.concatenate([sin_signed] * reps, axis=1)
    return seg * c + partner * s


def _inproj_kernel(*refs):
    x_refs = refs[:PROJ_SUB]
    c_ref, mod_ref, cmod_ref, nw_ref, w_ref, cos_ref, sin_ref, o_ref = refs[PROJ_SUB:]
    is_ctx = pl.program_id(1) == 0
    parts = []
    for s in range(PROJ_SUB):
        xt = x_refs[s][0]
        sh = mod_ref[0, :, 0:D]
        sc = mod_ref[0, :, D:2 * D]
        if s == 0:
            xt = jnp.where(is_ctx, c_ref[0], xt)
            sh = jnp.where(is_ctx, cmod_ref[0, :, 0:D], sh)
            sc = jnp.where(is_ctx, cmod_ref[0, :, D:2 * D], sc)
        parts.append(((_rms(xt) * nw_ref[...]) * (1.0 + sc) + sh).astype(BF16))
    hb = jnp.concatenate(parts, axis=0)
    lane = lax.broadcasted_iota(I32, (PROJ_SUB * TM_PROJ, 512), 1)
    lo_mask = (lane % 32) < 16
    cos = cos_ref[...]
    sin = sin_ref[...]

    def project(g):
        return jnp.dot(hb, w_ref[:, g * 512:(g + 1) * 512], preferred_element_type=F32)

    seg = project(0)
    for g in range(6):
        seg_next = project(g + 1) if g + 1 < 6 else None
        if g in (0, 3, 4):
            seg = _rope(seg, cos, sin, lo_mask)
        o_ref[0, :, g * 512:(g + 1) * 512] = seg.astype(BF16)
        seg = seg_next


def _inproj(x, ctx, mod3, norm_w, w_perm, cos_t, sin_t):
    tm = PROJ_SUB * TM_PROJ
    nj = RB // tm

    def x_map(s, b, j):
        return (b, jnp.maximum(PROJ_SUB * j + s - LC // TM_PROJ, 0), 0)

    return pl.pallas_call(
        _inproj_kernel,
        grid=(B, nj),
        in_specs=[
            *[pl.BlockSpec((1, TM_PROJ, D), functools.partial(x_map, s)) for s in range(PROJ_SUB)],
            pl.BlockSpec((1, LC, D), lambda b, j: (b, 0, 0)),
            pl.BlockSpec((1, 1, 2 * D), lambda b, j: (b, 0, 0)),
            pl.BlockSpec((1, 1, 2 * D), lambda b, j: (B, 0, 0)),
            pl.BlockSpec((1, D), lambda b, j: (0, 0)),
            pl.BlockSpec((D, IN_W), lambda b, j: (0, 0)),
            pl.BlockSpec((tm, 128), lambda b, j: (j, 0)),
            pl.BlockSpec((tm, 128), lambda b, j: (j, 0)),
        ],
        out_specs=pl.BlockSpec((1, tm, IN_W), lambda b, j: (b, j, 0)),
        out_shape=jax.ShapeDtypeStruct((B, RB, IN_W), BF16),
        compiler_params=_cparams(("arbitrary", "arbitrary")),
        name="inproj",
    )(*([x] * PROJ_SUB), ctx, mod3, mod3, norm_w, w_perm, cos_t, sin_t)


def _retention_kernel(ld_ref, qk_ref, v_ref, g_ref, o_ref, st_ref, kdec_ref, qdec_ref, mask_ref, cdec_ref):
    h = pl.program_id(0)
    lane = lax.broadcasted_iota(I32, (CH, 128), 1)
    fwd_lane = lane < DK

    @pl.when(pl.program_id(1) == 0)
    def _():
        lgf = ld_ref[0, h]
        lgb = ld_ref[1, h]
        pos = lax.broadcasted_iota(I32, (CH, 128), 0).astype(F32)
        kdec_ref[...] = jnp.where(fwd_lane, jnp.exp(lgf * (CH - 1 - pos)), jnp.exp(lgb * pos))
        qdec_ref[...] = jnp.where(fwd_lane, jnp.exp(lgf * (pos + 1.0)), jnp.exp(lgb * (CH - pos)))
        ii = lax.broadcasted_iota(I32, (CH, CH), 0)
        jj = lax.broadcasted_iota(I32, (CH, CH), 1)
        gap = (ii - jj).astype(F32)
        mask_ref[...] = (jnp.where(gap >= 0, jnp.exp(lgf * jnp.maximum(gap, 0.0)), 0.0)
                         + jnp.where(gap <= 0, jnp.exp(lgb * jnp.maximum(-gap, 0.0)), 0.0))
        ones = jnp.ones((DK, 128), F32)
        cdec_ref[0:DK, :] = jnp.exp(lgf * CH * ones)
        cdec_ref[DK:, :] = jnp.exp(lgb * CH * ones)

    kdec = kdec_ref[...]
    qdec = qdec_ref[...]
    mask = mask_ref[...]
    cf = cdec_ref[0:DK, :]
    cb = cdec_ref[DK:, :]

    def chunk(n):
        a = qk_ref[0, n * CH:(n + 1) * CH, :].astype(F32)
        swapped = pltpu.roll(a, DK, axis=1)
        return a, swapped

    kv = []
    for n in range(NCH):
        a, swapped = chunk(n)
        kk = jnp.where(fwd_lane, swapped, a)
        kb = (kk * kdec).astype(BF16)
        vn = v_ref[0, n * CH:(n + 1) * CH, :]
        kv.append(lax.dot_general(kb, vn, (((0,), (0,)), ((), ())), preferred_element_type=F32))
    sf = kv[0][:DK]
    for n in range(1, NCH):
        st_ref[n, 0:DK, :] = sf
        sf = cf * sf + kv[n][:DK]
    sb = kv[0][DK:]
    for n in range(NCH - 1, 0, -1):
        st_ref[n, DK:2 * DK, :] = sb
        sb = cb * sb + kv[n][DK:]

    for n in range(1, NCH):
        a, swapped = chunk(n)
        q = a[:, :DK].astype(BF16)
        k = swapped[:, :DK].astype(BF16)
        scores = lax.dot_general(q, k, (((1,), (1,)), ((), ())), preferred_element_type=F32)
        p = (scores * mask).astype(BF16)
        vn = v_ref[0, n * CH:(n + 1) * CH, :]
        qq = jnp.where(fwd_lane, a, swapped)
        qd = (qq * qdec).astype(BF16)
        o = (jnp.dot(p, vn, preferred_element_type=F32)
             + jnp.dot(qd, st_ref[n].astype(BF16), preferred_element_type=F32))
        gate = g_ref[0, n * CH:(n + 1) * CH, :].astype(F32)
        o_ref[0, (n - 1) * CH:n * CH, :] = (_rms(o) * _silu(gate)).astype(BF16)


def _retention(proj, log_decay):
    return pl.pallas_call(
        _retention_kernel,
        grid=(NH, B),
        in_specs=[
            pl.BlockSpec(memory_space=pltpu.SMEM),
            pl.BlockSpec((1, RB, 128), lambda h, b: (b, 0, h)),
            pl.BlockSpec((1, RB, 128), lambda h, b: (b, 0, NH + h)),
            pl.BlockSpec((1, RB, 128), lambda h, b: (b, 0, 2 * NH + h)),
        ],
        out_specs=pl.BlockSpec((1, T, 128), lambda h, b: (b, 0, h)),
        out_shape=jax.ShapeDtypeStruct((B, T, NH * HV), BF16),
        scratch_shapes=[pltpu.VMEM((NCH, 128, 128), F32), pltpu.VMEM((CH, 128), F32), pltpu.VMEM((CH, 128), F32),
                        pltpu.VMEM((CH, CH), F32), pltpu.VMEM((2 * DK, 128), F32)],
        compiler_params=_cparams(("arbitrary", "arbitrary")),
        name="retention",
    )(log_decay, proj, proj, proj)


def _diffattn_kernel(lam_ref, *refs, out_scale):
    q_refs = refs[:TQ_SUB]
    k_ref, v_ref, sw_ref, o_ref = refs[TQ_SUB:]
    lam = lam_ref[0]
    k = k_ref[0]
    v = v_ref[0]
    nt = (((1,), (1,)), ((), ()))

    def scores(qh):
        return lax.dot_general(qh, k, nt, preferred_element_type=F32)

    v_ones = jnp.concatenate([v, jnp.ones_like(v)], axis=1)

    def values(s):
        e = jnp.exp2(s - jnp.max(s, axis=-1, keepdims=True))
        ol = jnp.dot(e.astype(BF16), v_ones, preferred_element_type=F32)
        return ol[:, :HV], ol[:, HV:HV + 1]

    halves = []
    for i in range(TQ_SUB):
        q = q_refs[i][0]
        lane = lax.broadcasted_iota(I32, q.shape, 1)
        zero = jnp.zeros_like(q)
        halves += [jnp.where(lane < DK, q, zero), jnp.where(lane >= DK, q, zero)]
    outs = []
    ahead = [scores(h) for h in halves[:SCORE_AHEAD]]
    for c in range(len(halves)):
        if c + SCORE_AHEAD < len(halves):
            ahead.append(scores(halves[c + SCORE_AHEAD]))
        outs.append(values(ahead.pop(0)))
    for i in range(TQ_SUB):
        (o1, l1), (o2, l2) = outs[2 * i], outs[2 * i + 1]
        o = o1 / l1 - o2 * (lam / l2)
        o_ref[0, i * TM_PROJ:(i + 1) * TM_PROJ, :] = (_rms(o) * sw_ref[...] * out_scale).astype(BF16)


def _diffattn(proj, lam, subln_w, out_scale):
    tq = TQ_SUB * TM_PROJ
    nq = T // tq

    def q_map(i, b, h, j):
        return (b, LC // TM_PROJ + j * TQ_SUB + i, 3 * NH + h)

    return pl.pallas_call(
        functools.partial(_diffattn_kernel, out_scale=out_scale),
        grid=(B, NH, nq),
        in_specs=[
            pl.BlockSpec(memory_space=pltpu.SMEM),
            *[pl.BlockSpec((1, TM_PROJ, 128), functools.partial(q_map, i)) for i in range(TQ_SUB)],
            pl.BlockSpec((1, RB, 128), lambda b, h, j: (b, 0, 4 * NH + h)),
            pl.BlockSpec((1, RB, 128), lambda b, h, j: (b, 0, 5 * NH + h)),
            pl.BlockSpec((1, HV), lambda b, h, j: (0, 0)),
        ],
        out_specs=pl.BlockSpec((1, tq, 128), lambda b, h, j: (b, j, h)),
        out_shape=jax.ShapeDtypeStruct((B, T, NH * HV), BF16),
        compiler_params=_cparams(("arbitrary", "arbitrary", "arbitrary")),
        name="diffattn",
    )(lam, *([proj] * TQ_SUB), proj, proj, subln_w)


def _route(bz):
    grp = []
    for g in range(NGRP):
        m = bz[g * EPG:(g + 1) * EPG]
        best = None
        for i in range(EPG):
            for k in range(i + 1, EPG):
                pair = m[i] + m[k]
                best = pair if best is None else jnp.maximum(best, pair)
        grp.append(best)
    gbest = grp[0]
    gsel = jnp.zeros_like(gbest, dtype=I32)
    for g in range(1, NGRP):
        better = grp[g] > gbest
        gsel = jnp.where(better, g, gsel)
        gbest = jnp.where(better, grp[g], gbest)
    cb = [bz[i] for i in range(EPG)]
    for g in range(1, NGRP):
        pick = gsel == g
        cb = [jnp.where(pick, bz[g * EPG + i], cb[i]) for i in range(EPG)]
    i1 = jnp.zeros_like(gsel)
    b1 = cb[0]
    for i in range(1, EPG):
        better = cb[i] > b1
        i1 = jnp.where(better, i, i1)
        b1 = jnp.where(better, cb[i], b1)
    neg = jnp.full_like(b1, -jnp.inf)
    rest = [jnp.where(i1 == i, neg, cb[i]) for i in range(EPG)]
    i2 = jnp.zeros_like(gsel)
    b2 = rest[0]
    for i in range(1, EPG):
        better = rest[i] > b2
        i2 = jnp.where(better, i, i2)
        b2 = jnp.where(better, rest[i], b2)
    lo = jnp.minimum(i1, i2)
    hi = jnp.maximum(i1, i2)
    code = lo * EPG + hi
    pair = jnp.full_like(gsel, len(PAIR_A) - 1)
    for p in range(len(PAIR_A) - 1):
        a, b = min(PAIR_A[p], PAIR_B[p]), max(PAIR_A[p], PAIR_B[p])
        pair = jnp.where(code == a * EPG + b, p, pair)
    return gsel * len(PAIR_A) + pair


def _ffn_prologue(x1, row0, mod, nfw_ref, rw_ref, hp_ref, logit_ref):
    rows = x1.shape[0]
    sh2 = mod[:, 3 * D:4 * D]
    sc2 = mod[:, 4 * D:5 * D]
    h2 = (_rms(x1) * nfw_ref[...]) * (1.0 + sc2) + sh2
    _store_slabs(hp_ref, h2, row0)
    logit_ref[:, row0:row0 + rows] = _dot_3pass(rw_ref[...], h2, (((1,), (1,)), ((), ())))


def _ffn_out_specs(tm, n_tiles_per_b):
    specs = [
        pl.BlockSpec((1, tm, D), lambda b, j: (b, j, 0)),
        pl.BlockSpec((tm * SLAB, 128), lambda b, j: (b * n_tiles_per_b + j, 0)),
        pl.BlockSpec((NE, tm), lambda b, j: (0, b * n_tiles_per_b + j)),
    ]
    shapes = [
        jax.ShapeDtypeStruct((B, T, D), F32),
        jax.ShapeDtypeStruct((N * SLAB, 128), F32),
        jax.ShapeDtypeStruct((NE, N), F32),
    ]
    return specs, shapes


def _route_kernel(logit_ref, bias_ref, dest_ref, chunk_ref):
    r = logit_ref.shape[1]
    cls = _route([_sigmoid(logit_ref[e]) + bias_ref[e] for e in range(NE)])
    lane_incl = (lax.broadcasted_iota(I32, (128, 128), 0) <= lax.broadcasted_iota(I32, (128, 128), 1)).astype(BF16)
    rows_before = (lax.broadcasted_iota(I32, (r, r), 1) < lax.broadcasted_iota(I32, (r, r), 0)).astype(BF16)
    dest = jnp.zeros((r, 128), F32)
    start = jnp.zeros((1, 128), F32)
    ends = []
    for c in range(NCLS):
        onehot = jnp.where(cls == c, 1.0, 0.0)
        in_row = jnp.dot(onehot.astype(BF16), lane_incl, preferred_element_type=F32)
        row_tot = jnp.broadcast_to(in_row[:, 127:128], (r, 128))
        above = jnp.dot(rows_before, row_tot.astype(BF16), preferred_element_type=F32)
        dest = dest + onehot * (start + above + in_row - 1.0)
        start = start + jnp.sum(row_tot, axis=0, keepdims=True)
        ends.append(start)
    dest_ref[...] = dest.astype(I32)
    chunk_ref[...] = _chunk_list(ends)


def _chunk_list(ends):
    lane = lax.broadcasted_iota(I32, (1, 128), 1).astype(F32)
    offs = [jnp.zeros((1, 128), F32)] + ends[:-1]
    per = [jnp.floor((e - o + (TM_MOE - 1)) * (1.0 / TM_MOE)) for e, o in zip(ends, offs)]
    cum = []
    total = jnp.zeros((1, 128), F32)
    for p in per:
        total = total + p
        cum.append(total)
    m = jnp.minimum(lane, total - 1.0)
    cls = jnp.zeros((1, 128), F32)
    for c in range(NCLS):
        cls = cls + jnp.where(cum[c] <= m, 1.0, 0.0)
    start = jnp.zeros((1, 128), F32)
    for c in range(NCLS):
        start = jnp.where(cls == c, offs[c] + TM_MOE * (m - (cum[c] - per[c])), start)
    n_pair = len(PAIR_A)
    grp = jnp.zeros((1, 128), F32)
    for g in range(1, NGRP):
        grp = grp + jnp.where(cls >= g * n_pair, 1.0, 0.0)
    pair = cls - n_pair * grp
    loc_a = jnp.zeros((1, 128), F32)
    loc_b = jnp.zeros((1, 128), F32)
    for p in range(n_pair):
        loc_a = jnp.where(pair == p, float(PAIR_A[p]), loc_a)
        loc_b = jnp.where(pair == p, float(PAIR_B[p]), loc_b)
    sub = lax.broadcasted_iota(I32, (128, 128), 0).astype(F32)
    lan = lax.broadcasted_iota(I32, (128, 128), 1).astype(F32)
    none = 1000.0

    def changes(e):
        chg = jnp.where((lane == 0.0) | (e != pltpu.roll(e, 1, axis=1)), 1.0, 0.0)
        chg_col = jnp.sum(jnp.where(sub == lan, chg, 0.0), axis=1, keepdims=True)
        e_col = jnp.sum(jnp.where(sub == lan, e, 0.0), axis=1, keepdims=True)
        nxt_at = jnp.min(jnp.where((sub > lan) & (chg_col == 1.0), sub, none), axis=0, keepdims=True)
        nxt_e = jnp.sum(jnp.where(sub == nxt_at, e_col, 0.0), axis=0, keepdims=True)
        return chg, jnp.where(nxt_at < none, nxt_e, -1.0)

    e_a = EPG * grp + loc_a
    e_b = EPG * grp + loc_b
    chg_a, nxt_a = changes(e_a)
    chg_b, nxt_b = changes(e_b)
    return jnp.concatenate([start, e_a, e_b, chg_a, chg_b, nxt_a, nxt_b, total], axis=0).astype(I32)


def _route_tokens(logits_t, router_b):
    r = N // 128
    return pl.pallas_call(
        _route_kernel,
        in_specs=[pl.BlockSpec((NE, r, 128), lambda: (0, 0, 0)), pl.BlockSpec(memory_space=pltpu.SMEM)],
        out_specs=[pl.BlockSpec((r, 128), lambda: (0, 0)), pl.BlockSpec((8, 128), lambda: (0, 0))],
        out_shape=[jax.ShapeDtypeStruct((r, 128), I32), jax.ShapeDtypeStruct((8, 128), I32)],
        compiler_params=pltpu.CompilerParams(vmem_limit_bytes=VMEM_LIMIT),
        name="route",
    )(logits_t.reshape(NE, r, 128), router_b)


def _outproj_kernel(ret_ref, dif_ref, w_ref, x_ref, mod_ref, nfw_ref, rw_ref, x1_ref, hp_ref, logit_ref):
    mod = mod_ref[0]

    def mix(r0):
        rows = slice(r0, r0 + SUB_OUT)
        return (jnp.dot(ret_ref[0, rows, :], w_ref[0:NH * HV, :], preferred_element_type=F32)
                + jnp.dot(dif_ref[0, rows, :], w_ref[NH * HV:, :], preferred_element_type=F32))

    mx = mix(0)
    for r0 in range(0, TM_OUT, SUB_OUT):
        mx_next = mix(r0 + SUB_OUT) if r0 + SUB_OUT < TM_OUT else None
        x1 = x_ref[0, r0:r0 + SUB_OUT, :] + mod[:, 2 * D:3 * D] * mx
        x1_ref[0, r0:r0 + SUB_OUT, :] = x1
        _ffn_prologue(x1, r0, mod, nfw_ref, rw_ref, hp_ref, logit_ref)
        mx = mx_next


def _outproj(ret, dif, w_out, x, mod3, nfw, rw_t):
    nj = T // TM_OUT
    out_specs, out_shapes = _ffn_out_specs(TM_OUT, nj)
    return pl.pallas_call(
        _outproj_kernel,
        grid=(B, nj),
        in_specs=[
            pl.BlockSpec((1, TM_OUT, NH * HV), lambda b, j: (b, j, 0)),
            pl.BlockSpec((1, TM_OUT, NH * HV), lambda b, j: (b, j, 0)),
            pl.BlockSpec((2 * NH * HV, D), lambda b, j: (0, 0)),
            pl.BlockSpec((1, TM_OUT, D), lambda b, j: (b, j, 0)),
            pl.BlockSpec((1, 1, 6 * D), lambda b, j: (b, 0, 0)),
            pl.BlockSpec((1, D), lambda b, j: (0, 0)),
            pl.BlockSpec((NE, D), lambda b, j: (0, 0)),
        ],
        out_specs=out_specs,
        out_shape=out_shapes,
        compiler_params=_cparams(("arbitrary", "arbitrary")),
        name="outproj",
    )(ret, dif, w_out, x, mod3, nfw, rw_t)


def _pool_kernel(x_ref, prev_ref, next_ref, mod_ref, nmw_ref, pw_ref, ps_ref, nfw_ref, rw_ref,
                 x1_ref, hp_ref, logit_ref, ext_ref):
    i = pl.program_id(1)
    last = pl.num_programs(1) - 1
    mod = mod_ref[0]
    sh1 = mod[:, 0:D]
    sc1 = mod[:, D:2 * D]

    def modnorm(v):
        return (_rms(v) * nmw_ref[...]) * (1.0 + sc1) + sh1

    x = x_ref[0]
    hc = modnorm(x)
    ext_ref[0:HALO, :] = jnp.where(i > 0, modnorm(prev_ref[0]), 0.0)
    ext_ref[HALO:HALO + TM_POOL, :] = hc
    ext_ref[HALO + TM_POOL:, :] = jnp.where(i < last, modnorm(next_ref[0]), 0.0)
    pos = i * TM_POOL + lax.broadcasted_iota(I32, (TM_POOL, 1), 0)
    mixed = []
    for gi, w in enumerate(POOL_WINDOWS):
        left = w // 2
        right = w - 1 - left
        cols = slice(gi * PG, (gi + 1) * PG)
        tot = None
        for d in range(-left, right + 1):
            part = ext_ref[HALO + d:HALO + d + TM_POOL, cols]
            tot = part if tot is None else tot + part
        cnt = (jnp.minimum(pos + right + 1, T) - jnp.maximum(pos - left, 0)).astype(F32)
        pooled = (tot * (1.0 / cnt) - hc[:, cols]).astype(BF16)
        mixed.append(jnp.dot(pooled, pw_ref[gi], preferred_element_type=F32))
    mixed = jnp.concatenate(mixed, axis=1) * ps_ref[...]
    x1 = x + mod[:, 2 * D:3 * D] * mixed
    x1_ref[0] = x1
    _ffn_prologue(x1, 0, mod, nfw_ref, rw_ref, hp_ref, logit_ref)


def _pool_layer(x, mod3, nmw, pool_w, pool_scale, nfw, rw_t):
    ni = T // TM_POOL
    hb = TM_POOL // HALO
    out_specs, out_shapes = _ffn_out_specs(TM_POOL, ni)
    return pl.pallas_call(
        _pool_kernel,
        grid=(B, ni),
        in_specs=[
            pl.BlockSpec((1, TM_POOL, D), lambda b, i: (b, i, 0)),
            pl.BlockSpec((1, HALO, D), lambda b, i: (b, jnp.maximum(i * hb - 1, 0), 0)),
            pl.BlockSpec((1, HALO, D), lambda b, i: (b, jnp.minimum((i + 1) * hb, T // HALO - 1), 0)),
            pl.BlockSpec((1, 1, 6 * D), lambda b, i: (b, 0, 0)),
            pl.BlockSpec((1, D), lambda b, i: (0, 0)),
            pl.BlockSpec((len(POOL_WINDOWS), PG, PG), lambda b, i: (0, 0, 0)),
            pl.BlockSpec((1, D), lambda b, i: (0, 0)),
            pl.BlockSpec((1, D), lambda b, i: (0, 0)),
            pl.BlockSpec((NE, D), lambda b, i: (0, 0)),
        ],
        out_specs=out_specs,
        out_shape=out_shapes,
        scratch_shapes=[pltpu.VMEM((TM_POOL + 2 * HALO, D), F32)],
        compiler_params=_cparams(("arbitrary", "arbitrary")),
        name="pool_layer",
    )(x, x, x, mod3, nmw, pool_w, pool_scale, nfw, rw_t)


def _tile_copy(src_ref, dst_ref, sem, s, d, rows=1):
    s0 = pl.multiple_of(s * SLAB, SLAB)
    d0 = pl.multiple_of(d * SLAB, SLAB)
    return pltpu.make_async_copy(src_ref.at[pl.ds(s0, rows * SLAB)], dst_ref.at[pl.ds(d0, rows * SLAB)], sem)


def _issue_tile_copies(idx_ref, base, rows, start_one):
    def group(g, carry):
        r0 = g * PERM_UNROLL
        ids = [idx_ref[base + r0 + u] for u in range(PERM_UNROLL)]
        for u in range(PERM_UNROLL):
            start_one(r0 + u, ids[u], u % 2)
        return carry

    lax.fori_loop(0, rows // PERM_UNROLL, group, 0)


def _dispatch_kernel(dest_ref, src_ref, dst_ref, zero_ref, sem, pad_sem):
    base = pl.program_id(0) * TM_PERM

    @pl.when(pl.program_id(0) == 0)
    def _():
        zero_ref[...] = jnp.zeros_like(zero_ref)
        pad = _tile_copy(zero_ref, dst_ref, pad_sem, 0, N, TM_MOE)
        pad.start()
        pad.wait()

    def start_one(r, d, priority):
        _tile_copy(src_ref, dst_ref, sem, r, d).start(priority=priority)

    _issue_tile_copies(dest_ref, base, TM_PERM, start_one)
    _tile_copy(src_ref, dst_ref, sem, 0, 0, TM_PERM).wait()


def _dispatch(dest, src):
    return pl.pallas_call(
        _dispatch_kernel,
        grid_spec=pltpu.PrefetchScalarGridSpec(
            num_scalar_prefetch=1,
            grid=(N // TM_PERM,),
            in_specs=[pl.BlockSpec((TM_PERM * SLAB, 128), lambda i, dest: (i, 0))],
            out_specs=pl.BlockSpec(memory_space=pl.ANY),
            scratch_shapes=[pltpu.VMEM((TM_MOE * SLAB, 128), F32), pltpu.SemaphoreType.DMA(()),
                            pltpu.SemaphoreType.DMA(())],
        ),
        out_shape=jax.ShapeDtypeStruct(((N + TM_MOE) * SLAB, 128), src.dtype),
        compiler_params=_cparams(("arbitrary",)),
        name="dispatch",
    )(dest, src)


def _combine_kernel(dest_ref, x_ref, ys_ref, mod_ref, fw_ref, o_ref, ybuf_ref, sem, *, final):
    i = pl.program_id(0)
    n = pl.num_programs(0)
    slot = i % 2

    def gather(step, to_slot):
        def start_one(r, d, priority):
            _tile_copy(ys_ref, ybuf_ref.at[to_slot], sem.at[to_slot], d, r).start(priority=priority)

        _issue_tile_copies(dest_ref, step * TM_COMB, TM_COMB, start_one)

    @pl.when(i == 0)
    def _():
        gather(0, 0)

    @pl.when(i + 1 < n)
    def _():
        gather(i + 1, 1 - slot)

    _tile_copy(ys_ref, ybuf_ref.at[slot], sem.at[slot], 0, 0, TM_COMB).wait()
    out = x_ref[...] + mod_ref[0][:, 5 * D:6 * D] * _load_slabs(ybuf_ref.at[slot], TM_COMB)
    if final:
        out = _rms(out) * fw_ref[...]
    o_ref[...] = out


def _combine(dest, x1, ys, mod3, final_w, final):
    per_b = T // TM_COMB
    return pl.pallas_call(
        functools.partial(_combine_kernel, final=final),
        grid_spec=pltpu.PrefetchScalarGridSpec(
            num_scalar_prefetch=1,
            grid=(N // TM_COMB,),
            in_specs=[
                pl.BlockSpec((TM_COMB, D), lambda i, dest: (i, 0)),
                pl.BlockSpec(memory_space=pl.ANY),
                pl.BlockSpec((1, 1, 6 * D), lambda i, dest: (i // per_b, 0, 0)),
                pl.BlockSpec((1, D), lambda i, dest: (0, 0)),
            ],
            out_specs=pl.BlockSpec((TM_COMB, D), lambda i, dest: (i, 0)),
            scratch_shapes=[pltpu.VMEM((2, TM_COMB * SLAB, 128), F32), pltpu.SemaphoreType.DMA((2,))],
        ),
        out_shape=jax.ShapeDtypeStruct((N, D), F32),
        compiler_params=_cparams(("arbitrary",)),
        name="combine",
    )(dest, x1.reshape(N, D), ys, mod3, final_w)


def _moe_kernel(chunk_ref, rw_ref, xs_hbm, wg_hbm, wu_hbm, wd_hbm, ys_hbm,
                xbuf_ref, obuf_ref,
                sga_ref, sua_ref, sda_ref, sgb_ref, sub_ref, sdb_ref,
                ga_ref, ua_ref, da_ref, gb_ref, ub_ref, db_ref, wsem, xsem, osem, *, layer):
    row = lambda r: (lambda m: chunk_ref[r, m])
    start_of, n = row(0), chunk_ref[7, 0]
    slots = ((row(1), row(3), row(5), (sga_ref, sua_ref, sda_ref), (ga_ref, ua_ref, da_ref)),
             (row(2), row(4), row(6), (sgb_ref, sub_ref, sdb_ref), (gb_ref, ub_ref, db_ref)))

    def fetch(w_slot, expert):
        stage = slots[w_slot][3]
        return [pltpu.make_async_copy(w.at[layer, expert], s, wsem.at[w_slot])
                for w, s in zip((wg_hbm, wu_hbm, wd_hbm), stage)]

    def x_copy(step, buf):
        return _tile_copy(xs_hbm, xbuf_ref.at[buf], xsem.at[buf], start_of(step), 0, TM_MOE)

    def o_copy(step, buf):
        return _tile_copy(obuf_ref.at[buf], ys_hbm, osem.at[buf], 0, start_of(step), TM_MOE)

    @pl.when(pl.program_id(0) == 0)
    def _():
        obuf_ref[1] = jnp.zeros_like(obuf_ref[1])
        pad = _tile_copy(obuf_ref.at[1], ys_hbm, osem.at[1], 0, N, TM_MOE)
        pad.start()
        pad.wait()
        x_copy(0, 0).start()

    def chunk(m, slot):
        @pl.when(m + 1 < n)
        def _():
            x_copy(m + 1, 1 - slot).start()

        for w_slot, (expert_of, changed_at, next_of, stage, work) in enumerate(slots):
            @pl.when(m == 0)
            def _():
                for cp in fetch(w_slot, expert_of(0)):
                    cp.start(priority=1)

            @pl.when(changed_at(m) == 1)
            def _():
                for cp in fetch(w_slot, 0):
                    cp.wait()
                for s, w in zip(stage, work):
                    w[...] = s[...].astype(BF16)

                @pl.when(next_of(m) >= 0)
                def _():
                    for cp in fetch(w_slot, next_of(m)):
                        cp.start(priority=1)

        x_copy(m, slot).wait()
        hf = _load_slabs(xbuf_ref.at[slot], TM_MOE)
        h = hf.astype(BF16)

        def up(w_ref):
            return jnp.dot(h, w_ref[...], preferred_element_type=F32)

        def down(g, u, d_ref):
            return jnp.dot((_silu(g) * u).astype(BF16), d_ref[...], preferred_element_type=F32)

        g_a, u_a, g_b, u_b = up(ga_ref), up(ua_ref), up(gb_ref), up(ub_ref)
        y_a = down(g_a, u_a, da_ref)
        y_b = down(g_b, u_b, db_ref)
        s_a = _sigmoid(jnp.sum(hf * rw_ref[pl.ds(slots[0][0](m), 1), :], axis=1, keepdims=True))
        s_b = _sigmoid(jnp.sum(hf * rw_ref[pl.ds(slots[1][0](m), 1), :], axis=1, keepdims=True))
        denom = s_a + s_b
        val = (s_a / denom) * y_a + (s_b / denom) * y_b

        @pl.when(m >= 1)
        def _():
            o_copy(m - 1, 1 - slot).wait()

        _store_slabs(obuf_ref.at[slot], val)
        o_copy(m, slot).start()

        @pl.when(m == n - 1)
        def _():
            o_copy(m, slot).wait()

    for j in range(2):
        m = pl.program_id(0) * 2 + j
        pl.when(m < n)(functools.partial(chunk, m, j))


def _moe_sorted(chunks, xs, rw_t, wg, wu, wd, layer):
    mats = lambda dt: [pltpu.VMEM((D, DE), dt), pltpu.VMEM((D, DE), dt), pltpu.VMEM((DE, D), dt)]
    hbm = pl.BlockSpec(memory_space=pl.ANY)
    return pl.pallas_call(
        functools.partial(_moe_kernel, layer=layer),
        grid_spec=pltpu.PrefetchScalarGridSpec(
            num_scalar_prefetch=1,
            grid=(N_UNITS // 2,),
            in_specs=[pl.BlockSpec((NE, D), lambda m, *_: (0, 0)), hbm, hbm, hbm, hbm],
            out_specs=hbm,
            scratch_shapes=([pltpu.VMEM((2, TM_MOE * SLAB, 128), F32), pltpu.VMEM((2, TM_MOE * SLAB, 128), F32)]
                            + mats(F32) + mats(F32) + mats(BF16) + mats(BF16)
                            + [pltpu.SemaphoreType.DMA((2,))] * 3),
        ),
        out_shape=jax.ShapeDtypeStruct(xs.shape, F32),
        compiler_params=_cparams(("arbitrary",)),
        name="moe_sorted",
    )(chunks, rw_t, xs, wg, wu, wd)


def _moe_layer(x1, hp, logits, rw_t, router_b, wg, wu, wd, layer, mod3, final_w, final):
    dest, chunks = _route_tokens(logits, router_b)
    dest = dest.reshape(N)
    xs = _dispatch(dest, hp)
    ys = _moe_sorted(chunks, xs, rw_t, wg, wu, wd, layer)
    return _combine(dest, x1, ys, mod3, final_w, final).reshape(B, T, D)


def _rope_tables():
    half = 16
    inv = ROPE_BASE ** (-np.arange(half, dtype=np.float64) / half)
    t = np.arange(T)
    ang_r = (t // GRID_W)[:, None] * inv[None, :]
    ang_c = (t % GRID_W)[:, None] * inv[None, :]
    ang = np.concatenate([ang_r, ang_r, ang_c, ang_c], axis=1)
    sign = np.tile(np.concatenate([-np.ones(half), np.ones(half)]), 2)
    cos = np.concatenate([np.ones((LC, 64)), np.cos(ang)], axis=0)
    sin = np.concatenate([np.zeros((LC, 64)), np.sin(ang) * sign[None, :]], axis=0)
    return (jnp.asarray(np.tile(cos, (1, 2)), dtype=F32), jnp.asarray(np.tile(sin, (1, 2)), dtype=F32))


def _permute_w_in(w):
    rq = w[:, 0:256].reshape(D, NH, DK)
    dq = w[:, 256:768]
    rg = w[:, 768:1280]
    rk = w[:, 1280:1536].reshape(D, NH, DK)
    rv = w[:, 1536:2048]
    dk = w[:, 2048:2560]
    dv = w[:, 2560:3072]
    qk = jnp.concatenate([rq, rk * (DK ** -0.5)], axis=2).reshape(D, NH * 2 * DK)
    return jnp.concatenate([qk, rv, rg, dq * (DK ** -0.5 * math.log2(math.e)), dk, dv], axis=1).astype(BF16)


def kernel(x, c, ctx, c_ctx, ada_w, ada_b, norm_mix_w, norm_ffn_w, w_in, w_out, ret_log_decay, diff_lambda,
           diff_subln_w, pool_w, pool_scale, router_w, router_b, moe_w_gate, moe_w_up, moe_w_down, final_norm_w):
    assert x.shape == (B, T, D) and ctx.shape == (B, LC, D) and ada_w.shape[0] == 2
    cc = jnp.concatenate([c, c_ctx[None, :], jnp.zeros((16 - B - 1, D), F32)], axis=0)
    mod = _ada_mod(cc, ada_w, ada_b)
    rw_t = router_w.T
    fw = final_norm_w.reshape(1, D)
    experts = (moe_w_gate, moe_w_up, moe_w_down)

    mod0 = mod[0].reshape(16, 1, 6 * D)
    cos_t, sin_t = _rope_tables()
    proj = _inproj(x, ctx, mod0, norm_mix_w[0:1], _permute_w_in(w_in[0]), cos_t, sin_t)
    ret = _retention(proj, ret_log_decay[0])
    lam_init = 0.8 - 0.6 * math.exp(-0.3 * 0)
    lv = diff_lambda[0]
    lam = jnp.exp(jnp.sum(lv[0] * lv[1])) - jnp.exp(jnp.sum(lv[2] * lv[3])) + lam_init
    dif = _diffattn(proj, lam.reshape(1), diff_subln_w[0:1], 1.0 - lam_init)
    x1, hp, logits = _outproj(ret, dif, w_out[0].astype(BF16), x, mod0, norm_ffn_w[0:1], rw_t)
    x2 = _moe_layer(x1, hp, logits, rw_t, router_b, *experts, 0, mod0, fw, False)

    mod1 = mod[1].reshape(16, 1, 6 * D)
    x3, hp, logits = _pool_layer(x2, mod1, norm_mix_w[1:2], pool_w[0].astype(BF16), pool_scale[0:1],
                                 norm_ffn_w[1:2], rw_t)
    return _moe_layer(x3, hp, logits, rw_t, router_b, *experts, 1, mod1, fw, True)
```

```python
import functools
import math

import jax
import jax.numpy as jnp
import numpy as np
from jax import lax
from jax.experimental import pallas as pl
from jax.experimental.pallas import tpu as pltpu

F32 = jnp.float32
BF16 = jnp.bfloat16
I32 = jnp.int32

D = 1024
B = 8
T = 2048
N = B * T
GRID_W = 64
LC = 256
EPS = 1e-6
ROPE_BASE = 10000.0
NH = 4
DK = 64
HV = 128
CH = 256
RB = LC + T
NCH = RB // CH
POOL_WINDOWS = (2, 4, 8, 16)
PG = D // len(POOL_WINDOWS)
NE = 16
NGRP = 4
EPG = NE // NGRP
DE = 512
IN_W = 3072
HALO = 8
POOL_MARGIN = 16
assert POOL_WINDOWS == tuple(2 ** (k + 1) for k in range(len(POOL_WINDOWS))) and POOL_WINDOWS[-1] <= POOL_MARGIN

PAIR_A = (0, 0, 0, 1, 1, 3)
PAIR_B = (1, 2, 3, 3, 2, 2)
NCLS = NGRP * len(PAIR_A)
SLAB = D // 128

TM_PROJ = 256
PROJ_SUB = 3
TM_OUT = 1024
SUB_OUT = 512
TQ_SUB = 8
SCORE_AHEAD = 1
TM_POOL = 512
TM_MOE = 256
N_UNITS = N // TM_MOE + NCLS
TM_PERM = 2048
TM_COMB = 512
PERM_UNROLL = 16
VMEM_LIMIT = 56 * 1024 * 1024


def _cparams(sem):
    return pltpu.CompilerParams(dimension_semantics=sem, vmem_limit_bytes=VMEM_LIMIT)


def _sigmoid(x):
    return 1.0 / (1.0 + jnp.exp(-x))


def _silu(x):
    return x * _sigmoid(x)


def _rms(x):
    return x * lax.rsqrt(jnp.mean(x * x, axis=-1, keepdims=True) + EPS)


def _dot_3pass(a, b, dims):
    a_hi = a.astype(BF16)
    b_hi = b.astype(BF16)
    a_lo = (a - a_hi.astype(F32)).astype(BF16)
    b_lo = (b - b_hi.astype(F32)).astype(BF16)

    def dot(x, y):
        return lax.dot_general(x, y, dims, preferred_element_type=F32)

    return dot(a_hi, b_hi) + (dot(a_lo, b_hi) + dot(a_hi, b_lo))


def _load_slabs(ref, rows):
    return jnp.concatenate([ref[pl.ds(s, rows, stride=SLAB), :] for s in range(SLAB)], axis=1)


def _store_slabs(ref, val, row0=0):
    rows = val.shape[0]
    for s in range(SLAB):
        ref[pl.ds(row0 * SLAB + s, rows, stride=SLAB), :] = val[:, s * 128:(s + 1) * 128]


def _ada_kernel(cc_ref, w_ref, b_ref, o_ref):
    s = _silu(cc_ref[...])
    o_ref[0] = _dot_3pass(s, w_ref[0], (((1,), (0,)), ((), ()))) + b_ref[0]


def _ada_mod(cc, ada_w, ada_b):
    depth = ada_w.shape[0]
    tn = 1536
    return pl.pallas_call(
        _ada_kernel,
        grid=(depth, 6 * D // tn),
        in_specs=[
            pl.BlockSpec((16, D), lambda l, n: (0, 0)),
            pl.BlockSpec((1, D, tn), lambda l, n: (l, 0, n)),
            pl.BlockSpec((1, 1, tn), lambda l, n: (l, 0, n)),
        ],
        out_specs=pl.BlockSpec((1, 16, tn), lambda l, n: (l, 0, n)),
        out_shape=jax.ShapeDtypeStruct((depth, 16, 6 * D), F32),
        compiler_params=_cparams(("arbitrary", "arbitrary")),
        name="ada_mod",
    )(cc, ada_w, ada_b.reshape(depth, 1, 6 * D))


def _rope(seg, cos, sin_signed, lo_mask):
    w = seg.shape[1]
    from_hi = pltpu.roll(seg, w - 16, axis=1)
    from_lo = pltpu.roll(seg, 16, axis=1)
    partner = jnp.where(lo_mask, from_hi, from_lo)
    reps = w // cos.shape[1]
    c = jnp.concatenate([cos] * reps, axis=1)
    s = jnp.concatenate([sin_signed] * reps, axis=1)
    return seg * c + partner * s


def _inproj_kernel(*refs):
    x_refs = refs[:PROJ_SUB]
    c_ref, mod_ref, cmod_ref, nw_ref, w_ref, cos_ref, sin_ref, o_ref = refs[PROJ_SUB:]
    is_ctx = pl.program_id(1) == 0
    parts = []
    for s in range(PROJ_SUB):
        xt = x_refs[s][0]
        sh = mod_ref[0, :, 0:D]
        sc = mod_ref[0, :, D:2 * D]
        if s == 0:
            xt = jnp.where(is_ctx, c_ref[0], xt)
            sh = jnp.where(is_ctx, cmod_ref[0, :, 0:D], sh)
            sc = jnp.where(is_ctx, cmod_ref[0, :, D:2 * D], sc)
        parts.append(((_rms(xt) * nw_ref[...]) * (1.0 + sc) + sh).astype(BF16))
    hb = jnp.concatenate(parts, axis=0)
    lane = lax.broadcasted_iota(I32, (PROJ_SUB * TM_PROJ, 512), 1)
    lo_mask = (lane % 32) < 16
    cos = cos_ref[...]
    sin = sin_ref[...]

    def project(g):
        return jnp.dot(hb, w_ref[:, g * 512:(g + 1) * 512], preferred_element_type=F32)

    seg = project(0)
    for g in range(6):
        seg_next = project(g + 1) if g + 1 < 6 else None
        if g in (0, 3, 4):
            seg = _rope(seg, cos, sin, lo_mask)
        o_ref[0, :, g * 512:(g + 1) * 512] = seg.astype(BF16)
        seg = seg_next


def _inproj(x, ctx, mod3, norm_w, w_perm, cos_t, sin_t):
    tm = PROJ_SUB * TM_PROJ
    nj = RB // tm

    def x_map(s, b, j):
        return (b, jnp.maximum(PROJ_SUB * j + s - LC // TM_PROJ, 0), 0)

    return pl.pallas_call(
        _inproj_kernel,
        grid=(B, nj),
        in_specs=[
            *[pl.BlockSpec((1, TM_PROJ, D), functools.partial(x_map, s)) for s in range(PROJ_SUB)],
            pl.BlockSpec((1, LC, D), lambda b, j: (b, 0, 0)),
            pl.BlockSpec((1, 1, 2 * D), lambda b, j: (b, 0, 0)),
            pl.BlockSpec((1, 1, 2 * D), lambda b, j: (B, 0, 0)),
            pl.BlockSpec((1, D), lambda b, j: (0, 0)),
            pl.BlockSpec((D, IN_W), lambda b, j: (0, 0)),
            pl.BlockSpec((tm, 128), lambda b, j: (j, 0)),
            pl.BlockSpec((tm, 128), lambda b, j: (j, 0)),
        ],
        out_specs=pl.BlockSpec((1, tm, IN_W), lambda b, j: (b, j, 0)),
        out_shape=jax.ShapeDtypeStruct((B, RB, IN_W), BF16),
        compiler_params=_cparams(("arbitrary", "arbitrary")),
        name="inproj",
    )(*([x] * PROJ_SUB), ctx, mod3, mod3, norm_w, w_perm, cos_t, sin_t)


def _retention_kernel(ld_ref, qk_ref, v_ref, g_ref, o_ref, st_ref, kdec_ref, qdec_ref, mask_ref, cdec_ref):
    h = pl.program_id(0)
    lane = lax.broadcasted_iota(I32, (CH, 128), 1)
    fwd_lane = lane < DK

    @pl.when(pl.program_id(1) == 0)
    def _():
        lgf = ld_ref[0, h]
        lgb = ld_ref[1, h]
        pos = lax.broadcasted_iota(I32, (CH, 128), 0).astype(F32)
        kdec_ref[...] = jnp.where(fwd_lane, jnp.exp(lgf * (CH - 1 - pos)), jnp.exp(lgb * pos))
        qdec_ref[...] = jnp.where(fwd_lane, jnp.exp(lgf * (pos + 1.0)), jnp.exp(lgb * (CH - pos)))
        ii = lax.broadcasted_iota(I32, (CH, CH), 0)
        jj = lax.broadcasted_iota(I32, (CH, CH), 1)
        gap = (ii - jj).astype(F32)
        mask_ref[...] = (jnp.where(gap >= 0, jnp.exp(lgf * jnp.maximum(gap, 0.0)), 0.0)
                         + jnp.where(gap <= 0, jnp.exp(lgb * jnp.maximum(-gap, 0.0)), 0.0))
        ones = jnp.ones((DK, 128), F32)
        cdec_ref[0:DK, :] = jnp.exp(lgf * CH * ones)
        cdec_ref[DK:, :] = jnp.exp(lgb * CH * ones)

    kdec = kdec_ref[...]
    qdec = qdec_ref[...]
    mask = mask_ref[...]
    cf = cdec_ref[0:DK, :]
    cb = cdec_ref[DK:, :]

    def chunk(n):
        a = qk_ref[0, n * CH:(n + 1) * CH, :].astype(F32)
        swapped = pltpu.roll(a, DK, axis=1)
        return a, swapped

    kv = []
    for n in range(NCH):
        a, swapped = chunk(n)
        kk = jnp.where(fwd_lane, swapped, a)
        kb = (kk * kdec).astype(BF16)
        vn = v_ref[0, n * CH:(n + 1) * CH, :]
        kv.append(lax.dot_general(kb, vn, (((0,), (0,)), ((), ())), preferred_element_type=F32))
    sf = kv[0][:DK]
    for n in range(1, NCH):
        st_ref[n, 0:DK, :] = sf
        sf = cf * sf + kv[n][:DK]
    sb = kv[0][DK:]
    for n in range(NCH - 1, 0, -1):
        st_ref[n, DK:2 * DK, :] = sb
        sb = cb * sb + kv[n][DK:]

    for n in range(1, NCH):
        a, swapped = chunk(n)
        q = a[:, :DK].astype(BF16)
        k = swapped[:, :DK].astype(BF16)
        scores = lax.dot_general(q, k, (((1,), (1,)), ((), ())), preferred_element_type=F32)
        p = (scores * mask).astype(BF16)
        vn = v_ref[0, n * CH:(n + 1) * CH, :]
        qq = jnp.where(fwd_lane, a, swapped)
        qd = (qq * qdec).astype(BF16)
        o = (jnp.dot(p, vn, preferred_element_type=F32)
             + jnp.dot(qd, st_ref[n].astype(BF16), preferred_element_type=F32))
        gate = g_ref[0, n * CH:(n + 1) * CH, :].astype(F32)
        o_ref[0, (n - 1) * CH:n * CH, :] = (_rms(o) * _silu(gate)).astype(BF16)


def _retention(proj, log_decay):
    return pl.pallas_call(
        _retention_kernel,
        grid=(NH, B),
        in_specs=[
            pl.BlockSpec(memory_space=pltpu.SMEM),
            pl.BlockSpec((1, RB, 128), lambda h, b: (b, 0, h)),
            pl.BlockSpec((1, RB, 128), lambda h, b: (b, 0, NH + h)),
            pl.BlockSpec((1, RB, 128), lambda h, b: (b, 0, 2 * NH + h)),
        ],
        out_specs=pl.BlockSpec((1, T, 128), lambda h, b: (b, 0, h)),
        out_shape=jax.ShapeDtypeStruct((B, T, NH * HV), BF16),
        scratch_shapes=[pltpu.VMEM((NCH, 128, 128), F32), pltpu.VMEM((CH, 128), F32), pltpu.VMEM((CH, 128), F32),
                        pltpu.VMEM((CH, CH), F32), pltpu.VMEM((2 * DK, 128), F32)],
        compiler_params=_cparams(("arbitrary", "arbitrary")),
        name="retention",
    )(log_decay, proj, proj, proj)


def _diffattn_kernel(lam_ref, *refs, out_scale):
    q_refs = refs[:TQ_SUB]
    k_ref, v_ref, sw_ref, o_ref = refs[TQ_SUB:]
    lam = lam_ref[0]
    k = k_ref[0]
    v = v_ref[0]
    nt = (((1,), (1,)), ((), ()))

    def scores(qh):
        return lax.dot_general(qh, k, nt, preferred_element_type=F32)

    v_ones = jnp.concatenate([v, jnp.ones_like(v)], axis=1)

    def values(s):
        e = jnp.exp2(s - jnp.max(s, axis=-1, keepdims=True))
        ol = jnp.dot(e.astype(BF16), v_ones, preferred_element_type=F32)
        return ol[:, :HV], ol[:, HV:HV + 1]

    halves = []
    for i in range(TQ_SUB):
        q = q_refs[i][0]
        lane = lax.broadcasted_iota(I32, q.shape, 1)
        zero = jnp.zeros_like(q)
        halves += [jnp.where(lane < DK, q, zero), jnp.where(lane >= DK, q, zero)]
    outs = []
    ahead = [scores(h) for h in halves[:SCORE_AHEAD]]
    for c in range(len(halves)):
        if c + SCORE_AHEAD < len(halves):
            ahead.append(scores(halves[c + SCORE_AHEAD]))
        outs.append(values(ahead.pop(0)))
    for i in range(TQ_SUB):
        (o1, l1), (o2, l2) = outs[2 * i], outs[2 * i + 1]
        o = o1 / l1 - o2 * (lam / l2)
        o_ref[0, i * TM_PROJ:(i + 1) * TM_PROJ, :] = (_rms(o) * sw_ref[...] * out_scale).astype(BF16)


def _diffattn(proj, lam, subln_w, out_scale):
    tq = TQ_SUB * TM_PROJ
    nq = T // tq

    def q_map(i, b, h, j):
        return (b, LC // TM_PROJ + j * TQ_SUB + i, 3 * NH + h)

    return pl.pallas_call(
        functools.partial(_diffattn_kernel, out_scale=out_scale),
        grid=(B, NH, nq),
        in_specs=[
            pl.BlockSpec(memory_space=pltpu.SMEM),
            *[pl.BlockSpec((1, TM_PROJ, 128), functools.partial(q_map, i)) for i in range(TQ_SUB)],
            pl.BlockSpec((1, RB, 128), lambda b, h, j: (b, 0, 4 * NH + h)),
            pl.BlockSpec((1, RB, 128), lambda b, h, j: (b, 0, 5 * NH + h)),
            pl.BlockSpec((1, HV), lambda b, h, j: (0, 0)),
        ],
        out_specs=pl.BlockSpec((1, tq, 128), lambda b, h, j: (b, j, h)),
        out_shape=jax.ShapeDtypeStruct((B, T, NH * HV), BF16),
        compiler_params=_cparams(("arbitrary", "arbitrary", "arbitrary")),
        name="diffattn",
    )(lam, *([proj] * TQ_SUB), proj, proj, subln_w)


def _route(bz):
    grp = []
    for g in range(NGRP):
        m = bz[g * EPG:(g + 1) * EPG]
        best = None
        for i in range(EPG):
            for k in range(i + 1, EPG):
                pair = m[i] + m[k]
                best = pair if best is None else jnp.maximum(best, pair)
        grp.append(best)
    gbest = grp[0]
    gsel = jnp.zeros_like(gbest, dtype=I32)
    for g in range(1, NGRP):
        better = grp[g] > gbest
        gsel = jnp.where(better, g, gsel)
        gbest = jnp.where(better, grp[g], gbest)
    cb = [bz[i] for i in range(EPG)]
    for g in range(1, NGRP):
        pick = gsel == g
        cb = [jnp.where(pick, bz[g * EPG + i], cb[i]) for i in range(EPG)]
    i1 = jnp.zeros_like(gsel)
    b1 = cb[0]
    for i in range(1, EPG):
        better = cb[i] > b1
        i1 = jnp.where(better, i, i1)
        b1 = jnp.where(better, cb[i], b1)
    neg = jnp.full_like(b1, -jnp.inf)
    rest = [jnp.where(i1 == i, neg, cb[i]) for i in range(EPG)]
    i2 = jnp.zeros_like(gsel)
    b2 = rest[0]
    for i in range(1, EPG):
        better = rest[i] > b2
        i2 = jnp.where(better, i, i2)
        b2 = jnp.where(better, rest[i], b2)
    lo = jnp.minimum(i1, i2)
    hi = jnp.maximum(i1, i2)
    code = lo * EPG + hi
    pair = jnp.full_like(gsel, len(PAIR_A) - 1)
    for p in range(len(PAIR_A) - 1):
        a, b = min(PAIR_A[p], PAIR_B[p]), max(PAIR_A[p], PAIR_B[p])
        pair = jnp.where(code == a * EPG + b, p, pair)
    return gsel * len(PAIR_A) + pair


def _ffn_prologue(x1, row0, mod, nfw_ref, rw_ref, hp_ref, logit_ref):
    rows = x1.shape[0]
    sh2 = mod[:, 3 * D:4 * D]
    sc2 = mod[:, 4 * D:5 * D]
    h2 = (_rms(x1) * nfw_ref[...]) * (1.0 + sc2) + sh2
    _store_slabs(hp_ref, h2, row0)
    logit_ref[:, row0:row0 + rows] = _dot_3pass(rw_ref[...], h2, (((1,), (1,)), ((), ())))


def _ffn_out_specs(tm, n_tiles_per_b):
    specs = [
        pl.BlockSpec((1, tm, D), lambda b, j: (b, j, 0)),
        pl.BlockSpec((tm * SLAB, 128), lambda b, j: (b * n_tiles_per_b + j, 0)),
        pl.BlockSpec((NE, tm), lambda b, j: (0, b * n_tiles_per_b + j)),
    ]
    shapes = [
        jax.ShapeDtypeStruct((B, T, D), F32),
        jax.ShapeDtypeStruct((N * SLAB, 128), F32),
        jax.ShapeDtypeStruct((NE, N), F32),
    ]
    return specs, shapes


def _route_kernel(logit_ref, bias_ref, dest_ref, chunk_ref):
    r = logit_ref.shape[1]
    cls = _route([_sigmoid(logit_ref[e]) + bias_ref[e] for e in range(NE)])
    lane_incl = (lax.broadcasted_iota(I32, (128, 128), 0) <= lax.broadcasted_iota(I32, (128, 128), 1)).astype(BF16)
    rows_before = (lax.broadcasted_iota(I32, (r, r), 1) < lax.broadcasted_iota(I32, (r, r), 0)).astype(BF16)
    dest = jnp.zeros((r, 128), F32)
    start = jnp.zeros((1, 128), F32)
    ends = []
    for c in range(NCLS):
        onehot = jnp.where(cls == c, 1.0, 0.0)
        in_row = jnp.dot(onehot.astype(BF16), lane_incl, preferred_element_type=F32)
        row_tot = jnp.broadcast_to(in_row[:, 127:128], (r, 128))
        above = jnp.dot(rows_before, row_tot.astype(BF16), preferred_element_type=F32)
        dest = dest + onehot * (start + above + in_row - 1.0)
        start = start + jnp.sum(row_tot, axis=0, keepdims=True)
        ends.append(start)
    dest_ref[...] = dest.astype(I32)
    chunk_ref[...] = _chunk_list(ends)


def _chunk_list(ends):
    lane = lax.broadcasted_iota(I32, (1, 128), 1).astype(F32)
    offs = [jnp.zeros((1, 128), F32)] + ends[:-1]
    per = [jnp.floor((e - o + (TM_MOE - 1)) * (1.0 / TM_MOE)) for e, o in zip(ends, offs)]
    cum = []
    total = jnp.zeros((1, 128), F32)
    for p in per:
        total = total + p
        cum.append(total)
    m = jnp.minimum(lane, total - 1.0)
    cls = jnp.zeros((1, 128), F32)
    for c in range(NCLS):
        cls = cls + jnp.where(cum[c] <= m, 1.0, 0.0)
    start = jnp.zeros((1, 128), F32)
    for c in range(NCLS):
        start = jnp.where(cls == c, offs[c] + TM_MOE * (m - (cum[c] - per[c])), start)
    n_pair = len(PAIR_A)
    grp = jnp.zeros((1, 128), F32)
    for g in range(1, NGRP):
        grp = grp + jnp.where(cls >= g * n_pair, 1.0, 0.0)
    pair = cls - n_pair * grp
    loc_a = jnp.zeros((1, 128), F32)
    loc_b = jnp.zeros((1, 128), F32)
    for p in range(n_pair):
        loc_a = jnp.where(pair == p, float(PAIR_A[p]), loc_a)
        loc_b = jnp.where(pair == p, float(PAIR_B[p]), loc_b)
    sub = lax.broadcasted_iota(I32, (128, 128), 0).astype(F32)
    lan = lax.broadcasted_iota(I32, (128, 128), 1).astype(F32)
    none = 1000.0

    def changes(e):
        chg = jnp.where((lane == 0.0) | (e != pltpu.roll(e, 1, axis=1)), 1.0, 0.0)
        chg_col = jnp.sum(jnp.where(sub == lan, chg, 0.0), axis=1, keepdims=True)
        e_col = jnp.sum(jnp.where(sub == lan, e, 0.0), axis=1, keepdims=True)
        nxt_at = jnp.min(jnp.where((sub > lan) & (chg_col == 1.0), sub, none), axis=0, keepdims=True)
        nxt_e = jnp.sum(jnp.where(sub == nxt_at, e_col, 0.0), axis=0, keepdims=True)
        return chg, jnp.where(nxt_at < none, nxt_e, -1.0)

    e_a = EPG * grp + loc_a
    e_b = EPG * grp + loc_b
    chg_a, nxt_a = changes(e_a)
    chg_b, nxt_b = changes(e_b)
    return jnp.concatenate([start, e_a, e_b, chg_a, chg_b, nxt_a, nxt_b, total], axis=0).astype(I32)


def _route_tokens(logits_t, router_b):
    r = N // 128
    return pl.pallas_call(
        _route_kernel,
        in_specs=[pl.BlockSpec((NE, r, 128), lambda: (0, 0, 0)), pl.BlockSpec(memory_space=pltpu.SMEM)],
        out_specs=[pl.BlockSpec((r, 128), lambda: (0, 0)), pl.BlockSpec((8, 128), lambda: (0, 0))],
        out_shape=[jax.ShapeDtypeStruct((r, 128), I32), jax.ShapeDtypeStruct((8, 128), I32)],
        compiler_params=pltpu.CompilerParams(vmem_limit_bytes=VMEM_LIMIT),
        name="route",
    )(logits_t.reshape(NE, r, 128), router_b)


def _outproj_kernel(ret_ref, dif_ref, w_ref, x_ref, mod_ref, nfw_ref, rw_ref, x1_ref, hp_ref, logit_ref):
    mod = mod_ref[0]

    def mix(r0):
        rows = slice(r0, r0 + SUB_OUT)
        return (jnp.dot(ret_ref[0, rows, :], w_ref[0:NH * HV, :], preferred_element_type=F32)
                + jnp.dot(dif_ref[0, rows, :], w_ref[NH * HV:, :], preferred_element_type=F32))

    mx = mix(0)
    for r0 in range(0, TM_OUT, SUB_OUT):
        mx_next = mix(r0 + SUB_OUT) if r0 + SUB_OUT < TM_OUT else None
        x1 = x_ref[0, r0:r0 + SUB_OUT, :] + mod[:, 2 * D:3 * D] * mx
        x1_ref[0, r0:r0 + SUB_OUT, :] = x1
        _ffn_prologue(x1, r0, mod, nfw_ref, rw_ref, hp_ref, logit_ref)
        mx = mx_next


def _outproj(ret, dif, w_out, x, mod3, nfw, rw_t):
    nj = T // TM_OUT
    out_specs, out_shapes = _ffn_out_specs(TM_OUT, nj)
    return pl.pallas_call(
        _outproj_kernel,
        grid=(B, nj),
        in_specs=[
            pl.BlockSpec((1, TM_OUT, NH * HV), lambda b, j: (b, j, 0)),
            pl.BlockSpec((1, TM_OUT, NH * HV), lambda b, j: (b, j, 0)),
            pl.BlockSpec((2 * NH * HV, D), lambda b, j: (0, 0)),
            pl.BlockSpec((1, TM_OUT, D), lambda b, j: (b, j, 0)),
            pl.BlockSpec((1, 1, 6 * D), lambda b, j: (b, 0, 0)),
            pl.BlockSpec((1, D), lambda b, j: (0, 0)),
            pl.BlockSpec((NE, D), lambda b, j: (0, 0)),
        ],
        out_specs=out_specs,
        out_shape=out_shapes,
        compiler_params=_cparams(("arbitrary", "arbitrary")),
        name="outproj",
    )(ret, dif, w_out, x, mod3, nfw, rw_t)


def _pool_kernel(x_ref, prev_ref, next_ref, mod_ref, nmw_ref, pw_ref, ps_ref, nfw_ref, rw_ref,
                 x1_ref, hp_ref, logit_ref):
    i = pl.program_id(1)
    last = pl.num_programs(1) - 1
    mod = mod_ref[0]
    sh1 = mod[:, 0:D]
    sc1 = mod[:, D:2 * D]

    def modnorm(v):
        return (_rms(v) * nmw_ref[...]) * (1.0 + sc1) + sh1

    x = x_ref[0]
    hc = modnorm(x)
    zeros = jnp.zeros((POOL_MARGIN - HALO, D), F32)
    ext = jnp.concatenate([zeros, jnp.where(i > 0, modnorm(prev_ref[0]), 0.0), hc,
                           jnp.where(i < last, modnorm(next_ref[0]), 0.0), zeros], axis=0)
    n_ext = ext.shape[0]
    pos = i * TM_POOL + lax.broadcasted_iota(I32, (TM_POOL, 1), 0)
    mixed = []
    run = ext
    for gi, w in enumerate(POOL_WINDOWS):
        left = w // 2
        right = w - 1 - left
        cols = slice(gi * PG, (gi + 1) * PG)
        run = run + pltpu.roll(run, n_ext - w // 2, axis=0)
        tot = run[POOL_MARGIN - left:POOL_MARGIN - left + TM_POOL, 0:PG]
        if gi + 1 < len(POOL_WINDOWS):
            run = run[:, PG:]
        cnt = (jnp.minimum(pos + right + 1, T) - jnp.maximum(pos - left, 0)).astype(F32)
        pooled = (tot * (1.0 / cnt) - hc[:, cols]).astype(BF16)
        mixed.append(jnp.dot(pooled, pw_ref[gi], preferred_element_type=F32))
    mixed = jnp.concatenate(mixed, axis=1) * ps_ref[...]
    x1 = x + mod[:, 2 * D:3 * D] * mixed
    x1_ref[0] = x1
    _ffn_prologue(x1, 0, mod, nfw_ref, rw_ref, hp_ref, logit_ref)


def _pool_layer(x, mod3, nmw, pool_w, pool_scale, nfw, rw_t):
    ni = T // TM_POOL
    hb = TM_POOL // HALO
    out_specs, out_shapes = _ffn_out_specs(TM_POOL, ni)
    return pl.pallas_call(
        _pool_kernel,
        grid=(B, ni),
        in_specs=[
            pl.BlockSpec((1, TM_POOL, D), lambda b, i: (b, i, 0)),
            pl.BlockSpec((1, HALO, D), lambda b, i: (b, jnp.maximum(i * hb - 1, 0), 0)),
            pl.BlockSpec((1, HALO, D), lambda b, i: (b, jnp.minimum((i + 1) * hb, T // HALO - 1), 0)),
            pl.BlockSpec((1, 1, 6 * D), lambda b, i: (b, 0, 0)),
            pl.BlockSpec((1, D), lambda b, i: (0, 0)),
            pl.BlockSpec((len(POOL_WINDOWS), PG, PG), lambda b, i: (0, 0, 0)),
            pl.BlockSpec((1, D), lambda b, i: (0, 0)),
            pl.BlockSpec((1, D), lambda b, i: (0, 0)),
            pl.BlockSpec((NE, D), lambda b, i: (0, 0)),
        ],
        out_specs=out_specs,
        out_shape=out_shapes,
        compiler_params=_cparams(("arbitrary", "arbitrary")),
        name="pool_layer",
    )(x, x, x, mod3, nmw, pool_w, pool_scale, nfw, rw_t)


def _tile_copy(src_ref, dst_ref, sem, s, d, rows=1):
    s0 = pl.multiple_of(s * SLAB, SLAB)
    d0 = pl.multiple_of(d * SLAB, SLAB)
    return pltpu.make_async_copy(src_ref.at[pl.ds(s0, rows * SLAB)], dst_ref.at[pl.ds(d0, rows * SLAB)], sem)


def _issue_tile_copies(idx_ref, base, rows, start_one):
    def group(g, carry):
        r0 = g * PERM_UNROLL
        ids = [idx_ref[base + r0 + u] for u in range(PERM_UNROLL)]
        for u in range(PERM_UNROLL):
            start_one(r0 + u, ids[u], u % 2)
        return carry

    lax.fori_loop(0, rows // PERM_UNROLL, group, 0)


def _dispatch_kernel(dest_ref, src_ref, dst_ref, zero_ref, sem, pad_sem):
    base = pl.program_id(0) * TM_PERM

    @pl.when(pl.program_id(0) == 0)
    def _():
        zero_ref[...] = jnp.zeros_like(zero_ref)
        pad = _tile_copy(zero_ref, dst_ref, pad_sem, 0, N, TM_MOE)
        pad.start()
        pad.wait()

    def start_one(r, d, priority):
        _tile_copy(src_ref, dst_ref, sem, r, d).start(priority=priority)

    _issue_tile_copies(dest_ref, base, TM_PERM, start_one)
    _tile_copy(src_ref, dst_ref, sem, 0, 0, TM_PERM).wait()


def _dispatch(dest, src):
    return pl.pallas_call(
        _dispatch_kernel,
        grid_spec=pltpu.PrefetchScalarGridSpec(
            num_scalar_prefetch=1,
            grid=(N // TM_PERM,),
            in_specs=[pl.BlockSpec((TM_PERM * SLAB, 128), lambda i, dest: (i, 0))],
            out_specs=pl.BlockSpec(memory_space=pl.ANY),
            scratch_shapes=[pltpu.VMEM((TM_MOE * SLAB, 128), F32), pltpu.SemaphoreType.DMA(()),
                            pltpu.SemaphoreType.DMA(())],
        ),
        out_shape=jax.ShapeDtypeStruct(((N + TM_MOE) * SLAB, 128), src.dtype),
        compiler_params=_cparams(("arbitrary",)),
        name="dispatch",
    )(dest, src)


def _combine_kernel(dest_ref, x_ref, ys_ref, mod_ref, fw_ref, o_ref, ybuf_ref, sem, *, final):
    i = pl.program_id(0)
    n = pl.num_programs(0)
    slot = i % 2

    def gather(step, to_slot):
        def start_one(r, d, priority):
            _tile_copy(ys_ref, ybuf_ref.at[to_slot], sem.at[to_slot], d, r).start(priority=priority)

        _issue_tile_copies(dest_ref, step * TM_COMB, TM_COMB, start_one)

    @pl.when(i == 0)
    def _():
        gather(0, 0)

    @pl.when(i + 1 < n)
    def _():
        gather(i + 1, 1 - slot)

    _tile_copy(ys_ref, ybuf_ref.at[slot], sem.at[slot], 0, 0, TM_COMB).wait()
    out = x_ref[...] + mod_ref[0][:, 5 * D:6 * D] * _load_slabs(ybuf_ref.at[slot], TM_COMB)
    if final:
        out = _rms(out) * fw_ref[...]
    o_ref[...] = out


def _combine(dest, x1, ys, mod3, final_w, final):
    per_b = T // TM_COMB
    return pl.pallas_call(
        functools.partial(_combine_kernel, final=final),
        grid_spec=pltpu.PrefetchScalarGridSpec(
            num_scalar_prefetch=1,
            grid=(N // TM_COMB,),
            in_specs=[
                pl.BlockSpec((TM_COMB, D), lambda i, dest: (i, 0)),
                pl.BlockSpec(memory_space=pl.ANY),
                pl.BlockSpec((1, 1, 6 * D), lambda i, dest: (i // per_b, 0, 0)),
                pl.BlockSpec((1, D), lambda i, dest: (0, 0)),
            ],
            out_specs=pl.BlockSpec((TM_COMB, D), lambda i, dest: (i, 0)),
            scratch_shapes=[pltpu.VMEM((2, TM_COMB * SLAB, 128), F32), pltpu.SemaphoreType.DMA((2,))],
        ),
        out_shape=jax.ShapeDtypeStruct((N, D), F32),
        compiler_params=_cparams(("arbitrary",)),
        name="combine",
    )(dest, x1.reshape(N, D), ys, mod3, final_w)


def _moe_kernel(chunk_ref, rw_ref, xs_hbm, wg_hbm, wu_hbm, wd_hbm, ys_hbm,
                xbuf_ref, obuf_ref,
                sga_ref, sua_ref, sda_ref, sgb_ref, sub_ref, sdb_ref,
                ga_ref, ua_ref, da_ref, gb_ref, ub_ref, db_ref, wsem, xsem, osem, *, layer):
    row = lambda r: (lambda m: chunk_ref[r, m])
    start_of, n = row(0), chunk_ref[7, 0]
    slots = ((row(1), row(3), row(5), (sga_ref, sua_ref, sda_ref), (ga_ref, ua_ref, da_ref)),
             (row(2), row(4), row(6), (sgb_ref, sub_ref, sdb_ref), (gb_ref, ub_ref, db_ref)))

    def fetch(w_slot, expert):
        stage = slots[w_slot][3]
        return [pltpu.make_async_copy(w.at[layer, expert], s, wsem.at[w_slot])
                for w, s in zip((wg_hbm, wu_hbm, wd_hbm), stage)]

    def x_copy(step, buf):
        return _tile_copy(xs_hbm, xbuf_ref.at[buf], xsem.at[buf], start_of(step), 0, TM_MOE)

    def o_copy(step, buf):
        return _tile_copy(obuf_ref.at[buf], ys_hbm, osem.at[buf], 0, start_of(step), TM_MOE)

    @pl.when(pl.program_id(0) == 0)
    def _():
        obuf_ref[1] = jnp.zeros_like(obuf_ref[1])
        pad = _tile_copy(obuf_ref.at[1], ys_hbm, osem.at[1], 0, N, TM_MOE)
        pad.start()
        pad.wait()
        x_copy(0, 0).start()

    def chunk(m, slot):
        @pl.when(m + 1 < n)
        def _():
            x_copy(m + 1, 1 - slot).start()

        for w_slot, (expert_of, changed_at, next_of, stage, work) in enumerate(slots):
            @pl.when(m == 0)
            def _():
                for cp in fetch(w_slot, expert_of(0)):
                    cp.start(priority=1)

            @pl.when(changed_at(m) == 1)
            def _():
                for cp in fetch(w_slot, 0):
                    cp.wait()
                for s, w in zip(stage, work):
                    w[...] = s[...].astype(BF16)

                @pl.when(next_of(m) >= 0)
                def _():
                    for cp in fetch(w_slot, next_of(m)):
                        cp.start(priority=1)

        x_copy(m, slot).wait()
        hf = _load_slabs(xbuf_ref.at[slot], TM_MOE)
        h = hf.astype(BF16)

        def up(w_ref):
            return jnp.dot(h, w_ref[...], preferred_element_type=F32)

        def down(g, u, d_ref):
            return jnp.dot((_silu(g) * u).astype(BF16), d_ref[...], preferred_element_type=F32)

        g_a, u_a, g_b, u_b = up(ga_ref), up(ua_ref), up(gb_ref), up(ub_ref)
        y_a = down(g_a, u_a, da_ref)
        y_b = down(g_b, u_b, db_ref)
        s_a = _sigmoid(jnp.sum(hf * rw_ref[pl.ds(slots[0][0](m), 1), :], axis=1, keepdims=True))
        s_b = _sigmoid(jnp.sum(hf * rw_ref[pl.ds(slots[1][0](m), 1), :], axis=1, keepdims=True))
        denom = s_a + s_b
        val = (s_a / denom) * y_a + (s_b / denom) * y_b

        @pl.when(m >= 1)
        def _():
            o_copy(m - 1, 1 - slot).wait()

        _store_slabs(obuf_ref.at[slot], val)
        o_copy(m, slot).start()

        @pl.when(m == n - 1)
        def _():
            o_copy(m, slot).wait()

    for j in range(2):
        m = pl.program_id(0) * 2 + j
        pl.when(m < n)(functools.partial(chunk, m, j))


def _moe_sorted(chunks, xs, rw_t, wg, wu, wd, layer):
    mats = lambda dt: [pltpu.VMEM((D, DE), dt), pltpu.VMEM((D, DE), dt), pltpu.VMEM((DE, D), dt)]
    hbm = pl.BlockSpec(memory_space=pl.ANY)
    return pl.pallas_call(
        functools.partial(_moe_kernel, layer=layer),
        grid_spec=pltpu.PrefetchScalarGridSpec(
            num_scalar_prefetch=1,
            grid=(N_UNITS // 2,),
            in_specs=[pl.BlockSpec((NE, D), lambda m, *_: (0, 0)), hbm, hbm, hbm, hbm],
            out_specs=hbm,
            scratch_shapes=([pltpu.VMEM((2, TM_MOE * SLAB, 128), F32), pltpu.VMEM((2, TM_MOE * SLAB, 128), F32)]
                            + mats(F32) + mats(F32) + mats(BF16) + mats(BF16)
                            + [pltpu.SemaphoreType.DMA((2,))] * 3),
        ),
        out_shape=jax.ShapeDtypeStruct(xs.shape, F32),
        compiler_params=_cparams(("arbitrary",)),
        name="moe_sorted",
    )(chunks, rw_t, xs, wg, wu, wd)


def _moe_layer(x1, hp, logits, rw_t, router_b, wg, wu, wd, layer, mod3, final_w, final):
    dest, chunks = _route_tokens(logits, router_b)
    dest = dest.reshape(N)
    xs = _dispatch(dest, hp)
    ys = _moe_sorted(chunks, xs, rw_t, wg, wu, wd, layer)
    return _combine(dest, x1, ys, mod3, final_w, final).reshape(B, T, D)


def _rope_tables():
    half = 16
    inv = ROPE_BASE ** (-np.arange(half, dtype=np.float64) / half)
    t = np.arange(T)
    ang_r = (t // GRID_W)[:, None] * inv[None, :]
    ang_c = (t % GRID_W)[:, None] * inv[None, :]
    ang = np.concatenate([ang_r, ang_r, ang_c, ang_c], axis=1)
    sign = np.tile(np.concatenate([-np.ones(half), np.ones(half)]), 2)
    cos = np.concatenate([np.ones((LC, 64)), np.cos(ang)], axis=0)
    sin = np.concatenate([np.zeros((LC, 64)), np.sin(ang) * sign[None, :]], axis=0)
    return (jnp.asarray(np.tile(cos, (1, 2)), dtype=F32), jnp.asarray(np.tile(sin, (1, 2)), dtype=F32))


def _permute_w_in(w):
    rq = w[:, 0:256].reshape(D, NH, DK)
    dq = w[:, 256:768]
    rg = w[:, 768:1280]
    rk = w[:, 1280:1536].reshape(D, NH, DK)
    rv = w[:, 1536:2048]
    dk = w[:, 2048:2560]
    dv = w[:, 2560:3072]
    qk = jnp.concatenate([rq, rk * (DK ** -0.5)], axis=2).reshape(D, NH * 2 * DK)
    return jnp.concatenate([qk, rv, rg, dq * (DK ** -0.5 * math.log2(math.e)), dk, dv], axis=1).astype(BF16)


def kernel(x, c, ctx, c_ctx, ada_w, ada_b, norm_mix_w, norm_ffn_w, w_in, w_out, ret_log_decay, diff_lambda,
           diff_subln_w, pool_w, pool_scale, router_w, router_b, moe_w_gate, moe_w_up, moe_w_down, final_norm_w):
    assert x.shape == (B, T, D) and ctx.shape == (B, LC, D) and ada_w.shape[0] == 2
    cc = jnp.concatenate([c, c_ctx[None, :], jnp.zeros((16 - B - 1, D), F32)], axis=0)
    mod = _ada_mod(cc, ada_w, ada_b)
    rw_t = router_w.T
    fw = final_norm_w.reshape(1, D)
    experts = (moe_w_gate, moe_w_up, moe_w_down)

    mod0 = mod[0].reshape(16, 1, 6 * D)
    cos_t, sin_t = _rope_tables()
    proj = _inproj(x, ctx, mod0, norm_mix_w[0:1], _permute_w_in(w_in[0]), cos_t, sin_t)
    ret = _retention(proj, ret_log_decay[0])
    lam_init = 0.8 - 0.6 * math.exp(-0.3 * 0)
    lv = diff_lambda[0]
    lam = jnp.exp(jnp.sum(lv[0] * lv[1])) - jnp.exp(jnp.sum(lv[2] * lv[3])) + lam_init
    dif = _diffattn(proj, lam.reshape(1), diff_subln_w[0:1], 1.0 - lam_init)
    x1, hp, logits = _outproj(ret, dif, w_out[0].astype(BF16), x, mod0, norm_ffn_w[0:1], rw_t)
    x2 = _moe_layer(x1, hp, logits, rw_t, router_b, *experts, 0, mod0, fw, False)

    mod1 = mod[1].reshape(16, 1, 6 * D)
    x3, hp, logits = _pool_layer(x2, mod1, norm_mix_w[1:2], pool_w[0].astype(BF16), pool_scale[0:1],
                                 norm_ffn_w[1:2], rw_t)
    return _moe_layer(x3, hp, logits, rw_t, router_b, *experts, 1, mod1, fw, True)
```

```python
import functools
import math

import jax
import jax.numpy as jnp
import numpy as np
from jax import lax
from jax.experimental import pallas as pl
from jax.experimental.pallas import tpu as pltpu

F32 = jnp.float32
BF16 = jnp.bfloat16
I32 = jnp.int32

D = 1024
B = 8
T = 2048
N = B * T
GRID_W = 64
LC = 256
EPS = 1e-6
ROPE_BASE = 10000.0
NH = 4
DK = 64
HV = 128
CH = 256
RB = LC + T
NCH = RB // CH
POOL_WINDOWS = (2, 4, 8, 16)
PG = D // len(POOL_WINDOWS)
NE = 16
NGRP = 4
EPG = NE // NGRP
DE = 512
IN_W = 3072
HALO = 8
POOL_MARGIN = 16
assert POOL_WINDOWS == tuple(2 ** (k + 1) for k in range(len(POOL_WINDOWS))) and POOL_WINDOWS[-1] <= POOL_MARGIN

PAIR_A = (0, 0, 0, 1, 1, 3)
PAIR_B = (1, 2, 3, 3, 2, 2)
NCLS = NGRP * len(PAIR_A)
SLAB = D // 128

TM_PROJ = 256
PROJ_SUB = 3
TM_OUT = 1024
SUB_OUT = 512
TQ_SUB = 8
SCORE_AHEAD = 1
TM_POOL = 512
TM_MOE = 256
N_UNITS = N // TM_MOE + NCLS
TM_PERM = 2048
TM_COMB = 256
PERM_UNROLL = 16
VMEM_LIMIT = 56 * 1024 * 1024


def _cparams(sem):
    return pltpu.CompilerParams(dimension_semantics=sem, vmem_limit_bytes=VMEM_LIMIT)


def _sigmoid(x):
    return 1.0 / (1.0 + jnp.exp(-x))


def _silu(x):
    return x * _sigmoid(x)


def _rms(x):
    return x * lax.rsqrt(jnp.mean(x * x, axis=-1, keepdims=True) + EPS)


def _dot_3pass(a, b, dims):
    a_hi = a.astype(BF16)
    b_hi = b.astype(BF16)
    a_lo = (a - a_hi.astype(F32)).astype(BF16)
    b_lo = (b - b_hi.astype(F32)).astype(BF16)

    def dot(x, y):
        return lax.dot_general(x, y, dims, preferred_element_type=F32)

    return dot(a_hi, b_hi) + (dot(a_lo, b_hi) + dot(a_hi, b_lo))


def _load_slabs(ref, rows):
    return jnp.concatenate([ref[pl.ds(s, rows, stride=SLAB), :] for s in range(SLAB)], axis=1)


def _store_slabs(ref, val, row0=0):
    rows = val.shape[0]
    for s in range(SLAB):
        ref[pl.ds(row0 * SLAB + s, rows, stride=SLAB), :] = val[:, s * 128:(s + 1) * 128]


def _ada_kernel(cc_ref, w_ref, b_ref, o_ref):
    s = _silu(cc_ref[...])
    o_ref[0] = _dot_3pass(s, w_ref[0], (((1,), (0,)), ((), ()))) + b_ref[0]


def _ada_mod(cc, ada_w, ada_b):
    depth = ada_w.shape[0]
    tn = 3072
    return pl.pallas_call(
        _ada_kernel,
        grid=(depth, 6 * D // tn),
        in_specs=[
            pl.BlockSpec((16, D), lambda l, n: (0, 0)),
            pl.BlockSpec((1, D, tn), lambda l, n: (l, 0, n)),
            pl.BlockSpec((1, 1, tn), lambda l, n: (l, 0, n)),
        ],
        out_specs=pl.BlockSpec((1, 16, tn), lambda l, n: (l, 0, n)),
        out_shape=jax.ShapeDtypeStruct((depth, 16, 6 * D), F32),
        compiler_params=_cparams(("arbitrary", "arbitrary")),
        name="ada_mod",
    )(cc, ada_w, ada_b.reshape(depth, 1, 6 * D))


def _rope(seg, cos, sin_signed, lo_mask):
    w = seg.shape[1]
    from_hi = pltpu.roll(seg, w - 16, axis=1)
    from_lo = pltpu.roll(seg, 16, axis=1)
    partner = jnp.where(lo_mask, from_hi, from_lo)
    reps = w // cos.shape[1]
    c = jnp.concatenate([cos] * reps, axis=1)
    s = jnp.concatenate([sin_signed] * reps, axis=1)
    return seg * c + partner * s


def _inproj_kernel(*refs):
    x_refs = refs[:PROJ_SUB]
    c_ref, mod_ref, cmod_ref, nw_ref, w_ref, cos_ref, sin_ref, o_ref = refs[PROJ_SUB:]
    is_ctx = pl.program_id(1) == 0
    parts = []
    for s in range(PROJ_SUB):
        xt = x_refs[s][0]
        sh = mod_ref[0, :, 0:D]
        sc = mod_ref[0, :, D:2 * D]
        if s == 0:
            xt = jnp.where(is_ctx, c_ref[0], xt)
            sh = jnp.where(is_ctx, cmod_ref[0, :, 0:D], sh)
            sc = jnp.where(is_ctx, cmod_ref[0, :, D:2 * D], sc)
        parts.append(((_rms(xt) * nw_ref[...]) * (1.0 + sc) + sh).astype(BF16))
    hb = jnp.concatenate(parts, axis=0)
    lane = lax.broadcasted_iota(I32, (PROJ_SUB * TM_PROJ, 512), 1)
    lo_mask = (lane % 32) < 16
    cos = cos_ref[...]
    sin = sin_ref[...]

    def project(g):
        return jnp.dot(hb, w_ref[:, g * 512:(g + 1) * 512], preferred_element_type=F32)

    seg = project(0)
    for g in range(6):
        seg_next = project(g + 1) if g + 1 < 6 else None
        if g in (0, 3, 4):
            seg = _rope(seg, cos, sin, lo_mask)
        o_ref[0, :, g * 512:(g + 1) * 512] = seg.astype(BF16)
        seg = seg_next


def _inproj(x, ctx, mod3, norm_w, w_perm, cos_t, sin_t):
    tm = PROJ_SUB * TM_PROJ
    nj = RB // tm

    def x_map(s, b, j):
        return (b, jnp.maximum(PROJ_SUB * j + s - LC // TM_PROJ, 0), 0)

    return pl.pallas_call(
        _inproj_kernel,
        grid=(B, nj),
        in_specs=[
            *[pl.BlockSpec((1, TM_PROJ, D), functools.partial(x_map, s)) for s in range(PROJ_SUB)],
            pl.BlockSpec((1, LC, D), lambda b, j: (b, 0, 0)),
            pl.BlockSpec((1, 1, 2 * D), lambda b, j: (b, 0, 0)),
            pl.BlockSpec((1, 1, 2 * D), lambda b, j: (B, 0, 0)),
            pl.BlockSpec((1, D), lambda b, j: (0, 0)),
            pl.BlockSpec((D, IN_W), lambda b, j: (0, 0)),
            pl.BlockSpec((tm, 128), lambda b, j: (j, 0)),
            pl.BlockSpec((tm, 128), lambda b, j: (j, 0)),
        ],
        out_specs=pl.BlockSpec((1, tm, IN_W), lambda b, j: (b, j, 0)),
        out_shape=jax.ShapeDtypeStruct((B, RB, IN_W), BF16),
        compiler_params=_cparams(("arbitrary", "arbitrary")),
        name="inproj",
    )(*([x] * PROJ_SUB), ctx, mod3, mod3, norm_w, w_perm, cos_t, sin_t)


def _retention_kernel(ld_ref, qk_ref, v_ref, g_ref, o_ref, st_ref, kdec_ref, qdec_ref, mask_ref, cdec_ref):
    h = pl.program_id(0)
    lane = lax.broadcasted_iota(I32, (CH, 128), 1)
    fwd_lane = lane < DK

    @pl.when(pl.program_id(1) == 0)
    def _():
        lgf = ld_ref[0, h]
        lgb = ld_ref[1, h]
        pos = lax.broadcasted_iota(I32, (CH, 128), 0).astype(F32)
        kdec_ref[...] = jnp.where(fwd_lane, jnp.exp(lgf * (CH - 1 - pos)), jnp.exp(lgb * pos))
        qdec_ref[...] = jnp.where(fwd_lane, jnp.exp(lgf * (pos + 1.0)), jnp.exp(lgb * (CH - pos)))
        ii = lax.broadcasted_iota(I32, (CH, CH), 0)
        jj = lax.broadcasted_iota(I32, (CH, CH), 1)
        gap = (ii - jj).astype(F32)
        mask_ref[...] = (jnp.where(gap >= 0, jnp.exp(lgf * jnp.maximum(gap, 0.0)), 0.0)
                         + jnp.where(gap <= 0, jnp.exp(lgb * jnp.maximum(-gap, 0.0)), 0.0))
        ones = jnp.ones((DK, 128), F32)
        cdec_ref[0:DK, :] = jnp.exp(lgf * CH * ones)
        cdec_ref[DK:, :] = jnp.exp(lgb * CH * ones)

    kdec = kdec_ref[...]
    qdec = qdec_ref[...]
    mask = mask_ref[...]
    cf = cdec_ref[0:DK, :]
    cb = cdec_ref[DK:, :]

    def chunk(n):
        a = qk_ref[0, n * CH:(n + 1) * CH, :].astype(F32)
        swapped = pltpu.roll(a, DK, axis=1)
        return a, swapped

    kv = []
    for n in range(NCH):
        a, swapped = chunk(n)
        kk = jnp.where(fwd_lane, swapped, a)
        kb = (kk * kdec).astype(BF16)
        vn = v_ref[0, n * CH:(n + 1) * CH, :]
        kv.append(lax.dot_general(kb, vn, (((0,), (0,)), ((), ())), preferred_element_type=F32))
    sf = kv[0][:DK]
    for n in range(1, NCH):
        st_ref[n, 0:DK, :] = sf
        sf = cf * sf + kv[n][:DK]
    sb = kv[0][DK:]
    for n in range(NCH - 1, 0, -1):
        st_ref[n, DK:2 * DK, :] = sb
        sb = cb * sb + kv[n][DK:]

    def chunk_scores(n):
        a, swapped = chunk(n)
        q = a[:, :DK].astype(BF16)
        k = swapped[:, :DK].astype(BF16)
        return a, swapped, lax.dot_general(q, k, (((1,), (1,)), ((), ())), preferred_element_type=F32)

    nxt = chunk_scores(1)
    for n in range(1, NCH):
        a, swapped, scores = nxt
        nxt = chunk_scores(n + 1) if n + 1 < NCH else None
        p = (scores * mask).astype(BF16)
        vn = v_ref[0, n * CH:(n + 1) * CH, :]
        qq = jnp.where(fwd_lane, a, swapped)
        qd = (qq * qdec).astype(BF16)
        o = (jnp.dot(p, vn, preferred_element_type=F32)
             + jnp.dot(qd, st_ref[n].astype(BF16), preferred_element_type=F32))
        gate = g_ref[0, n * CH:(n + 1) * CH, :].astype(F32)
        o_ref[0, (n - 1) * CH:n * CH, :] = (_rms(o) * _silu(gate)).astype(BF16)


def _retention(proj, log_decay):
    return pl.pallas_call(
        _retention_kernel,
        grid=(NH, B),
        in_specs=[
            pl.BlockSpec(memory_space=pltpu.SMEM),
            pl.BlockSpec((1, RB, 128), lambda h, b: (b, 0, h)),
            pl.BlockSpec((1, RB, 128), lambda h, b: (b, 0, NH + h)),
            pl.BlockSpec((1, RB, 128), lambda h, b: (b, 0, 2 * NH + h)),
        ],
        out_specs=pl.BlockSpec((1, T, 128), lambda h, b: (b, 0, h)),
        out_shape=jax.ShapeDtypeStruct((B, T, NH * HV), BF16),
        scratch_shapes=[pltpu.VMEM((NCH, 128, 128), F32), pltpu.VMEM((CH, 128), F32), pltpu.VMEM((CH, 128), F32),
                        pltpu.VMEM((CH, CH), F32), pltpu.VMEM((2 * DK, 128), F32)],
        compiler_params=_cparams(("arbitrary", "arbitrary")),
        name="retention",
    )(log_decay, proj, proj, proj)


def _diffattn_kernel(lam_ref, *refs, out_scale):
    q_refs = refs[:TQ_SUB]
    k_ref, v_ref, sw_ref, o_ref = refs[TQ_SUB:]
    lam = lam_ref[0]
    k = k_ref[0]
    v = v_ref[0]
    nt = (((1,), (1,)), ((), ()))

    def scores(qh):
        return lax.dot_general(qh, k, nt, preferred_element_type=F32)

    v_ones = jnp.concatenate([v, jnp.ones_like(v)], axis=1)

    def values(s):
        e = jnp.exp2(s - jnp.max(s, axis=-1, keepdims=True))
        ol = jnp.dot(e.astype(BF16), v_ones, preferred_element_type=F32)
        return ol[:, :HV], ol[:, HV:HV + 1]

    halves = []
    for i in range(TQ_SUB):
        q = q_refs[i][0]
        lane = lax.broadcasted_iota(I32, q.shape, 1)
        zero = jnp.zeros_like(q)
        halves += [jnp.where(lane < DK, q, zero), jnp.where(lane >= DK, q, zero)]
    outs = []
    ahead = [scores(h) for h in halves[:SCORE_AHEAD]]
    for c in range(len(halves)):
        if c + SCORE_AHEAD < len(halves):
            ahead.append(scores(halves[c + SCORE_AHEAD]))
        outs.append(values(ahead.pop(0)))
    for i in range(TQ_SUB):
        (o1, l1), (o2, l2) = outs[2 * i], outs[2 * i + 1]
        o = o1 / l1 - o2 * (lam / l2)
        o_ref[0, i * TM_PROJ:(i + 1) * TM_PROJ, :] = (_rms(o) * sw_ref[...] * out_scale).astype(BF16)


def _diffattn(proj, lam, subln_w, out_scale):
    tq = TQ_SUB * TM_PROJ
    nq = T // tq

    def q_map(i, b, h, j):
        return (b, LC // TM_PROJ + j * TQ_SUB + i, 3 * NH + h)

    return pl.pallas_call(
        functools.partial(_diffattn_kernel, out_scale=out_scale),
        grid=(B, NH, nq),
        in_specs=[
            pl.BlockSpec(memory_space=pltpu.SMEM),
            *[pl.BlockSpec((1, TM_PROJ, 128), functools.partial(q_map, i)) for i in range(TQ_SUB)],
            pl.BlockSpec((1, RB, 128), lambda b, h, j: (b, 0, 4 * NH + h)),
            pl.BlockSpec((1, RB, 128), lambda b, h, j: (b, 0, 5 * NH + h)),
            pl.BlockSpec((1, HV), lambda b, h, j: (0, 0)),
        ],
        out_specs=pl.BlockSpec((1, tq, 128), lambda b, h, j: (b, j, h)),
        out_shape=jax.ShapeDtypeStruct((B, T, NH * HV), BF16),
        compiler_params=_cparams(("arbitrary", "arbitrary", "arbitrary")),
        name="diffattn",
    )(lam, *([proj] * TQ_SUB), proj, proj, subln_w)


def _route(bz):
    grp = []
    for g in range(NGRP):
        m = bz[g * EPG:(g + 1) * EPG]
        best = None
        for i in range(EPG):
            for k in range(i + 1, EPG):
                pair = m[i] + m[k]
                best = pair if best is None else jnp.maximum(best, pair)
        grp.append(best)
    gbest = grp[0]
    gsel = jnp.zeros_like(gbest, dtype=I32)
    for g in range(1, NGRP):
        better = grp[g] > gbest
        gsel = jnp.where(better, g, gsel)
        gbest = jnp.where(better, grp[g], gbest)
    cb = [bz[i] for i in range(EPG)]
    for g in range(1, NGRP):
        pick = gsel == g
        cb = [jnp.where(pick, bz[g * EPG + i], cb[i]) for i in range(EPG)]
    i1 = jnp.zeros_like(gsel)
    b1 = cb[0]
    for i in range(1, EPG):
        better = cb[i] > b1
        i1 = jnp.where(better, i, i1)
        b1 = jnp.where(better, cb[i], b1)
    neg = jnp.full_like(b1, -jnp.inf)
    rest = [jnp.where(i1 == i, neg, cb[i]) for i in range(EPG)]
    i2 = jnp.zeros_like(gsel)
    b2 = rest[0]
    for i in range(1, EPG):
        better = rest[i] > b2
        i2 = jnp.where(better, i, i2)
        b2 = jnp.where(better, rest[i], b2)
    lo = jnp.minimum(i1, i2)
    hi = jnp.maximum(i1, i2)
    code = lo * EPG + hi
    pair = jnp.full_like(gsel, len(PAIR_A) - 1)
    for p in range(len(PAIR_A) - 1):
        a, b = min(PAIR_A[p], PAIR_B[p]), max(PAIR_A[p], PAIR_B[p])
        pair = jnp.where(code == a * EPG + b, p, pair)
    return gsel * len(PAIR_A) + pair


def _ffn_prologue(x1, row0, mod, nfw_ref, rw_ref, hp_ref, logit_ref):
    rows = x1.shape[0]
    sh2 = mod[:, 3 * D:4 * D]
    sc2 = mod[:, 4 * D:5 * D]
    h2 = (_rms(x1) * nfw_ref[...]) * (1.0 + sc2) + sh2
    _store_slabs(hp_ref, h2, row0)
    logit_ref[:, row0:row0 + rows] = _dot_3pass(rw_ref[...], h2, (((1,), (1,)), ((), ())))


def _ffn_out_specs(tm, n_tiles_per_b):
    specs = [
        pl.BlockSpec((1, tm, D), lambda b, j: (b, j, 0)),
        pl.BlockSpec((tm * SLAB, 128), lambda b, j: (b * n_tiles_per_b + j, 0)),
        pl.BlockSpec((NE, tm), lambda b, j: (0, b * n_tiles_per_b + j)),
    ]
    shapes = [
        jax.ShapeDtypeStruct((B, T, D), F32),
        jax.ShapeDtypeStruct((N * SLAB, 128), F32),
        jax.ShapeDtypeStruct((NE, N), F32),
    ]
    return specs, shapes


def _route_kernel(logit_ref, bias_ref, dest_ref, chunk_ref):
    r = logit_ref.shape[1]
    cls = _route([_sigmoid(logit_ref[e]) + bias_ref[e] for e in range(NE)])
    lane_incl = (lax.broadcasted_iota(I32, (128, 128), 0) <= lax.broadcasted_iota(I32, (128, 128), 1)).astype(BF16)
    rows_before = (lax.broadcasted_iota(I32, (r, r), 1) < lax.broadcasted_iota(I32, (r, r), 0)).astype(BF16)
    dest = jnp.zeros((r, 128), F32)
    start = jnp.zeros((1, 128), F32)
    ends = []
    for c in range(NCLS):
        onehot = jnp.where(cls == c, 1.0, 0.0)
        in_row = jnp.dot(onehot.astype(BF16), lane_incl, preferred_element_type=F32)
        row_tot = jnp.broadcast_to(in_row[:, 127:128], (r, 128))
        above = jnp.dot(rows_before, row_tot.astype(BF16), preferred_element_type=F32)
        dest = dest + onehot * (start + above + in_row - 1.0)
        start = start + jnp.sum(row_tot, axis=0, keepdims=True)
        ends.append(start)
    dest_ref[...] = dest.astype(I32)
    chunk_ref[...] = _chunk_list(ends)


def _chunk_list(ends):
    lane = lax.broadcasted_iota(I32, (1, 128), 1).astype(F32)
    offs = [jnp.zeros((1, 128), F32)] + ends[:-1]
    per = [jnp.floor((e - o + (TM_MOE - 1)) * (1.0 / TM_MOE)) for e, o in zip(ends, offs)]
    cum = []
    total = jnp.zeros((1, 128), F32)
    for p in per:
        total = total + p
        cum.append(total)
    m = jnp.minimum(lane, total - 1.0)
    cls = jnp.zeros((1, 128), F32)
    for c in range(NCLS):
        cls = cls + jnp.where(cum[c] <= m, 1.0, 0.0)
    start = jnp.zeros((1, 128), F32)
    for c in range(NCLS):
        start = jnp.where(cls == c, offs[c] + TM_MOE * (m - (cum[c] - per[c])), start)
    n_pair = len(PAIR_A)
    grp = jnp.zeros((1, 128), F32)
    for g in range(1, NGRP):
        grp = grp + jnp.where(cls >= g * n_pair, 1.0, 0.0)
    pair = cls - n_pair * grp
    loc_a = jnp.zeros((1, 128), F32)
    loc_b = jnp.zeros((1, 128), F32)
    for p in range(n_pair):
        loc_a = jnp.where(pair == p, float(PAIR_A[p]), loc_a)
        loc_b = jnp.where(pair == p, float(PAIR_B[p]), loc_b)
    sub = lax.broadcasted_iota(I32, (128, 128), 0).astype(F32)
    lan = lax.broadcasted_iota(I32, (128, 128), 1).astype(F32)
    none = 1000.0

    def changes(e):
        chg = jnp.where((lane == 0.0) | (e != pltpu.roll(e, 1, axis=1)), 1.0, 0.0)
        chg_col = jnp.sum(jnp.where(sub == lan, chg, 0.0), axis=1, keepdims=True)
        e_col = jnp.sum(jnp.where(sub == lan, e, 0.0), axis=1, keepdims=True)
        nxt_at = jnp.min(jnp.where((sub > lan) & (chg_col == 1.0), sub, none), axis=0, keepdims=True)
        nxt_e = jnp.sum(jnp.where(sub == nxt_at, e_col, 0.0), axis=0, keepdims=True)
        return chg, jnp.where(nxt_at < none, nxt_e, -1.0)

    e_a = EPG * grp + loc_a
    e_b = EPG * grp + loc_b
    chg_a, nxt_a = changes(e_a)
    chg_b, nxt_b = changes(e_b)
    return jnp.concatenate([start, e_a, e_b, chg_a, chg_b, nxt_a, nxt_b, total], axis=0).astype(I32)


def _route_tokens(logits_t, router_b):
    r = N // 128
    return pl.pallas_call(
        _route_kernel,
        in_specs=[pl.BlockSpec((NE, r, 128), lambda: (0, 0, 0)), pl.BlockSpec(memory_space=pltpu.SMEM)],
        out_specs=[pl.BlockSpec((r, 128), lambda: (0, 0)), pl.BlockSpec((8, 128), lambda: (0, 0))],
        out_shape=[jax.ShapeDtypeStruct((r, 128), I32), jax.ShapeDtypeStruct((8, 128), I32)],
        compiler_params=pltpu.CompilerParams(vmem_limit_bytes=VMEM_LIMIT),
        name="route",
    )(logits_t.reshape(NE, r, 128), router_b)


def _outproj_kernel(ret_ref, dif_ref, w_ref, x_ref, mod_ref, nfw_ref, rw_ref, x1_ref, hp_ref, logit_ref):
    mod = mod_ref[0]

    def mix(r0):
        rows = slice(r0, r0 + SUB_OUT)
        return (jnp.dot(ret_ref[0, rows, :], w_ref[0:NH * HV, :], preferred_element_type=F32)
                + jnp.dot(dif_ref[0, rows, :], w_ref[NH * HV:, :], preferred_element_type=F32))

    mx = mix(0)
    for r0 in range(0, TM_OUT, SUB_OUT):
        mx_next = mix(r0 + SUB_OUT) if r0 + SUB_OUT < TM_OUT else None
        x1 = x_ref[0, r0:r0 + SUB_OUT, :] + mod[:, 2 * D:3 * D] * mx
        x1_ref[0, r0:r0 + SUB_OUT, :] = x1
        _ffn_prologue(x1, r0, mod, nfw_ref, rw_ref, hp_ref, logit_ref)
        mx = mx_next


def _outproj(ret, dif, w_out, x, mod3, nfw, rw_t):
    nj = T // TM_OUT
    out_specs, out_shapes = _ffn_out_specs(TM_OUT, nj)
    return pl.pallas_call(
        _outproj_kernel,
        grid=(B, nj),
        in_specs=[
            pl.BlockSpec((1, TM_OUT, NH * HV), lambda b, j: (b, j, 0)),
            pl.BlockSpec((1, TM_OUT, NH * HV), lambda b, j: (b, j, 0)),
            pl.BlockSpec((2 * NH * HV, D), lambda b, j: (0, 0)),
            pl.BlockSpec((1, TM_OUT, D), lambda b, j: (b, j, 0)),
            pl.BlockSpec((1, 1, 6 * D), lambda b, j: (b, 0, 0)),
            pl.BlockSpec((1, D), lambda b, j: (0, 0)),
            pl.BlockSpec((NE, D), lambda b, j: (0, 0)),
        ],
        out_specs=out_specs,
        out_shape=out_shapes,
        compiler_params=_cparams(("arbitrary", "arbitrary")),
        name="outproj",
    )(ret, dif, w_out, x, mod3, nfw, rw_t)


def _pool_kernel(x_ref, prev_ref, next_ref, mod_ref, nmw_ref, pw_ref, ps_ref, nfw_ref, rw_ref,
                 x1_ref, hp_ref, logit_ref):
    i = pl.program_id(1)
    last = pl.num_programs(1) - 1
    mod = mod_ref[0]
    sh1 = mod[:, 0:D]
    sc1 = mod[:, D:2 * D]

    def modnorm(v):
        return (_rms(v) * nmw_ref[...]) * (1.0 + sc1) + sh1

    x = x_ref[0]
    hc = modnorm(x)
    zeros = jnp.zeros((POOL_MARGIN - HALO, D), F32)
    ext = jnp.concatenate([zeros, jnp.where(i > 0, modnorm(prev_ref[0]), 0.0), hc,
                           jnp.where(i < last, modnorm(next_ref[0]), 0.0), zeros], axis=0)
    n_ext = ext.shape[0]
    pos = i * TM_POOL + lax.broadcasted_iota(I32, (TM_POOL, 1), 0)
    mixed = []
    run = ext
    for gi, w in enumerate(POOL_WINDOWS):
        left = w // 2
        right = w - 1 - left
        cols = slice(gi * PG, (gi + 1) * PG)
        run = run + pltpu.roll(run, n_ext - w // 2, axis=0)
        tot = run[POOL_MARGIN - left:POOL_MARGIN - left + TM_POOL, 0:PG]
        if gi + 1 < len(POOL_WINDOWS):
            run = run[:, PG:]
        cnt = (jnp.minimum(pos + right + 1, T) - jnp.maximum(pos - left, 0)).astype(F32)
        pooled = (tot * (1.0 / cnt) - hc[:, cols]).astype(BF16)
        mixed.append(jnp.dot(pooled, pw_ref[gi], preferred_element_type=F32))
    mixed = jnp.concatenate(mixed, axis=1) * ps_ref[...]
    x1 = x + mod[:, 2 * D:3 * D] * mixed
    x1_ref[0] = x1
    _ffn_prologue(x1, 0, mod, nfw_ref, rw_ref, hp_ref, logit_ref)


def _pool_layer(x, mod3, nmw, pool_w, pool_scale, nfw, rw_t):
    ni = T // TM_POOL
    hb = TM_POOL // HALO
    out_specs, out_shapes = _ffn_out_specs(TM_POOL, ni)
    return pl.pallas_call(
        _pool_kernel,
        grid=(B, ni),
        in_specs=[
            pl.BlockSpec((1, TM_POOL, D), lambda b, i: (b, i, 0)),
            pl.BlockSpec((1, HALO, D), lambda b, i: (b, jnp.maximum(i * hb - 1, 0), 0)),
            pl.BlockSpec((1, HALO, D), lambda b, i: (b, jnp.minimum((i + 1) * hb, T // HALO - 1), 0)),
            pl.BlockSpec((1, 1, 6 * D), lambda b, i: (b, 0, 0)),
            pl.BlockSpec((1, D), lambda b, i: (0, 0)),
            pl.BlockSpec((len(POOL_WINDOWS), PG, PG), lambda b, i: (0, 0, 0)),
            pl.BlockSpec((1, D), lambda b, i: (0, 0)),
            pl.BlockSpec((1, D), lambda b, i: (0, 0)),
            pl.BlockSpec((NE, D), lambda b, i: (0, 0)),
        ],
        out_specs=out_specs,
        out_shape=out_shapes,
        compiler_params=_cparams(("arbitrary", "arbitrary")),
        name="pool_layer",
    )(x, x, x, mod3, nmw, pool_w, pool_scale, nfw, rw_t)


def _tile_copy(src_ref, dst_ref, sem, s, d, rows=1):
    s0 = pl.multiple_of(s * SLAB, SLAB)
    d0 = pl.multiple_of(d * SLAB, SLAB)
    return pltpu.make_async_copy(src_ref.at[pl.ds(s0, rows * SLAB)], dst_ref.at[pl.ds(d0, rows * SLAB)], sem)


def _issue_tile_copies(idx_ref, base, rows, start_one):
    def group(g, carry):
        r0 = g * PERM_UNROLL
        ids = [idx_ref[base + r0 + u] for u in range(PERM_UNROLL)]
        for u in range(PERM_UNROLL):
            start_one(r0 + u, ids[u], u % 2)
        return carry

    lax.fori_loop(0, rows // PERM_UNROLL, group, 0)


def _dispatch_kernel(dest_ref, src_ref, dst_ref, zero_ref, sem, pad_sem):
    base = pl.program_id(0) * TM_PERM

    @pl.when(pl.program_id(0) == 0)
    def _():
        zero_ref[...] = jnp.zeros_like(zero_ref)
        pad = _tile_copy(zero_ref, dst_ref, pad_sem, 0, N, TM_MOE)
        pad.start()
        pad.wait()

    def start_one(r, d, priority):
        _tile_copy(src_ref, dst_ref, sem, r, d).start(priority=priority)

    _issue_tile_copies(dest_ref, base, TM_PERM, start_one)
    _tile_copy(src_ref, dst_ref, sem, 0, 0, TM_PERM).wait()


def _dispatch(dest, src):
    return pl.pallas_call(
        _dispatch_kernel,
        grid_spec=pltpu.PrefetchScalarGridSpec(
            num_scalar_prefetch=1,
            grid=(N // TM_PERM,),
            in_specs=[pl.BlockSpec((TM_PERM * SLAB, 128), lambda i, dest: (i, 0))],
            out_specs=pl.BlockSpec(memory_space=pl.ANY),
            scratch_shapes=[pltpu.VMEM((TM_MOE * SLAB, 128), F32), pltpu.SemaphoreType.DMA(()),
                            pltpu.SemaphoreType.DMA(())],
        ),
        out_shape=jax.ShapeDtypeStruct(((N + TM_MOE) * SLAB, 128), src.dtype),
        compiler_params=_cparams(("arbitrary",)),
        name="dispatch",
    )(dest, src)


def _combine_kernel(dest_ref, x_ref, ys_ref, mod_ref, fw_ref, o_ref, ybuf_ref, sem, *, final):
    i = pl.program_id(0)
    n = pl.num_programs(0)
    slot = i % 2

    def gather(step, to_slot):
        def start_one(r, d, priority):
            _tile_copy(ys_ref, ybuf_ref.at[to_slot], sem.at[to_slot], d, r).start(priority=priority)

        _issue_tile_copies(dest_ref, step * TM_COMB, TM_COMB, start_one)

    @pl.when(i == 0)
    def _():
        gather(0, 0)

    @pl.when(i + 1 < n)
    def _():
        gather(i + 1, 1 - slot)

    _tile_copy(ys_ref, ybuf_ref.at[slot], sem.at[slot], 0, 0, TM_COMB).wait()
    out = x_ref[...] + mod_ref[0][:, 5 * D:6 * D] * _load_slabs(ybuf_ref.at[slot], TM_COMB)
    if final:
        out = _rms(out) * fw_ref[...]
    o_ref[...] = out


def _combine(dest, x1, ys, mod3, final_w, final):
    per_b = T // TM_COMB
    return pl.pallas_call(
        functools.partial(_combine_kernel, final=final),
        grid_spec=pltpu.PrefetchScalarGridSpec(
            num_scalar_prefetch=1,
            grid=(N // TM_COMB,),
            in_specs=[
                pl.BlockSpec((TM_COMB, D), lambda i, dest: (i, 0)),
                pl.BlockSpec(memory_space=pl.ANY),
                pl.BlockSpec((1, 1, 6 * D), lambda i, dest: (i // per_b, 0, 0)),
                pl.BlockSpec((1, D), lambda i, dest: (0, 0)),
            ],
            out_specs=pl.BlockSpec((TM_COMB, D), lambda i, dest: (i, 0)),
            scratch_shapes=[pltpu.VMEM((2, TM_COMB * SLAB, 128), F32), pltpu.SemaphoreType.DMA((2,))],
        ),
        out_shape=jax.ShapeDtypeStruct((N, D), F32),
        compiler_params=_cparams(("arbitrary",)),
        name="combine",
    )(dest, x1.reshape(N, D), ys, mod3, final_w)


def _moe_kernel(chunk_ref, rw_ref, xs_hbm, wg_hbm, wu_hbm, wd_hbm, ys_hbm,
                xbuf_ref, obuf_ref,
                sga_ref, sua_ref, sda_ref, sgb_ref, sub_ref, sdb_ref,
                ga_ref, ua_ref, da_ref, gb_ref, ub_ref, db_ref, wsem, xsem, osem, *, layer):
    row = lambda r: (lambda m: chunk_ref[r, m])
    start_of, n = row(0), chunk_ref[7, 0]
    slots = ((row(1), row(3), row(5), (sga_ref, sua_ref, sda_ref), (ga_ref, ua_ref, da_ref)),
             (row(2), row(4), row(6), (sgb_ref, sub_ref, sdb_ref), (gb_ref, ub_ref, db_ref)))

    def fetch(w_slot, expert):
        stage = slots[w_slot][3]
        return [pltpu.make_async_copy(w.at[layer, expert], s, wsem.at[w_slot])
                for w, s in zip((wg_hbm, wu_hbm, wd_hbm), stage)]

    def x_copy(step, buf):
        return _tile_copy(xs_hbm, xbuf_ref.at[buf], xsem.at[buf], start_of(step), 0, TM_MOE)

    def o_copy(step, buf):
        return _tile_copy(obuf_ref.at[buf], ys_hbm, osem.at[buf], 0, start_of(step), TM_MOE)

    @pl.when(pl.program_id(0) == 0)
    def _():
        obuf_ref[1] = jnp.zeros_like(obuf_ref[1])
        pad = _tile_copy(obuf_ref.at[1], ys_hbm, osem.at[1], 0, N, TM_MOE)
        pad.start()
        pad.wait()
        x_copy(0, 0).start()

    def chunk(m, slot):
        @pl.when(m + 1 < n)
        def _():
            x_copy(m + 1, 1 - slot).start()

        for w_slot, (expert_of, changed_at, next_of, stage, work) in enumerate(slots):
            @pl.when(m == 0)
            def _():
                for cp in fetch(w_slot, expert_of(0)):
                    cp.start(priority=1)

            @pl.when(changed_at(m) == 1)
            def _():
                for cp in fetch(w_slot, 0):
                    cp.wait()
                for s, w in zip(stage, work):
                    w[...] = s[...].astype(BF16)

                @pl.when(next_of(m) >= 0)
                def _():
                    for cp in fetch(w_slot, next_of(m)):
                        cp.start(priority=1)

        x_copy(m, slot).wait()
        hf = _load_slabs(xbuf_ref.at[slot], TM_MOE)
        h = hf.astype(BF16)

        def up(w_ref):
            return jnp.dot(h, w_ref[...], preferred_element_type=F32)

        def down(g, u, d_ref):
            return jnp.dot((_silu(g) * u).astype(BF16), d_ref[...], preferred_element_type=F32)

        g_a, u_a, g_b, u_b = up(ga_ref), up(ua_ref), up(gb_ref), up(ub_ref)
        y_a = down(g_a, u_a, da_ref)
        y_b = down(g_b, u_b, db_ref)
        s_a = _sigmoid(jnp.sum(hf * rw_ref[pl.ds(slots[0][0](m), 1), :], axis=1, keepdims=True))
        s_b = _sigmoid(jnp.sum(hf * rw_ref[pl.ds(slots[1][0](m), 1), :], axis=1, keepdims=True))
        denom = s_a + s_b
        val = (s_a / denom) * y_a + (s_b / denom) * y_b

        @pl.when(m >= 1)
        def _():
            o_copy(m - 1, 1 - slot).wait()

        _store_slabs(obuf_ref.at[slot], val)
        o_copy(m, slot).start()

        @pl.when(m == n - 1)
        def _():
            o_copy(m, slot).wait()

    for j in range(2):
        m = pl.program_id(0) * 2 + j
        pl.when(m < n)(functools.partial(chunk, m, j))


def _moe_sorted(chunks, xs, rw_t, wg, wu, wd, layer):
    mats = lambda dt: [pltpu.VMEM((D, DE), dt), pltpu.VMEM((D, DE), dt), pltpu.VMEM((DE, D), dt)]
    hbm = pl.BlockSpec(memory_space=pl.ANY)
    return pl.pallas_call(
        functools.partial(_moe_kernel, layer=layer),
        grid_spec=pltpu.PrefetchScalarGridSpec(
            num_scalar_prefetch=1,
            grid=(N_UNITS // 2,),
            in_specs=[pl.BlockSpec((NE, D), lambda m, *_: (0, 0)), hbm, hbm, hbm, hbm],
            out_specs=hbm,
            scratch_shapes=([pltpu.VMEM((2, TM_MOE * SLAB, 128), F32), pltpu.VMEM((2, TM_MOE * SLAB, 128), F32)]
                            + mats(F32) + mats(F32) + mats(BF16) + mats(BF16)
                            + [pltpu.SemaphoreType.DMA((2,))] * 3),
        ),
        out_shape=jax.ShapeDtypeStruct(xs.shape, F32),
        compiler_params=_cparams(("arbitrary",)),
        name="moe_sorted",
    )(chunks, rw_t, xs, wg, wu, wd)


def _moe_layer(x1, hp, logits, rw_t, router_b, wg, wu, wd, layer, mod3, final_w, final):
    dest, chunks = _route_tokens(logits, router_b)
    dest = dest.reshape(N)
    xs = _dispatch(dest, hp)
    ys = _moe_sorted(chunks, xs, rw_t, wg, wu, wd, layer)
    return _combine(dest, x1, ys, mod3, final_w, final).reshape(B, T, D)


def _rope_tables():
    half = 16
    inv = ROPE_BASE ** (-np.arange(half, dtype=np.float64) / half)
    t = np.arange(T)
    ang_r = (t // GRID_W)[:, None] * inv[None, :]
    ang_c = (t % GRID_W)[:, None] * inv[None, :]
    ang = np.concatenate([ang_r, ang_r, ang_c, ang_c], axis=1)
    sign = np.tile(np.concatenate([-np.ones(half), np.ones(half)]), 2)
    cos = np.concatenate([np.ones((LC, 64)), np.cos(ang)], axis=0)
    sin = np.concatenate([np.zeros((LC, 64)), np.sin(ang) * sign[None, :]], axis=0)
    return (jnp.asarray(np.tile(cos, (1, 2)), dtype=F32), jnp.asarray(np.tile(sin, (1, 2)), dtype=F32))


def _permute_w_in(w):
    rq = w[:, 0:256].reshape(D, NH, DK)
    dq = w[:, 256:768]
    rg = w[:, 768:1280]
    rk = w[:, 1280:1536].reshape(D, NH, DK)
    rv = w[:, 1536:2048]
    dk = w[:, 2048:2560]
    dv = w[:, 2560:3072]
    qk = jnp.concatenate([rq, rk * (DK ** -0.5)], axis=2).reshape(D, NH * 2 * DK)
    return jnp.concatenate([qk, rv, rg, dq * (DK ** -0.5 * math.log2(math.e)), dk, dv], axis=1).astype(BF16)


def kernel(x, c, ctx, c_ctx, ada_w, ada_b, norm_mix_w, norm_ffn_w, w_in, w_out, ret_log_decay, diff_lambda,
           diff_subln_w, pool_w, pool_scale, router_w, router_b, moe_w_gate, moe_w_up, moe_w_down, final_norm_w):
    assert x.shape == (B, T, D) and ctx.shape == (B, LC, D) and ada_w.shape[0] == 2
    cc = jnp.concatenate([c, c_ctx[None, :], jnp.zeros((16 - B - 1, D), F32)], axis=0)
    mod = _ada_mod(cc, ada_w, ada_b)
    rw_t = router_w.T
    fw = final_norm_w.reshape(1, D)
    experts = (moe_w_gate, moe_w_up, moe_w_down)

    mod0 = mod[0].reshape(16, 1, 6 * D)
    cos_t, sin_t = _rope_tables()
    proj = _inproj(x, ctx, mod0, norm_mix_w[0:1], _permute_w_in(w_in[0]), cos_t, sin_t)
    ret = _retention(proj, ret_log_decay[0])
    lam_init = 0.8 - 0.6 * math.exp(-0.3 * 0)
    lv = diff_lambda[0]
    lam = jnp.exp(jnp.sum(lv[0] * lv[1])) - jnp.exp(jnp.sum(lv[2] * lv[3])) + lam_init
    dif = _diffattn(proj, lam.reshape(1), diff_subln_w[0:1], 1.0 - lam_init)
    x1, hp, logits = _outproj(ret, dif, w_out[0].astype(BF16), x, mod0, norm_ffn_w[0:1], rw_t)
    x2 = _moe_layer(x1, hp, logits, rw_t, router_b, *experts, 0, mod0, fw, False)

    mod1 = mod[1].reshape(16, 1, 6 * D)
    x3, hp, logits = _pool_layer(x2, mod1, norm_mix_w[1:2], pool_w[0].astype(BF16), pool_scale[0:1],
                                 norm_ffn_w[1:2], rw_t)
    return _moe_layer(x3, hp, logits, rw_t, router_b, *experts, 1, mod1, fw, True)
```

```python
import functools
import math

import jax
import jax.numpy as jnp
import numpy as np
from jax import lax
from jax.experimental import pallas as pl
from jax.experimental.pallas import tpu as pltpu

F32 = jnp.float32
BF16 = jnp.bfloat16
I32 = jnp.int32

D = 1024
B = 8
T = 2048
N = B * T
GRID_W = 64
LC = 256
EPS = 1e-6
ROPE_BASE = 10000.0
NH = 4
DK = 64
HV = 128
CH = 256
RB = LC + T
NCH = RB // CH
POOL_WINDOWS = (2, 4, 8, 16)
PG = D // len(POOL_WINDOWS)
NE = 16
NGRP = 4
EPG = NE // NGRP
DE = 512
IN_W = 3072
HALO = 8
POOL_MARGIN = 16
assert POOL_WINDOWS == tuple(2 ** (k + 1) for k in range(len(POOL_WINDOWS))) and POOL_WINDOWS[-1] <= POOL_MARGIN

PAIR_A = (0, 0, 0, 1, 1, 3)
PAIR_B = (1, 2, 3, 3, 2, 2)
NCLS = NGRP * len(PAIR_A)
SLAB = D // 128

TM_PROJ = 256
PROJ_SUB = 3
TM_OUT = 1024
SUB_OUT = 512
TQ_SUB = 8
SCORE_AHEAD = 1
TM_POOL = 512
TM_MOE = 256
N_UNITS = N // TM_MOE + NCLS
TM_PERM = 2048
TM_COMB = 512
PERM_UNROLL = 16
VMEM_LIMIT = 56 * 1024 * 1024


def _cparams(sem):
    return pltpu.CompilerParams(dimension_semantics=sem, vmem_limit_bytes=VMEM_LIMIT)


def _sigmoid(x):
    return 1.0 / (1.0 + jnp.exp(-x))


def _silu(x):
    return x * _sigmoid(x)


def _rms(x):
    return x * lax.rsqrt(jnp.mean(x * x, axis=-1, keepdims=True) + EPS)


def _dot_3pass(a, b, dims):
    a_hi = a.astype(BF16)
    b_hi = b.astype(BF16)
    a_lo = (a - a_hi.astype(F32)).astype(BF16)
    b_lo = (b - b_hi.astype(F32)).astype(BF16)

    def dot(x, y):
        return lax.dot_general(x, y, dims, preferred_element_type=F32)

    return dot(a_hi, b_hi) + (dot(a_lo, b_hi) + dot(a_hi, b_lo))


def _load_slabs(ref, rows):
    return jnp.concatenate([ref[pl.ds(s, rows, stride=SLAB), :] for s in range(SLAB)], axis=1)


def _store_slabs(ref, val, row0=0):
    rows = val.shape[0]
    for s in range(SLAB):
        ref[pl.ds(row0 * SLAB + s, rows, stride=SLAB), :] = val[:, s * 128:(s + 1) * 128]


def _ada_kernel(cc_ref, w_ref, b_ref, o_ref):
    s = _silu(cc_ref[...])
    o_ref[0] = _dot_3pass(s, w_ref[0], (((1,), (0,)), ((), ()))) + b_ref[0]


def _ada_mod(cc, ada_w, ada_b):
    depth = ada_w.shape[0]
    tn = 1536
    return pl.pallas_call(
        _ada_kernel,
        grid=(depth, 6 * D // tn),
        in_specs=[
            pl.BlockSpec((16, D), lambda l, n: (0, 0)),
            pl.BlockSpec((1, D, tn), lambda l, n: (l, 0, n)),
            pl.BlockSpec((1, 1, tn), lambda l, n: (l, 0, n)),
        ],
        out_specs=pl.BlockSpec((1, 16, tn), lambda l, n: (l, 0, n)),
        out_shape=jax.ShapeDtypeStruct((depth, 16, 6 * D), F32),
        compiler_params=_cparams(("arbitrary", "arbitrary")),
        name="ada_mod",
    )(cc, ada_w, ada_b.reshape(depth, 1, 6 * D))


def _rope(seg, cos, sin_signed, lo_mask):
    w = seg.shape[1]
    from_hi = pltpu.roll(seg, w - 16, axis=1)
    from_lo = pltpu.roll(seg, 16, axis=1)
    partner = jnp.where(lo_mask, from_hi, from_lo)
    reps = w // cos.shape[1]
    c = jnp.concatenate([cos] * reps, axis=1)
    s = jnp.concatenate([sin_signed] * reps, axis=1)
    return seg * c + partner * s


def _inproj_kernel(*refs):
    x_refs = refs[:PROJ_SUB]
    c_ref, mod_ref, cmod_ref, nw_ref, w_ref, cos_ref, sin_ref, o_ref = refs[PROJ_SUB:]
    is_ctx = pl.program_id(1) == 0
    parts = []
    for s in range(PROJ_SUB):
        xt = x_refs[s][0]
        sh = mod_ref[0, :, 0:D]
        sc = mod_ref[0, :, D:2 * D]
        if s == 0:
            xt = jnp.where(is_ctx, c_ref[0], xt)
            sh = jnp.where(is_ctx, cmod_ref[0, :, 0:D], sh)
            sc = jnp.where(is_ctx, cmod_ref[0, :, D:2 * D], sc)
        parts.append(((_rms(xt) * nw_ref[...]) * (1.0 + sc) + sh).astype(BF16))
    hb = jnp.concatenate(parts, axis=0)
    lane = lax.broadcasted_iota(I32, (PROJ_SUB * TM_PROJ, 512), 1)
    lo_mask = (lane % 32) < 16
    cos = cos_ref[...]
    sin = sin_ref[...]

    def project(g):
        return jnp.dot(hb, w_ref[:, g * 512:(g + 1) * 512], preferred_element_type=F32)

    seg = project(0)
    for g in range(6):
        seg_next = project(g + 1) if g + 1 < 6 else None
        if g in (0, 3, 4):
            seg = _rope(seg, cos, sin, lo_mask)
        o_ref[0, :, g * 512:(g + 1) * 512] = seg.astype(BF16)
        seg = seg_next


def _inproj(x, ctx, mod3, norm_w, w_perm, cos_t, sin_t):
    tm = PROJ_SUB * TM_PROJ
    nj = RB // tm

    def x_map(s, b, j):
        return (b, jnp.maximum(PROJ_SUB * j + s - LC // TM_PROJ, 0), 0)

    return pl.pallas_call(
        _inproj_kernel,
        grid=(B, nj),
        in_specs=[
            *[pl.BlockSpec((1, TM_PROJ, D), functools.partial(x_map, s)) for s in range(PROJ_SUB)],
            pl.BlockSpec((1, LC, D), lambda b, j: (b, 0, 0)),
            pl.BlockSpec((1, 1, 2 * D), lambda b, j: (b, 0, 0)),
            pl.BlockSpec((1, 1, 2 * D), lambda b, j: (B, 0, 0)),
            pl.BlockSpec((1, D), lambda b, j: (0, 0)),
            pl.BlockSpec((D, IN_W), lambda b, j: (0, 0)),
            pl.BlockSpec((tm, 128), lambda b, j: (j, 0)),
            pl.BlockSpec((tm, 128), lambda b, j: (j, 0)),
        ],
        out_specs=pl.BlockSpec((1, tm, IN_W), lambda b, j: (b, j, 0)),
        out_shape=jax.ShapeDtypeStruct((B, RB, IN_W), BF16),
        compiler_params=_cparams(("arbitrary", "arbitrary")),
        name="inproj",
    )(*([x] * PROJ_SUB), ctx, mod3, mod3, norm_w, w_perm, cos_t, sin_t)


def _retention_kernel(ld_ref, qk_ref, v_ref, g_ref, o_ref, st_ref, kdec_ref, qdec_ref, mask_ref, cdec_ref):
    h = pl.program_id(0)
    lane = lax.broadcasted_iota(I32, (CH, 128), 1)
    fwd_lane = lane < DK

    @pl.when(pl.program_id(1) == 0)
    def _():
        lgf = ld_ref[0, h]
        lgb = ld_ref[1, h]
        pos = lax.broadcasted_iota(I32, (CH, 128), 0).astype(F32)
        kdec_ref[...] = jnp.where(fwd_lane, jnp.exp(lgf * (CH - 1 - pos)), jnp.exp(lgb * pos))
        qdec_ref[...] = jnp.where(fwd_lane, jnp.exp(lgf * (pos + 1.0)), jnp.exp(lgb * (CH - pos)))
        ii = lax.broadcasted_iota(I32, (CH, CH), 0)
        jj = lax.broadcasted_iota(I32, (CH, CH), 1)
        gap = (ii - jj).astype(F32)
        mask_ref[...] = (jnp.where(gap >= 0, jnp.exp(lgf * jnp.maximum(gap, 0.0)), 0.0)
                         + jnp.where(gap <= 0, jnp.exp(lgb * jnp.maximum(-gap, 0.0)), 0.0))
        ones = jnp.ones((DK, 128), F32)
        cdec_ref[0:DK, :] = jnp.exp(lgf * CH * ones)
        cdec_ref[DK:, :] = jnp.exp(lgb * CH * ones)

    kdec = kdec_ref[...]
    qdec = qdec_ref[...]
    mask = mask_ref[...]
    cf = cdec_ref[0:DK, :]
    cb = cdec_ref[DK:, :]

    def chunk(n):
        a = qk_ref[0, n * CH:(n + 1) * CH, :].astype(F32)
        swapped = pltpu.roll(a, DK, axis=1)
        return a, swapped

    kv = []
    for n in range(NCH):
        a, swapped = chunk(n)
        kk = jnp.where(fwd_lane, swapped, a)
        kb = (kk * kdec).astype(BF16)
        vn = v_ref[0, n * CH:(n + 1) * CH, :]
        kv.append(lax.dot_general(kb, vn, (((0,), (0,)), ((), ())), preferred_element_type=F32))
    sf = kv[0][:DK]
    for n in range(1, NCH):
        st_ref[n, 0:DK, :] = sf
        sf = cf * sf + kv[n][:DK]
    sb = kv[0][DK:]
    for n in range(NCH - 1, 0, -1):
        st_ref[n, DK:2 * DK, :] = sb
        sb = cb * sb + kv[n][DK:]

    def chunk_scores(n):
        a, swapped = chunk(n)
        q = a[:, :DK].astype(BF16)
        k = swapped[:, :DK].astype(BF16)
        return a, swapped, lax.dot_general(q, k, (((1,), (1,)), ((), ())), preferred_element_type=F32)

    nxt = chunk_scores(1)
    for n in range(1, NCH):
        a, swapped, scores = nxt
        nxt = chunk_scores(n + 1) if n + 1 < NCH else None
        p = (scores * mask).astype(BF16)
        vn = v_ref[0, n * CH:(n + 1) * CH, :]
        qq = jnp.where(fwd_lane, a, swapped)
        qd = (qq * qdec).astype(BF16)
        o = (jnp.dot(p, vn, preferred_element_type=F32)
             + jnp.dot(qd, st_ref[n].astype(BF16), preferred_element_type=F32))
        gate = g_ref[0, n * CH:(n + 1) * CH, :].astype(F32)
        o_ref[0, (n - 1) * CH:n * CH, :] = (_rms(o) * _silu(gate)).astype(BF16)


def _retention(proj, log_decay):
    return pl.pallas_call(
        _retention_kernel,
        grid=(NH, B),
        in_specs=[
            pl.BlockSpec(memory_space=pltpu.SMEM),
            pl.BlockSpec((1, RB, 128), lambda h, b: (b, 0, h)),
            pl.BlockSpec((1, RB, 128), lambda h, b: (b, 0, NH + h)),
            pl.BlockSpec((1, RB, 128), lambda h, b: (b, 0, 2 * NH + h)),
        ],
        out_specs=pl.BlockSpec((1, T, 128), lambda h, b: (b, 0, h)),
        out_shape=jax.ShapeDtypeStruct((B, T, NH * HV), BF16),
        scratch_shapes=[pltpu.VMEM((NCH, 128, 128), F32), pltpu.VMEM((CH, 128), F32), pltpu.VMEM((CH, 128), F32),
                        pltpu.VMEM((CH, CH), F32), pltpu.VMEM((2 * DK, 128), F32)],
        compiler_params=_cparams(("arbitrary", "arbitrary")),
        name="retention",
    )(log_decay, proj, proj, proj)


def _diffattn_kernel(lam_ref, *refs, out_scale):
    q_refs = refs[:TQ_SUB]
    k_ref, v_ref, sw_ref, o_ref = refs[TQ_SUB:]
    lam = lam_ref[0]
    k = k_ref[0]
    v = v_ref[0]
    nt = (((1,), (1,)), ((), ()))

    def scores(qh):
        return lax.dot_general(qh, k, nt, preferred_element_type=F32)

    v_ones = jnp.concatenate([v, jnp.ones_like(v)], axis=1)

    def values(s):
        e = jnp.exp2(s - jnp.max(s, axis=-1, keepdims=True))
        ol = jnp.dot(e.astype(BF16), v_ones, preferred_element_type=F32)
        return ol[:, :HV], ol[:, HV:HV + 1]

    halves = []
    for i in range(TQ_SUB):
        q = q_refs[i][0]
        lane = lax.broadcasted_iota(I32, q.shape, 1)
        zero = jnp.zeros_like(q)
        halves += [jnp.where(lane < DK, q, zero), jnp.where(lane >= DK, q, zero)]
    outs = []
    ahead = [scores(h) for h in halves[:SCORE_AHEAD]]
    for c in range(len(halves)):
        if c + SCORE_AHEAD < len(halves):
            ahead.append(scores(halves[c + SCORE_AHEAD]))
        outs.append(values(ahead.pop(0)))
    for i in range(TQ_SUB):
        (o1, l1), (o2, l2) = outs[2 * i], outs[2 * i + 1]
        o = o1 / l1 - o2 * (lam / l2)
        o_ref[0, i * TM_PROJ:(i + 1) * TM_PROJ, :] = (_rms(o) * sw_ref[...] * out_scale).astype(BF16)


def _diffattn(proj, lam, subln_w, out_scale):
    tq = TQ_SUB * TM_PROJ
    nq = T // tq

    def q_map(i, b, h, j):
        return (b, LC // TM_PROJ + j * TQ_SUB + i, 3 * NH + h)

    return pl.pallas_call(
        functools.partial(_diffattn_kernel, out_scale=out_scale),
        grid=(B, NH, nq),
        in_specs=[
            pl.BlockSpec(memory_space=pltpu.SMEM),
            *[pl.BlockSpec((1, TM_PROJ, 128), functools.partial(q_map, i)) for i in range(TQ_SUB)],
            pl.BlockSpec((1, RB, 128), lambda b, h, j: (b, 0, 4 * NH + h)),
            pl.BlockSpec((1, RB, 128), lambda b, h, j: (b, 0, 5 * NH + h)),
            pl.BlockSpec((1, HV), lambda b, h, j: (0, 0)),
        ],
        out_specs=pl.BlockSpec((1, tq, 128), lambda b, h, j: (b, j, h)),
        out_shape=jax.ShapeDtypeStruct((B, T, NH * HV), BF16),
        compiler_params=_cparams(("arbitrary", "arbitrary", "arbitrary")),
        name="diffattn",
    )(lam, *([proj] * TQ_SUB), proj, proj, subln_w)


def _route(bz):
    grp = []
    for g in range(NGRP):
        m = bz[g * EPG:(g + 1) * EPG]
        best = None
        for i in range(EPG):
            for k in range(i + 1, EPG):
                pair = m[i] + m[k]
                best = pair if best is None else jnp.maximum(best, pair)
        grp.append(best)
    gbest = grp[0]
    gsel = jnp.zeros_like(gbest, dtype=I32)
    for g in range(1, NGRP):
        better = grp[g] > gbest
        gsel = jnp.where(better, g, gsel)
        gbest = jnp.where(better, grp[g], gbest)
    cb = [bz[i] for i in range(EPG)]
    for g in range(1, NGRP):
        pick = gsel == g
        cb = [jnp.where(pick, bz[g * EPG + i], cb[i]) for i in range(EPG)]
    i1 = jnp.zeros_like(gsel)
    b1 = cb[0]
    for i in range(1, EPG):
        better = cb[i] > b1
        i1 = jnp.where(better, i, i1)
        b1 = jnp.where(better, cb[i], b1)
    neg = jnp.full_like(b1, -jnp.inf)
    rest = [jnp.where(i1 == i, neg, cb[i]) for i in range(EPG)]
    i2 = jnp.zeros_like(gsel)
    b2 = rest[0]
    for i in range(1, EPG):
        better = rest[i] > b2
        i2 = jnp.where(better, i, i2)
        b2 = jnp.where(better, rest[i], b2)
    lo = jnp.minimum(i1, i2)
    hi = jnp.maximum(i1, i2)
    code = lo * EPG + hi
    pair = jnp.full_like(gsel, len(PAIR_A) - 1)
    for p in range(len(PAIR_A) - 1):
        a, b = min(PAIR_A[p], PAIR_B[p]), max(PAIR_A[p], PAIR_B[p])
        pair = jnp.where(code == a * EPG + b, p, pair)
    return gsel * len(PAIR_A) + pair


def _ffn_prologue(x1, row0, mod, nfw_ref, rw_ref, hp_ref, logit_ref):
    rows = x1.shape[0]
    sh2 = mod[:, 3 * D:4 * D]
    sc2 = mod[:, 4 * D:5 * D]
    h2 = (_rms(x1) * nfw_ref[...]) * (1.0 + sc2) + sh2
    _store_slabs(hp_ref, h2, row0)
    logit_ref[:, row0:row0 + rows] = _dot_3pass(rw_ref[...], h2, (((1,), (1,)), ((), ())))


def _ffn_out_specs(tm, n_tiles_per_b):
    specs = [
        pl.BlockSpec((1, tm, D), lambda b, j: (b, j, 0)),
        pl.BlockSpec((tm * SLAB, 128), lambda b, j: (b * n_tiles_per_b + j, 0)),
        pl.BlockSpec((NE, tm), lambda b, j: (0, b * n_tiles_per_b + j)),
    ]
    shapes = [
        jax.ShapeDtypeStruct((B, T, D), F32),
        jax.ShapeDtypeStruct((N * SLAB, 128), F32),
        jax.ShapeDtypeStruct((NE, N), F32),
    ]
    return specs, shapes


def _route_kernel(logit_ref, bias_ref, dest_ref, chunk_ref):
    r = logit_ref.shape[1]
    cls = _route([_sigmoid(logit_ref[e]) + bias_ref[e] for e in range(NE)])
    lane_incl = (lax.broadcasted_iota(I32, (128, 128), 0) <= lax.broadcasted_iota(I32, (128, 128), 1)).astype(BF16)
    rows_before = (lax.broadcasted_iota(I32, (r, r), 1) < lax.broadcasted_iota(I32, (r, r), 0)).astype(BF16)
    dest = jnp.zeros((r, 128), F32)
    start = jnp.zeros((1, 128), F32)
    ends = []
    for c in range(NCLS):
        onehot = jnp.where(cls == c, 1.0, 0.0)
        in_row = jnp.dot(onehot.astype(BF16), lane_incl, preferred_element_type=F32)
        row_tot = jnp.broadcast_to(in_row[:, 127:128], (r, 128))
        above = jnp.dot(rows_before, row_tot.astype(BF16), preferred_element_type=F32)
        dest = dest + onehot * (start + above + in_row - 1.0)
        start = start + jnp.sum(row_tot, axis=0, keepdims=True)
        ends.append(start)
    dest_ref[...] = dest.astype(I32)
    chunk_ref[...] = _chunk_list(ends)


def _chunk_list(ends):
    lane = lax.broadcasted_iota(I32, (1, 128), 1).astype(F32)
    offs = [jnp.zeros((1, 128), F32)] + ends[:-1]
    per = [jnp.floor((e - o + (TM_MOE - 1)) * (1.0 / TM_MOE)) for e, o in zip(ends, offs)]
    cum = []
    total = jnp.zeros((1, 128), F32)
    for p in per:
        total = total + p
        cum.append(total)
    m = jnp.minimum(lane, total - 1.0)
    cls = jnp.zeros((1, 128), F32)
    for c in range(NCLS):
        cls = cls + jnp.where(cum[c] <= m, 1.0, 0.0)
    start = jnp.zeros((1, 128), F32)
    for c in range(NCLS):
        start = jnp.where(cls == c, offs[c] + TM_MOE * (m - (cum[c] - per[c])), start)
    n_pair = len(PAIR_A)
    grp = jnp.zeros((1, 128), F32)
    for g in range(1, NGRP):
        grp = grp + jnp.where(cls >= g * n_pair, 1.0, 0.0)
    pair = cls - n_pair * grp
    loc_a = jnp.zeros((1, 128), F32)
    loc_b = jnp.zeros((1, 128), F32)
    for p in range(n_pair):
        loc_a = jnp.where(pair == p, float(PAIR_A[p]), loc_a)
        loc_b = jnp.where(pair == p, float(PAIR_B[p]), loc_b)
    sub = lax.broadcasted_iota(I32, (128, 128), 0).astype(F32)
    lan = lax.broadcasted_iota(I32, (128, 128), 1).astype(F32)
    none = 1000.0

    def changes(e):
        chg = jnp.where((lane == 0.0) | (e != pltpu.roll(e, 1, axis=1)), 1.0, 0.0)
        chg_col = jnp.sum(jnp.where(sub == lan, chg, 0.0), axis=1, keepdims=True)
        e_col = jnp.sum(jnp.where(sub == lan, e, 0.0), axis=1, keepdims=True)
        nxt_at = jnp.min(jnp.where((sub > lan) & (chg_col == 1.0), sub, none), axis=0, keepdims=True)
        nxt_e = jnp.sum(jnp.where(sub == nxt_at, e_col, 0.0), axis=0, keepdims=True)
        return chg, jnp.where(nxt_at < none, nxt_e, -1.0)

    e_a = EPG * grp + loc_a
    e_b = EPG * grp + loc_b
    chg_a, nxt_a = changes(e_a)
    chg_b, nxt_b = changes(e_b)
    return jnp.concatenate([start, e_a, e_b, chg_a, chg_b, nxt_a, nxt_b, total], axis=0).astype(I32)


def _route_tokens(logits_t, router_b):
    r = N // 128
    return pl.pallas_call(
        _route_kernel,
        in_specs=[pl.BlockSpec((NE, r, 128), lambda: (0, 0, 0)), pl.BlockSpec(memory_space=pltpu.SMEM)],
        out_specs=[pl.BlockSpec((r, 128), lambda: (0, 0)), pl.BlockSpec((8, 128), lambda: (0, 0))],
        out_shape=[jax.ShapeDtypeStruct((r, 128), I32), jax.ShapeDtypeStruct((8, 128), I32)],
        compiler_params=pltpu.CompilerParams(vmem_limit_bytes=VMEM_LIMIT),
        name="route",
    )(logits_t.reshape(NE, r, 128), router_b)


def _outproj_kernel(ret_ref, dif_ref, w_ref, x_ref, mod_ref, nfw_ref, rw_ref, x1_ref, hp_ref, logit_ref):
    mod = mod_ref[0]

    def mix(r0):
        rows = slice(r0, r0 + SUB_OUT)
        return (jnp.dot(ret_ref[0, rows, :], w_ref[0:NH * HV, :], preferred_element_type=F32)
                + jnp.dot(dif_ref[0, rows, :], w_ref[NH * HV:, :], preferred_element_type=F32))

    mx = mix(0)
    for r0 in range(0, TM_OUT, SUB_OUT):
        mx_next = mix(r0 + SUB_OUT) if r0 + SUB_OUT < TM_OUT else None
        x1 = x_ref[0, r0:r0 + SUB_OUT, :] + mod[:, 2 * D:3 * D] * mx
        x1_ref[0, r0:r0 + SUB_OUT, :] = x1
        _ffn_prologue(x1, r0, mod, nfw_ref, rw_ref, hp_ref, logit_ref)
        mx = mx_next


def _outproj(ret, dif, w_out, x, mod3, nfw, rw_t):
    nj = T // TM_OUT
    out_specs, out_shapes = _ffn_out_specs(TM_OUT, nj)
    return pl.pallas_call(
        _outproj_kernel,
        grid=(B, nj),
        in_specs=[
            pl.BlockSpec((1, TM_OUT, NH * HV), lambda b, j: (b, j, 0)),
            pl.BlockSpec((1, TM_OUT, NH * HV), lambda b, j: (b, j, 0)),
            pl.BlockSpec((2 * NH * HV, D), lambda b, j: (0, 0)),
            pl.BlockSpec((1, TM_OUT, D), lambda b, j: (b, j, 0)),
            pl.BlockSpec((1, 1, 6 * D), lambda b, j: (b, 0, 0)),
            pl.BlockSpec((1, D), lambda b, j: (0, 0)),
            pl.BlockSpec((NE, D), lambda b, j: (0, 0)),
        ],
        out_specs=out_specs,
        out_shape=out_shapes,
        compiler_params=_cparams(("arbitrary", "arbitrary")),
        name="outproj",
    )(ret, dif, w_out, x, mod3, nfw, rw_t)


def _pool_kernel(x_ref, prev_ref, next_ref, mod_ref, nmw_ref, pw_ref, ps_ref, nfw_ref, rw_ref,
                 x1_ref, hp_ref, logit_ref):
    i = pl.program_id(1)
    last = pl.num_programs(1) - 1
    mod = mod_ref[0]
    sh1 = mod[:, 0:D]
    sc1 = mod[:, D:2 * D]

    def modnorm(v):
        return (_rms(v) * nmw_ref[...]) * (1.0 + sc1) + sh1

    x = x_ref[0]
    hc = modnorm(x)
    zeros = jnp.zeros((POOL_MARGIN - HALO, D), F32)
    ext = jnp.concatenate([zeros, jnp.where(i > 0, modnorm(prev_ref[0]), 0.0), hc,
                           jnp.where(i < last, modnorm(next_ref[0]), 0.0), zeros], axis=0)
    n_ext = ext.shape[0]
    pos = i * TM_POOL + lax.broadcasted_iota(I32, (TM_POOL, 1), 0)
    mixed = []
    run = ext
    for gi, w in enumerate(POOL_WINDOWS):
        left = w // 2
        right = w - 1 - left
        cols = slice(gi * PG, (gi + 1) * PG)
        run = run + pltpu.roll(run, n_ext - w // 2, axis=0)
        tot = run[POOL_MARGIN - left:POOL_MARGIN - left + TM_POOL, 0:PG]
        if gi + 1 < len(POOL_WINDOWS):
            run = run[:, PG:]
        cnt = (jnp.minimum(pos + right + 1, T) - jnp.maximum(pos - left, 0)).astype(F32)
        pooled = (tot * (1.0 / cnt) - hc[:, cols]).astype(BF16)
        mixed.append(jnp.dot(pooled, pw_ref[gi], preferred_element_type=F32))
    mixed = jnp.concatenate(mixed, axis=1) * ps_ref[...]
    x1 = x + mod[:, 2 * D:3 * D] * mixed
    x1_ref[0] = x1
    _ffn_prologue(x1, 0, mod, nfw_ref, rw_ref, hp_ref, logit_ref)


def _pool_layer(x, mod3, nmw, pool_w, pool_scale, nfw, rw_t):
    ni = T // TM_POOL
    hb = TM_POOL // HALO
    out_specs, out_shapes = _ffn_out_specs(TM_POOL, ni)
    return pl.pallas_call(
        _pool_kernel,
        grid=(B, ni),
        in_specs=[
            pl.BlockSpec((1, TM_POOL, D), lambda b, i: (b, i, 0)),
            pl.BlockSpec((1, HALO, D), lambda b, i: (b, jnp.maximum(i * hb - 1, 0), 0)),
            pl.BlockSpec((1, HALO, D), lambda b, i: (b, jnp.minimum((i + 1) * hb, T // HALO - 1), 0)),
            pl.BlockSpec((1, 1, 6 * D), lambda b, i: (b, 0, 0)),
            pl.BlockSpec((1, D), lambda b, i: (0, 0)),
            pl.BlockSpec((len(POOL_WINDOWS), PG, PG), lambda b, i: (0, 0, 0)),
            pl.BlockSpec((1, D), lambda b, i: (0, 0)),
            pl.BlockSpec((1, D), lambda b, i: (0, 0)),
            pl.BlockSpec((NE, D), lambda b, i: (0, 0)),
        ],
        out_specs=out_specs,
        out_shape=out_shapes,
        compiler_params=_cparams(("arbitrary", "arbitrary")),
        name="pool_layer",
    )(x, x, x, mod3, nmw, pool_w, pool_scale, nfw, rw_t)


def _tile_copy(src_ref, dst_ref, sem, s, d, rows=1):
    s0 = pl.multiple_of(s * SLAB, SLAB)
    d0 = pl.multiple_of(d * SLAB, SLAB)
    return pltpu.make_async_copy(src_ref.at[pl.ds(s0, rows * SLAB)], dst_ref.at[pl.ds(d0, rows * SLAB)], sem)


def _issue_tile_copies(idx_ref, base, rows, start_one):
    def group(g, carry):
        r0 = g * PERM_UNROLL
        ids = [idx_ref[base + r0 + u] for u in range(PERM_UNROLL)]
        for u in range(PERM_UNROLL):
            start_one(r0 + u, ids[u], u % 2)
        return carry

    lax.fori_loop(0, rows // PERM_UNROLL, group, 0)


def _dispatch_kernel(dest_ref, src_ref, dst_ref, zero_ref, sem, pad_sem):
    base = pl.program_id(0) * TM_PERM

    @pl.when(pl.program_id(0) == 0)
    def _():
        zero_ref[...] = jnp.zeros_like(zero_ref)
        pad = _tile_copy(zero_ref, dst_ref, pad_sem, 0, N, TM_MOE)
        pad.start()
        pad.wait()

    def start_one(r, d, priority):
        _tile_copy(src_ref, dst_ref, sem, r, d).start(priority=priority)

    _issue_tile_copies(dest_ref, base, TM_PERM, start_one)
    _tile_copy(src_ref, dst_ref, sem, 0, 0, TM_PERM).wait()


def _dispatch(dest, src):
    return pl.pallas_call(
        _dispatch_kernel,
        grid_spec=pltpu.PrefetchScalarGridSpec(
            num_scalar_prefetch=1,
            grid=(N // TM_PERM,),
            in_specs=[pl.BlockSpec((TM_PERM * SLAB, 128), lambda i, dest: (i, 0))],
            out_specs=pl.BlockSpec(memory_space=pl.ANY),
            scratch_shapes=[pltpu.VMEM((TM_MOE * SLAB, 128), F32), pltpu.SemaphoreType.DMA(()),
                            pltpu.SemaphoreType.DMA(())],
        ),
        out_shape=jax.ShapeDtypeStruct(((N + TM_MOE) * SLAB, 128), src.dtype),
        compiler_params=_cparams(("arbitrary",)),
        name="dispatch",
    )(dest, src)


def _combine_kernel(dest_ref, x_ref, ys_ref, mod_ref, fw_ref, o_ref, ybuf_ref, sem, *, final):
    i = pl.program_id(0)
    n = pl.num_programs(0)
    slot = i % 2

    def gather(step, to_slot):
        def start_one(r, d, priority):
            _tile_copy(ys_ref, ybuf_ref.at[to_slot], sem.at[to_slot], d, r).start(priority=priority)

        _issue_tile_copies(dest_ref, step * TM_COMB, TM_COMB, start_one)

    @pl.when(i == 0)
    def _():
        gather(0, 0)

    @pl.when(i + 1 < n)
    def _():
        gather(i + 1, 1 - slot)

    _tile_copy(ys_ref, ybuf_ref.at[slot], sem.at[slot], 0, 0, TM_COMB).wait()
    out = x_ref[...] + mod_ref[0][:, 5 * D:6 * D] * _load_slabs(ybuf_ref.at[slot], TM_COMB)
    if final:
        out = _rms(out) * fw_ref[...]
    o_ref[...] = out


def _combine(dest, x1, ys, mod3, final_w, final):
    per_b = T // TM_COMB
    return pl.pallas_call(
        functools.partial(_combine_kernel, final=final),
        grid_spec=pltpu.PrefetchScalarGridSpec(
            num_scalar_prefetch=1,
            grid=(N // TM_COMB,),
            in_specs=[
                pl.BlockSpec((TM_COMB, D), lambda i, dest: (i, 0)),
                pl.BlockSpec(memory_space=pl.ANY),
                pl.BlockSpec((1, 1, 6 * D), lambda i, dest: (i // per_b, 0, 0)),
                pl.BlockSpec((1, D), lambda i, dest: (0, 0)),
            ],
            out_specs=pl.BlockSpec((TM_COMB, D), lambda i, dest: (i, 0)),
            scratch_shapes=[pltpu.VMEM((2, TM_COMB * SLAB, 128), F32), pltpu.SemaphoreType.DMA((2,))],
        ),
        out_shape=jax.ShapeDtypeStruct((N, D), F32),
        compiler_params=_cparams(("arbitrary",)),
        name="combine",
    )(dest, x1.reshape(N, D), ys, mod3, final_w)


def _moe_kernel(chunk_ref, rw_ref, xs_hbm, wg_hbm, wu_hbm, wd_hbm, ys_hbm,
                xbuf_ref, obuf_ref,
                sga_ref, sua_ref, sda_ref, sgb_ref, sub_ref, sdb_ref,
                ga_ref, ua_ref, da_ref, gb_ref, ub_ref, db_ref, wsem, xsem, osem, *, layer):
    row = lambda r: (lambda m: chunk_ref[r, m])
    start_of, n = row(0), chunk_ref[7, 0]
    slots = ((row(1), row(3), row(5), (sga_ref, sua_ref, sda_ref), (ga_ref, ua_ref, da_ref)),
             (row(2), row(4), row(6), (sgb_ref, sub_ref, sdb_ref), (gb_ref, ub_ref, db_ref)))

    def fetch(w_slot, expert):
        stage = slots[w_slot][3]
        return [pltpu.make_async_copy(w.at[layer, expert], s, wsem.at[w_slot])
                for w, s in zip((wg_hbm, wu_hbm, wd_hbm), stage)]

    def x_copy(step, buf):
        return _tile_copy(xs_hbm, xbuf_ref.at[buf], xsem.at[buf], start_of(step), 0, TM_MOE)

    def o_copy(step, buf):
        return _tile_copy(obuf_ref.at[buf], ys_hbm, osem.at[buf], 0, start_of(step), TM_MOE)

    @pl.when(pl.program_id(0) == 0)
    def _():
        obuf_ref[1] = jnp.zeros_like(obuf_ref[1])
        pad = _tile_copy(obuf_ref.at[1], ys_hbm, osem.at[1], 0, N, TM_MOE)
        pad.start()
        pad.wait()
        x_copy(0, 0).start()

    def chunk(m, slot):
        @pl.when(m + 1 < n)
        def _():
            x_copy(m + 1, 1 - slot).start()

        for w_slot, (expert_of, changed_at, next_of, stage, work) in enumerate(slots):
            @pl.when(m == 0)
            def _():
                for cp in fetch(w_slot, expert_of(0)):
                    cp.start(priority=1)

            @pl.when(changed_at(m) == 1)
            def _():
                for cp in fetch(w_slot, 0):
                    cp.wait()
                for s, w in zip(stage, work):
                    w[...] = s[...].astype(BF16)

                @pl.when(next_of(m) >= 0)
                def _():
                    for cp in fetch(w_slot, next_of(m)):
                        cp.start(priority=1)

        x_copy(m, slot).wait()
        hf = _load_slabs(xbuf_ref.at[slot], TM_MOE)
        h = hf.astype(BF16)

        def up(w_ref):
            return jnp.dot(h, w_ref[...], preferred_element_type=F32)

        def down(g, u, d_ref):
            return jnp.dot((_silu(g) * u).astype(BF16), d_ref[...], preferred_element_type=F32)

        g_a, u_a, g_b, u_b = up(ga_ref), up(ua_ref), up(gb_ref), up(ub_ref)
        y_a = down(g_a, u_a, da_ref)
        y_b = down(g_b, u_b, db_ref)
        s_a = _sigmoid(jnp.sum(hf * rw_ref[pl.ds(slots[0][0](m), 1), :], axis=1, keepdims=True))
        s_b = _sigmoid(jnp.sum(hf * rw_ref[pl.ds(slots[1][0](m), 1), :], axis=1, keepdims=True))
        denom = s_a + s_b
        val = (s_a / denom) * y_a + (s_b / denom) * y_b

        @pl.when(m >= 1)
        def _():
            o_copy(m - 1, 1 - slot).wait()

        _store_slabs(obuf_ref.at[slot], val)
        o_copy(m, slot).start()

        @pl.when(m == n - 1)
        def _():
            o_copy(m, slot).wait()

    for j in range(2):
        m = pl.program_id(0) * 2 + j
        pl.when(m < n)(functools.partial(chunk, m, j))


def _moe_sorted(chunks, xs, rw_t, wg, wu, wd, layer):
    mats = lambda dt: [pltpu.VMEM((D, DE), dt), pltpu.VMEM((D, DE), dt), pltpu.VMEM((DE, D), dt)]
    hbm = pl.BlockSpec(memory_space=pl.ANY)
    return pl.pallas_call(
        functools.partial(_moe_kernel, layer=layer),
        grid_spec=pltpu.PrefetchScalarGridSpec(
            num_scalar_prefetch=1,
            grid=(N_UNITS // 2,),
            in_specs=[pl.BlockSpec((NE, D), lambda m, *_: (0, 0)), hbm, hbm, hbm, hbm],
            out_specs=hbm,
            scratch_shapes=([pltpu.VMEM((2, TM_MOE * SLAB, 128), F32), pltpu.VMEM((2, TM_MOE * SLAB, 128), F32)]
                            + mats(F32) + mats(F32) + mats(BF16) + mats(BF16)
                            + [pltpu.SemaphoreType.DMA((2,))] * 3),
        ),
        out_shape=jax.ShapeDtypeStruct(xs.shape, F32),
        compiler_params=_cparams(("arbitrary",)),
        name="moe_sorted",
    )(chunks, rw_t, xs, wg, wu, wd)


def _moe_layer(x1, hp, logits, rw_t, router_b, wg, wu, wd, layer, mod3, final_w, final):
    dest, chunks = _route_tokens(logits, router_b)
    dest = dest.reshape(N)
    xs = _dispatch(dest, hp)
    ys = _moe_sorted(chunks, xs, rw_t, wg, wu, wd, layer)
    return _combine(dest, x1, ys, mod3, final_w, final).reshape(B, T, D)


def _rope_tables():
    half = 16
    inv = ROPE_BASE ** (-np.arange(half, dtype=np.float64) / half)
    t = np.arange(T)
    ang_r = (t // GRID_W)[:, None] * inv[None, :]
    ang_c = (t % GRID_W)[:, None] * inv[None, :]
    ang = np.concatenate([ang_r, ang_r, ang_c, ang_c], axis=1)
    sign = np.tile(np.concatenate([-np.ones(half), np.ones(half)]), 2)
    cos = np.concatenate([np.ones((LC, 64)), np.cos(ang)], axis=0)
    sin = np.concatenate([np.zeros((LC, 64)), np.sin(ang) * sign[None, :]], axis=0)
    return (jnp.asarray(np.tile(cos, (1, 2)), dtype=F32), jnp.asarray(np.tile(sin, (1, 2)), dtype=F32))


def _permute_w_in(w):
    rq = w[:, 0:256].reshape(D, NH, DK)
    dq = w[:, 256:768]
    rg = w[:, 768:1280]
    rk = w[:, 1280:1536].reshape(D, NH, DK)
    rv = w[:, 1536:2048]
    dk = w[:, 2048:2560]
    dv = w[:, 2560:3072]
    qk = jnp.concatenate([rq, rk * (DK ** -0.5)], axis=2).reshape(D, NH * 2 * DK)
    return jnp.concatenate([qk, rv, rg, dq * (DK ** -0.5 * math.log2(math.e)), dk, dv], axis=1).astype(BF16)


def kernel(x, c, ctx, c_ctx, ada_w, ada_b, norm_mix_w, norm_ffn_w, w_in, w_out, ret_log_decay, diff_lambda,
           diff_subln_w, pool_w, pool_scale, router_w, router_b, moe_w_gate, moe_w_up, moe_w_down, final_norm_w):
    assert x.shape == (B, T, D) and ctx.shape == (B, LC, D) and ada_w.shape[0] == 2
    cc = jnp.concatenate([c, c_ctx[None, :], jnp.zeros((16 - B - 1, D), F32)], axis=0)
    mod = _ada_mod(cc, ada_w, ada_b)
    rw_t = router_w.T
    fw = final_norm_w.reshape(1, D)
    experts = (moe_w_gate, moe_w_up, moe_w_down)

    mod0 = mod[0].reshape(16, 1, 6 * D)
    cos_t, sin_t = _rope_tables()
    proj = _inproj(x, ctx, mod0, norm_mix_w[0:1], _permute_w_in(w_in[0]), cos_t, sin_t)
    ret = _retention(proj, ret_log_decay[0])
    lam_init = 0.8 - 0.6 * math.exp(-0.3 * 0)
    lv = diff_lambda[0]
    lam = jnp.exp(jnp.sum(lv[0] * lv[1])) - jnp.exp(jnp.sum(lv[2] * lv[3])) + lam_init
    dif = _diffattn(proj, lam.reshape(1), diff_subln_w[0:1], 1.0 - lam_init)
    x1, hp, logits = _outproj(ret, dif, w_out[0].astype(BF16), x, mod0, norm_ffn_w[0:1], rw_t)
    x2 = _moe_layer(x1, hp, logits, rw_t, router_b, *experts, 0, mod0, fw, False)

    mod1 = mod[1].reshape(16, 1, 6 * D)
    x3, hp, logits = _pool_layer(x2, mod1, norm_mix_w[1:2], pool_w[0].astype(BF16), pool_scale[0:1],
                                 norm_ffn_w[1:2], rw_t)
    return _moe_layer(x3, hp, logits, rw_t, router_b, *experts, 1, mod1, fw, True)
```

```python
import functools
import math

import jax
import jax.numpy as jnp
import numpy as np
from jax import lax
from jax.experimental import pallas as pl
from jax.experimental.pallas import tpu as pltpu

F32 = jnp.float32
BF16 = jnp.bfloat16
I32 = jnp.int32

D = 1024
B = 8
T = 2048
N = B * T
GRID_W = 64
LC = 256
EPS = 1e-6
ROPE_BASE = 10000.0
NH = 4
DK = 64
HV = 128
CH = 256
RB = LC + T
NCH = RB // CH
POOL_WINDOWS = (2, 4, 8, 16)
PG = D // len(POOL_WINDOWS)
NE = 16
NGRP = 4
EPG = NE // NGRP
DE = 512
IN_W = 3072
HALO = 8
POOL_MARGIN = 16
assert POOL_WINDOWS == tuple(2 ** (k + 1) for k in range(len(POOL_WINDOWS))) and POOL_WINDOWS[-1] <= POOL_MARGIN

PAIR_A = (0, 0, 0, 1, 1, 3)
PAIR_B = (1, 2, 3, 3, 2, 2)
NCLS = NGRP * len(PAIR_A)
SLAB = D // 128

TM_PROJ = 256
PROJ_SUB = 3
TM_OUT = 1024
SUB_OUT = 512
TQ_SUB = 8
SCORE_AHEAD = 1
TM_POOL = 1024
TM_MOE = 256
N_UNITS = N // TM_MOE + NCLS
TM_PERM = 2048
TM_COMB = 512
PERM_UNROLL = 16
VMEM_LIMIT = 56 * 1024 * 1024


def _cparams(sem):
    return pltpu.CompilerParams(dimension_semantics=sem, vmem_limit_bytes=VMEM_LIMIT)


def _sigmoid(x):
    return 1.0 / (1.0 + jnp.exp(-x))


def _silu(x):
    return x * _sigmoid(x)


def _rms(x):
    return x * lax.rsqrt(jnp.mean(x * x, axis=-1, keepdims=True) + EPS)


def _dot_3pass(a, b, dims):
    a_hi = a.astype(BF16)
    b_hi = b.astype(BF16)
    a_lo = (a - a_hi.astype(F32)).astype(BF16)
    b_lo = (b - b_hi.astype(F32)).astype(BF16)

    def dot(x, y):
        return lax.dot_general(x, y, dims, preferred_element_type=F32)

    return dot(a_hi, b_hi) + (dot(a_lo, b_hi) + dot(a_hi, b_lo))


def _load_slabs(ref, rows):
    return jnp.concatenate([ref[pl.ds(s, rows, stride=SLAB), :] for s in range(SLAB)], axis=1)


def _store_slabs(ref, val, row0=0):
    rows = val.shape[0]
    for s in range(SLAB):
        ref[pl.ds(row0 * SLAB + s, rows, stride=SLAB), :] = val[:, s * 128:(s + 1) * 128]


def _ada_kernel(cc_ref, w_ref, b_ref, o_ref):
    s = _silu(cc_ref[...])
    o_ref[0] = _dot_3pass(s, w_ref[0], (((1,), (0,)), ((), ()))) + b_ref[0]


def _ada_mod(cc, ada_w, ada_b):
    depth = ada_w.shape[0]
    tn = 1536
    return pl.pallas_call(
        _ada_kernel,
        grid=(depth, 6 * D // tn),
        in_specs=[
            pl.BlockSpec((16, D), lambda l, n: (0, 0)),
            pl.BlockSpec((1, D, tn), lambda l, n: (l, 0, n)),
            pl.BlockSpec((1, 1, tn), lambda l, n: (l, 0, n)),
        ],
        out_specs=pl.BlockSpec((1, 16, tn), lambda l, n: (l, 0, n)),
        out_shape=jax.ShapeDtypeStruct((depth, 16, 6 * D), F32),
        compiler_params=_cparams(("arbitrary", "arbitrary")),
        name="ada_mod",
    )(cc, ada_w, ada_b.reshape(depth, 1, 6 * D))


def _rope(seg, cos, sin_signed, lo_mask):
    w = seg.shape[1]
    from_hi = pltpu.roll(seg, w - 16, axis=1)
    from_lo = pltpu.roll(seg, 16, axis=1)
    partner = jnp.where(lo_mask, from_hi, from_lo)
    reps = w // cos.shape[1]
    c = jnp.concatenate([cos] * reps, axis=1)
    s = jnp.concatenate([sin_signed] * reps, axis=1)
    return seg * c + partner * s


def _inproj_kernel(*refs):
    x_refs = refs[:PROJ_SUB]
    c_ref, mod_ref, cmod_ref, nw_ref, w_ref, cos_ref, sin_ref, o_ref = refs[PROJ_SUB:]
    is_ctx = pl.program_id(1) == 0
    parts = []
    for s in range(PROJ_SUB):
        xt = x_refs[s][0]
        sh = mod_ref[0, :, 0:D]
        sc = mod_ref[0, :, D:2 * D]
        if s == 0:
            xt = jnp.where(is_ctx, c_ref[0], xt)
            sh = jnp.where(is_ctx, cmod_ref[0, :, 0:D], sh)
            sc = jnp.where(is_ctx, cmod_ref[0, :, D:2 * D], sc)
        parts.append(((_rms(xt) * nw_ref[...]) * (1.0 + sc) + sh).astype(BF16))
    hb = jnp.concatenate(parts, axis=0)
    lane = lax.broadcasted_iota(I32, (PROJ_SUB * TM_PROJ, 512), 1)
    lo_mask = (lane % 32) < 16
    cos = cos_ref[...]
    sin = sin_ref[...]

    def project(g):
        return jnp.dot(hb, w_ref[:, g * 512:(g + 1) * 512], preferred_element_type=F32)

    seg = project(0)
    for g in range(6):
        seg_next = project(g + 1) if g + 1 < 6 else None
        if g in (0, 3, 4):
            seg = _rope(seg, cos, sin, lo_mask)
        o_ref[0, :, g * 512:(g + 1) * 512] = seg.astype(BF16)
        seg = seg_next


def _inproj(x, ctx, mod3, norm_w, w_perm, cos_t, sin_t):
    tm = PROJ_SUB * TM_PROJ
    nj = RB // tm

    def x_map(s, b, j):
        return (b, jnp.maximum(PROJ_SUB * j + s - LC // TM_PROJ, 0), 0)

    return pl.pallas_call(
        _inproj_kernel,
        grid=(B, nj),
        in_specs=[
            *[pl.BlockSpec((1, TM_PROJ, D), functools.partial(x_map, s)) for s in range(PROJ_SUB)],
            pl.BlockSpec((1, LC, D), lambda b, j: (b, 0, 0)),
            pl.BlockSpec((1, 1, 2 * D), lambda b, j: (b, 0, 0)),
            pl.BlockSpec((1, 1, 2 * D), lambda b, j: (B, 0, 0)),
            pl.BlockSpec((1, D), lambda b, j: (0, 0)),
            pl.BlockSpec((D, IN_W), lambda b, j: (0, 0)),
            pl.BlockSpec((tm, 128), lambda b, j: (j, 0)),
            pl.BlockSpec((tm, 128), lambda b, j: (j, 0)),
        ],
        out_specs=pl.BlockSpec((1, tm, IN_W), lambda b, j: (b, j, 0)),
        out_shape=jax.ShapeDtypeStruct((B, RB, IN_W), BF16),
        compiler_params=_cparams(("arbitrary", "arbitrary")),
        name="inproj",
    )(*([x] * PROJ_SUB), ctx, mod3, mod3, norm_w, w_perm, cos_t, sin_t)


def _retention_kernel(ld_ref, qk_ref, v_ref, g_ref, o_ref, st_ref, kdec_ref, qdec_ref, mask_ref, cdec_ref):
    h = pl.program_id(0)
    lane = lax.broadcasted_iota(I32, (CH, 128), 1)
    fwd_lane = lane < DK

    @pl.when(pl.program_id(1) == 0)
    def _():
        lgf = ld_ref[0, h]
        lgb = ld_ref[1, h]
        pos = lax.broadcasted_iota(I32, (CH, 128), 0).astype(F32)
        kdec_ref[...] = jnp.where(fwd_lane, jnp.exp(lgf * (CH - 1 - pos)), jnp.exp(lgb * pos))
        qdec_ref[...] = jnp.where(fwd_lane, jnp.exp(lgf * (pos + 1.0)), jnp.exp(lgb * (CH - pos)))
        ii = lax.broadcasted_iota(I32, (CH, CH), 0)
        jj = lax.broadcasted_iota(I32, (CH, CH), 1)
        gap = (ii - jj).astype(F32)
        mask_ref[...] = (jnp.where(gap >= 0, jnp.exp(lgf * jnp.maximum(gap, 0.0)), 0.0)
                         + jnp.where(gap <= 0, jnp.exp(lgb * jnp.maximum(-gap, 0.0)), 0.0))
        ones = jnp.ones((DK, 128), F32)
        cdec_ref[0:DK, :] = jnp.exp(lgf * CH * ones)
        cdec_ref[DK:, :] = jnp.exp(lgb * CH * ones)

    kdec = kdec_ref[...]
    qdec = qdec_ref[...]
    mask = mask_ref[...]
    cf = cdec_ref[0:DK, :]
    cb = cdec_ref[DK:, :]

    def chunk(n):
        a = qk_ref[0, n * CH:(n + 1) * CH, :].astype(F32)
        swapped = pltpu.roll(a, DK, axis=1)
        return a, swapped

    kv = []
    for n in range(NCH):
        a, swapped = chunk(n)
        kk = jnp.where(fwd_lane, swapped, a)
        kb = (kk * kdec).astype(BF16)
        vn = v_ref[0, n * CH:(n + 1) * CH, :]
        kv.append(lax.dot_general(kb, vn, (((0,), (0,)), ((), ())), preferred_element_type=F32))
    sf = kv[0][:DK]
    for n in range(1, NCH):
        st_ref[n, 0:DK, :] = sf
        sf = cf * sf + kv[n][:DK]
    sb = kv[0][DK:]
    for n in range(NCH - 1, 0, -1):
        st_ref[n, DK:2 * DK, :] = sb
        sb = cb * sb + kv[n][DK:]

    def chunk_scores(n):
        a, swapped = chunk(n)
        q = a[:, :DK].astype(BF16)
        k = swapped[:, :DK].astype(BF16)
        return a, swapped, lax.dot_general(q, k, (((1,), (1,)), ((), ())), preferred_element_type=F32)

    nxt = chunk_scores(1)
    for n in range(1, NCH):
        a, swapped, scores = nxt
        nxt = chunk_scores(n + 1) if n + 1 < NCH else None
        p = (scores * mask).astype(BF16)
        vn = v_ref[0, n * CH:(n + 1) * CH, :]
        qq = jnp.where(fwd_lane, a, swapped)
        qd = (qq * qdec).astype(BF16)
        o = (jnp.dot(p, vn, preferred_element_type=F32)
             + jnp.dot(qd, st_ref[n].astype(BF16), preferred_element_type=F32))
        gate = g_ref[0, n * CH:(n + 1) * CH, :].astype(F32)
        o_ref[0, (n - 1) * CH:n * CH, :] = (_rms(o) * _silu(gate)).astype(BF16)


def _retention(proj, log_decay):
    return pl.pallas_call(
        _retention_kernel,
        grid=(NH, B),
        in_specs=[
            pl.BlockSpec(memory_space=pltpu.SMEM),
            pl.BlockSpec((1, RB, 128), lambda h, b: (b, 0, h)),
            pl.BlockSpec((1, RB, 128), lambda h, b: (b, 0, NH + h)),
            pl.BlockSpec((1, RB, 128), lambda h, b: (b, 0, 2 * NH + h)),
        ],
        out_specs=pl.BlockSpec((1, T, 128), lambda h, b: (b, 0, h)),
        out_shape=jax.ShapeDtypeStruct((B, T, NH * HV), BF16),
        scratch_shapes=[pltpu.VMEM((NCH, 128, 128), F32), pltpu.VMEM((CH, 128), F32), pltpu.VMEM((CH, 128), F32),
                        pltpu.VMEM((CH, CH), F32), pltpu.VMEM((2 * DK, 128), F32)],
        compiler_params=_cparams(("arbitrary", "arbitrary")),
        name="retention",
    )(log_decay, proj, proj, proj)


def _diffattn_kernel(lam_ref, *refs, out_scale):
    q_refs = refs[:TQ_SUB]
    k_ref, v_ref, sw_ref, o_ref = refs[TQ_SUB:]
    lam = lam_ref[0]
    k = k_ref[0]
    v = v_ref[0]
    nt = (((1,), (1,)), ((), ()))

    def scores(qh):
        return lax.dot_general(qh, k, nt, preferred_element_type=F32)

    v_ones = jnp.concatenate([v, jnp.ones_like(v)], axis=1)

    def values(s):
        e = jnp.exp2(s - jnp.max(s, axis=-1, keepdims=True))
        ol = jnp.dot(e.astype(BF16), v_ones, preferred_element_type=F32)
        return ol[:, :HV], ol[:, HV:HV + 1]

    halves = []
    for i in range(TQ_SUB):
        q = q_refs[i][0]
        lane = lax.broadcasted_iota(I32, q.shape, 1)
        zero = jnp.zeros_like(q)
        halves += [jnp.where(lane < DK, q, zero), jnp.where(lane >= DK, q, zero)]
    outs = []
    ahead = [scores(h) for h in halves[:SCORE_AHEAD]]
    for c in range(len(halves)):
        if c + SCORE_AHEAD < len(halves):
            ahead.append(scores(halves[c + SCORE_AHEAD]))
        outs.append(values(ahead.pop(0)))
    for i in range(TQ_SUB):
        (o1, l1), (o2, l2) = outs[2 * i], outs[2 * i + 1]
        o = o1 / l1 - o2 * (lam / l2)
        o_ref[0, i * TM_PROJ:(i + 1) * TM_PROJ, :] = (_rms(o) * sw_ref[...] * out_scale).astype(BF16)


def _diffattn(proj, lam, subln_w, out_scale):
    tq = TQ_SUB * TM_PROJ
    nq = T // tq

    def q_map(i, b, h, j):
        return (b, LC // TM_PROJ + j * TQ_SUB + i, 3 * NH + h)

    return pl.pallas_call(
        functools.partial(_diffattn_kernel, out_scale=out_scale),
        grid=(B, NH, nq),
        in_specs=[
            pl.BlockSpec(memory_space=pltpu.SMEM),
            *[pl.BlockSpec((1, TM_PROJ, 128), functools.partial(q_map, i)) for i in range(TQ_SUB)],
            pl.BlockSpec((1, RB, 128), lambda b, h, j: (b, 0, 4 * NH + h)),
            pl.BlockSpec((1, RB, 128), lambda b, h, j: (b, 0, 5 * NH + h)),
            pl.BlockSpec((1, HV), lambda b, h, j: (0, 0)),
        ],
        out_specs=pl.BlockSpec((1, tq, 128), lambda b, h, j: (b, j, h)),
        out_shape=jax.ShapeDtypeStruct((B, T, NH * HV), BF16),
        compiler_params=_cparams(("arbitrary", "arbitrary", "arbitrary")),
        name="diffattn",
    )(lam, *([proj] * TQ_SUB), proj, proj, subln_w)


def _route(bz):
    grp = []
    for g in range(NGRP):
        m = bz[g * EPG:(g + 1) * EPG]
        best = None
        for i in range(EPG):
            for k in range(i + 1, EPG):
                pair = m[i] + m[k]
                best = pair if best is None else jnp.maximum(best, pair)
        grp.append(best)
    gbest = grp[0]
    gsel = jnp.zeros_like(gbest, dtype=I32)
    for g in range(1, NGRP):
        better = grp[g] > gbest
        gsel = jnp.where(better, g, gsel)
        gbest = jnp.where(better, grp[g], gbest)
    cb = [bz[i] for i in range(EPG)]
    for g in range(1, NGRP):
        pick = gsel == g
        cb = [jnp.where(pick, bz[g * EPG + i], cb[i]) for i in range(EPG)]
    i1 = jnp.zeros_like(gsel)
    b1 = cb[0]
    for i in range(1, EPG):
        better = cb[i] > b1
        i1 = jnp.where(better, i, i1)
        b1 = jnp.where(better, cb[i], b1)
    neg = jnp.full_like(b1, -jnp.inf)
    rest = [jnp.where(i1 == i, neg, cb[i]) for i in range(EPG)]
    i2 = jnp.zeros_like(gsel)
    b2 = rest[0]
    for i in range(1, EPG):
        better = rest[i] > b2
        i2 = jnp.where(better, i, i2)
        b2 = jnp.where(better, rest[i], b2)
    lo = jnp.minimum(i1, i2)
    hi = jnp.maximum(i1, i2)
    code = lo * EPG + hi
    pair = jnp.full_like(gsel, len(PAIR_A) - 1)
    for p in range(len(PAIR_A) - 1):
        a, b = min(PAIR_A[p], PAIR_B[p]), max(PAIR_A[p], PAIR_B[p])
        pair = jnp.where(code == a * EPG + b, p, pair)
    return gsel * len(PAIR_A) + pair


def _ffn_prologue(x1, row0, mod, nfw_ref, rw_ref, hp_ref, logit_ref):
    rows = x1.shape[0]
    sh2 = mod[:, 3 * D:4 * D]
    sc2 = mod[:, 4 * D:5 * D]
    h2 = (_rms(x1) * nfw_ref[...]) * (1.0 + sc2) + sh2
    _store_slabs(hp_ref, h2, row0)
    logits = _dot_3pass(rw_ref[...], h2, (((1,), (1,)), ((), ())))
    for k in range(rows // 128):
        logit_ref[:, row0 // 128 + k, :] = logits[:, k * 128:(k + 1) * 128]


def _ffn_out_specs(tm, n_tiles_per_b):
    specs = [
        pl.BlockSpec((1, tm, D), lambda b, j: (b, j, 0)),
        pl.BlockSpec((tm * SLAB, 128), lambda b, j: (b * n_tiles_per_b + j, 0)),
        pl.BlockSpec((NE, tm // 128, 128), lambda b, j: (0, b * n_tiles_per_b + j, 0)),
    ]
    shapes = [
        jax.ShapeDtypeStruct((B, T, D), F32),
        jax.ShapeDtypeStruct((N * SLAB, 128), F32),
        jax.ShapeDtypeStruct((NE, N // 128, 128), F32),
    ]
    return specs, shapes


def _route_kernel(logit_ref, bias_ref, dest_ref, chunk_ref):
    r = logit_ref.shape[1]
    cls = _route([_sigmoid(logit_ref[e]) + bias_ref[e] for e in range(NE)])
    lane_incl = (lax.broadcasted_iota(I32, (128, 128), 0) <= lax.broadcasted_iota(I32, (128, 128), 1)).astype(BF16)
    rows_before = (lax.broadcasted_iota(I32, (r, r), 1) < lax.broadcasted_iota(I32, (r, r), 0)).astype(BF16)
    dest = jnp.zeros((r, 128), F32)
    start = jnp.zeros((1, 128), F32)
    ends = []
    for c in range(NCLS):
        onehot = jnp.where(cls == c, 1.0, 0.0)
        in_row = jnp.dot(onehot.astype(BF16), lane_incl, preferred_element_type=F32)
        row_tot = jnp.broadcast_to(in_row[:, 127:128], (r, 128))
        above = jnp.dot(rows_before, row_tot.astype(BF16), preferred_element_type=F32)
        dest = dest + onehot * (start + above + in_row - 1.0)
        start = start + jnp.sum(row_tot, axis=0, keepdims=True)
        ends.append(start)
    dest_ref[...] = dest.astype(I32)
    chunk_ref[...] = _chunk_list(ends)


def _chunk_list(ends):
    lane = lax.broadcasted_iota(I32, (1, 128), 1).astype(F32)
    offs = [jnp.zeros((1, 128), F32)] + ends[:-1]
    per = [jnp.floor((e - o + (TM_MOE - 1)) * (1.0 / TM_MOE)) for e, o in zip(ends, offs)]
    cum = []
    total = jnp.zeros((1, 128), F32)
    for p in per:
        total = total + p
        cum.append(total)
    m = jnp.minimum(lane, total - 1.0)
    cls = jnp.zeros((1, 128), F32)
    for c in range(NCLS):
        cls = cls + jnp.where(cum[c] <= m, 1.0, 0.0)
    start = jnp.zeros((1, 128), F32)
    for c in range(NCLS):
        start = jnp.where(cls == c, offs[c] + TM_MOE * (m - (cum[c] - per[c])), start)
    n_pair = len(PAIR_A)
    grp = jnp.zeros((1, 128), F32)
    for g in range(1, NGRP):
        grp = grp + jnp.where(cls >= g * n_pair, 1.0, 0.0)
    pair = cls - n_pair * grp
    loc_a = jnp.zeros((1, 128), F32)
    loc_b = jnp.zeros((1, 128), F32)
    for p in range(n_pair):
        loc_a = jnp.where(pair == p, float(PAIR_A[p]), loc_a)
        loc_b = jnp.where(pair == p, float(PAIR_B[p]), loc_b)
    sub = lax.broadcasted_iota(I32, (128, 128), 0).astype(F32)
    lan = lax.broadcasted_iota(I32, (128, 128), 1).astype(F32)
    none = 1000.0

    def changes(e):
        chg = jnp.where((lane == 0.0) | (e != pltpu.roll(e, 1, axis=1)), 1.0, 0.0)
        chg_col = jnp.sum(jnp.where(sub == lan, chg, 0.0), axis=1, keepdims=True)
        e_col = jnp.sum(jnp.where(sub == lan, e, 0.0), axis=1, keepdims=True)
        nxt_at = jnp.min(jnp.where((sub > lan) & (chg_col == 1.0), sub, none), axis=0, keepdims=True)
        nxt_e = jnp.sum(jnp.where(sub == nxt_at, e_col, 0.0), axis=0, keepdims=True)
        return chg, jnp.where(nxt_at < none, nxt_e, -1.0)

    e_a = EPG * grp + loc_a
    e_b = EPG * grp + loc_b
    chg_a, nxt_a = changes(e_a)
    chg_b, nxt_b = changes(e_b)
    return jnp.concatenate([start, e_a, e_b, chg_a, chg_b, nxt_a, nxt_b, total], axis=0).astype(I32)


def _route_tokens(logits_t, router_b):
    r = N // 128
    return pl.pallas_call(
        _route_kernel,
        in_specs=[pl.BlockSpec((NE, r, 128), lambda: (0, 0, 0)), pl.BlockSpec(memory_space=pltpu.SMEM)],
        out_specs=[pl.BlockSpec((r, 128), lambda: (0, 0)), pl.BlockSpec((8, 128), lambda: (0, 0))],
        out_shape=[jax.ShapeDtypeStruct((r, 128), I32), jax.ShapeDtypeStruct((8, 128), I32)],
        compiler_params=pltpu.CompilerParams(vmem_limit_bytes=VMEM_LIMIT),
        name="route",
    )(logits_t, router_b)


def _outproj_kernel(ret_ref, dif_ref, w_ref, x_ref, mod_ref, nfw_ref, rw_ref, x1_ref, hp_ref, logit_ref):
    mod = mod_ref[0]

    def mix(r0):
        rows = slice(r0, r0 + SUB_OUT)
        return (jnp.dot(ret_ref[0, rows, :], w_ref[0:NH * HV, :], preferred_element_type=F32)
                + jnp.dot(dif_ref[0, rows, :], w_ref[NH * HV:, :], preferred_element_type=F32))

    mx = mix(0)
    for r0 in range(0, TM_OUT, SUB_OUT):
        mx_next = mix(r0 + SUB_OUT) if r0 + SUB_OUT < TM_OUT else None
        x1 = x_ref[0, r0:r0 + SUB_OUT, :] + mod[:, 2 * D:3 * D] * mx
        x1_ref[0, r0:r0 + SUB_OUT, :] = x1
        _ffn_prologue(x1, r0, mod, nfw_ref, rw_ref, hp_ref, logit_ref)
        mx = mx_next


def _outproj(ret, dif, w_out, x, mod3, nfw, rw_t):
    nj = T // TM_OUT
    out_specs, out_shapes = _ffn_out_specs(TM_OUT, nj)
    return pl.pallas_call(
        _outproj_kernel,
        grid=(B, nj),
        in_specs=[
            pl.BlockSpec((1, TM_OUT, NH * HV), lambda b, j: (b, j, 0)),
            pl.BlockSpec((1, TM_OUT, NH * HV), lambda b, j: (b, j, 0)),
            pl.BlockSpec((2 * NH * HV, D), lambda b, j: (0, 0)),
            pl.BlockSpec((1, TM_OUT, D), lambda b, j: (b, j, 0)),
            pl.BlockSpec((1, 1, 6 * D), lambda b, j: (b, 0, 0)),
            pl.BlockSpec((1, D), lambda b, j: (0, 0)),
            pl.BlockSpec((NE, D), lambda b, j: (0, 0)),
        ],
        out_specs=out_specs,
        out_shape=out_shapes,
        compiler_params=_cparams(("arbitrary", "arbitrary")),
        name="outproj",
    )(ret, dif, w_out, x, mod3, nfw, rw_t)


def _pool_kernel(x_ref, prev_ref, next_ref, mod_ref, nmw_ref, pw_ref, ps_ref, nfw_ref, rw_ref,
                 x1_ref, hp_ref, logit_ref):
    i = pl.program_id(1)
    last = pl.num_programs(1) - 1
    mod = mod_ref[0]
    sh1 = mod[:, 0:D]
    sc1 = mod[:, D:2 * D]

    def modnorm(v):
        return (_rms(v) * nmw_ref[...]) * (1.0 + sc1) + sh1

    x = x_ref[0]
    hc = modnorm(x)
    zeros = jnp.zeros((POOL_MARGIN - HALO, D), F32)
    ext = jnp.concatenate([zeros, jnp.where(i > 0, modnorm(prev_ref[0]), 0.0), hc,
                           jnp.where(i < last, modnorm(next_ref[0]), 0.0), zeros], axis=0)
    n_ext = ext.shape[0]
    pos = i * TM_POOL + lax.broadcasted_iota(I32, (TM_POOL, 1), 0)
    mixed = []
    run = ext
    for gi, w in enumerate(POOL_WINDOWS):
        left = w // 2
        right = w - 1 - left
        cols = slice(gi * PG, (gi + 1) * PG)
        run = run + pltpu.roll(run, n_ext - w // 2, axis=0)
        tot = run[POOL_MARGIN - left:POOL_MARGIN - left + TM_POOL, 0:PG]
        if gi + 1 < len(POOL_WINDOWS):
            run = run[:, PG:]
        cnt = (jnp.minimum(pos + right + 1, T) - jnp.maximum(pos - left, 0)).astype(F32)
        pooled = (tot * (1.0 / cnt) - hc[:, cols]).astype(BF16)
        mixed.append(jnp.dot(pooled, pw_ref[gi], preferred_element_type=F32))
    mixed = jnp.concatenate(mixed, axis=1) * ps_ref[...]
    x1 = x + mod[:, 2 * D:3 * D] * mixed
    x1_ref[0] = x1
    _ffn_prologue(x1, 0, mod, nfw_ref, rw_ref, hp_ref, logit_ref)


def _pool_layer(x, mod3, nmw, pool_w, pool_scale, nfw, rw_t):
    ni = T // TM_POOL
    hb = TM_POOL // HALO
    out_specs, out_shapes = _ffn_out_specs(TM_POOL, ni)
    return pl.pallas_call(
        _pool_kernel,
        grid=(B, ni),
        in_specs=[
            pl.BlockSpec((1, TM_POOL, D), lambda b, i: (b, i, 0)),
            pl.BlockSpec((1, HALO, D), lambda b, i: (b, jnp.maximum(i * hb - 1, 0), 0)),
            pl.BlockSpec((1, HALO, D), lambda b, i: (b, jnp.minimum((i + 1) * hb, T // HALO - 1), 0)),
            pl.BlockSpec((1, 1, 6 * D), lambda b, i: (b, 0, 0)),
            pl.BlockSpec((1, D), lambda b, i: (0, 0)),
            pl.BlockSpec((len(POOL_WINDOWS), PG, PG), lambda b, i: (0, 0, 0)),
            pl.BlockSpec((1, D), lambda b, i: (0, 0)),
            pl.BlockSpec((1, D), lambda b, i: (0, 0)),
            pl.BlockSpec((NE, D), lambda b, i: (0, 0)),
        ],
        out_specs=out_specs,
        out_shape=out_shapes,
        compiler_params=_cparams(("arbitrary", "arbitrary")),
        name="pool_layer",
    )(x, x, x, mod3, nmw, pool_w, pool_scale, nfw, rw_t)


def _tile_copy(src_ref, dst_ref, sem, s, d, rows=1):
    s0 = pl.multiple_of(s * SLAB, SLAB)
    d0 = pl.multiple_of(d * SLAB, SLAB)
    return pltpu.make_async_copy(src_ref.at[pl.ds(s0, rows * SLAB)], dst_ref.at[pl.ds(d0, rows * SLAB)], sem)


def _issue_tile_copies(idx_ref, base, rows, start_one):
    def group(g, carry):
        r0 = g * PERM_UNROLL
        ids = [idx_ref[base + r0 + u] for u in range(PERM_UNROLL)]
        for u in range(PERM_UNROLL):
            start_one(r0 + u, ids[u], u % 2)
        return carry

    lax.fori_loop(0, rows // PERM_UNROLL, group, 0)


def _dispatch_kernel(dest_ref, src_ref, dst_ref, zero_ref, sem, pad_sem):
    base = pl.program_id(0) * TM_PERM

    @pl.when(pl.program_id(0) == 0)
    def _():
        zero_ref[...] = jnp.zeros_like(zero_ref)
        pad = _tile_copy(zero_ref, dst_ref, pad_sem, 0, N, TM_MOE)
        pad.start()
        pad.wait()

    def start_one(r, d, priority):
        _tile_copy(src_ref, dst_ref, sem, r, d).start(priority=priority)

    _issue_tile_copies(dest_ref, base, TM_PERM, start_one)
    _tile_copy(src_ref, dst_ref, sem, 0, 0, TM_PERM).wait()


def _dispatch(dest, src):
    return pl.pallas_call(
        _dispatch_kernel,
        grid_spec=pltpu.PrefetchScalarGridSpec(
            num_scalar_prefetch=1,
            grid=(N // TM_PERM,),
            in_specs=[pl.BlockSpec((TM_PERM * SLAB, 128), lambda i, dest: (i, 0))],
            out_specs=pl.BlockSpec(memory_space=pl.ANY),
            scratch_shapes=[pltpu.VMEM((TM_MOE * SLAB, 128), F32), pltpu.SemaphoreType.DMA(()),
                            pltpu.SemaphoreType.DMA(())],
        ),
        out_shape=jax.ShapeDtypeStruct(((N + TM_MOE) * SLAB, 128), src.dtype),
        compiler_params=_cparams(("arbitrary",)),
        name="dispatch",
    )(dest, src)


def _combine_kernel(dest_ref, x_ref, ys_ref, mod_ref, fw_ref, o_ref, ybuf_ref, sem, *, final):
    i = pl.program_id(0)
    n = pl.num_programs(0)
    slot = i % 2

    def gather(step, to_slot):
        def start_one(r, d, priority):
            _tile_copy(ys_ref, ybuf_ref.at[to_slot], sem.at[to_slot], d, r).start(priority=priority)

        _issue_tile_copies(dest_ref, step * TM_COMB, TM_COMB, start_one)

    @pl.when(i == 0)
    def _():
        gather(0, 0)

    @pl.when(i + 1 < n)
    def _():
        gather(i + 1, 1 - slot)

    _tile_copy(ys_ref, ybuf_ref.at[slot], sem.at[slot], 0, 0, TM_COMB).wait()
    out = x_ref[...] + mod_ref[0][:, 5 * D:6 * D] * _load_slabs(ybuf_ref.at[slot], TM_COMB)
    if final:
        out = _rms(out) * fw_ref[...]
    o_ref[...] = out


def _combine(dest, x1, ys, mod3, final_w, final):
    per_b = T // TM_COMB
    return pl.pallas_call(
        functools.partial(_combine_kernel, final=final),
        grid_spec=pltpu.PrefetchScalarGridSpec(
            num_scalar_prefetch=1,
            grid=(N // TM_COMB,),
            in_specs=[
                pl.BlockSpec((TM_COMB, D), lambda i, dest: (i, 0)),
                pl.BlockSpec(memory_space=pl.ANY),
                pl.BlockSpec((1, 1, 6 * D), lambda i, dest: (i // per_b, 0, 0)),
                pl.BlockSpec((1, D), lambda i, dest: (0, 0)),
            ],
            out_specs=pl.BlockSpec((TM_COMB, D), lambda i, dest: (i, 0)),
            scratch_shapes=[pltpu.VMEM((2, TM_COMB * SLAB, 128), F32), pltpu.SemaphoreType.DMA((2,))],
        ),
        out_shape=jax.ShapeDtypeStruct((N, D), F32),
        compiler_params=_cparams(("arbitrary",)),
        name="combine",
    )(dest, x1.reshape(N, D), ys, mod3, final_w)


def _moe_kernel(chunk_ref, rw_ref, xs_hbm, wg_hbm, wu_hbm, wd_hbm, ys_hbm,
                xbuf_ref, obuf_ref,
                sga_ref, sua_ref, sda_ref, sgb_ref, sub_ref, sdb_ref,
                ga_ref, ua_ref, da_ref, gb_ref, ub_ref, db_ref, wsem, xsem, osem, *, layer):
    row = lambda r: (lambda m: chunk_ref[r, m])
    start_of, n = row(0), chunk_ref[7, 0]
    slots = ((row(1), row(3), row(5), (sga_ref, sua_ref, sda_ref), (ga_ref, ua_ref, da_ref)),
             (row(2), row(4), row(6), (sgb_ref, sub_ref, sdb_ref), (gb_ref, ub_ref, db_ref)))

    def fetch(w_slot, expert):
        stage = slots[w_slot][3]
        return [pltpu.make_async_copy(w.at[layer, expert], s, wsem.at[w_slot])
                for w, s in zip((wg_hbm, wu_hbm, wd_hbm), stage)]

    def x_copy(step, buf):
        return _tile_copy(xs_hbm, xbuf_ref.at[buf], xsem.at[buf], start_of(step), 0, TM_MOE)

    def o_copy(step, buf):
        return _tile_copy(obuf_ref.at[buf], ys_hbm, osem.at[buf], 0, start_of(step), TM_MOE)

    @pl.when(pl.program_id(0) == 0)
    def _():
        obuf_ref[1] = jnp.zeros_like(obuf_ref[1])
        pad = _tile_copy(obuf_ref.at[1], ys_hbm, osem.at[1], 0, N, TM_MOE)
        pad.start()
        pad.wait()
        x_copy(0, 0).start()

    def chunk(m, slot):
        @pl.when(m + 1 < n)
        def _():
            x_copy(m + 1, 1 - slot).start()

        for w_slot, (expert_of, changed_at, next_of, stage, work) in enumerate(slots):
            @pl.when(m == 0)
            def _():
                for cp in fetch(w_slot, expert_of(0)):
                    cp.start(priority=1)

            @pl.when(changed_at(m) == 1)
            def _():
                for cp in fetch(w_slot, 0):
                    cp.wait()
                for s, w in zip(stage, work):
                    w[...] = s[...].astype(BF16)

                @pl.when(next_of(m) >= 0)
                def _():
                    for cp in fetch(w_slot, next_of(m)):
                        cp.start(priority=1)

        x_copy(m, slot).wait()
        hf = _load_slabs(xbuf_ref.at[slot], TM_MOE)
        h = hf.astype(BF16)

        def up(w_ref):
            return jnp.dot(h, w_ref[...], preferred_element_type=F32)

        def down(g, u, d_ref):
            return jnp.dot((_silu(g) * u).astype(BF16), d_ref[...], preferred_element_type=F32)

        g_a, u_a, g_b, u_b = up(ga_ref), up(ua_ref), up(gb_ref), up(ub_ref)
        y_a = down(g_a, u_a, da_ref)
        y_b = down(g_b, u_b, db_ref)
        s_a = _sigmoid(jnp.sum(hf * rw_ref[pl.ds(slots[0][0](m), 1), :], axis=1, keepdims=True))
        s_b = _sigmoid(jnp.sum(hf * rw_ref[pl.ds(slots[1][0](m), 1), :], axis=1, keepdims=True))
        denom = s_a + s_b
        val = (s_a / denom) * y_a + (s_b / denom) * y_b

        @pl.when(m >= 1)
        def _():
            o_copy(m - 1, 1 - slot).wait()

        _store_slabs(obuf_ref.at[slot], val)
        o_copy(m, slot).start()

        @pl.when(m == n - 1)
        def _():
            o_copy(m, slot).wait()

    for j in range(2):
        m = pl.program_id(0) * 2 + j
        pl.when(m < n)(functools.partial(chunk, m, j))


def _moe_sorted(chunks, xs, rw_t, wg, wu, wd, layer):
    mats = lambda dt: [pltpu.VMEM((D, DE), dt), pltpu.VMEM((D, DE), dt), pltpu.VMEM((DE, D), dt)]
    hbm = pl.BlockSpec(memory_space=pl.ANY)
    return pl.pallas_call(
        functools.partial(_moe_kernel, layer=layer),
        grid_spec=pltpu.PrefetchScalarGridSpec(
            num_scalar_prefetch=1,
            grid=(N_UNITS // 2,),
            in_specs=[pl.BlockSpec((NE, D), lambda m, *_: (0, 0)), hbm, hbm, hbm, hbm],
            out_specs=hbm,
            scratch_shapes=([pltpu.VMEM((2, TM_MOE * SLAB, 128), F32), pltpu.VMEM((2, TM_MOE * SLAB, 128), F32)]
                            + mats(F32) + mats(F32) + mats(BF16) + mats(BF16)
                            + [pltpu.SemaphoreType.DMA((2,))] * 3),
        ),
        out_shape=jax.ShapeDtypeStruct(xs.shape, F32),
        compiler_params=_cparams(("arbitrary",)),
        name="moe_sorted",
    )(chunks, rw_t, xs, wg, wu, wd)


def _moe_layer(x1, hp, logits, rw_t, router_b, wg, wu, wd, layer, mod3, final_w, final):
    dest, chunks = _route_tokens(logits, router_b)
    dest = dest.reshape(N)
    xs = _dispatch(dest, hp)
    ys = _moe_sorted(chunks, xs, rw_t, wg, wu, wd, layer)
    return _combine(dest, x1, ys, mod3, final_w, final).reshape(B, T, D)


def _rope_tables():
    half = 16
    inv = ROPE_BASE ** (-np.arange(half, dtype=np.float64) / half)
    t = np.arange(T)
    ang_r = (t // GRID_W)[:, None] * inv[None, :]
    ang_c = (t % GRID_W)[:, None] * inv[None, :]
    ang = np.concatenate([ang_r, ang_r, ang_c, ang_c], axis=1)
    sign = np.tile(np.concatenate([-np.ones(half), np.ones(half)]), 2)
    cos = np.concatenate([np.ones((LC, 64)), np.cos(ang)], axis=0)
    sin = np.concatenate([np.zeros((LC, 64)), np.sin(ang) * sign[None, :]], axis=0)
    return (jnp.asarray(np.tile(cos, (1, 2)), dtype=F32), jnp.asarray(np.tile(sin, (1, 2)), dtype=F32))


def _permute_w_in(w):
    rq = w[:, 0:256].reshape(D, NH, DK)
    dq = w[:, 256:768]
    rg = w[:, 768:1280]
    rk = w[:, 1280:1536].reshape(D, NH, DK)
    rv = w[:, 1536:2048]
    dk = w[:, 2048:2560]
    dv = w[:, 2560:3072]
    qk = jnp.concatenate([rq, rk * (DK ** -0.5)], axis=2).reshape(D, NH * 2 * DK)
    return jnp.concatenate([qk, rv, rg, dq * (DK ** -0.5 * math.log2(math.e)), dk, dv], axis=1).astype(BF16)


def kernel(x, c, ctx, c_ctx, ada_w, ada_b, norm_mix_w, norm_ffn_w, w_in, w_out, ret_log_decay, diff_lambda,
           diff_subln_w, pool_w, pool_scale, router_w, router_b, moe_w_gate, moe_w_up, moe_w_down, final_norm_w):
    assert x.shape == (B, T, D) and ctx.shape == (B, LC, D) and ada_w.shape[0] == 2
    cc = jnp.concatenate([c, c_ctx[None, :], jnp.zeros((16 - B - 1, D), F32)], axis=0)
    mod = _ada_mod(cc, ada_w, ada_b)
    rw_t = router_w.T
    fw = final_norm_w.reshape(1, D)
    experts = (moe_w_gate, moe_w_up, moe_w_down)

    mod0 = mod[0].reshape(16, 1, 6 * D)
    cos_t, sin_t = _rope_tables()
    proj = _inproj(x, ctx, mod0, norm_mix_w[0:1], _permute_w_in(w_in[0]), cos_t, sin_t)
    ret = _retention(proj, ret_log_decay[0])
    lam_init = 0.8 - 0.6 * math.exp(-0.3 * 0)
    lv = diff_lambda[0]
    lam = jnp.exp(jnp.sum(lv[0] * lv[1])) - jnp.exp(jnp.sum(lv[2] * lv[3])) + lam_init
    dif = _diffattn(proj, lam.reshape(1), diff_subln_w[0:1], 1.0 - lam_init)
    x1, hp, logits = _outproj(ret, dif, w_out[0].astype(BF16), x, mod0, norm_ffn_w[0:1], rw_t)
    x2 = _moe_layer(x1, hp, logits, rw_t, router_b, *experts, 0, mod0, fw, False)

    mod1 = mod[1].reshape(16, 1, 6 * D)
    x3, hp, logits = _pool_layer(x2, mod1, norm_mix_w[1:2], pool_w[0].astype(BF16), pool_scale[0:1],
                                 norm_ffn_w[1:2], rw_t)
    return _moe_layer(x3, hp, logits, rw_t, router_b, *experts, 1, mod1, fw, True)
```

```python
import functools
import math

import jax
import jax.numpy as jnp
import numpy as np
from jax import lax
from jax.experimental import pallas as pl
from jax.experimental.pallas import tpu as pltpu

F32 = jnp.float32
BF16 = jnp.bfloat16
I32 = jnp.int32

D = 1024
B = 8
T = 2048
N = B * T
GRID_W = 64
LC = 256
EPS = 1e-6
ROPE_BASE = 10000.0
NH = 4
DK = 64
HV = 128
CH = 256
RB = LC + T
NCH = RB // CH
POOL_WINDOWS = (2, 4, 8, 16)
PG = D // len(POOL_WINDOWS)
NE = 16
NGRP = 4
EPG = NE // NGRP
DE = 512
IN_W = 3072
HALO = 8
POOL_MARGIN = 16
assert POOL_WINDOWS == tuple(2 ** (k + 1) for k in range(len(POOL_WINDOWS))) and POOL_WINDOWS[-1] <= POOL_MARGIN

PAIR_A = (0, 0, 0, 1, 1, 3)
PAIR_B = (1, 2, 3, 3, 2, 2)
NCLS = NGRP * len(PAIR_A)
SLAB = D // 128

TM_PROJ = 256
PROJ_SUB = 3
TM_OUT = 1024
SUB_OUT = 512
TQ_SUB = 8
SCORE_AHEAD = 1
TM_POOL = 1024
TM_MOE = 256
N_UNITS = N // TM_MOE + NCLS
TM_PERM = 4096
TM_COMB = 512
PERM_UNROLL = 16
VMEM_LIMIT = 56 * 1024 * 1024


def _cparams(sem):
    return pltpu.CompilerParams(dimension_semantics=sem, vmem_limit_bytes=VMEM_LIMIT)


def _sigmoid(x):
    return 1.0 / (1.0 + jnp.exp(-x))


def _silu(x):
    return x * _sigmoid(x)


def _rms(x):
    return x * lax.rsqrt(jnp.mean(x * x, axis=-1, keepdims=True) + EPS)


def _dot_3pass(a, b, dims):
    a_hi = a.astype(BF16)
    b_hi = b.astype(BF16)
    a_lo = (a - a_hi.astype(F32)).astype(BF16)
    b_lo = (b - b_hi.astype(F32)).astype(BF16)

    def dot(x, y):
        return lax.dot_general(x, y, dims, preferred_element_type=F32)

    return dot(a_hi, b_hi) + (dot(a_lo, b_hi) + dot(a_hi, b_lo))


def _load_slabs(ref, rows):
    return jnp.concatenate([ref[pl.ds(s, rows, stride=SLAB), :] for s in range(SLAB)], axis=1)


def _store_slabs(ref, val, row0=0):
    rows = val.shape[0]
    for s in range(SLAB):
        ref[pl.ds(row0 * SLAB + s, rows, stride=SLAB), :] = val[:, s * 128:(s + 1) * 128]


def _ada_kernel(cc_ref, w_ref, b_ref, o_ref):
    s = _silu(cc_ref[...])
    o_ref[0] = _dot_3pass(s, w_ref[0], (((1,), (0,)), ((), ()))) + b_ref[0]


def _ada_mod(cc, ada_w, ada_b):
    depth = ada_w.shape[0]
    tn = 1536
    return pl.pallas_call(
        _ada_kernel,
        grid=(depth, 6 * D // tn),
        in_specs=[
            pl.BlockSpec((16, D), lambda l, n: (0, 0)),
            pl.BlockSpec((1, D, tn), lambda l, n: (l, 0, n)),
            pl.BlockSpec((1, 1, tn), lambda l, n: (l, 0, n)),
        ],
        out_specs=pl.BlockSpec((1, 16, tn), lambda l, n: (l, 0, n)),
        out_shape=jax.ShapeDtypeStruct((depth, 16, 6 * D), F32),
        compiler_params=_cparams(("arbitrary", "arbitrary")),
        name="ada_mod",
    )(cc, ada_w, ada_b.reshape(depth, 1, 6 * D))


def _rope(seg, cos, sin_signed, lo_mask):
    w = seg.shape[1]
    from_hi = pltpu.roll(seg, w - 16, axis=1)
    from_lo = pltpu.roll(seg, 16, axis=1)
    partner = jnp.where(lo_mask, from_hi, from_lo)
    reps = w // cos.shape[1]
    c = jnp.concatenate([cos] * reps, axis=1)
    s = jnp.concatenate([sin_signed] * reps, axis=1)
    return seg * c + partner * s


def _inproj_kernel(*refs):
    x_refs = refs[:PROJ_SUB]
    c_ref, mod_ref, cmod_ref, nw_ref, w_ref, cos_ref, sin_ref, o_ref = refs[PROJ_SUB:]
    is_ctx = pl.program_id(1) == 0
    parts = []
    for s in range(PROJ_SUB):
        xt = x_refs[s][0]
        sh = mod_ref[0, :, 0:D]
        sc = mod_ref[0, :, D:2 * D]
        if s == 0:
            xt = jnp.where(is_ctx, c_ref[0], xt)
            sh = jnp.where(is_ctx, cmod_ref[0, :, 0:D], sh)
            sc = jnp.where(is_ctx, cmod_ref[0, :, D:2 * D], sc)
        parts.append(((_rms(xt) * nw_ref[...]) * (1.0 + sc) + sh).astype(BF16))
    hb = jnp.concatenate(parts, axis=0)
    lane = lax.broadcasted_iota(I32, (PROJ_SUB * TM_PROJ, 512), 1)
    lo_mask = (lane % 32) < 16
    cos = cos_ref[...]
    sin = sin_ref[...]

    def project(g):
        return jnp.dot(hb, w_ref[:, g * 512:(g + 1) * 512], preferred_element_type=F32)

    seg = project(0)
    for g in range(6):
        seg_next = project(g + 1) if g + 1 < 6 else None
        if g in (0, 3, 4):
            seg = _rope(seg, cos, sin, lo_mask)
        o_ref[0, :, g * 512:(g + 1) * 512] = seg.astype(BF16)
        seg = seg_next


def _inproj(x, ctx, mod3, norm_w, w_perm, cos_t, sin_t):
    tm = PROJ_SUB * TM_PROJ
    nj = RB // tm

    def x_map(s, b, j):
        return (b, jnp.maximum(PROJ_SUB * j + s - LC // TM_PROJ, 0), 0)

    return pl.pallas_call(
        _inproj_kernel,
        grid=(B, nj),
        in_specs=[
            *[pl.BlockSpec((1, TM_PROJ, D), functools.partial(x_map, s)) for s in range(PROJ_SUB)],
            pl.BlockSpec((1, LC, D), lambda b, j: (b, 0, 0)),
            pl.BlockSpec((1, 1, 2 * D), lambda b, j: (b, 0, 0)),
            pl.BlockSpec((1, 1, 2 * D), lambda b, j: (B, 0, 0)),
            pl.BlockSpec((1, D), lambda b, j: (0, 0)),
            pl.BlockSpec((D, IN_W), lambda b, j: (0, 0)),
            pl.BlockSpec((tm, 128), lambda b, j: (j, 0)),
            pl.BlockSpec((tm, 128), lambda b, j: (j, 0)),
        ],
        out_specs=pl.BlockSpec((1, tm, IN_W), lambda b, j: (b, j, 0)),
        out_shape=jax.ShapeDtypeStruct((B, RB, IN_W), BF16),
        compiler_params=_cparams(("arbitrary", "arbitrary")),
        name="inproj",
    )(*([x] * PROJ_SUB), ctx, mod3, mod3, norm_w, w_perm, cos_t, sin_t)


def _retention_kernel(ld_ref, qk_ref, v_ref, g_ref, o_ref, st_ref, kdec_ref, qdec_ref, mask_ref, cdec_ref):
    h = pl.program_id(0)
    lane = lax.broadcasted_iota(I32, (CH, 128), 1)
    fwd_lane = lane < DK

    @pl.when(pl.program_id(1) == 0)
    def _():
        lgf = ld_ref[0, h]
        lgb = ld_ref[1, h]
        pos = lax.broadcasted_iota(I32, (CH, 128), 0).astype(F32)
        kdec_ref[...] = jnp.where(fwd_lane, jnp.exp(lgf * (CH - 1 - pos)), jnp.exp(lgb * pos))
        qdec_ref[...] = jnp.where(fwd_lane, jnp.exp(lgf * (pos + 1.0)), jnp.exp(lgb * (CH - pos)))
        ii = lax.broadcasted_iota(I32, (CH, CH), 0)
        jj = lax.broadcasted_iota(I32, (CH, CH), 1)
        gap = (ii - jj).astype(F32)
        mask_ref[...] = (jnp.where(gap >= 0, jnp.exp(lgf * jnp.maximum(gap, 0.0)), 0.0)
                         + jnp.where(gap <= 0, jnp.exp(lgb * jnp.maximum(-gap, 0.0)), 0.0))
        ones = jnp.ones((DK, 128), F32)
        cdec_ref[0:DK, :] = jnp.exp(lgf * CH * ones)
        cdec_ref[DK:, :] = jnp.exp(lgb * CH * ones)

    kdec = kdec_ref[...]
    qdec = qdec_ref[...]
    mask = mask_ref[...]
    cf = cdec_ref[0:DK, :]
    cb = cdec_ref[DK:, :]

    def chunk(n):
        a = qk_ref[0, n * CH:(n + 1) * CH, :].astype(F32)
        swapped = pltpu.roll(a, DK, axis=1)
        return a, swapped

    kv = []
    for n in range(NCH):
        a, swapped = chunk(n)
        kk = jnp.where(fwd_lane, swapped, a)
        kb = (kk * kdec).astype(BF16)
        vn = v_ref[0, n * CH:(n + 1) * CH, :]
        kv.append(lax.dot_general(kb, vn, (((0,), (0,)), ((), ())), preferred_element_type=F32))
    sf = kv[0][:DK]
    for n in range(1, NCH):
        st_ref[n, 0:DK, :] = sf
        sf = cf * sf + kv[n][:DK]
    sb = kv[0][DK:]
    for n in range(NCH - 1, 0, -1):
        st_ref[n, DK:2 * DK, :] = sb
        sb = cb * sb + kv[n][DK:]

    def chunk_scores(n):
        a, swapped = chunk(n)
        q = a[:, :DK].astype(BF16)
        k = swapped[:, :DK].astype(BF16)
        return a, swapped, lax.dot_general(q, k, (((1,), (1,)), ((), ())), preferred_element_type=F32)

    nxt = chunk_scores(1)
    for n in range(1, NCH):
        a, swapped, scores = nxt
        nxt = chunk_scores(n + 1) if n + 1 < NCH else None
        p = (scores * mask).astype(BF16)
        vn = v_ref[0, n * CH:(n + 1) * CH, :]
        qq = jnp.where(fwd_lane, a, swapped)
        qd = (qq * qdec).astype(BF16)
        o = (jnp.dot(p, vn, preferred_element_type=F32)
             + jnp.dot(qd, st_ref[n].astype(BF16), preferred_element_type=F32))
        gate = g_ref[0, n * CH:(n + 1) * CH, :].astype(F32)
        o_ref[0, (n - 1) * CH:n * CH, :] = (_rms(o) * _silu(gate)).astype(BF16)


def _retention(proj, log_decay):
    return pl.pallas_call(
        _retention_kernel,
        grid=(NH, B),
        in_specs=[
            pl.BlockSpec(memory_space=pltpu.SMEM),
            pl.BlockSpec((1, RB, 128), lambda h, b: (b, 0, h)),
            pl.BlockSpec((1, RB, 128), lambda h, b: (b, 0, NH + h)),
            pl.BlockSpec((1, RB, 128), lambda h, b: (b, 0, 2 * NH + h)),
        ],
        out_specs=pl.BlockSpec((1, T, 128), lambda h, b: (b, 0, h)),
        out_shape=jax.ShapeDtypeStruct((B, T, NH * HV), BF16),
        scratch_shapes=[pltpu.VMEM((NCH, 128, 128), F32), pltpu.VMEM((CH, 128), F32), pltpu.VMEM((CH, 128), F32),
                        pltpu.VMEM((CH, CH), F32), pltpu.VMEM((2 * DK, 128), F32)],
        compiler_params=_cparams(("arbitrary", "arbitrary")),
        name="retention",
    )(log_decay, proj, proj, proj)


def _diffattn_kernel(lam_ref, *refs, out_scale):
    q_refs = refs[:TQ_SUB]
    k_ref, v_ref, sw_ref, o_ref = refs[TQ_SUB:]
    lam = lam_ref[0]
    k = k_ref[0]
    v = v_ref[0]
    nt = (((1,), (1,)), ((), ()))

    def scores(qh):
        return lax.dot_general(qh, k, nt, preferred_element_type=F32)

    v_ones = jnp.concatenate([v, jnp.ones_like(v)], axis=1)

    def values(s):
        e = jnp.exp2(s - jnp.max(s, axis=-1, keepdims=True))
        ol = jnp.dot(e.astype(BF16), v_ones, preferred_element_type=F32)
        return ol[:, :HV], ol[:, HV:HV + 1]

    halves = []
    for i in range(TQ_SUB):
        q = q_refs[i][0]
        lane = lax.broadcasted_iota(I32, q.shape, 1)
        zero = jnp.zeros_like(q)
        halves += [jnp.where(lane < DK, q, zero), jnp.where(lane >= DK, q, zero)]
    outs = []
    ahead = [scores(h) for h in halves[:SCORE_AHEAD]]
    for c in range(len(halves)):
        if c + SCORE_AHEAD < len(halves):
            ahead.append(scores(halves[c + SCORE_AHEAD]))
        outs.append(values(ahead.pop(0)))
    for i in range(TQ_SUB):
        (o1, l1), (o2, l2) = outs[2 * i], outs[2 * i + 1]
        o = o1 / l1 - o2 * (lam / l2)
        o_ref[0, i * TM_PROJ:(i + 1) * TM_PROJ, :] = (_rms(o) * sw_ref[...] * out_scale).astype(BF16)


def _diffattn(proj, lam, subln_w, out_scale):
    tq = TQ_SUB * TM_PROJ
    nq = T // tq

    def q_map(i, b, h, j):
        return (b, LC // TM_PROJ + j * TQ_SUB + i, 3 * NH + h)

    return pl.pallas_call(
        functools.partial(_diffattn_kernel, out_scale=out_scale),
        grid=(B, NH, nq),
        in_specs=[
            pl.BlockSpec(memory_space=pltpu.SMEM),
            *[pl.BlockSpec((1, TM_PROJ, 128), functools.partial(q_map, i)) for i in range(TQ_SUB)],
            pl.BlockSpec((1, RB, 128), lambda b, h, j: (b, 0, 4 * NH + h)),
            pl.BlockSpec((1, RB, 128), lambda b, h, j: (b, 0, 5 * NH + h)),
            pl.BlockSpec((1, HV), lambda b, h, j: (0, 0)),
        ],
        out_specs=pl.BlockSpec((1, tq, 128), lambda b, h, j: (b, j, h)),
        out_shape=jax.ShapeDtypeStruct((B, T, NH * HV), BF16),
        compiler_params=_cparams(("arbitrary", "arbitrary", "arbitrary")),
        name="diffattn",
    )(lam, *([proj] * TQ_SUB), proj, proj, subln_w)


def _route(bz):
    grp = []
    for g in range(NGRP):
        m = bz[g * EPG:(g + 1) * EPG]
        best = None
        for i in range(EPG):
            for k in range(i + 1, EPG):
                pair = m[i] + m[k]
                best = pair if best is None else jnp.maximum(best, pair)
        grp.append(best)
    gbest = grp[0]
    gsel = jnp.zeros_like(gbest, dtype=I32)
    for g in range(1, NGRP):
        better = grp[g] > gbest
        gsel = jnp.where(better, g, gsel)
        gbest = jnp.where(better, grp[g], gbest)
    cb = [bz[i] for i in range(EPG)]
    for g in range(1, NGRP):
        pick = gsel == g
        cb = [jnp.where(pick, bz[g * EPG + i], cb[i]) for i in range(EPG)]
    i1 = jnp.zeros_like(gsel)
    b1 = cb[0]
    for i in range(1, EPG):
        better = cb[i] > b1
        i1 = jnp.where(better, i, i1)
        b1 = jnp.where(better, cb[i], b1)
    neg = jnp.full_like(b1, -jnp.inf)
    rest = [jnp.where(i1 == i, neg, cb[i]) for i in range(EPG)]
    i2 = jnp.zeros_like(gsel)
    b2 = rest[0]
    for i in range(1, EPG):
        better = rest[i] > b2
        i2 = jnp.where(better, i, i2)
        b2 = jnp.where(better, rest[i], b2)
    lo = jnp.minimum(i1, i2)
    hi = jnp.maximum(i1, i2)
    code = lo * EPG + hi
    pair = jnp.full_like(gsel, len(PAIR_A) - 1)
    for p in range(len(PAIR_A) - 1):
        a, b = min(PAIR_A[p], PAIR_B[p]), max(PAIR_A[p], PAIR_B[p])
        pair = jnp.where(code == a * EPG + b, p, pair)
    return gsel * len(PAIR_A) + pair


def _ffn_prologue(x1, row0, mod, nfw_ref, rw_ref, hp_ref, logit_ref):
    rows = x1.shape[0]
    sh2 = mod[:, 3 * D:4 * D]
    sc2 = mod[:, 4 * D:5 * D]
    h2 = (_rms(x1) * nfw_ref[...]) * (1.0 + sc2) + sh2
    _store_slabs(hp_ref, h2, row0)
    logits = _dot_3pass(rw_ref[...], h2, (((1,), (1,)), ((), ())))
    for k in range(rows // 128):
        logit_ref[:, row0 // 128 + k, :] = logits[:, k * 128:(k + 1) * 128]


def _ffn_out_specs(tm, n_tiles_per_b):
    specs = [
        pl.BlockSpec((1, tm, D), lambda b, j: (b, j, 0)),
        pl.BlockSpec((tm * SLAB, 128), lambda b, j: (b * n_tiles_per_b + j, 0)),
        pl.BlockSpec((NE, tm // 128, 128), lambda b, j: (0, b * n_tiles_per_b + j, 0)),
    ]
    shapes = [
        jax.ShapeDtypeStruct((B, T, D), F32),
        jax.ShapeDtypeStruct((N * SLAB, 128), F32),
        jax.ShapeDtypeStruct((NE, N // 128, 128), F32),
    ]
    return specs, shapes


def _route_kernel(logit_ref, bias_ref, dest_ref, chunk_ref):
    r = logit_ref.shape[1]
    cls = _route([_sigmoid(logit_ref[e]) + bias_ref[e] for e in range(NE)])
    lane_incl = (lax.broadcasted_iota(I32, (128, 128), 0) <= lax.broadcasted_iota(I32, (128, 128), 1)).astype(BF16)
    rows_before = (lax.broadcasted_iota(I32, (r, r), 1) < lax.broadcasted_iota(I32, (r, r), 0)).astype(BF16)
    dest = jnp.zeros((r, 128), F32)
    start = jnp.zeros((1, 128), F32)
    ends = []
    for c in range(NCLS):
        onehot = jnp.where(cls == c, 1.0, 0.0)
        in_row = jnp.dot(onehot.astype(BF16), lane_incl, preferred_element_type=F32)
        row_tot = jnp.broadcast_to(in_row[:, 127:128], (r, 128))
        above = jnp.dot(rows_before, row_tot.astype(BF16), preferred_element_type=F32)
        dest = dest + onehot * (start + above + in_row - 1.0)
        start = start + jnp.sum(row_tot, axis=0, keepdims=True)
        ends.append(start)
    dest_ref[...] = dest.astype(I32)
    chunk_ref[...] = _chunk_list(ends)


def _chunk_list(ends):
    lane = lax.broadcasted_iota(I32, (1, 128), 1).astype(F32)
    offs = [jnp.zeros((1, 128), F32)] + ends[:-1]
    per = [jnp.floor((e - o + (TM_MOE - 1)) * (1.0 / TM_MOE)) for e, o in zip(ends, offs)]
    cum = []
    total = jnp.zeros((1, 128), F32)
    for p in per:
        total = total + p
        cum.append(total)
    m = jnp.minimum(lane, total - 1.0)
    cls = jnp.zeros((1, 128), F32)
    for c in range(NCLS):
        cls = cls + jnp.where(cum[c] <= m, 1.0, 0.0)
    start = jnp.zeros((1, 128), F32)
    for c in range(NCLS):
        start = jnp.where(cls == c, offs[c] + TM_MOE * (m - (cum[c] - per[c])), start)
    n_pair = len(PAIR_A)
    grp = jnp.zeros((1, 128), F32)
    for g in range(1, NGRP):
        grp = grp + jnp.where(cls >= g * n_pair, 1.0, 0.0)
    pair = cls - n_pair * grp
    loc_a = jnp.zeros((1, 128), F32)
    loc_b = jnp.zeros((1, 128), F32)
    for p in range(n_pair):
        loc_a = jnp.where(pair == p, float(PAIR_A[p]), loc_a)
        loc_b = jnp.where(pair == p, float(PAIR_B[p]), loc_b)
    sub = lax.broadcasted_iota(I32, (128, 128), 0).astype(F32)
    lan = lax.broadcasted_iota(I32, (128, 128), 1).astype(F32)
    none = 1000.0

    def changes(e):
        chg = jnp.where((lane == 0.0) | (e != pltpu.roll(e, 1, axis=1)), 1.0, 0.0)
        chg_col = jnp.sum(jnp.where(sub == lan, chg, 0.0), axis=1, keepdims=True)
        e_col = jnp.sum(jnp.where(sub == lan, e, 0.0), axis=1, keepdims=True)
        nxt_at = jnp.min(jnp.where((sub > lan) & (chg_col == 1.0), sub, none), axis=0, keepdims=True)
        nxt_e = jnp.sum(jnp.where(sub == nxt_at, e_col, 0.0), axis=0, keepdims=True)
        return chg, jnp.where(nxt_at < none, nxt_e, -1.0)

    e_a = EPG * grp + loc_a
    e_b = EPG * grp + loc_b
    chg_a, nxt_a = changes(e_a)
    chg_b, nxt_b = changes(e_b)
    return jnp.concatenate([start, e_a, e_b, chg_a, chg_b, nxt_a, nxt_b, total], axis=0).astype(I32)


def _route_tokens(logits_t, router_b):
    r = N // 128
    return pl.pallas_call(
        _route_kernel,
        in_specs=[pl.BlockSpec((NE, r, 128), lambda: (0, 0, 0)), pl.BlockSpec(memory_space=pltpu.SMEM)],
        out_specs=[pl.BlockSpec((r, 128), lambda: (0, 0)), pl.BlockSpec((8, 128), lambda: (0, 0))],
        out_shape=[jax.ShapeDtypeStruct((r, 128), I32), jax.ShapeDtypeStruct((8, 128), I32)],
        compiler_params=pltpu.CompilerParams(vmem_limit_bytes=VMEM_LIMIT),
        name="route",
    )(logits_t, router_b)


def _outproj_kernel(ret_ref, dif_ref, w_ref, x_ref, mod_ref, nfw_ref, rw_ref, x1_ref, hp_ref, logit_ref):
    mod = mod_ref[0]

    def mix(r0):
        rows = slice(r0, r0 + SUB_OUT)
        return (jnp.dot(ret_ref[0, rows, :], w_ref[0:NH * HV, :], preferred_element_type=F32)
                + jnp.dot(dif_ref[0, rows, :], w_ref[NH * HV:, :], preferred_element_type=F32))

    mx = mix(0)
    for r0 in range(0, TM_OUT, SUB_OUT):
        mx_next = mix(r0 + SUB_OUT) if r0 + SUB_OUT < TM_OUT else None
        x1 = x_ref[0, r0:r0 + SUB_OUT, :] + mod[:, 2 * D:3 * D] * mx
        x1_ref[0, r0:r0 + SUB_OUT, :] = x1
        _ffn_prologue(x1, r0, mod, nfw_ref, rw_ref, hp_ref, logit_ref)
        mx = mx_next


def _outproj(ret, dif, w_out, x, mod3, nfw, rw_t):
    nj = T // TM_OUT
    out_specs, out_shapes = _ffn_out_specs(TM_OUT, nj)
    return pl.pallas_call(
        _outproj_kernel,
        grid=(B, nj),
        in_specs=[
            pl.BlockSpec((1, TM_OUT, NH * HV), lambda b, j: (b, j, 0)),
            pl.BlockSpec((1, TM_OUT, NH * HV), lambda b, j: (b, j, 0)),
            pl.BlockSpec((2 * NH * HV, D), lambda b, j: (0, 0)),
            pl.BlockSpec((1, TM_OUT, D), lambda b, j: (b, j, 0)),
            pl.BlockSpec((1, 1, 6 * D), lambda b, j: (b, 0, 0)),
            pl.BlockSpec((1, D), lambda b, j: (0, 0)),
            pl.BlockSpec((NE, D), lambda b, j: (0, 0)),
        ],
        out_specs=out_specs,
        out_shape=out_shapes,
        compiler_params=_cparams(("arbitrary", "arbitrary")),
        name="outproj",
    )(ret, dif, w_out, x, mod3, nfw, rw_t)


def _pool_kernel(x_ref, prev_ref, next_ref, mod_ref, nmw_ref, pw_ref, ps_ref, nfw_ref, rw_ref,
                 x1_ref, hp_ref, logit_ref):
    i = pl.program_id(1)
    last = pl.num_programs(1) - 1
    mod = mod_ref[0]
    sh1 = mod[:, 0:D]
    sc1 = mod[:, D:2 * D]

    def modnorm(v):
        return (_rms(v) * nmw_ref[...]) * (1.0 + sc1) + sh1

    x = x_ref[0]
    hc = modnorm(x)
    zeros = jnp.zeros((POOL_MARGIN - HALO, D), F32)
    ext = jnp.concatenate([zeros, jnp.where(i > 0, modnorm(prev_ref[0]), 0.0), hc,
                           jnp.where(i < last, modnorm(next_ref[0]), 0.0), zeros], axis=0)
    n_ext = ext.shape[0]
    pos = i * TM_POOL + lax.broadcasted_iota(I32, (TM_POOL, 1), 0)
    mixed = []
    run = ext
    for gi, w in enumerate(POOL_WINDOWS):
        left = w // 2
        right = w - 1 - left
        cols = slice(gi * PG, (gi + 1) * PG)
        run = run + pltpu.roll(run, n_ext - w // 2, axis=0)
        tot = run[POOL_MARGIN - left:POOL_MARGIN - left + TM_POOL, 0:PG]
        if gi + 1 < len(POOL_WINDOWS):
            run = run[:, PG:]
        cnt = (jnp.minimum(pos + right + 1, T) - jnp.maximum(pos - left, 0)).astype(F32)
        pooled = (tot * (1.0 / cnt) - hc[:, cols]).astype(BF16)
        mixed.append(jnp.dot(pooled, pw_ref[gi], preferred_element_type=F32))
    mixed = jnp.concatenate(mixed, axis=1) * ps_ref[...]
    x1 = x + mod[:, 2 * D:3 * D] * mixed
    x1_ref[0] = x1
    _ffn_prologue(x1, 0, mod, nfw_ref, rw_ref, hp_ref, logit_ref)


def _pool_layer(x, mod3, nmw, pool_w, pool_scale, nfw, rw_t):
    ni = T // TM_POOL
    hb = TM_POOL // HALO
    out_specs, out_shapes = _ffn_out_specs(TM_POOL, ni)
    return pl.pallas_call(
        _pool_kernel,
        grid=(B, ni),
        in_specs=[
            pl.BlockSpec((1, TM_POOL, D), lambda b, i: (b, i, 0)),
            pl.BlockSpec((1, HALO, D), lambda b, i: (b, jnp.maximum(i * hb - 1, 0), 0)),
            pl.BlockSpec((1, HALO, D), lambda b, i: (b, jnp.minimum((i + 1) * hb, T // HALO - 1), 0)),
            pl.BlockSpec((1, 1, 6 * D), lambda b, i: (b, 0, 0)),
            pl.BlockSpec((1, D), lambda b, i: (0, 0)),
            pl.BlockSpec((len(POOL_WINDOWS), PG, PG), lambda b, i: (0, 0, 0)),
            pl.BlockSpec((1, D), lambda b, i: (0, 0)),
            pl.BlockSpec((1, D), lambda b, i: (0, 0)),
            pl.BlockSpec((NE, D), lambda b, i: (0, 0)),
        ],
        out_specs=out_specs,
        out_shape=out_shapes,
        compiler_params=_cparams(("arbitrary", "arbitrary")),
        name="pool_layer",
    )(x, x, x, mod3, nmw, pool_w, pool_scale, nfw, rw_t)


def _tile_copy(src_ref, dst_ref, sem, s, d, rows=1):
    s0 = pl.multiple_of(s * SLAB, SLAB)
    d0 = pl.multiple_of(d * SLAB, SLAB)
    return pltpu.make_async_copy(src_ref.at[pl.ds(s0, rows * SLAB)], dst_ref.at[pl.ds(d0, rows * SLAB)], sem)


def _issue_tile_copies(idx_ref, base, rows, start_one):
    def group(g, carry):
        r0 = g * PERM_UNROLL
        ids = [idx_ref[base + r0 + u] for u in range(PERM_UNROLL)]
        for u in range(PERM_UNROLL):
            start_one(r0 + u, ids[u], u % 2)
        return carry

    lax.fori_loop(0, rows // PERM_UNROLL, group, 0)


def _dispatch_kernel(dest_ref, src_ref, dst_ref, zero_ref, sem, pad_sem):
    base = pl.program_id(0) * TM_PERM

    @pl.when(pl.program_id(0) == 0)
    def _():
        zero_ref[...] = jnp.zeros_like(zero_ref)
        pad = _tile_copy(zero_ref, dst_ref, pad_sem, 0, N, TM_MOE)
        pad.start()
        pad.wait()

    def start_one(r, d, priority):
        _tile_copy(src_ref, dst_ref, sem, r, d).start(priority=priority)

    _issue_tile_copies(dest_ref, base, TM_PERM, start_one)
    _tile_copy(src_ref, dst_ref, sem, 0, 0, TM_PERM).wait()


def _dispatch(dest, src):
    return pl.pallas_call(
        _dispatch_kernel,
        grid_spec=pltpu.PrefetchScalarGridSpec(
            num_scalar_prefetch=1,
            grid=(N // TM_PERM,),
            in_specs=[pl.BlockSpec((TM_PERM * SLAB, 128), lambda i, dest: (i, 0))],
            out_specs=pl.BlockSpec(memory_space=pl.ANY),
            scratch_shapes=[pltpu.VMEM((TM_MOE * SLAB, 128), F32), pltpu.SemaphoreType.DMA(()),
                            pltpu.SemaphoreType.DMA(())],
        ),
        out_shape=jax.ShapeDtypeStruct(((N + TM_MOE) * SLAB, 128), src.dtype),
        compiler_params=_cparams(("arbitrary",)),
        name="dispatch",
    )(dest, src)


def _combine_kernel(dest_ref, x_ref, ys_ref, mod_ref, fw_ref, o_ref, ybuf_ref, sem, *, final):
    i = pl.program_id(0)
    n = pl.num_programs(0)
    slot = i % 2

    def gather(step, to_slot):
        def start_one(r, d, priority):
            _tile_copy(ys_ref, ybuf_ref.at[to_slot], sem.at[to_slot], d, r).start(priority=priority)

        _issue_tile_copies(dest_ref, step * TM_COMB, TM_COMB, start_one)

    @pl.when(i == 0)
    def _():
        gather(0, 0)

    @pl.when(i + 1 < n)
    def _():
        gather(i + 1, 1 - slot)

    _tile_copy(ys_ref, ybuf_ref.at[slot], sem.at[slot], 0, 0, TM_COMB).wait()
    out = x_ref[...] + mod_ref[0][:, 5 * D:6 * D] * _load_slabs(ybuf_ref.at[slot], TM_COMB)
    if final:
        out = _rms(out) * fw_ref[...]
    o_ref[...] = out


def _combine(dest, x1, ys, mod3, final_w, final):
    per_b = T // TM_COMB
    return pl.pallas_call(
        functools.partial(_combine_kernel, final=final),
        grid_spec=pltpu.PrefetchScalarGridSpec(
            num_scalar_prefetch=1,
            grid=(N // TM_COMB,),
            in_specs=[
                pl.BlockSpec((TM_COMB, D), lambda i, dest: (i, 0)),
                pl.BlockSpec(memory_space=pl.ANY),
                pl.BlockSpec((1, 1, 6 * D), lambda i, dest: (i // per_b, 0, 0)),
                pl.BlockSpec((1, D), lambda i, dest: (0, 0)),
            ],
            out_specs=pl.BlockSpec((TM_COMB, D), lambda i, dest: (i, 0)),
            scratch_shapes=[pltpu.VMEM((2, TM_COMB * SLAB, 128), F32), pltpu.SemaphoreType.DMA((2,))],
        ),
        out_shape=jax.ShapeDtypeStruct((N, D), F32),
        compiler_params=_cparams(("arbitrary",)),
        name="combine",
    )(dest, x1.reshape(N, D), ys, mod3, final_w)


def _moe_kernel(chunk_ref, rw_ref, xs_hbm, wg_hbm, wu_hbm, wd_hbm, ys_hbm,
                xbuf_ref, obuf_ref,
                sga_ref, sua_ref, sda_ref, sgb_ref, sub_ref, sdb_ref,
                ga_ref, ua_ref, da_ref, gb_ref, ub_ref, db_ref, wsem, xsem, osem, *, layer):
    row = lambda r: (lambda m: chunk_ref[r, m])
    start_of, n = row(0), chunk_ref[7, 0]
    slots = ((row(1), row(3), row(5), (sga_ref, sua_ref, sda_ref), (ga_ref, ua_ref, da_ref)),
             (row(2), row(4), row(6), (sgb_ref, sub_ref, sdb_ref), (gb_ref, ub_ref, db_ref)))

    def fetch(w_slot, expert):
        stage = slots[w_slot][3]
        return [pltpu.make_async_copy(w.at[layer, expert], s, wsem.at[w_slot])
                for w, s in zip((wg_hbm, wu_hbm, wd_hbm), stage)]

    def x_copy(step, buf):
        return _tile_copy(xs_hbm, xbuf_ref.at[buf], xsem.at[buf], start_of(step), 0, TM_MOE)

    def o_copy(step, buf):
        return _tile_copy(obuf_ref.at[buf], ys_hbm, osem.at[buf], 0, start_of(step), TM_MOE)

    @pl.when(pl.program_id(0) == 0)
    def _():
        obuf_ref[1] = jnp.zeros_like(obuf_ref[1])
        pad = _tile_copy(obuf_ref.at[1], ys_hbm, osem.at[1], 0, N, TM_MOE)
        pad.start()
        pad.wait()
        x_copy(0, 0).start()

    def chunk(m, slot):
        @pl.when(m + 1 < n)
        def _():
            x_copy(m + 1, 1 - slot).start()

        for w_slot, (expert_of, changed_at, next_of, stage, work) in enumerate(slots):
            @pl.when(m == 0)
            def _():
                for cp in fetch(w_slot, expert_of(0)):
                    cp.start(priority=1)

            @pl.when(changed_at(m) == 1)
            def _():
                for cp in fetch(w_slot, 0):
                    cp.wait()
                for s, w in zip(stage, work):
                    w[...] = s[...].astype(BF16)

                @pl.when(next_of(m) >= 0)
                def _():
                    for cp in fetch(w_slot, next_of(m)):
                        cp.start(priority=1)

        x_copy(m, slot).wait()
        hf = _load_slabs(xbuf_ref.at[slot], TM_MOE)
        h = hf.astype(BF16)

        def up(w_ref):
            return jnp.dot(h, w_ref[...], preferred_element_type=F32)

        def down(g, u, d_ref):
            return jnp.dot((_silu(g) * u).astype(BF16), d_ref[...], preferred_element_type=F32)

        g_a, u_a, g_b, u_b = up(ga_ref), up(ua_ref), up(gb_ref), up(ub_ref)
        y_a = down(g_a, u_a, da_ref)
        y_b = down(g_b, u_b, db_ref)
        s_a = _sigmoid(jnp.sum(hf * rw_ref[pl.ds(slots[0][0](m), 1), :], axis=1, keepdims=True))
        s_b = _sigmoid(jnp.sum(hf * rw_ref[pl.ds(slots[1][0](m), 1), :], axis=1, keepdims=True))
        denom = s_a + s_b
        val = (s_a / denom) * y_a + (s_b / denom) * y_b

        @pl.when(m >= 1)
        def _():
            o_copy(m - 1, 1 - slot).wait()

        _store_slabs(obuf_ref.at[slot], val)
        o_copy(m, slot).start()

        @pl.when(m == n - 1)
        def _():
            o_copy(m, slot).wait()

    for j in range(2):
        m = pl.program_id(0) * 2 + j
        pl.when(m < n)(functools.partial(chunk, m, j))


def _moe_sorted(chunks, xs, rw_t, wg, wu, wd, layer):
    mats = lambda dt: [pltpu.VMEM((D, DE), dt), pltpu.VMEM((D, DE), dt), pltpu.VMEM((DE, D), dt)]
    hbm = pl.BlockSpec(memory_space=pl.ANY)
    return pl.pallas_call(
        functools.partial(_moe_kernel, layer=layer),
        grid_spec=pltpu.PrefetchScalarGridSpec(
            num_scalar_prefetch=1,
            grid=(N_UNITS // 2,),
            in_specs=[pl.BlockSpec((NE, D), lambda m, *_: (0, 0)), hbm, hbm, hbm, hbm],
            out_specs=hbm,
            scratch_shapes=([pltpu.VMEM((2, TM_MOE * SLAB, 128), F32), pltpu.VMEM((2, TM_MOE * SLAB, 128), F32)]
                            + mats(F32) + mats(F32) + mats(BF16) + mats(BF16)
                            + [pltpu.SemaphoreType.DMA((2,))] * 3),
        ),
        out_shape=jax.ShapeDtypeStruct(xs.shape, F32),
        compiler_params=_cparams(("arbitrary",)),
        name="moe_sorted",
    )(chunks, rw_t, xs, wg, wu, wd)


def _moe_layer(x1, hp, logits, rw_t, router_b, wg, wu, wd, layer, mod3, final_w, final):
    dest, chunks = _route_tokens(logits, router_b)
    dest = dest.reshape(N)
    xs = _dispatch(dest, hp)
    ys = _moe_sorted(chunks, xs, rw_t, wg, wu, wd, layer)
    return _combine(dest, x1, ys, mod3, final_w, final).reshape(B, T, D)


def _rope_tables():
    half = 16
    inv = ROPE_BASE ** (-np.arange(half, dtype=np.float64) / half)
    t = np.arange(T)
    ang_r = (t // GRID_W)[:, None] * inv[None, :]
    ang_c = (t % GRID_W)[:, None] * inv[None, :]
    ang = np.concatenate([ang_r, ang_r, ang_c, ang_c], axis=1)
    sign = np.tile(np.concatenate([-np.ones(half), np.ones(half)]), 2)
    cos = np.concatenate([np.ones((LC, 64)), np.cos(ang)], axis=0)
    sin = np.concatenate([np.zeros((LC, 64)), np.sin(ang) * sign[None, :]], axis=0)
    return (jnp.asarray(np.tile(cos, (1, 2)), dtype=F32), jnp.asarray(np.tile(sin, (1, 2)), dtype=F32))


def _permute_w_in(w):
    rq = w[:, 0:256].reshape(D, NH, DK)
    dq = w[:, 256:768]
    rg = w[:, 768:1280]
    rk = w[:, 1280:1536].reshape(D, NH, DK)
    rv = w[:, 1536:2048]
    dk = w[:, 2048:2560]
    dv = w[:, 2560:3072]
    qk = jnp.concatenate([rq, rk * (DK ** -0.5)], axis=2).reshape(D, NH * 2 * DK)
    return jnp.concatenate([qk, rv, rg, dq * (DK ** -0.5 * math.log2(math.e)), dk, dv], axis=1).astype(BF16)


def kernel(x, c, ctx, c_ctx, ada_w, ada_b, norm_mix_w, norm_ffn_w, w_in, w_out, ret_log_decay, diff_lambda,
           diff_subln_w, pool_w, pool_scale, router_w, router_b, moe_w_gate, moe_w_up, moe_w_down, final_norm_w):
    assert x.shape == (B, T, D) and ctx.shape == (B, LC, D) and ada_w.shape[0] == 2
    cc = jnp.concatenate([c, c_ctx[None, :], jnp.zeros((16 - B - 1, D), F32)], axis=0)
    mod = _ada_mod(cc, ada_w, ada_b)
    rw_t = router_w.T
    fw = final_norm_w.reshape(1, D)
    experts = (moe_w_gate, moe_w_up, moe_w_down)

    mod0 = mod[0].reshape(16, 1, 6 * D)
    cos_t, sin_t = _rope_tables()
    proj = _inproj(x, ctx, mod0, norm_mix_w[0:1], _permute_w_in(w_in[0]), cos_t, sin_t)
    ret = _retention(proj, ret_log_decay[0])
    lam_init = 0.8 - 0.6 * math.exp(-0.3 * 0)
    lv = diff_lambda[0]
    lam = jnp.exp(jnp.sum(lv[0] * lv[1])) - jnp.exp(jnp.sum(lv[2] * lv[3])) + lam_init
    dif = _diffattn(proj, lam.reshape(1), diff_subln_w[0:1], 1.0 - lam_init)
    x1, hp, logits = _outproj(ret, dif, w_out[0].astype(BF16), x, mod0, norm_ffn_w[0:1], rw_t)
    x2 = _moe_layer(x1, hp, logits, rw_t, router_b, *experts, 0, mod0, fw, False)

    mod1 = mod[1].reshape(16, 1, 6 * D)
    x3, hp, logits = _pool_layer(x2, mod1, norm_mix_w[1:2], pool_w[0].astype(BF16), pool_scale[0:1],
                                 norm_ffn_w[1:2], rw_t)
    return _moe_layer(x3, hp, logits, rw_t, router_b, *experts, 1, mod1, fw, True)
```
